```python
import math
import jax, jax.numpy as jnp
from jax import lax
import numpy as np

D_MODEL = 1024
BATCH = 4
SEQ = 4096
DEPTH = 1
DEC_BATCH = 32
DEC_SEQ = 8
PAST_LEN = 8192
PAGE_SIZE = 128

MIX_WIDTH = D_MODEL
ATTN_WIDTH = MIX_WIDTH // 2
LRU_WIDTH = MIX_WIDTH - ATTN_WIDTH
N_HEADS = 4
V_DIM = ATTN_WIDTH // N_HEADS
QK_DIM = V_DIM // 2
N_LRU_BLOCKS = 8
LRU_BLOCK = LRU_WIDTH // N_LRU_BLOCKS
CONV_W = 4
LRU_C = 8.0
IN_WIDTH = 3 * ATTN_WIDTH + 2 * LRU_WIDTH
N_GROUPS = 4
EXPERTS_PER_GROUP = 8
N_EXPERTS = N_GROUPS * EXPERTS_PER_GROUP
TOP_K = 2
D_EXPERT = D_MODEL // 4
Q_BLOCK = 128
LN_EPS = 1e-5
DEEPNORM_ALPHA = (2 * DEPTH) ** 0.25
DEEPNORM_BETA = (8 * DEPTH) ** -0.25

kernel_name = 'hymba_diffattn_rglru_hmoe_step'

F32 = jnp.float32


def _layer_norm(x, g, b):
    xf = x.astype(F32)
    mu = jnp.mean(xf, axis=-1, keepdims=True)
    var = jnp.mean(jnp.square(xf - mu), axis=-1, keepdims=True)
    return ((xf - mu) * lax.rsqrt(var + LN_EPS) * g + b).astype(x.dtype)


def _split_proj(x, w_in):
    B, S, _ = x.shape
    z = x @ w_in
    q, k, v, xl, gl = jnp.split(
        z, [ATTN_WIDTH, 2 * ATTN_WIDTH, 3 * ATTN_WIDTH, 3 * ATTN_WIDTH + LRU_WIDTH], axis=-1)
    q = q.reshape(B, S, N_HEADS, 2, QK_DIM)
    k = k.reshape(B, S, N_HEADS, 2, QK_DIM)
    v = v.reshape(B, S, N_HEADS, V_DIM)
    return q, k, v, xl, gl


def _diff_lambda(lq1, lk1, lq2, lk2, lam_init):
    return (jnp.exp(jnp.sum(lq1.astype(F32) * lk1.astype(F32)))
            - jnp.exp(jnp.sum(lq2.astype(F32) * lk2.astype(F32))) + lam_init)


def _diff_weights(s, lam):
    p = jax.nn.softmax(s, axis=-1)
    return p[:, :, 0] - lam * p[:, :, 1]


def _attn_prompt(q, k, v, lam):
    B, S = q.shape[:2]
    nb = S // Q_BLOCK
    scale = QK_DIM ** -0.5
    qb = q.reshape(B, nb, Q_BLOCK, N_HEADS, 2, QK_DIM).swapaxes(0, 1)
    kpos = jnp.arange(S)
    v32 = v.astype(F32)

    def block(args):
        qi, i = args
        s = jnp.einsum('bqhmd,bkhmd->bhmqk', qi, k).astype(F32) * scale
        qpos = i * Q_BLOCK + jnp.arange(Q_BLOCK)
        mask = kpos[None, :] <= qpos[:, None]
        s = jnp.where(mask, s, -jnp.inf)
        w = _diff_weights(s, lam)
        return jnp.einsum('bhqk,bkhv->bqhv', w, v32)

    o = lax.map(block, (qb, jnp.arange(nb)))
    return o.swapaxes(0, 1).reshape(B, S, N_HEADS, V_DIM)


def _attn_sample(q, k_new, v_new, k_past, v_past, lam):
    Sd = q.shape[1]
    P = k_past.shape[1]
    scale = QK_DIM ** -0.5
    s_past = jnp.einsum('bqhmd,bkhmd->bhmqk', q, k_past).astype(F32) * scale
    s_new = jnp.einsum('bqhmd,bkhmd->bhmqk', q, k_new).astype(F32) * scale
    causal = jnp.tril(jnp.ones((Sd, Sd), dtype=bool))
    s_new = jnp.where(causal, s_new, -jnp.inf)
    w = _diff_weights(jnp.concatenate([s_past, s_new], axis=-1), lam)
    return (jnp.einsum('bhqk,bkhv->bqhv', w[..., :P], v_past.astype(F32))
            + jnp.einsum('bhqk,bkhv->bqhv', w[..., P:], v_new.astype(F32)))


def _head_rmsnorm(o, g, lam_init):
    of = o.astype(F32)
    of = of * lax.rsqrt(jnp.mean(jnp.square(of), axis=-1, keepdims=True) + LN_EPS)
    return of * g * (1.0 - lam_init)


def _causal_conv(x, buf, w, b):
    S = x.shape[1]
    xp = jnp.concatenate([buf.astype(x.dtype), x], axis=1)
    out = b + sum(w[j] * xp[:, j:j + S] for j in range(CONV_W))
    return out, xp[:, -(CONV_W - 1):]


def _linear_combine(c1, c2):
    a1, b1 = c1
    a2, b2 = c2
    return a1 * a2, a2 * b1 + b2


def _recurrent_branch(xl, gl, conv_buf, h0, conv_w, conv_b, w_a, b_a, w_x, b_x, lam_lru):
    xc, new_buf = _causal_conv(xl, conv_buf, conv_w, conv_b)
    B, S, C = xc.shape
    xb = xc.reshape(B, S, N_LRU_BLOCKS, LRU_BLOCK)
    r = jax.nn.sigmoid(jnp.einsum('bshi,hij->bshj', xb, w_a).astype(F32) + b_a).reshape(B, S, C)
    ig = jax.nn.sigmoid(jnp.einsum('bshi,hij->bshj', xb, w_x).astype(F32) + b_x).reshape(B, S, C)
    log_a = -LRU_C * r * jax.nn.softplus(-lam_lru.astype(F32))
    a = jnp.exp(log_a)
    mult = jnp.sqrt(jnp.maximum(-jnp.expm1(2.0 * log_a), 0.0))
    u = mult * ig * xc.astype(F32)
    u = u.at[:, 0].add(a[:, 0] * h0.astype(F32))
    _, h = lax.associative_scan(_linear_combine, (a, u), axis=1)
    y = h * jax.nn.gelu(gl.astype(F32))
    return y.astype(xl.dtype), h[:, -1].astype(xl.dtype), new_buf


def _merge(o_attn, y_lru, w_out):
    B, S = y_lru.shape[:2]
    cat = jnp.concatenate([o_attn.reshape(B, S, ATTN_WIDTH).astype(y_lru.dtype), y_lru], axis=-1)
    return cat @ w_out


def _hier_moe(t, wgr, bgr, wer, ber, wg, wu, wd):
    T = t.shape[0]
    glog = (t @ wgr).astype(F32) + bgr
    gprob = jax.nn.softmax(glog, axis=-1)
    gid = jnp.argmax(glog, axis=-1)
    gw = jnp.take_along_axis(gprob, gid[:, None], axis=-1)[:, 0]
    elog = ((t @ wer).astype(F32) + ber).reshape(T, N_GROUPS, EXPERTS_PER_GROUP)
    within = jnp.take_along_axis(elog, gid[:, None, None], axis=1)[:, 0]
    vals, idx = lax.top_k(within, TOP_K)
    ew = jax.nn.softmax(vals, axis=-1) * gw[:, None]
    eid = gid[:, None] * EXPERTS_PER_GROUP + idx
    comb = jnp.sum(jax.nn.one_hot(eid, N_EXPERTS, dtype=F32) * ew[..., None], axis=1)
    comb = comb.reshape(T, N_GROUPS, EXPERTS_PER_GROUP)
    y = jnp.zeros((T, t.shape[1]), F32)
    for g in range(N_GROUPS):
        sl = slice(g * EXPERTS_PER_GROUP, (g + 1) * EXPERTS_PER_GROUP)
        h = (jax.nn.silu(jnp.einsum('td,edf->tef', t, wg[sl]).astype(F32))
             * jnp.einsum('td,edf->tef', t, wu[sl]).astype(F32))
        y = y + jnp.einsum('tef,efd->td', h * comb[:, g, :, None], wd[sl].astype(F32))
    return y.astype(t.dtype)


def setup_inputs(seed: int = 0) -> dict:
    key = jax.random.key(seed)
    ks = jax.random.split(key, 32)
    n_pages = PAST_LEN // PAGE_SIZE
    n_phys = (5 * DEC_BATCH * n_pages + 3) // 4

    def nrm(k, shape, scale):
        return jax.random.normal(k, shape, F32) * scale

    x_prompt = nrm(ks[0], (BATCH, SEQ, D_MODEL), 1.0)
    x_sample = nrm(ks[1], (DEC_BATCH, DEC_SEQ, D_MODEL), 1.0)
    cache_k = nrm(ks[2], (DEPTH, n_phys, PAGE_SIZE, N_HEADS, 2 * QK_DIM), 1.0)
    cache_v = nrm(ks[3], (DEPTH, n_phys, PAGE_SIZE, N_HEADS, V_DIM), 1.0)
    state_h = nrm(ks[4], (DEPTH, DEC_BATCH, LRU_WIDTH), 0.5)
    state_conv = nrm(ks[5], (DEPTH, DEC_BATCH, CONV_W - 1, LRU_WIDTH), 1.0)
    page_table = jax.random.permutation(ks[6], n_phys)[:DEC_BATCH * n_pages].reshape(
        DEC_BATCH, n_pages).astype(jnp.int32)
    u = jax.random.uniform(ks[7], (DEPTH, LRU_WIDTH), F32, 0.9, 0.999)
    s = u ** (1.0 / LRU_C)
    lru_lambda = jnp.log(s) - jnp.log1p(-s)
    return {
        'x_prompt': x_prompt,
        'x_sample': x_sample,
        'cache_k': cache_k,
        'cache_v': cache_v,
        'state_h': state_h,
        'state_conv': state_conv,
        'page_table': page_table,
        'w_in': nrm(ks[8], (DEPTH, D_MODEL, IN_WIDTH), D_MODEL ** -0.5),
        'lambda_q1': nrm(ks[9], (DEPTH, QK_DIM), 0.1),
        'lambda_k1': nrm(ks[10], (DEPTH, QK_DIM), 0.1),
        'lambda_q2': nrm(ks[11], (DEPTH, QK_DIM), 0.1),
        'lambda_k2': nrm(ks[12], (DEPTH, QK_DIM), 0.1),
        'subln_g': 1.0 + nrm(ks[13], (DEPTH, V_DIM), 0.02),
        'conv_w': nrm(ks[14], (DEPTH, CONV_W, LRU_WIDTH), CONV_W ** -0.5),
        'conv_b': nrm(ks[15], (DEPTH, LRU_WIDTH), 0.02),
        'w_rg_a': nrm(ks[16], (DEPTH, N_LRU_BLOCKS, LRU_BLOCK, LRU_BLOCK), LRU_BLOCK ** -0.5),
        'b_rg_a': nrm(ks[17], (DEPTH, N_LRU_BLOCKS, LRU_BLOCK), 0.1),
        'w_rg_x': nrm(ks[18], (DEPTH, N_LRU_BLOCKS, LRU_BLOCK, LRU_BLOCK), LRU_BLOCK ** -0.5),
        'b_rg_x': nrm(ks[19], (DEPTH, N_LRU_BLOCKS, LRU_BLOCK), 0.1),
        'lru_lambda': lru_lambda,
        'w_out': nrm(ks[20], (DEPTH, MIX_WIDTH, D_MODEL), DEEPNORM_BETA * MIX_WIDTH ** -0.5),
        'ln1_g': 1.0 + nrm(ks[21], (DEPTH, D_MODEL), 0.02),
        'ln1_b': nrm(ks[22], (DEPTH, D_MODEL), 0.02),
        'w_group_router': nrm(ks[23], (DEPTH, D_MODEL, N_GROUPS), D_MODEL ** -0.5),
        'b_group_router': nrm(ks[24], (DEPTH, N_GROUPS), 0.01),
        'w_expert_router': nrm(ks[25], (DEPTH, D_MODEL, N_EXPERTS), D_MODEL ** -0.5),
        'b_expert_router': nrm(ks[26], (DEPTH, N_EXPERTS), 0.01),
        'w_gate_e': nrm(ks[27], (DEPTH, N_EXPERTS, D_MODEL, D_EXPERT), D_MODEL ** -0.5),
        'w_up_e': nrm(ks[28], (DEPTH, N_EXPERTS, D_MODEL, D_EXPERT), D_MODEL ** -0.5),
        'w_down_e': nrm(ks[29], (DEPTH, N_EXPERTS, D_EXPERT, D_MODEL), DEEPNORM_BETA * D_EXPERT ** -0.5),
        'ln2_g': 1.0 + nrm(ks[30], (DEPTH, D_MODEL), 0.02),
        'ln2_b': nrm(ks[31], (DEPTH, D_MODEL), 0.02),
    }


def reference(x_prompt, x_sample, cache_k, cache_v, state_h, state_conv, page_table,
              w_in, lambda_q1, lambda_k1, lambda_q2, lambda_k2, subln_g,
              conv_w, conv_b, w_rg_a, b_rg_a, w_rg_x, b_rg_x, lru_lambda, w_out,
              ln1_g, ln1_b, w_group_router, b_group_router, w_expert_router, b_expert_router,
              w_gate_e, w_up_e, w_down_e, ln2_g, ln2_b):
    xp, xs = x_prompt, x_sample
    Bp, Sp = xp.shape[:2]
    Bd, Sd = xs.shape[:2]
    past = page_table.shape[1] * PAGE_SIZE
    kp_l, vp_l, hp_l, cp_l, ks_l, vs_l, hs_l, cs_l = [], [], [], [], [], [], [], []
    for l in range(DEPTH):
        lam_init = 0.8 - 0.6 * math.exp(-0.3 * l)
        lam = _diff_lambda(lambda_q1[l], lambda_k1[l], lambda_q2[l], lambda_k2[l], lam_init)
        lru_args = (conv_w[l], conv_b[l], w_rg_a[l], b_rg_a[l], w_rg_x[l], b_rg_x[l], lru_lambda[l])

        q, k, v, xl, gl = _split_proj(xp, w_in[l])
        o = _head_rmsnorm(_attn_prompt(q, k, v, lam), subln_g[l], lam_init)
        y_lru, h_last, cbuf = _recurrent_branch(
            xl, gl, jnp.zeros((Bp, CONV_W - 1, LRU_WIDTH), xp.dtype),
            jnp.zeros((Bp, LRU_WIDTH), F32), *lru_args)
        xp = _layer_norm(DEEPNORM_ALPHA * xp + _merge(o, y_lru, w_out[l]), ln1_g[l], ln1_b[l])
        kp_l.append(k.reshape(Bp, Sp, N_HEADS, 2 * QK_DIM))
        vp_l.append(v)
        hp_l.append(h_last)
        cp_l.append(cbuf)

        q, k, v, xl, gl = _split_proj(xs, w_in[l])
        k_past = cache_k[l, page_table].reshape(Bd, past, N_HEADS, 2, QK_DIM)
        v_past = cache_v[l, page_table].reshape(Bd, past, N_HEADS, V_DIM)
        o = _head_rmsnorm(_attn_sample(q, k, v, k_past, v_past, lam), subln_g[l], lam_init)
        y_lru, h_last, cbuf = _recurrent_branch(xl, gl, state_conv[l], state_h[l], *lru_args)
        xs = _layer_norm(DEEPNORM_ALPHA * xs + _merge(o, y_lru, w_out[l]), ln1_g[l], ln1_b[l])
        ks_l.append(k.reshape(Bd, Sd, N_HEADS, 2 * QK_DIM))
        vs_l.append(v)
        hs_l.append(h_last)
        cs_l.append(cbuf)

        tok = jnp.concatenate([xp.reshape(-1, D_MODEL), xs.reshape(-1, D_MODEL)], axis=0)
        f = _hier_moe(tok, w_group_router[l], b_group_router[l], w_expert_router[l],
                      b_expert_router[l], w_gate_e[l], w_up_e[l], w_down_e[l])
        n_p = Bp * Sp
        xp = _layer_norm(DEEPNORM_ALPHA * xp + f[:n_p].reshape(xp.shape), ln2_g[l], ln2_b[l])
        xs = _layer_norm(DEEPNORM_ALPHA * xs + f[n_p:].reshape(xs.shape), ln2_g[l], ln2_b[l])

    return (xp, xs, jnp.stack(kp_l), jnp.stack(vp_l), jnp.stack(hp_l), jnp.stack(cp_l),
            jnp.stack(ks_l), jnp.stack(vs_l), jnp.stack(hs_l), jnp.stack(cs_l))
```

```python
import functools
import math

import jax
import jax.numpy as jnp
from jax import lax
from jax.experimental import pallas as pl
from jax.experimental.pallas import tpu as pltpu

F32 = jnp.float32
BF16 = jnp.bfloat16
I32 = jnp.int32

D_MODEL = 1024
ATTN_WIDTH = 512
LRU_WIDTH = 512
N_HEADS = 4
V_DIM = 128
QK_DIM = 64
N_LRU_BLOCKS = 8
LRU_BLOCK = 64
CONV_W = 4
LRU_C = 8.0
IN_WIDTH = 3 * ATTN_WIDTH + 2 * LRU_WIDTH
N_GROUPS = 4
EXPERTS_PER_GROUP = 8
N_EXPERTS = 32
D_EXPERT = 256
PAGE_SIZE = 128
LN_EPS = 1e-5
DEPTH = 1
DEEPNORM_ALPHA = (2 * DEPTH) ** 0.25
LAM_INIT = 0.8 - 0.6 * math.exp(-0.3 * 0)
QK_SCALE = QK_DIM ** -0.5

SUBLANES = 8
LANES = 128
VMEM_LIMIT_BYTES = 48 * 1024 * 1024

INPROJ_TM = 512
ATTN_TQ = 512
ATTN_TK = 512
LRU_TS = 512
TOK_TM = 256
PAGES_PER_STEP = 8
ROUTE_LANES = 128


def _cparams(n_axes):
    return pltpu.CompilerParams(
        dimension_semantics=("arbitrary",) * n_axes,
        vmem_limit_bytes=VMEM_LIMIT_BYTES)


def _layer_norm(x, g, b):
    mu = jnp.mean(x, axis=-1, keepdims=True)
    xc = x - mu
    var = jnp.mean(xc * xc, axis=-1, keepdims=True)
    return xc * lax.rsqrt(var + LN_EPS) * g + b


def _diff_lambda(lq1, lk1, lq2, lk2):
    s1 = jnp.sum(lq1 * lk1, axis=-1, keepdims=True)
    s2 = jnp.sum(lq2 * lk2, axis=-1, keepdims=True)
    return jnp.exp(s1) - jnp.exp(s2) + LAM_INIT


def _inproj_kernel(x_ref, w_ref, q_ref, k_ref, v_ref, xl_ref, gl_ref, *, masked_q):
    x = x_ref[...].astype(BF16)

    def section(idx, width):
        return jnp.dot(x, w_ref[:, idx:idx + width], preferred_element_type=F32)

    q = section(0, ATTN_WIDTH) * QK_SCALE
    if masked_q:
        lane = lax.broadcasted_iota(I32, q.shape, 1)
        first = (lane % V_DIM) < QK_DIM
        q_ref[0] = jnp.where(first, q, 0.0).astype(q_ref.dtype)
        q_ref[1] = jnp.where(first, 0.0, q).astype(q_ref.dtype)
    else:
        q_ref[...] = q
    k_ref[...] = section(ATTN_WIDTH, ATTN_WIDTH)
    v_ref[...] = section(2 * ATTN_WIDTH, ATTN_WIDTH)
    xl_ref[...] = section(3 * ATTN_WIDTH, LRU_WIDTH)
    gl_ref[...] = section(3 * ATTN_WIDTH + LRU_WIDTH, LRU_WIDTH)


def _inproj(x2d, w_bf16, *, masked_q, tm):
    t = x2d.shape[0]
    assert t % tm == 0
    row = lambda i: (i, 0)
    spec512 = pl.BlockSpec((tm, ATTN_WIDTH), row)
    if masked_q:
        q_shape = jax.ShapeDtypeStruct((2, t, ATTN_WIDTH), BF16)
        q_spec = pl.BlockSpec((2, tm, ATTN_WIDTH), lambda i: (0, i, 0))
    else:
        q_shape = jax.ShapeDtypeStruct((t, ATTN_WIDTH), F32)
        q_spec = spec512
    f32_out = jax.ShapeDtypeStruct((t, ATTN_WIDTH), F32)
    return pl.pallas_call(
        functools.partial(_inproj_kernel, masked_q=masked_q),
        grid=(t // tm,),
        in_specs=[pl.BlockSpec((tm, D_MODEL), row),
                  pl.BlockSpec((D_MODEL, IN_WIDTH), lambda i: (0, 0))],
        out_specs=[q_spec, spec512, spec512, spec512, spec512],
        out_shape=[q_shape, f32_out, f32_out, f32_out, f32_out],
        compiler_params=_cparams(1),
        name="inproj_prompt" if masked_q else "inproj_sample",
    )(x2d, w_bf16)


def _head_rmsnorm(o, g):
    ms = jnp.mean(o * o, axis=-1, keepdims=True)
    return o * lax.rsqrt(ms + LN_EPS) * g * (1.0 - LAM_INIT)


def _attn_prompt_kernel(q_ref, k_ref, v_ref, lq1_ref, lk1_ref, lq2_ref, lk2_ref, g_ref,
                        o_ref, m_ref, l_ref, acc_ref):
    tq, tk = ATTN_TQ, ATTN_TK
    i = pl.program_id(2)
    q = q_ref[...].reshape(2 * tq, V_DIM)

    m_ref[...] = jnp.full(m_ref.shape, -jnp.inf, F32)
    l_ref[...] = jnp.zeros(l_ref.shape, F32)
    acc_ref[...] = jnp.zeros(acc_ref.shape, F32)

    def chunk(start, masked):
        k = k_ref[pl.ds(start, tk), :].astype(BF16)
        v = v_ref[pl.ds(start, tk), :].astype(BF16)
        s = lax.dot_general(q, k, (((1,), (1,)), ((), ())), preferred_element_type=F32)
        if masked:
            row = lax.broadcasted_iota(I32, s.shape, 0) % tq
            col = lax.broadcasted_iota(I32, s.shape, 1)
            s = jnp.where(col <= row, s, -jnp.inf)
        m_prev = m_ref[...]
        m_new = jnp.maximum(m_prev, jnp.max(s, axis=-1, keepdims=True))
        alpha = jnp.exp(m_prev - m_new)
        p = jnp.exp(s - m_new)
        l_ref[...] = alpha * l_ref[...] + jnp.sum(p, axis=-1, keepdims=True)
        acc_ref[...] = alpha * acc_ref[...] + jnp.dot(p.astype(BF16), v,
                                                      preferred_element_type=F32)
        m_ref[...] = m_new

    def body(j, carry):
        chunk(pl.multiple_of(j * tk, tk), masked=False)
        return carry

    lax.fori_loop(0, i * (tq // tk), body, 0)
    chunk(pl.multiple_of(i * tq, tq), masked=True)

    lam = _diff_lambda(lq1_ref[...], lk1_ref[...], lq2_ref[...], lk2_ref[...])
    o_all = acc_ref[...] / l_ref[...]
    o = o_all[:tq] - lam * o_all[tq:]
    o_ref[...] = _head_rmsnorm(o, g_ref[...]).astype(o_ref.dtype)


def _attn_prompt(qm, k, v, lam_params, subln_g, *, batch, seq):
    assert ATTN_TQ == ATTN_TK and seq % ATTN_TQ == 0
    nq = seq // ATTN_TQ
    t = batch * seq
    vec = lambda n: pl.BlockSpec((1, n), lambda b, h, i: (0, 0))
    kv_spec = pl.BlockSpec((seq, V_DIM), lambda b, h, i: (b, h))
    return pl.pallas_call(
        _attn_prompt_kernel,
        grid=(batch, N_HEADS, nq),
        in_specs=[pl.BlockSpec((2, ATTN_TQ, V_DIM), lambda b, h, i: (0, b * nq + i, h)),
                  kv_spec, kv_spec,
                  vec(QK_DIM), vec(QK_DIM), vec(QK_DIM), vec(QK_DIM), vec(V_DIM)],
        out_specs=pl.BlockSpec((ATTN_TQ, V_DIM), lambda b, h, i: (b * nq + i, h)),
        out_shape=jax.ShapeDtypeStruct((t, ATTN_WIDTH), BF16),
        scratch_shapes=[pltpu.VMEM((2 * ATTN_TQ, 1), F32),
                        pltpu.VMEM((2 * ATTN_TQ, 1), F32),
                        pltpu.VMEM((2 * ATTN_TQ, V_DIM), F32)],
        compiler_params=_cparams(3),
        name="attn_prompt",
    )(qm, k, v, *lam_params, subln_g)


def _attn_sample_kernel(pt_ref, q_ref, kn_ref, vn_ref, ck_hbm, cv_hbm,
                        lq1_ref, lk1_ref, lq2_ref, lk2_ref, g_ref,
                        o_ref, kbuf, vbuf, ksem, vsem, m_ref, l_ref, acc_ref,
                        *, n_chunks, n_steps, dec_seq):
    gpp = PAGES_PER_STEP
    n_maps = 2 * N_HEADS
    rows = n_maps * dec_seq
    g = pl.program_id(0)
    c = g % n_chunks
    slot = g % 2

    def page_copies(step, sl):
        copies = []
        for p in range(gpp):
            page = pt_ref[step * gpp + p]
            src = pl.ds(page * PAGE_SIZE, PAGE_SIZE)
            dst = pl.ds(p * PAGE_SIZE, PAGE_SIZE)
            copies.append(pltpu.make_async_copy(ck_hbm.at[src], kbuf.at[sl, dst], ksem.at[sl]))
            copies.append(pltpu.make_async_copy(cv_hbm.at[src], vbuf.at[sl, dst], vsem.at[sl]))
        return copies

    @pl.when(g == 0)
    def _():
        for cp in page_copies(0, 0):
            cp.start()

    @pl.when(g + 1 < n_steps)
    def _():
        for cp in page_copies(g + 1, 1 - slot):
            cp.start()

    q = q_ref[...]
    qt = jnp.concatenate([q] * n_maps, axis=0)
    rgrp = lax.broadcasted_iota(I32, qt.shape, 0) // dec_seq
    lgrp = lax.broadcasted_iota(I32, qt.shape, 1) // QK_DIM
    qblk = jnp.where(rgrp == lgrp, qt, 0.0).astype(BF16)

    @pl.when(c == 0)
    def _():
        m_ref[...] = jnp.full(m_ref.shape, -jnp.inf, F32)
        l_ref[...] = jnp.zeros(l_ref.shape, F32)
        acc_ref[...] = jnp.zeros(acc_ref.shape, F32)

    def update(k, v, mask):
        s = lax.dot_general(qblk, k, (((1,), (1,)), ((), ())), preferred_element_type=F32)
        if mask is not None:
            s = jnp.where(mask, s, -jnp.inf)
        m_prev = m_ref[...]
        m_new = jnp.maximum(m_prev, jnp.max(s, axis=-1, keepdims=True))
        alpha = jnp.exp(m_prev - m_new)
        p = jnp.exp(s - m_new)
        l_ref[...] = alpha * l_ref[...] + jnp.sum(p, axis=-1, keepdims=True)
        acc_ref[...] = alpha * acc_ref[...] + jnp.dot(p.astype(BF16), v,
                                                      preferred_element_type=F32)
        m_ref[...] = m_new

    for cp in page_copies(g, slot):
        cp.wait()
    update(kbuf[slot].astype(BF16), vbuf[slot].astype(BF16), None)

    @pl.when(c == n_chunks - 1)
    def _():
        pad = jnp.zeros((LANES - dec_seq, ATTN_WIDTH), F32)
        kn = jnp.concatenate([kn_ref[...], pad], axis=0).astype(BF16)
        vn = jnp.concatenate([vn_ref[...], pad], axis=0).astype(BF16)
        qi = lax.broadcasted_iota(I32, (rows, LANES), 0) % dec_seq
        kj = lax.broadcasted_iota(I32, (rows, LANES), 1)
        update(kn, vn, kj <= qi)

        lam = _diff_lambda(lq1_ref[...], lk1_ref[...], lq2_ref[...], lk2_ref[...])
        o_all = acc_ref[...] / l_ref[...]
        lane_head = lax.broadcasted_iota(I32, (dec_seq, ATTN_WIDTH), 1) // V_DIM
        o = jnp.zeros((dec_seq, ATTN_WIDTH), F32)
        for h in range(N_HEADS):
            r1 = (2 * h) * dec_seq
            r2 = (2 * h + 1) * dec_seq
            oh = o_all[r1:r1 + dec_seq] - lam * o_all[r2:r2 + dec_seq]
            o = jnp.where(lane_head == h, oh, o)
        sq = o * o
        ms = jnp.zeros_like(o)
        for h in range(N_HEADS):
            part = jnp.sum(jnp.where(lane_head == h, sq, 0.0), axis=-1, keepdims=True)
            ms = jnp.where(lane_head == h, part * (1.0 / V_DIM), ms)
        o_ref[...] = o * lax.rsqrt(ms + LN_EPS) * g_ref[...] * (1.0 - LAM_INIT)


def _attn_sample(q, k_new, v_new, cache_k2d, cache_v2d, page_table, lam_params, subln_g4,
                 *, dec_batch, dec_seq):
    n_pages = page_table.shape[1]
    assert n_pages % PAGES_PER_STEP == 0 and dec_seq % SUBLANES == 0
    n_chunks = n_pages // PAGES_PER_STEP
    n_steps = dec_batch * n_chunks
    rows = 2 * N_HEADS * dec_seq
    chunk_keys = PAGES_PER_STEP * PAGE_SIZE
    tok = pl.BlockSpec((dec_seq, ATTN_WIDTH), lambda g, pt: (g // n_chunks, 0))
    vec = lambda n: pl.BlockSpec((1, n), lambda g, pt: (0, 0))
    any_spec = pl.BlockSpec(memory_space=pl.ANY)
    grid_spec = pltpu.PrefetchScalarGridSpec(
        num_scalar_prefetch=1,
        grid=(n_steps,),
        in_specs=[tok, tok, tok, any_spec, any_spec,
                  vec(QK_DIM), vec(QK_DIM), vec(QK_DIM), vec(QK_DIM), vec(ATTN_WIDTH)],
        out_specs=tok,
        scratch_shapes=[pltpu.VMEM((2, chunk_keys, ATTN_WIDTH), F32),
                        pltpu.VMEM((2, chunk_keys, ATTN_WIDTH), F32),
                        pltpu.SemaphoreType.DMA((2,)),
                        pltpu.SemaphoreType.DMA((2,)),
                        pltpu.VMEM((rows, 1), F32),
                        pltpu.VMEM((rows, 1), F32),
                        pltpu.VMEM((rows, ATTN_WIDTH), F32)])
    return pl.pallas_call(
        functools.partial(_attn_sample_kernel, n_chunks=n_chunks, n_steps=n_steps,
                          dec_seq=dec_seq),
        grid_spec=grid_spec,
        out_shape=jax.ShapeDtypeStruct((dec_batch * dec_seq, ATTN_WIDTH), F32),
        compiler_params=_cparams(1),
        name="attn_sample",
    )(page_table.reshape(-1), q, k_new, v_new, cache_k2d, cache_v2d, *lam_params, subln_g4)


def _softplus(x):
    y = jnp.exp(-jnp.abs(x))
    w = 1.0 + y
    log1p = jnp.where(w == 1.0, y, jnp.log(w) * (y / (w - 1.0)))
    return jnp.maximum(x, 0.0) + log1p


def _rglru_kernel(xl_ref, gl_ref, conv0_ref, h0_ref, cw_ref, cb_ref, wg_ref, bg_ref, lam_ref,
                  y_ref, hlast_ref, tail_ref, hc_ref, *, ts):
    s = pl.program_id(1)

    @pl.when(s == 0)
    def _():
        tail_ref[...] = conv0_ref[0]
        hc_ref[...] = h0_ref[0]

    x = xl_ref[...]
    xfull = jnp.concatenate([tail_ref[...], x], axis=0)
    cw = cw_ref[...]
    xc = cb_ref[...] + cw[CONV_W - 1:CONV_W] * x
    for d in range(1, CONV_W):
        shifted = pltpu.roll(xfull, d, 0)[SUBLANES:]
        xc = xc + cw[CONV_W - 1 - d:CONV_W - d] * shifted
    tail_ref[...] = x[ts - SUBLANES:]

    gates = jnp.dot(xc.astype(BF16), wg_ref[...], preferred_element_type=F32) + bg_ref[...]
    r = jax.nn.sigmoid(gates[:, :LRU_WIDTH])
    ig = jax.nn.sigmoid(gates[:, LRU_WIDTH:])
    log_a = -LRU_C * r * _softplus(-lam_ref[...])
    a = jnp.exp(log_a)
    mult = jnp.sqrt(jnp.maximum(-(a * a + 1.0) * jnp.tanh(log_a), 0.0))
    u = mult * ig * xc

    row = lax.broadcasted_iota(I32, a.shape, 0)
    d = 1
    while d < ts:
        keep = row >= d
        a_prev = jnp.where(keep, pltpu.roll(a, d, 0), 1.0)
        u_prev = jnp.where(keep, pltpu.roll(u, d, 0), 0.0)
        u = a * u_prev + u
        a = a * a_prev
        d *= 2
    h = a * hc_ref[...] + u
    h_end = h[ts - 1:ts]
    hc_ref[...] = h_end
    hlast_ref[0] = h_end
    y_ref[...] = (h * jax.nn.gelu(gl_ref[...], approximate=True)).astype(y_ref.dtype)


def _rglru(xl, gl, conv0, h0, conv_w, conv_b, w_gates, b_gates, lam, *, batch, seq, ts):
    assert seq % ts == 0 and ts % SUBLANES == 0
    ns = seq // ts
    tile = pl.BlockSpec((ts, LRU_WIDTH), lambda b, s: (b * ns + s, 0))
    const = lambda shape: pl.BlockSpec(shape, lambda b, s: (0,) * len(shape))
    return pl.pallas_call(
        functools.partial(_rglru_kernel, ts=ts),
        grid=(batch, ns),
        in_specs=[tile, tile,
                  pl.BlockSpec((1, SUBLANES, LRU_WIDTH), lambda b, s: (b, 0, 0)),
                  pl.BlockSpec((1, 1, LRU_WIDTH), lambda b, s: (b, 0, 0)),
                  const((CONV_W, LRU_WIDTH)), const((1, LRU_WIDTH)),
                  const((LRU_WIDTH, 2 * LRU_WIDTH)), const((1, 2 * LRU_WIDTH)),
                  const((1, LRU_WIDTH))],
        out_specs=[tile, pl.BlockSpec((1, 1, LRU_WIDTH), lambda b, s: (b, 0, 0))],
        out_shape=[jax.ShapeDtypeStruct((batch * seq, LRU_WIDTH), BF16 if ts > SUBLANES else F32),
                   jax.ShapeDtypeStruct((batch, 1, LRU_WIDTH), F32)],
        scratch_shapes=[pltpu.VMEM((SUBLANES, LRU_WIDTH), F32),
                        pltpu.VMEM((1, LRU_WIDTH), F32)],
        compiler_params=_cparams(2),
        name="rglru_prompt" if ts > SUBLANES else "rglru_sample",
    )(xl, gl, conv0, h0, conv_w, conv_b, w_gates, b_gates, lam)


def _route(logits):
    lane = lax.broadcasted_iota(I32, logits.shape, 1).astype(F32)
    big = float(ROUTE_LANES)
    neg = -jnp.inf
    is_g = lane < N_GROUPS
    gmax = jnp.max(jnp.where(is_g, logits, neg), axis=-1, keepdims=True)
    gsum = jnp.sum(jnp.where(is_g, jnp.exp(logits - gmax), 0.0), axis=-1, keepdims=True)
    gw = 1.0 / gsum
    gid = jnp.min(jnp.where(is_g & (logits == gmax), lane, big), axis=-1, keepdims=True)
    lo = N_GROUPS + EXPERTS_PER_GROUP * gid
    in_grp = (lane >= lo) & (lane < lo + EXPERTS_PER_GROUP)
    v1 = jnp.max(jnp.where(in_grp, logits, neg), axis=-1, keepdims=True)
    i1 = jnp.min(jnp.where(in_grp & (logits == v1), lane, big), axis=-1, keepdims=True)
    rest = in_grp & (lane != i1)
    v2 = jnp.max(jnp.where(rest, logits, neg), axis=-1, keepdims=True)
    i2 = jnp.min(jnp.where(rest & (logits == v2), lane, big), axis=-1, keepdims=True)
    t = jnp.exp(v2 - v1)
    w1 = gw / (1.0 + t)
    w2 = gw * t / (1.0 + t)
    info = jnp.where(lane == 0, i1 - N_GROUPS,
                     jnp.where(lane == 1, i2 - N_GROUPS,
                               jnp.where(lane == 2, w1, jnp.where(lane == 3, w2, 0.0))))
    return info


def _merge_kernel(xp_ref, xs_ref, op_ref, os_ref, yp_ref, ys_ref, wo_ref, g_ref, b_ref,
                  wr_ref, br_ref, x1_ref, info_ref, *, n_prompt_tiles):
    i = pl.program_id(0)

    def body(x_ref, o_ref, y_ref):
        m = jnp.dot(o_ref[...].astype(BF16), wo_ref[:ATTN_WIDTH, :], preferred_element_type=F32)
        m = m + jnp.dot(y_ref[...].astype(BF16), wo_ref[ATTN_WIDTH:, :],
                        preferred_element_type=F32)
        x1 = _layer_norm(DEEPNORM_ALPHA * x_ref[...] + m, g_ref[...], b_ref[...])
        x1_ref[...] = x1
        logits = jnp.dot(x1, wr_ref[...], preferred_element_type=F32,
                         precision=lax.Precision.HIGHEST) + br_ref[...]
        info_ref[...] = _route(logits)

    @pl.when(i < n_prompt_tiles)
    def _():
        body(xp_ref, op_ref, yp_ref)

    @pl.when(i >= n_prompt_tiles)
    def _():
        body(xs_ref, os_ref, ys_ref)


def _merge(xp, xs, o_p, o_s, y_p, y_s, wo_bf16, g1, b1, wr, br):
    tm = TOK_TM
    tp, ts = xp.shape[0], xs.shape[0]
    assert tp % tm == 0 and ts % tm == 0
    npt, nst = tp // tm, ts // tm
    t_all = tp + ts
    pidx = lambda i: (jnp.minimum(i, npt - 1), 0)
    sidx = lambda i: (jnp.maximum(i - npt, 0), 0)
    const = lambda shape: pl.BlockSpec(shape, lambda i: (0, 0))
    row = lambda i: (i, 0)
    return pl.pallas_call(
        functools.partial(_merge_kernel, n_prompt_tiles=npt),
        grid=(npt + nst,),
        in_specs=[pl.BlockSpec((tm, D_MODEL), pidx), pl.BlockSpec((tm, D_MODEL), sidx),
                  pl.BlockSpec((tm, ATTN_WIDTH), pidx), pl.BlockSpec((tm, ATTN_WIDTH), sidx),
                  pl.BlockSpec((tm, LRU_WIDTH), pidx), pl.BlockSpec((tm, LRU_WIDTH), sidx),
                  const((D_MODEL, D_MODEL)), const((1, D_MODEL)), const((1, D_MODEL)),
                  const((D_MODEL, ROUTE_LANES)), const((1, ROUTE_LANES))],
        out_specs=[pl.BlockSpec((tm, D_MODEL), row), pl.BlockSpec((tm, ROUTE_LANES), row)],
        out_shape=[jax.ShapeDtypeStruct((t_all, D_MODEL), F32),
                   jax.ShapeDtypeStruct((t_all, ROUTE_LANES), F32)],
        compiler_params=_cparams(1),
        name="merge_ln1_route",
    )(xp, xs, o_p, o_s, y_p, y_s, wo_bf16, g1, b1, wr, br)


def _expert_kernel(te_ref, tr_ref, src_ref, srcn_ref, dst_ref,
                   x_hbm, wg_ref, wu_ref, wd_ref, out_hbm,
                   xbuf, ybuf, gsem, ssem, *, n_tiles):
    tm = TOK_TM
    i = pl.program_id(0)
    slot = i % 2
    rows_here = tr_ref[i]

    def gather_row(idx_ref, r, sl):
        return pltpu.make_async_copy(x_hbm.at[pl.ds(idx_ref[0, 0, r], 1)],
                                     xbuf.at[sl, pl.ds(r, 1)], gsem.at[sl])

    def scatter_row(r, sl):
        return pltpu.make_async_copy(ybuf.at[sl, pl.ds(r, 1)],
                                     out_hbm.at[pl.ds(dst_ref[0, 0, r], 1)], ssem.at[sl])

    def start_gather(idx_ref, sl):
        def body(r, c):
            gather_row(idx_ref, r, sl).start()
            return c
        lax.fori_loop(0, tm, body, 0, unroll=8)

    def wait_scatter(n, sl):
        def body(r, c):
            pltpu.make_async_copy(ybuf.at[sl, pl.ds(0, 1)], out_hbm.at[pl.ds(0, 1)],
                                  ssem.at[sl]).wait()
            return c
        lax.fori_loop(0, n, body, 0)

    @pl.when((i == 0) & (rows_here > 0))
    def _():
        start_gather(src_ref, 0)

    @pl.when(i + 1 < n_tiles)
    def _():
        @pl.when(tr_ref[jnp.minimum(i + 1, n_tiles - 1)] > 0)
        def _():
            start_gather(srcn_ref, 1 - slot)

    @pl.when(i >= 2)
    def _():
        wait_scatter(tr_ref[jnp.maximum(i - 2, 0)], slot)

    @pl.when(rows_here > 0)
    def _():
        def wbody(r, c):
            pltpu.make_async_copy(x_hbm.at[pl.ds(0, 1)], xbuf.at[slot, pl.ds(0, 1)],
                                  gsem.at[slot]).wait()
            return c
        lax.fori_loop(0, tm, wbody, 0, unroll=8)

        x = xbuf[slot].astype(BF16)
        gate = jnp.dot(x, wg_ref[0].astype(BF16), preferred_element_type=F32)
        up = jnp.dot(x, wu_ref[0].astype(BF16), preferred_element_type=F32)
        h = (gate * jax.nn.sigmoid(gate) * up).astype(BF16)
        ybuf[slot] = jnp.dot(h, wd_ref[0].astype(BF16), preferred_element_type=F32)

        def sbody(r, c):
            scatter_row(r, slot).start()
            return c
        lax.fori_loop(0, rows_here, sbody, 0)

    @pl.when(i == n_tiles - 1)
    def _():
        if n_tiles > 1:
            wait_scatter(tr_ref[jnp.maximum(i - 1, 0)], 1 - slot)
        wait_scatter(rows_here, slot)


def _experts(x1, tile_expert, tile_rows, src_tiles, dst_tiles, wg, wu, wd, *, n_pairs):
    tm = TOK_TM
    n_tiles = tile_expert.shape[0]
    idx_cur = pl.BlockSpec((1, 1, tm), lambda i, te, tr: (i, 0, 0), memory_space=pltpu.SMEM)
    idx_next = pl.BlockSpec((1, 1, tm), lambda i, te, tr: (jnp.minimum(i + 1, n_tiles - 1), 0, 0),
                            memory_space=pltpu.SMEM)
    any_spec = pl.BlockSpec(memory_space=pl.ANY)
    grid_spec = pltpu.PrefetchScalarGridSpec(
        num_scalar_prefetch=2,
        grid=(n_tiles,),
        in_specs=[idx_cur, idx_next, idx_cur, any_spec,
                  pl.BlockSpec((1, D_MODEL, D_EXPERT), lambda i, te, tr: (te[i], 0, 0)),
                  pl.BlockSpec((1, D_MODEL, D_EXPERT), lambda i, te, tr: (te[i], 0, 0)),
                  pl.BlockSpec((1, D_EXPERT, D_MODEL), lambda i, te, tr: (te[i], 0, 0))],
        out_specs=any_spec,
        scratch_shapes=[pltpu.VMEM((2, tm, D_MODEL), F32),
                        pltpu.VMEM((2, tm, D_MODEL), F32),
                        pltpu.SemaphoreType.DMA((2,)),
                        pltpu.SemaphoreType.DMA((2,))])
    return pl.pallas_call(
        functools.partial(_expert_kernel, n_tiles=n_tiles),
        grid_spec=grid_spec,
        out_shape=jax.ShapeDtypeStruct((n_pairs, D_MODEL), F32),
        compiler_params=_cparams(1),
        name="expert_mlp",
    )(tile_expert, tile_rows, src_tiles, src_tiles, dst_tiles, x1, wg, wu, wd)


def _expert_plan(info, n_tiles):
    tm = TOK_TM
    t_all = info.shape[0]
    n_pairs = 2 * t_all
    ek = jnp.concatenate([info[:, 0], info[:, 1]]).astype(I32)
    sk, order = lax.sort((ek, lax.iota(I32, n_pairs)), num_keys=1, is_stable=True)
    start = jnp.searchsorted(sk, jnp.arange(N_EXPERTS + 1, dtype=I32), side="left").astype(I32)
    counts = start[1:] - start[:-1]
    tiles_e = (counts + tm - 1) // tm
    cum = jnp.cumsum(tiles_e)
    total = cum[-1]
    ti = jnp.arange(n_tiles, dtype=I32)
    e_i = jnp.minimum(jnp.searchsorted(cum, ti, side="right"), N_EXPERTS - 1).astype(I32)
    e_last = jnp.minimum(jnp.searchsorted(cum, total - 1, side="right"), N_EXPERTS - 1).astype(I32)
    j = ti - (cum - tiles_e)[e_i]
    live = ti < total
    tile_expert = jnp.where(live, e_i, e_last)
    tile_rows = jnp.where(live, jnp.clip(counts[e_i] - j * tm, 0, tm), 0).astype(I32)
    tile_start = jnp.where(live, start[e_i] + j * tm, 0)
    pos = jnp.minimum(tile_start[:, None] + jnp.arange(tm, dtype=I32)[None, :], n_pairs - 1)
    dst_tiles = order[pos]
    src_tiles = dst_tiles % t_all
    return (tile_expert, tile_rows, src_tiles.reshape(n_tiles, 1, tm),
            dst_tiles.reshape(n_tiles, 1, tm))


def _combine_kernel(x1_ref, ya_ref, yb_ref, info_ref, g_ref, b_ref, outp_ref, outs_ref,
                    *, n_prompt_tiles):
    i = pl.program_id(0)
    info = info_ref[...]
    w1 = info[:, 2:3]
    w2 = info[:, 3:4]
    f = w1 * ya_ref[...] + w2 * yb_ref[...]
    out = _layer_norm(DEEPNORM_ALPHA * x1_ref[...] + f, g_ref[...], b_ref[...])

    @pl.when(i < n_prompt_tiles)
    def _():
        outp_ref[...] = out

    @pl.when(i >= n_prompt_tiles)
    def _():
        outs_ref[...] = out


def _combine(x1, y_pairs, info, g2, b2, *, n_prompt, n_sample):
    tm = TOK_TM
    assert n_sample == tm
    npt = n_prompt // tm
    nt = npt + 1
    row = lambda i: (i, 0)
    const = lambda shape: pl.BlockSpec(shape, lambda i: (0, 0))
    return pl.pallas_call(
        functools.partial(_combine_kernel, n_prompt_tiles=npt),
        grid=(nt,),
        in_specs=[pl.BlockSpec((tm, D_MODEL), row),
                  pl.BlockSpec((tm, D_MODEL), row),
                  pl.BlockSpec((tm, D_MODEL), lambda i: (nt + i, 0)),
                  pl.BlockSpec((tm, ROUTE_LANES), row),
                  const((1, D_MODEL)), const((1, D_MODEL))],
        out_specs=[pl.BlockSpec((tm, D_MODEL), lambda i: (jnp.minimum(i, npt - 1), 0)),
                   pl.BlockSpec((tm, D_MODEL), lambda i: (0, 0))],
        out_shape=[jax.ShapeDtypeStruct((n_prompt, D_MODEL), F32),
                   jax.ShapeDtypeStruct((n_sample, D_MODEL), F32)],
        compiler_params=_cparams(1),
        name="combine_ln2",
    )(x1, y_pairs, y_pairs, info, g2, b2)


def _block_diag(w):
    nb, n, _ = w.shape
    eye = jnp.eye(nb, dtype=w.dtype)
    return (eye[:, None, :, None] * w[:, :, None, :]).reshape(nb * n, nb * n)


def kernel(x_prompt, x_sample, cache_k, cache_v, state_h, state_conv, page_table, w_in, lambda_q1, lambda_k1, lambda_q2, lambda_k2, subln_g, conv_w, conv_b, w_rg_a, b_rg_a, w_rg_x, b_rg_x, lru_lambda, w_out, ln1_g, ln1_b, w_group_router, b_group_router, w_expert_router, b_expert_router, w_gate_e, w_up_e, w_down_e, ln2_g, ln2_b):
    bp, sp, _ = x_prompt.shape
    bd, sd, _ = x_sample.shape
    assert w_in.shape[0] == DEPTH == 1
    l = 0
    n_prompt, n_sample = bp * sp, bd * sd
    t_all = n_prompt + n_sample

    xp = x_prompt.reshape(n_prompt, D_MODEL)
    xs = x_sample.reshape(n_sample, D_MODEL)
    w_in_b = w_in[l].astype(BF16)
    w_out_b = w_out[l].astype(BF16)
    lam_params = (lambda_q1[l][None], lambda_k1[l][None], lambda_q2[l][None], lambda_k2[l][None])
    w_gates = jnp.concatenate([_block_diag(w_rg_a[l]), _block_diag(w_rg_x[l])], axis=1).astype(BF16)
    b_gates = jnp.concatenate([b_rg_a[l].reshape(1, -1), b_rg_x[l].reshape(1, -1)], axis=1)
    lru_args = (conv_w[l], conv_b[l][None], w_gates, b_gates, lru_lambda[l][None])

    qm, k_p, v_p, xl_p, gl_p = _inproj(xp, w_in_b, masked_q=True, tm=INPROJ_TM)
    o_p = _attn_prompt(qm, k_p, v_p, lam_params, subln_g[l][None], batch=bp, seq=sp)
    y_p, h_p = _rglru(xl_p, gl_p, jnp.zeros((bp, SUBLANES, LRU_WIDTH), F32),
                      jnp.zeros((bp, 1, LRU_WIDTH), F32), *lru_args, batch=bp, seq=sp, ts=LRU_TS)

    q_s, k_s, v_s, xl_s, gl_s = _inproj(xs, w_in_b, masked_q=False, tm=n_sample)
    n_phys = cache_k.shape[1]
    ck = cache_k[l].reshape(n_phys * PAGE_SIZE, ATTN_WIDTH)
    cv = cache_v[l].reshape(n_phys * PAGE_SIZE, ATTN_WIDTH)
    o_s = _attn_sample(q_s, k_s, v_s, ck, cv, page_table, lam_params,
                       jnp.tile(subln_g[l][None], (1, N_HEADS)), dec_batch=bd, dec_seq=sd)
    conv0 = jnp.concatenate(
        [jnp.zeros((bd, SUBLANES - (CONV_W - 1), LRU_WIDTH), F32), state_conv[l]], axis=1)
    y_s, h_s = _rglru(xl_s, gl_s, conv0, state_h[l][:, None, :], *lru_args,
                      batch=bd, seq=sd, ts=sd)

    w_router = jnp.concatenate(
        [w_group_router[l], w_expert_router[l],
         jnp.zeros((D_MODEL, ROUTE_LANES - N_GROUPS - N_EXPERTS), F32)], axis=1)
    b_router = jnp.concatenate(
        [b_group_router[l], b_expert_router[l],
         jnp.zeros((ROUTE_LANES - N_GROUPS - N_EXPERTS,), F32)])[None]
    x1, info = _merge(xp, xs, o_p, o_s, y_p, y_s, w_out_b, ln1_g[l][None], ln1_b[l][None],
                      w_router, b_router)
    n_tiles = (2 * t_all) // TOK_TM + N_EXPERTS
    tile_expert, tile_rows, src_tiles, dst_tiles = _expert_plan(info, n_tiles)
    y_pairs = _experts(x1, tile_expert, tile_rows, src_tiles, dst_tiles,
                       w_gate_e[l], w_up_e[l], w_down_e[l], n_pairs=2 * t_all)
    out_p, out_s = _combine(x1, y_pairs, info, ln2_g[l][None], ln2_b[l][None],
                            n_prompt=n_prompt, n_sample=n_sample)

    conv_p = xl_p.reshape(bp, sp, LRU_WIDTH)[:, sp - (CONV_W - 1):]
    conv_s = jnp.concatenate([state_conv[l], xl_s.reshape(bd, sd, LRU_WIDTH)],
                             axis=1)[:, -(CONV_W - 1):]
    return (out_p.reshape(bp, sp, D_MODEL),
            out_s.reshape(bd, sd, D_MODEL),
            k_p.reshape(1, bp, sp, N_HEADS, V_DIM),
            v_p.reshape(1, bp, sp, N_HEADS, V_DIM),
            h_p.reshape(1, bp, LRU_WIDTH),
            conv_p[None],
            k_s.reshape(1, bd, sd, N_HEADS, V_DIM),
            v_s.reshape(1, bd, sd, N_HEADS, V_DIM),
            h_s.reshape(1, bd, LRU_WIDTH),
            conv_s[None])
```

```python
import functools
import math

import jax
import jax.numpy as jnp
from jax import lax
from jax.experimental import pallas as pl
from jax.experimental.pallas import tpu as pltpu

F32 = jnp.float32
BF16 = jnp.bfloat16
I32 = jnp.int32

D_MODEL = 1024
ATTN_WIDTH = 512
LRU_WIDTH = 512
N_HEADS = 4
V_DIM = 128
QK_DIM = 64
CONV_W = 4
LRU_C = 8.0
IN_WIDTH = 3 * ATTN_WIDTH + 2 * LRU_WIDTH
N_GROUPS = 4
EXPERTS_PER_GROUP = 8
N_EXPERTS = 32
D_EXPERT = 256
PAGE_SIZE = 128
LN_EPS = 1e-5
DEPTH = 1
DEEPNORM_ALPHA = (2 * DEPTH) ** 0.25
LAM_INIT = 0.8 - 0.6 * math.exp(-0.3 * 0)
QK_SCALE = QK_DIM ** -0.5 * math.log2(math.e)

SUBLANES = 8
LANES = 128
VMEM_LIMIT_BYTES = 48 * 1024 * 1024

INPROJ_TM = 512
ATTN_TQ = 512
ATTN_TK = 512
ATTN_CB = 256
LRU_TS = 512
TOK_TM = 256
PAGES_PER_STEP = 8
ROUTE_LANES = 128
TILE_ROWS = D_MODEL // LANES


def _cparams(n_axes):
    return pltpu.CompilerParams(
        dimension_semantics=("arbitrary",) * n_axes,
        vmem_limit_bytes=VMEM_LIMIT_BYTES)


def _layer_norm(x, g, b):
    mu = jnp.mean(x, axis=-1, keepdims=True)
    xc = x - mu
    var = jnp.mean(xc * xc, axis=-1, keepdims=True)
    return xc * lax.rsqrt(var + LN_EPS) * g + b


def _diff_lambda(lq1, lk1, lq2, lk2):
    s1 = jnp.sum(lq1 * lk1, axis=-1, keepdims=True)
    s2 = jnp.sum(lq2 * lk2, axis=-1, keepdims=True)
    return jnp.exp(s1) - jnp.exp(s2) + LAM_INIT


def _to_token_tiles(ref, x):
    tm = x.shape[0]
    for s in range(TILE_ROWS):
        ref[pl.ds(s, tm, stride=TILE_ROWS), :] = x[:, s * LANES:(s + 1) * LANES]


def _from_token_tiles(ref, tm):
    return jnp.concatenate(
        [ref[pl.ds(s, tm, stride=TILE_ROWS), :] for s in range(TILE_ROWS)], axis=1)


def _inproj_kernel(x_ref, w_ref, *out_refs, prompt):
    x = x_ref[...].astype(BF16)
    tm = x.shape[0]

    def section(idx, width):
        return jnp.dot(x, w_ref[:, idx:idx + width], preferred_element_type=F32)

    q = section(0, ATTN_WIDTH) * QK_SCALE
    k = section(ATTN_WIDTH, ATTN_WIDTH)
    v = section(2 * ATTN_WIDTH, ATTN_WIDTH)
    if prompt:
        qt_ref, kh_ref, vt_ref, k_ref, v_ref, xl_ref, gl_ref = out_refs
        chan = lax.broadcasted_iota(I32, (V_DIM, tm), 0)
        for h in range(N_HEADS):
            sl = slice(h * V_DIM, (h + 1) * V_DIM)
            qt = q[:, sl].T
            qt_ref[h, 0] = jnp.where(chan < QK_DIM, qt, 0.0).astype(BF16)
            qt_ref[h, 1] = jnp.where(chan >= QK_DIM, qt, 0.0).astype(BF16)
            kh_ref[h] = k[:, sl].astype(BF16)
            vt_ref[h, 0] = v[:, sl].T.astype(BF16)
    else:
        q_ref, k_ref, v_ref, xl_ref, gl_ref = out_refs
        q_ref[...] = q
    for h in range(N_HEADS):
        sl = slice(h * V_DIM, (h + 1) * V_DIM)
        k_ref[pl.ds(h, tm, stride=N_HEADS), :] = k[:, sl]
        v_ref[pl.ds(h, tm, stride=N_HEADS), :] = v[:, sl]
    xl_ref[...] = section(3 * ATTN_WIDTH, LRU_WIDTH)
    gl_ref[...] = section(3 * ATTN_WIDTH + LRU_WIDTH, LRU_WIDTH)


def _inproj(x2d, w_bf16, *, prompt, tm):
    t = x2d.shape[0]
    assert t % tm == 0
    row = lambda i: (i, 0)
    spec512 = pl.BlockSpec((tm, ATTN_WIDTH), row)
    head_rows = pl.BlockSpec((tm * N_HEADS, V_DIM), row)
    f32_512 = jax.ShapeDtypeStruct((t, ATTN_WIDTH), F32)
    f32_heads = jax.ShapeDtypeStruct((t * N_HEADS, V_DIM), F32)
    if prompt:
        assert tm == ATTN_TK
        out_shape = [jax.ShapeDtypeStruct((N_HEADS, 2, V_DIM, t), BF16),
                     jax.ShapeDtypeStruct((N_HEADS, t, V_DIM), BF16),
                     jax.ShapeDtypeStruct((N_HEADS, t // tm, V_DIM, tm), BF16)]
        out_specs = [pl.BlockSpec((N_HEADS, 2, V_DIM, tm), lambda i: (0, 0, 0, i)),
                     pl.BlockSpec((N_HEADS, tm, V_DIM), lambda i: (0, i, 0)),
                     pl.BlockSpec((N_HEADS, 1, V_DIM, tm), lambda i: (0, i, 0, 0))]
    else:
        out_shape = [f32_512]
        out_specs = [spec512]
    out_shape += [f32_heads, f32_heads, f32_512, f32_512]
    out_specs += [head_rows, head_rows, spec512, spec512]
    return pl.pallas_call(
        functools.partial(_inproj_kernel, prompt=prompt),
        grid=(t // tm,),
        in_specs=[pl.BlockSpec((tm, D_MODEL), row),
                  pl.BlockSpec((D_MODEL, IN_WIDTH), lambda i: (0, 0))],
        out_specs=out_specs,
        out_shape=out_shape,
        compiler_params=_cparams(1),
        name="inproj_prompt" if prompt else "inproj_sample",
    )(x2d, w_bf16)


def _attn_prompt_kernel(qt_ref, k_ref, vt_ref, lq1_ref, lk1_ref, lq2_ref, lk2_ref, g_ref,
                        o_ref, m_ref, l_ref, acc_ref):
    tq, tk, cb = ATTN_TQ, ATTN_TK, ATTN_CB
    n_cols = 2 * tq
    i = pl.program_id(2)

    m_ref[...] = jnp.full(m_ref.shape, -jnp.inf, F32)
    l_ref[...] = jnp.zeros(l_ref.shape, F32)
    acc_ref[...] = jnp.zeros(acc_ref.shape, F32)

    def chunk(j, masked):
        k = k_ref[0, pl.ds(pl.multiple_of(j * tk, tk), tk), :]
        vt = vt_ref[0, j]
        for c in range(n_cols // cb):
            mp, off = divmod(c * cb, tq)
            cs = slice(c * cb, (c + 1) * cb)
            qt = qt_ref[0, mp, :, off:off + cb]
            s = jnp.dot(k, qt, preferred_element_type=F32)
            if masked:
                key = lax.broadcasted_iota(I32, s.shape, 0)
                qpos = lax.broadcasted_iota(I32, s.shape, 1) + off
                s = jnp.where(key <= qpos, s, -jnp.inf)
            m_prev = m_ref[:, cs]
            m_new = jnp.maximum(m_prev, jnp.max(s, axis=0, keepdims=True))
            alpha = jnp.exp2(m_prev - m_new)
            p = jnp.exp2(s - m_new)
            l_ref[:, cs] = alpha * l_ref[:, cs] + jnp.sum(p, axis=0, keepdims=True)
            acc_ref[:, cs] = alpha * acc_ref[:, cs] + jnp.dot(
                vt, p.astype(BF16), preferred_element_type=F32)
            m_ref[:, cs] = m_new

    def body(j, carry):
        chunk(j, masked=False)
        return carry

    lax.fori_loop(0, i, body, 0)
    chunk(i, masked=True)

    lam = _diff_lambda(lq1_ref[...], lk1_ref[...], lq2_ref[...], lk2_ref[...])
    o_all = acc_ref[...] / l_ref[...]
    o = o_all[:, :tq] - lam * o_all[:, tq:]
    ms = jnp.mean(o * o, axis=0, keepdims=True)
    o = o * lax.rsqrt(ms + LN_EPS) * g_ref[...] * (1.0 - LAM_INIT)
    o_ref[...] = o.T.astype(o_ref.dtype)


def _attn_prompt(qt, kh, vt, lam_params, subln_g_col, *, batch, seq):
    assert ATTN_TQ == ATTN_TK and seq % ATTN_TQ == 0 and ATTN_TQ % ATTN_CB == 0
    nq = seq // ATTN_TQ
    t = batch * seq
    vec = lambda n: pl.BlockSpec((1, n), lambda b, h, i: (0, 0))
    return pl.pallas_call(
        _attn_prompt_kernel,
        grid=(batch, N_HEADS, nq),
        in_specs=[pl.BlockSpec((1, 2, V_DIM, ATTN_TQ), lambda b, h, i: (h, 0, 0, b * nq + i)),
                  pl.BlockSpec((1, seq, V_DIM), lambda b, h, i: (h, b, 0)),
                  pl.BlockSpec((1, nq, V_DIM, ATTN_TK), lambda b, h, i: (h, b, 0, 0)),
                  vec(QK_DIM), vec(QK_DIM), vec(QK_DIM), vec(QK_DIM),
                  pl.BlockSpec((V_DIM, 1), lambda b, h, i: (0, 0))],
        out_specs=pl.BlockSpec((ATTN_TQ, V_DIM), lambda b, h, i: (b * nq + i, h)),
        out_shape=jax.ShapeDtypeStruct((t, ATTN_WIDTH), BF16),
        scratch_shapes=[pltpu.VMEM((1, 2 * ATTN_TQ), F32),
                        pltpu.VMEM((1, 2 * ATTN_TQ), F32),
                        pltpu.VMEM((V_DIM, 2 * ATTN_TQ), F32)],
        compiler_params=_cparams(3),
        name="attn_prompt",
    )(qt, kh, vt, *lam_params, subln_g_col)


def _attn_sample_kernel(pt_ref, q_ref, kn_ref, vn_ref, ck_hbm, cv_hbm,
                        lq1_ref, lk1_ref, lq2_ref, lk2_ref, g_ref,
                        o_ref, kbuf, vbuf, ksem, vsem, m_ref, l_ref, acc_ref,
                        *, n_chunks, n_steps, dec_seq):
    gpp = PAGES_PER_STEP
    page_rows = PAGE_SIZE * N_HEADS
    chunk_keys = gpp * PAGE_SIZE
    hrows = 2 * dec_seq
    g = pl.program_id(0)
    c = g % n_chunks
    slot = g % 2

    def page_copies(step, sl):
        copies = []
        for p in range(gpp):
            page = pt_ref[step * gpp + p]
            src = pl.ds(pl.multiple_of(page * page_rows, page_rows), page_rows)
            dst = pl.ds(p * page_rows, page_rows)
            copies.append(pltpu.make_async_copy(ck_hbm.at[src], kbuf.at[sl, dst], ksem.at[sl]))
            copies.append(pltpu.make_async_copy(cv_hbm.at[src], vbuf.at[sl, dst], vsem.at[sl]))
        return copies

    @pl.when(g == 0)
    def _():
        for cp in page_copies(0, 0):
            cp.start()

    @pl.when(g + 1 < n_steps)
    def _():
        for cp in page_copies(g + 1, 1 - slot):
            cp.start()

    @pl.when(c == 0)
    def _():
        m_ref[...] = jnp.full(m_ref.shape, -jnp.inf, F32)
        l_ref[...] = jnp.zeros(l_ref.shape, F32)
        acc_ref[...] = jnp.zeros(acc_ref.shape, F32)

    q = q_ref[...]
    lane = lax.broadcasted_iota(I32, (dec_seq, V_DIM), 1)

    def head_q(h):
        qh = q[:, h * V_DIM:(h + 1) * V_DIM]
        return jnp.concatenate([jnp.where(lane < QK_DIM, qh, 0.0),
                                jnp.where(lane >= QK_DIM, qh, 0.0)], axis=0).astype(BF16)

    def update(h, k, v, mask):
        rs = slice(h * hrows, (h + 1) * hrows)
        s = lax.dot_general(head_q(h), k, (((1,), (1,)), ((), ())),
                            preferred_element_type=F32)
        if mask is not None:
            s = jnp.where(mask, s, -jnp.inf)
        m_prev = m_ref[rs]
        m_new = jnp.maximum(m_prev, jnp.max(s, axis=-1, keepdims=True))
        alpha = jnp.exp2(m_prev - m_new)
        p = jnp.exp2(s - m_new)
        l_ref[rs] = alpha * l_ref[rs] + jnp.sum(p, axis=-1, keepdims=True)
        acc_ref[rs] = alpha * acc_ref[rs] + jnp.dot(p.astype(BF16), v,
                                                    preferred_element_type=F32)
        m_ref[rs] = m_new

    for cp in page_copies(g, slot):
        cp.wait()
    for h in range(N_HEADS):
        rows = pl.ds(h, chunk_keys, stride=N_HEADS)
        update(h, kbuf[slot, rows, :].astype(BF16), vbuf[slot, rows, :].astype(BF16), None)

    @pl.when(c == n_chunks - 1)
    def _():
        pad = jnp.zeros((LANES - dec_seq, V_DIM), F32)
        qi = lax.broadcasted_iota(I32, (hrows, LANES), 0) % dec_seq
        kj = lax.broadcasted_iota(I32, (hrows, LANES), 1)
        lam = _diff_lambda(lq1_ref[...], lk1_ref[...], lq2_ref[...], lk2_ref[...])
        for h in range(N_HEADS):
            rows = pl.ds(h, dec_seq, stride=N_HEADS)
            kn = jnp.concatenate([kn_ref[rows, :], pad], axis=0).astype(BF16)
            vn = jnp.concatenate([vn_ref[rows, :], pad], axis=0).astype(BF16)
            update(h, kn, vn, kj <= qi)
            rs = slice(h * hrows, (h + 1) * hrows)
            o_all = acc_ref[rs] / l_ref[rs]
            o = o_all[:dec_seq] - lam * o_all[dec_seq:]
            ms = jnp.mean(o * o, axis=-1, keepdims=True)
            o_ref[:, h * V_DIM:(h + 1) * V_DIM] = (
                o * lax.rsqrt(ms + LN_EPS) * g_ref[...] * (1.0 - LAM_INIT))


def _attn_sample(q, k_new, v_new, cache_k_rows, cache_v_rows, page_table, lam_params, subln_g,
                 *, dec_batch, dec_seq):
    n_pages = page_table.shape[1]
    assert n_pages % PAGES_PER_STEP == 0 and dec_seq % SUBLANES == 0
    n_chunks = n_pages // PAGES_PER_STEP
    n_steps = dec_batch * n_chunks
    rows = 2 * N_HEADS * dec_seq
    chunk_rows = PAGES_PER_STEP * PAGE_SIZE * N_HEADS
    tok = pl.BlockSpec((dec_seq, ATTN_WIDTH), lambda g, pt: (g // n_chunks, 0))
    tok_heads = pl.BlockSpec((dec_seq * N_HEADS, V_DIM), lambda g, pt: (g // n_chunks, 0))
    vec = lambda n: pl.BlockSpec((1, n), lambda g, pt: (0, 0))
    any_spec = pl.BlockSpec(memory_space=pl.ANY)
    grid_spec = pltpu.PrefetchScalarGridSpec(
        num_scalar_prefetch=1,
        grid=(n_steps,),
        in_specs=[tok, tok_heads, tok_heads, any_spec, any_spec,
                  vec(QK_DIM), vec(QK_DIM), vec(QK_DIM), vec(QK_DIM), vec(V_DIM)],
        out_specs=tok,
        scratch_shapes=[pltpu.VMEM((2, chunk_rows, V_DIM), F32),
                        pltpu.VMEM((2, chunk_rows, V_DIM), F32),
                        pltpu.SemaphoreType.DMA((2,)),
                        pltpu.SemaphoreType.DMA((2,)),
                        pltpu.VMEM((rows, 1), F32),
                        pltpu.VMEM((rows, 1), F32),
                        pltpu.VMEM((rows, V_DIM), F32)])
    return pl.pallas_call(
        functools.partial(_attn_sample_kernel, n_chunks=n_chunks, n_steps=n_steps,
                          dec_seq=dec_seq),
        grid_spec=grid_spec,
        out_shape=jax.ShapeDtypeStruct((dec_batch * dec_seq, ATTN_WIDTH), F32),
        compiler_params=_cparams(1),
        name="attn_sample",
    )(page_table.reshape(-1), q, k_new, v_new, cache_k_rows, cache_v_rows, *lam_params, subln_g)


def _softplus(x):
    y = jnp.exp(-jnp.abs(x))
    w = 1.0 + y
    log1p = jnp.where(w == 1.0, y, jnp.log(w) * (y / (w - 1.0)))
    return jnp.maximum(x, 0.0) + log1p


def _rglru_kernel(xl_ref, gl_ref, conv0_ref, h0_ref, cw_ref, cb_ref, wg_ref, bg_ref, lam_ref,
                  y_ref, hlast_ref, tail_ref, hc_ref, *, ts):
    s = pl.program_id(1)

    @pl.when(s == 0)
    def _():
        tail_ref[...] = conv0_ref[0]
        hc_ref[...] = h0_ref[0]

    x = xl_ref[...]
    xfull = jnp.concatenate([tail_ref[...], x], axis=0)
    cw = cw_ref[...]
    xc = cb_ref[...] + cw[CONV_W - 1:CONV_W] * x
    for d in range(1, CONV_W):
        shifted = pltpu.roll(xfull, d, 0)[SUBLANES:]
        xc = xc + cw[CONV_W - 1 - d:CONV_W - d] * shifted
    tail_ref[...] = x[ts - SUBLANES:]

    gates = jnp.dot(xc.astype(BF16), wg_ref[...], preferred_element_type=F32) + bg_ref[...]
    r = jax.nn.sigmoid(gates[:, :LRU_WIDTH])
    ig = jax.nn.sigmoid(gates[:, LRU_WIDTH:])
    log_a = -LRU_C * r * _softplus(-lam_ref[...])
    a = jnp.exp(log_a)
    mult = jnp.sqrt(jnp.maximum(-(a * a + 1.0) * jnp.tanh(log_a), 0.0))
    u = mult * ig * xc

    row = lax.broadcasted_iota(I32, a.shape, 0)
    d = 1
    while d < ts:
        keep = row >= d
        a_prev = jnp.where(keep, pltpu.roll(a, d, 0), 1.0)
        u_prev = jnp.where(keep, pltpu.roll(u, d, 0), 0.0)
        u = a * u_prev + u
        a = a * a_prev
        d *= 2
    h = a * hc_ref[...] + u
    h_end = h[ts - 1:ts]
    hc_ref[...] = h_end
    hlast_ref[0] = h_end
    y_ref[...] = (h * jax.nn.gelu(gl_ref[...], approximate=True)).astype(y_ref.dtype)


def _rglru(xl, gl, conv0, h0, conv_w, conv_b, w_gates, b_gates, lam, *, batch, seq, ts):
    assert seq % ts == 0 and ts % SUBLANES == 0
    ns = seq // ts
    tile = pl.BlockSpec((ts, LRU_WIDTH), lambda b, s: (b * ns + s, 0))
    const = lambda shape: pl.BlockSpec(shape, lambda b, s: (0,) * len(shape))
    return pl.pallas_call(
        functools.partial(_rglru_kernel, ts=ts),
        grid=(batch, ns),
        in_specs=[tile, tile,
                  pl.BlockSpec((1, SUBLANES, LRU_WIDTH), lambda b, s: (b, 0, 0)),
                  pl.BlockSpec((1, 1, LRU_WIDTH), lambda b, s: (b, 0, 0)),
                  const((CONV_W, LRU_WIDTH)), const((1, LRU_WIDTH)),
                  const((LRU_WIDTH, 2 * LRU_WIDTH)), const((1, 2 * LRU_WIDTH)),
                  const((1, LRU_WIDTH))],
        out_specs=[tile, pl.BlockSpec((1, 1, LRU_WIDTH), lambda b, s: (b, 0, 0))],
        out_shape=[jax.ShapeDtypeStruct((batch * seq, LRU_WIDTH), BF16 if ts > SUBLANES else F32),
                   jax.ShapeDtypeStruct((batch, 1, LRU_WIDTH), F32)],
        scratch_shapes=[pltpu.VMEM((SUBLANES, LRU_WIDTH), F32),
                        pltpu.VMEM((1, LRU_WIDTH), F32)],
        compiler_params=_cparams(2),
        name="rglru_prompt" if ts > SUBLANES else "rglru_sample",
    )(xl, gl, conv0, h0, conv_w, conv_b, w_gates, b_gates, lam)


def _route(logits):
    lane = lax.broadcasted_iota(I32, logits.shape, 1).astype(F32)
    big = float(ROUTE_LANES)
    neg = -jnp.inf
    is_g = lane < N_GROUPS
    gmax = jnp.max(jnp.where(is_g, logits, neg), axis=-1, keepdims=True)
    gsum = jnp.sum(jnp.where(is_g, jnp.exp(logits - gmax), 0.0), axis=-1, keepdims=True)
    gw = 1.0 / gsum
    gid = jnp.min(jnp.where(is_g & (logits == gmax), lane, big), axis=-1, keepdims=True)
    lo = N_GROUPS + EXPERTS_PER_GROUP * gid
    in_grp = (lane >= lo) & (lane < lo + EXPERTS_PER_GROUP)
    v1 = jnp.max(jnp.where(in_grp, logits, neg), axis=-1, keepdims=True)
    i1 = jnp.min(jnp.where(in_grp & (logits == v1), lane, big), axis=-1, keepdims=True)
    rest = in_grp & (lane != i1)
    v2 = jnp.max(jnp.where(rest, logits, neg), axis=-1, keepdims=True)
    i2 = jnp.min(jnp.where(rest & (logits == v2), lane, big), axis=-1, keepdims=True)
    t = jnp.exp(v2 - v1)
    w1 = gw / (1.0 + t)
    w2 = gw * t / (1.0 + t)
    info = jnp.where(lane == 0, i1 - N_GROUPS,
                     jnp.where(lane == 1, i2 - N_GROUPS,
                               jnp.where(lane == 2, w1, jnp.where(lane == 3, w2, 0.0))))
    return info


def _merge_kernel(xp_ref, xs_ref, op_ref, os_ref, yp_ref, ys_ref, wo_ref, g_ref, b_ref,
                  wr_ref, br_ref, x1_ref, x1t_ref, info_ref, *, n_prompt_tiles):
    i = pl.program_id(0)

    def body(x_ref, o_ref, y_ref):
        m = jnp.dot(o_ref[...].astype(BF16), wo_ref[:ATTN_WIDTH, :], preferred_element_type=F32)
        m = m + jnp.dot(y_ref[...].astype(BF16), wo_ref[ATTN_WIDTH:, :],
                        preferred_element_type=F32)
        x1 = _layer_norm(DEEPNORM_ALPHA * x_ref[...] + m, g_ref[...], b_ref[...])
        x1_ref[...] = x1
        _to_token_tiles(x1t_ref, x1)
        logits = jnp.dot(x1, wr_ref[...], preferred_element_type=F32,
                         precision=lax.Precision.HIGHEST) + br_ref[...]
        info_ref[...] = _route(logits)

    @pl.when(i < n_prompt_tiles)
    def _():
        body(xp_ref, op_ref, yp_ref)

    @pl.when(i >= n_prompt_tiles)
    def _():
        body(xs_ref, os_ref, ys_ref)


def _merge(xp, xs, o_p, o_s, y_p, y_s, wo_bf16, g1, b1, wr, br):
    tm = TOK_TM
    tp, ts = xp.shape[0], xs.shape[0]
    assert tp % tm == 0 and ts % tm == 0
    npt, nst = tp // tm, ts // tm
    t_all = tp + ts
    pidx = lambda i: (jnp.minimum(i, npt - 1), 0)
    sidx = lambda i: (jnp.maximum(i - npt, 0), 0)
    const = lambda shape: pl.BlockSpec(shape, lambda i: (0, 0))
    row = lambda i: (i, 0)
    return pl.pallas_call(
        functools.partial(_merge_kernel, n_prompt_tiles=npt),
        grid=(npt + nst,),
        in_specs=[pl.BlockSpec((tm, D_MODEL), pidx), pl.BlockSpec((tm, D_MODEL), sidx),
                  pl.BlockSpec((tm, ATTN_WIDTH), pidx), pl.BlockSpec((tm, ATTN_WIDTH), sidx),
                  pl.BlockSpec((tm, LRU_WIDTH), pidx), pl.BlockSpec((tm, LRU_WIDTH), sidx),
                  const((D_MODEL, D_MODEL)), const((1, D_MODEL)), const((1, D_MODEL)),
                  const((D_MODEL, ROUTE_LANES)), const((1, ROUTE_LANES))],
        out_specs=[pl.BlockSpec((tm, D_MODEL), row),
                   pl.BlockSpec((tm * TILE_ROWS, LANES), row),
                   pl.BlockSpec((tm, ROUTE_LANES), row)],
        out_shape=[jax.ShapeDtypeStruct((t_all, D_MODEL), F32),
                   jax.ShapeDtypeStruct((t_all * TILE_ROWS, LANES), F32),
                   jax.ShapeDtypeStruct((t_all, ROUTE_LANES), F32)],
        compiler_params=_cparams(1),
        name="merge_ln1_route",
    )(xp, xs, o_p, o_s, y_p, y_s, wo_bf16, g1, b1, wr, br)


def _expert_kernel(te_ref, nlive_ref, src_ref, srcn_ref, dstp_ref,
                   x_hbm, wg_ref, wu_ref, wd_ref, out_hbm,
                   xbuf, ybuf, gsem, ssem):
    tm = TOK_TM
    tr = TILE_ROWS
    i = pl.program_id(0)
    slot = i % 2
    other = 1 - slot
    n_live = nlive_ref[0]

    def start_gather(idx_ref, sl):
        for r in range(tm):
            src = pl.ds(pl.multiple_of(idx_ref[0, 0, r], tr), tr)
            pltpu.make_async_copy(x_hbm.at[src], xbuf.at[sl, pl.ds(r * tr, tr)],
                                  gsem.at[sl]).start()

    def start_scatter(idx_ref, sl):
        for r in range(tm):
            dst = pl.ds(pl.multiple_of(idx_ref[0, 0, r], tr), tr)
            pltpu.make_async_copy(ybuf.at[sl, pl.ds(r * tr, tr)], out_hbm.at[dst],
                                  ssem.at[sl]).start()

    def wait_tile(buf, sem):
        pltpu.make_async_copy(x_hbm.at[pl.ds(0, tm * tr)], buf, sem).wait()

    @pl.when(i == 0)
    def _():
        ybuf[1] = jnp.zeros(ybuf.shape[1:], F32)
        start_gather(src_ref, 0)

    @pl.when((i >= 1) & (i <= n_live))
    def _():
        wait_tile(ybuf.at[slot], ssem.at[slot])

    @pl.when(i <= n_live)
    def _():
        wait_tile(xbuf.at[slot], gsem.at[slot])
        start_gather(srcn_ref, other)
        start_scatter(dstp_ref, other)

        x = _from_token_tiles(xbuf.at[slot], tm).astype(BF16)
        gate = jnp.dot(x, wg_ref[0].astype(BF16), preferred_element_type=F32)
        up = jnp.dot(x, wu_ref[0].astype(BF16), preferred_element_type=F32)
        h = (gate * jax.nn.sigmoid(gate) * up).astype(BF16)
        y = jnp.dot(h, wd_ref[0].astype(BF16), preferred_element_type=F32)
        _to_token_tiles(ybuf.at[slot], y)

    @pl.when(i == n_live)
    def _():
        wait_tile(xbuf.at[other], gsem.at[other])
        wait_tile(ybuf.at[other], ssem.at[other])


def _experts(x1t, tile_expert, n_live, src_tiles, dst_prev_tiles, wg, wu, wd, *, n_rows):
    tm = TOK_TM
    n_tiles = tile_expert.shape[0]
    last = n_tiles - 1
    smem_tile = lambda fn: pl.BlockSpec((1, 1, tm), fn, memory_space=pltpu.SMEM)
    weight = lambda shape: pl.BlockSpec(
        shape, lambda i, te, nl: (te[jnp.minimum(i, last)], 0, 0))
    any_spec = pl.BlockSpec(memory_space=pl.ANY)
    grid_spec = pltpu.PrefetchScalarGridSpec(
        num_scalar_prefetch=2,
        grid=(n_tiles + 1,),
        in_specs=[smem_tile(lambda i, te, nl: (jnp.minimum(i, last), 0, 0)),
                  smem_tile(lambda i, te, nl: (jnp.minimum(i + 1, last), 0, 0)),
                  smem_tile(lambda i, te, nl: (i, 0, 0)),
                  any_spec,
                  weight((1, D_MODEL, D_EXPERT)), weight((1, D_MODEL, D_EXPERT)),
                  weight((1, D_EXPERT, D_MODEL))],
        out_specs=any_spec,
        scratch_shapes=[pltpu.VMEM((2, tm * TILE_ROWS, LANES), F32),
                        pltpu.VMEM((2, tm * TILE_ROWS, LANES), F32),
                        pltpu.SemaphoreType.DMA((2,)),
                        pltpu.SemaphoreType.DMA((2,))])
    return pl.pallas_call(
        _expert_kernel,
        grid_spec=grid_spec,
        out_shape=jax.ShapeDtypeStruct((n_rows * TILE_ROWS, LANES), F32),
        compiler_params=_cparams(1),
        name="expert_mlp",
    )(tile_expert, n_live, src_tiles, src_tiles, dst_prev_tiles, x1t, wg, wu, wd)


def _expert_plan(info, n_tiles):
    tm = TOK_TM
    t_all = info.shape[0]
    n_pairs = 2 * t_all
    ek = jnp.concatenate([info[:, 0], info[:, 1]]).astype(I32)
    _, order = lax.sort((ek, lax.iota(I32, n_pairs)), num_keys=1, is_stable=True)
    experts = jnp.arange(N_EXPERTS, dtype=I32)
    counts = jnp.sum((ek[:, None] == experts[None, :]).astype(I32), axis=0)
    start = jnp.cumsum(counts) - counts
    tiles_e = (counts + tm - 1) // tm
    cum = jnp.cumsum(tiles_e)
    n_live = cum[-1]
    ti = jnp.arange(n_tiles, dtype=I32)
    expert_of = lambda t: jnp.minimum(
        jnp.sum((cum[None, :] <= t[:, None]).astype(I32), axis=1), N_EXPERTS - 1)
    e_i = expert_of(ti)
    e_last = expert_of(n_live[None] - 1)[0]
    j = ti - (cum - tiles_e)[e_i]
    live = ti < n_live
    tile_expert = jnp.where(live, e_i, e_last).astype(I32)
    tile_rows = jnp.where(live, jnp.clip(counts[e_i] - j * tm, 0, tm), 0)
    tile_start = jnp.where(live, start[e_i] + j * tm, 0)
    r = jnp.arange(tm, dtype=I32)[None, :]
    valid = r < tile_rows[:, None]
    pair = order[jnp.minimum(tile_start[:, None] + r, n_pairs - 1)]
    src = jnp.where(valid, pair % t_all, 0) * TILE_ROWS
    dump = (n_pairs + r) * TILE_ROWS
    dst = jnp.where(valid, pair, n_pairs + r) * TILE_ROWS
    dst_prev = jnp.concatenate([dump, dst], axis=0)
    return (tile_expert, n_live[None].astype(I32), src.reshape(n_tiles, 1, tm).astype(I32),
            dst_prev.reshape(n_tiles + 1, 1, tm).astype(I32))


def _combine_kernel(x1_ref, ya_ref, yb_ref, info_ref, g_ref, b_ref, outp_ref, outs_ref,
                    *, n_prompt_tiles):
    tm = TOK_TM
    i = pl.program_id(0)
    info = info_ref[...]
    w1 = info[:, 2:3]
    w2 = info[:, 3:4]
    f = w1 * _from_token_tiles(ya_ref, tm) + w2 * _from_token_tiles(yb_ref, tm)
    out = _layer_norm(DEEPNORM_ALPHA * x1_ref[...] + f, g_ref[...], b_ref[...])

    @pl.when(i < n_prompt_tiles)
    def _():
        outp_ref[...] = out

    @pl.when(i >= n_prompt_tiles)
    def _():
        outs_ref[...] = out


def _combine(x1, y_pairs, info, g2, b2, *, n_prompt, n_sample):
    tm = TOK_TM
    assert n_sample == tm
    npt = n_prompt // tm
    nt = npt + 1
    row = lambda i: (i, 0)
    const = lambda shape: pl.BlockSpec(shape, lambda i: (0, 0))
    return pl.pallas_call(
        functools.partial(_combine_kernel, n_prompt_tiles=npt),
        grid=(nt,),
        in_specs=[pl.BlockSpec((tm, D_MODEL), row),
                  pl.BlockSpec((tm * TILE_ROWS, LANES), row),
                  pl.BlockSpec((tm * TILE_ROWS, LANES), lambda i: (nt + i, 0)),
                  pl.BlockSpec((tm, ROUTE_LANES), row),
                  const((1, D_MODEL)), const((1, D_MODEL))],
        out_specs=[pl.BlockSpec((tm, D_MODEL), lambda i: (jnp.minimum(i, npt - 1), 0)),
                   pl.BlockSpec((tm, D_MODEL), lambda i: (0, 0))],
        out_shape=[jax.ShapeDtypeStruct((n_prompt, D_MODEL), F32),
                   jax.ShapeDtypeStruct((n_sample, D_MODEL), F32)],
        compiler_params=_cparams(1),
        name="combine_ln2",
    )(x1, y_pairs, y_pairs, info, g2, b2)


def _block_diag(w):
    nb, n, _ = w.shape
    eye = jnp.eye(nb, dtype=w.dtype)
    return (eye[:, None, :, None] * w[:, :, None, :]).reshape(nb * n, nb * n)


def kernel(x_prompt, x_sample, cache_k, cache_v, state_h, state_conv, page_table, w_in, lambda_q1, lambda_k1, lambda_q2, lambda_k2, subln_g, conv_w, conv_b, w_rg_a, b_rg_a, w_rg_x, b_rg_x, lru_lambda, w_out, ln1_g, ln1_b, w_group_router, b_group_router, w_expert_router, b_expert_router, w_gate_e, w_up_e, w_down_e, ln2_g, ln2_b):
    bp, sp, _ = x_prompt.shape
    bd, sd, _ = x_sample.shape
    assert w_in.shape[0] == DEPTH == 1
    l = 0
    n_prompt, n_sample = bp * sp, bd * sd
    t_all = n_prompt + n_sample

    xp = x_prompt.reshape(n_prompt, D_MODEL)
    xs = x_sample.reshape(n_sample, D_MODEL)
    w_in_b = w_in[l].astype(BF16)
    w_out_b = w_out[l].astype(BF16)
    lam_params = (lambda_q1[l][None], lambda_k1[l][None], lambda_q2[l][None], lambda_k2[l][None])
    w_gates = jnp.concatenate([_block_diag(w_rg_a[l]), _block_diag(w_rg_x[l])], axis=1).astype(BF16)
    b_gates = jnp.concatenate([b_rg_a[l].reshape(1, -1), b_rg_x[l].reshape(1, -1)], axis=1)
    lru_args = (conv_w[l], conv_b[l][None], w_gates, b_gates, lru_lambda[l][None])

    qt, kh, vt, k_p, v_p, xl_p, gl_p = _inproj(xp, w_in_b, prompt=True, tm=INPROJ_TM)
    o_p = _attn_prompt(qt, kh, vt, lam_params, subln_g[l][:, None], batch=bp, seq=sp)
    y_p, h_p = _rglru(xl_p, gl_p, jnp.zeros((bp, SUBLANES, LRU_WIDTH), F32),
                      jnp.zeros((bp, 1, LRU_WIDTH), F32), *lru_args, batch=bp, seq=sp, ts=LRU_TS)

    q_s, k_s, v_s, xl_s, gl_s = _inproj(xs, w_in_b, prompt=False, tm=n_sample)
    n_phys = cache_k.shape[1]
    ck = cache_k.reshape(DEPTH * n_phys * PAGE_SIZE * N_HEADS, V_DIM)
    cv = cache_v.reshape(DEPTH * n_phys * PAGE_SIZE * N_HEADS, V_DIM)
    o_s = _attn_sample(q_s, k_s, v_s, ck, cv, page_table, lam_params, subln_g[l][None],
                       dec_batch=bd, dec_seq=sd)
    conv0 = jnp.concatenate(
        [jnp.zeros((bd, SUBLANES - (CONV_W - 1), LRU_WIDTH), F32), state_conv[l]], axis=1)
    y_s, h_s = _rglru(xl_s, gl_s, conv0, state_h[l][:, None, :], *lru_args,
                      batch=bd, seq=sd, ts=sd)

    w_router = jnp.concatenate(
        [w_group_router[l], w_expert_router[l],
         jnp.zeros((D_MODEL, ROUTE_LANES - N_GROUPS - N_EXPERTS), F32)], axis=1)
    b_router = jnp.concatenate(
        [b_group_router[l], b_expert_router[l],
         jnp.zeros((ROUTE_LANES - N_GROUPS - N_EXPERTS,), F32)])[None]
    x1, x1t, info = _merge(xp, xs, o_p, o_s, y_p, y_s, w_out_b, ln1_g[l][None], ln1_b[l][None],
                           w_router, b_router)
    n_tiles = (2 * t_all) // TOK_TM + N_EXPERTS
    tile_expert, n_live, src_tiles, dst_prev_tiles = _expert_plan(info, n_tiles)
    y_pairs = _experts(x1t, tile_expert, n_live, src_tiles, dst_prev_tiles,
                       w_gate_e[l], w_up_e[l], w_down_e[l], n_rows=2 * t_all + TOK_TM)
    out_p, out_s = _combine(x1, y_pairs, info, ln2_g[l][None], ln2_b[l][None],
                            n_prompt=n_prompt, n_sample=n_sample)

    conv_p = xl_p.reshape(bp, sp, LRU_WIDTH)[:, sp - (CONV_W - 1):]
    conv_s = jnp.concatenate([state_conv[l], xl_s.reshape(bd, sd, LRU_WIDTH)],
                             axis=1)[:, -(CONV_W - 1):]
    return (out_p.reshape(bp, sp, D_MODEL),
            out_s.reshape(bd, sd, D_MODEL),
            k_p.reshape(1, bp, sp, N_HEADS, V_DIM),
            v_p.reshape(1, bp, sp, N_HEADS, V_DIM),
            h_p.reshape(1, bp, LRU_WIDTH),
            conv_p[None],
            k_s.reshape(1, bd, sd, N_HEADS, V_DIM),
            v_s.reshape(1, bd, sd, N_HEADS, V_DIM),
            h_s.reshape(1, bd, LRU_WIDTH),
            conv_s[None])
```

```python
import functools
import math

import jax
import jax.numpy as jnp
from jax import lax
from jax.experimental import pallas as pl
from jax.experimental.pallas import tpu as pltpu

F32 = jnp.float32
BF16 = jnp.bfloat16
I32 = jnp.int32

D_MODEL = 1024
ATTN_WIDTH = 512
LRU_WIDTH = 512
N_HEADS = 4
V_DIM = 128
QK_DIM = 64
CONV_W = 4
LRU_C = 8.0
IN_WIDTH = 3 * ATTN_WIDTH + 2 * LRU_WIDTH
N_GROUPS = 4
EXPERTS_PER_GROUP = 8
N_EXPERTS = 32
D_EXPERT = 256
PAGE_SIZE = 128
LN_EPS = 1e-5
DEPTH = 1
DEEPNORM_ALPHA = (2 * DEPTH) ** 0.25
LAM_INIT = 0.8 - 0.6 * math.exp(-0.3 * 0)
QK_SCALE = QK_DIM ** -0.5 * math.log2(math.e)

SUBLANES = 8
LANES = 128
VMEM_LIMIT_BYTES = 48 * 1024 * 1024

INPROJ_TM = 512
ATTN_TQ = 512
ATTN_TK = 512
ATTN_CB = 256
LRU_TS = 512
TOK_TM = 256
PAGES_PER_STEP = 8
ROUTE_LANES = 128
TILE_ROWS = D_MODEL // LANES


def _cparams(n_axes):
    return pltpu.CompilerParams(
        dimension_semantics=("arbitrary",) * n_axes,
        vmem_limit_bytes=VMEM_LIMIT_BYTES)


def _layer_norm(x, g, b):
    mu = jnp.mean(x, axis=-1, keepdims=True)
    xc = x - mu
    var = jnp.mean(xc * xc, axis=-1, keepdims=True)
    return xc * lax.rsqrt(var + LN_EPS) * g + b


def _diff_lambda(lq1, lk1, lq2, lk2):
    s1 = jnp.sum(lq1 * lk1, axis=-1, keepdims=True)
    s2 = jnp.sum(lq2 * lk2, axis=-1, keepdims=True)
    return jnp.exp(s1) - jnp.exp(s2) + LAM_INIT


def _to_token_tiles(ref, x):
    tm = x.shape[0]
    for s in range(TILE_ROWS):
        ref[pl.ds(s, tm, stride=TILE_ROWS), :] = x[:, s * LANES:(s + 1) * LANES]


def _from_token_tiles(ref, tm):
    return jnp.concatenate(
        [ref[pl.ds(s, tm, stride=TILE_ROWS), :] for s in range(TILE_ROWS)], axis=1)


def _inproj_kernel(x_ref, w_ref, *out_refs, prompt):
    x = x_ref[...].astype(BF16)
    tm = x.shape[0]

    def section(idx, width):
        return jnp.dot(x, w_ref[:, idx:idx + width], preferred_element_type=F32)

    q = section(0, ATTN_WIDTH) * QK_SCALE
    k = section(ATTN_WIDTH, ATTN_WIDTH)
    v = section(2 * ATTN_WIDTH, ATTN_WIDTH)
    if prompt:
        qt_ref, kh_ref, vt_ref, k_ref, v_ref, xl_ref, gl_ref = out_refs
        chan = lax.broadcasted_iota(I32, (V_DIM, tm), 0)
        for h in range(N_HEADS):
            sl = slice(h * V_DIM, (h + 1) * V_DIM)
            qt = q[:, sl].T
            qt_ref[h, 0] = jnp.where(chan < QK_DIM, qt, 0.0).astype(BF16)
            qt_ref[h, 1] = jnp.where(chan >= QK_DIM, qt, 0.0).astype(BF16)
            kh_ref[h] = k[:, sl].astype(BF16)
            vt_ref[h, 0] = v[:, sl].T.astype(BF16)
    else:
        q_ref, k_ref, v_ref, xl_ref, gl_ref = out_refs
        q_ref[...] = q
    for h in range(N_HEADS):
        sl = slice(h * V_DIM, (h + 1) * V_DIM)
        k_ref[pl.ds(h, tm, stride=N_HEADS), :] = k[:, sl]
        v_ref[pl.ds(h, tm, stride=N_HEADS), :] = v[:, sl]
    xl_ref[...] = section(3 * ATTN_WIDTH, LRU_WIDTH)
    gl_ref[...] = section(3 * ATTN_WIDTH + LRU_WIDTH, LRU_WIDTH)


def _inproj(x2d, w_bf16, *, prompt, tm):
    t = x2d.shape[0]
    assert t % tm == 0
    row = lambda i: (i, 0)
    spec512 = pl.BlockSpec((tm, ATTN_WIDTH), row)
    head_rows = pl.BlockSpec((tm * N_HEADS, V_DIM), row)
    f32_512 = jax.ShapeDtypeStruct((t, ATTN_WIDTH), F32)
    f32_heads = jax.ShapeDtypeStruct((t * N_HEADS, V_DIM), F32)
    if prompt:
        assert tm == ATTN_TK
        out_shape = [jax.ShapeDtypeStruct((N_HEADS, 2, V_DIM, t), BF16),
                     jax.ShapeDtypeStruct((N_HEADS, t, V_DIM), BF16),
                     jax.ShapeDtypeStruct((N_HEADS, t // tm, V_DIM, tm), BF16)]
        out_specs = [pl.BlockSpec((N_HEADS, 2, V_DIM, tm), lambda i: (0, 0, 0, i)),
                     pl.BlockSpec((N_HEADS, tm, V_DIM), lambda i: (0, i, 0)),
                     pl.BlockSpec((N_HEADS, 1, V_DIM, tm), lambda i: (0, i, 0, 0))]
    else:
        out_shape = [f32_512]
        out_specs = [spec512]
    out_shape += [f32_heads, f32_heads, f32_512, f32_512]
    out_specs += [head_rows, head_rows, spec512, spec512]
    return pl.pallas_call(
        functools.partial(_inproj_kernel, prompt=prompt),
        grid=(t // tm,),
        in_specs=[pl.BlockSpec((tm, D_MODEL), row),
                  pl.BlockSpec((D_MODEL, IN_WIDTH), lambda i: (0, 0))],
        out_specs=out_specs,
        out_shape=out_shape,
        compiler_params=_cparams(1),
        name="inproj_prompt" if prompt else "inproj_sample",
    )(x2d, w_bf16)


def _attn_prompt_kernel(qt_ref, k_ref, vt_ref, lq1_ref, lk1_ref, lq2_ref, lk2_ref, g_ref,
                        o_ref, s_ref, p_ref, a_ref, m_ref, l_ref, acc_ref):
    tq, tk, cb = ATTN_TQ, ATTN_TK, ATTN_CB
    n_cols = 2 * tq
    i = pl.program_id(2)

    m_ref[...] = jnp.full(m_ref.shape, -jnp.inf, F32)
    l_ref[...] = jnp.zeros(l_ref.shape, F32)
    acc_ref[...] = jnp.zeros(acc_ref.shape, F32)
    p_ref[1] = jnp.zeros(p_ref.shape[1:], BF16)
    a_ref[1] = jnp.ones(a_ref.shape[1:], F32)

    def scores(j, slot):
        k = k_ref[0, pl.ds(pl.multiple_of(j * tk, tk), tk), :]
        for mp in range(2):
            s_ref[slot, :, mp * tq:(mp + 1) * tq] = jnp.dot(
                k, qt_ref[0, mp], preferred_element_type=F32)

    def softmax(slot, masked):
        for c in range(n_cols // cb):
            off = (c * cb) % tq
            cs = slice(c * cb, (c + 1) * cb)
            s = s_ref[slot, :, cs]
            if masked:
                key = lax.broadcasted_iota(I32, s.shape, 0)
                qpos = lax.broadcasted_iota(I32, s.shape, 1) + off
                s = jnp.where(key <= qpos, s, -jnp.inf)
            m_prev = m_ref[:, cs]
            m_new = jnp.maximum(m_prev, jnp.max(s, axis=0, keepdims=True))
            alpha = jnp.exp2(m_prev - m_new)
            p = jnp.exp2(s - m_new)
            l_ref[:, cs] = alpha * l_ref[:, cs] + jnp.sum(p, axis=0, keepdims=True)
            m_ref[:, cs] = m_new
            p_ref[slot, :, cs] = p.astype(BF16)
            a_ref[slot, :, cs] = alpha

    def accumulate(j, slot):
        acc_ref[...] = a_ref[slot] * acc_ref[...] + jnp.dot(
            vt_ref[0, j], p_ref[slot], preferred_element_type=F32)

    scores(0, 0)

    def body(j, carry):
        cur = j % 2
        scores(j + 1, 1 - cur)
        softmax(cur, masked=False)
        accumulate(jnp.maximum(j - 1, 0), 1 - cur)
        return carry

    lax.fori_loop(0, i, body, 0)
    last = i % 2
    accumulate(jnp.maximum(i - 1, 0), 1 - last)
    softmax(last, masked=True)
    accumulate(i, last)

    lam = _diff_lambda(lq1_ref[...], lk1_ref[...], lq2_ref[...], lk2_ref[...])
    o_all = acc_ref[...] / l_ref[...]
    o = o_all[:, :tq] - lam * o_all[:, tq:]
    ms = jnp.mean(o * o, axis=0, keepdims=True)
    o = o * lax.rsqrt(ms + LN_EPS) * g_ref[...] * (1.0 - LAM_INIT)
    o_ref[...] = o.T.astype(o_ref.dtype)


def _attn_prompt(qt, kh, vt, lam_params, subln_g_col, *, batch, seq):
    assert ATTN_TQ == ATTN_TK and seq % ATTN_TQ == 0 and ATTN_TQ % ATTN_CB == 0
    nq = seq // ATTN_TQ
    t = batch * seq
    vec = lambda n: pl.BlockSpec((1, n), lambda b, h, i: (0, 0))
    return pl.pallas_call(
        _attn_prompt_kernel,
        grid=(batch, N_HEADS, nq),
        in_specs=[pl.BlockSpec((1, 2, V_DIM, ATTN_TQ), lambda b, h, i: (h, 0, 0, b * nq + i)),
                  pl.BlockSpec((1, seq, V_DIM), lambda b, h, i: (h, b, 0)),
                  pl.BlockSpec((1, nq, V_DIM, ATTN_TK), lambda b, h, i: (h, b, 0, 0)),
                  vec(QK_DIM), vec(QK_DIM), vec(QK_DIM), vec(QK_DIM),
                  pl.BlockSpec((V_DIM, 1), lambda b, h, i: (0, 0))],
        out_specs=pl.BlockSpec((ATTN_TQ, V_DIM), lambda b, h, i: (b * nq + i, h)),
        out_shape=jax.ShapeDtypeStruct((t, ATTN_WIDTH), BF16),
        scratch_shapes=[pltpu.VMEM((2, ATTN_TK, 2 * ATTN_TQ), F32),
                        pltpu.VMEM((2, ATTN_TK, 2 * ATTN_TQ), BF16),
                        pltpu.VMEM((2, 1, 2 * ATTN_TQ), F32),
                        pltpu.VMEM((1, 2 * ATTN_TQ), F32),
                        pltpu.VMEM((1, 2 * ATTN_TQ), F32),
                        pltpu.VMEM((V_DIM, 2 * ATTN_TQ), F32)],
        compiler_params=_cparams(3),
        name="attn_prompt",
    )(qt, kh, vt, *lam_params, subln_g_col)


def _attn_sample_kernel(pt_ref, q_ref, kn_ref, vn_ref, ck_hbm, cv_hbm,
                        lq1_ref, lk1_ref, lq2_ref, lk2_ref, g_ref,
                        o_ref, kbuf, vbuf, ksem, vsem, bias_ref, m_ref, l_ref, acc_ref,
                        *, n_chunks, n_steps, dec_seq):
    gpp = PAGES_PER_STEP
    page_rows = PAGE_SIZE * N_HEADS
    hrows = 2 * dec_seq
    g = pl.program_id(0)
    c = g % n_chunks
    slot = g % 2

    def page_copies(step, sl):
        copies = []
        for p in range(gpp):
            page = pt_ref[step * gpp + p]
            src = pl.ds(pl.multiple_of(page * page_rows, page_rows), page_rows)
            dst = pl.ds(p * page_rows, page_rows)
            copies.append(pltpu.make_async_copy(ck_hbm.at[src], kbuf.at[sl, dst], ksem.at[sl]))
            copies.append(pltpu.make_async_copy(cv_hbm.at[src], vbuf.at[sl, dst], vsem.at[sl]))
        return copies

    @pl.when(g == 0)
    def _():
        for cp in page_copies(0, 0):
            cp.start()

    @pl.when(g + 1 < n_steps)
    def _():
        for cp in page_copies(g + 1, 1 - slot):
            cp.start()

    def head_bias(shape):
        row_head = lax.broadcasted_iota(I32, shape, 0) // hrows
        col_head = lax.broadcasted_iota(I32, shape, 1) % N_HEADS
        return row_head == col_head

    @pl.when(g == 0)
    def _():
        bias_ref[...] = jnp.where(head_bias(bias_ref.shape), 0.0, -jnp.inf)

    @pl.when(c == 0)
    def _():
        m_ref[...] = jnp.full(m_ref.shape, -jnp.inf, F32)
        l_ref[...] = jnp.zeros(l_ref.shape, F32)
        acc_ref[...] = jnp.zeros(acc_ref.shape, F32)

    q = q_ref[...]
    lane = lax.broadcasted_iota(I32, (dec_seq, V_DIM), 1)
    q_rows = []
    for h in range(N_HEADS):
        qh = q[:, h * V_DIM:(h + 1) * V_DIM]
        q_rows += [jnp.where(lane < QK_DIM, qh, 0.0), jnp.where(lane >= QK_DIM, qh, 0.0)]
    qblk = jnp.concatenate(q_rows, axis=0).astype(BF16)

    def update(k, v, bias):
        s = lax.dot_general(qblk, k, (((1,), (1,)), ((), ())),
                            preferred_element_type=F32) + bias
        m_prev = m_ref[...]
        m_new = jnp.maximum(m_prev, jnp.max(s, axis=-1, keepdims=True))
        alpha = jnp.exp2(m_prev - m_new)
        p = jnp.exp2(s - m_new)
        l_ref[...] = alpha * l_ref[...] + jnp.sum(p, axis=-1, keepdims=True)
        acc_ref[...] = alpha * acc_ref[...] + jnp.dot(p.astype(BF16), v,
                                                      preferred_element_type=F32)
        m_ref[...] = m_new

    for cp in page_copies(g, slot):
        cp.wait()
    update(kbuf[slot].astype(BF16), vbuf[slot].astype(BF16), bias_ref[...])

    @pl.when(c == n_chunks - 1)
    def _():
        new_rows = dec_seq * N_HEADS
        pad = jnp.zeros((LANES - new_rows, V_DIM), F32)
        kn = jnp.concatenate([kn_ref[...], pad], axis=0).astype(BF16)
        vn = jnp.concatenate([vn_ref[...], pad], axis=0).astype(BF16)
        shape = (N_HEADS * hrows, LANES)
        qi = lax.broadcasted_iota(I32, shape, 0) % dec_seq
        kj = lax.broadcasted_iota(I32, shape, 1) // N_HEADS
        update(kn, vn, jnp.where(head_bias(shape) & (kj <= qi), 0.0, -jnp.inf))

        lam = _diff_lambda(lq1_ref[...], lk1_ref[...], lq2_ref[...], lk2_ref[...])
        o_heads = acc_ref[...] / l_ref[...]
        for h in range(N_HEADS):
            o_all = o_heads[h * hrows:(h + 1) * hrows]
            o = o_all[:dec_seq] - lam * o_all[dec_seq:]
            ms = jnp.mean(o * o, axis=-1, keepdims=True)
            o_ref[:, h * V_DIM:(h + 1) * V_DIM] = (
                o * lax.rsqrt(ms + LN_EPS) * g_ref[...] * (1.0 - LAM_INIT))


def _attn_sample(q, k_new, v_new, cache_k_rows, cache_v_rows, page_table, lam_params, subln_g,
                 *, dec_batch, dec_seq):
    n_pages = page_table.shape[1]
    assert n_pages % PAGES_PER_STEP == 0 and dec_seq % SUBLANES == 0
    n_chunks = n_pages // PAGES_PER_STEP
    n_steps = dec_batch * n_chunks
    rows = 2 * N_HEADS * dec_seq
    chunk_rows = PAGES_PER_STEP * PAGE_SIZE * N_HEADS
    tok = pl.BlockSpec((dec_seq, ATTN_WIDTH), lambda g, pt: (g // n_chunks, 0))
    tok_heads = pl.BlockSpec((dec_seq * N_HEADS, V_DIM), lambda g, pt: (g // n_chunks, 0))
    vec = lambda n: pl.BlockSpec((1, n), lambda g, pt: (0, 0))
    any_spec = pl.BlockSpec(memory_space=pl.ANY)
    grid_spec = pltpu.PrefetchScalarGridSpec(
        num_scalar_prefetch=1,
        grid=(n_steps,),
        in_specs=[tok, tok_heads, tok_heads, any_spec, any_spec,
                  vec(QK_DIM), vec(QK_DIM), vec(QK_DIM), vec(QK_DIM), vec(V_DIM)],
        out_specs=tok,
        scratch_shapes=[pltpu.VMEM((2, chunk_rows, V_DIM), F32),
                        pltpu.VMEM((2, chunk_rows, V_DIM), F32),
                        pltpu.SemaphoreType.DMA((2,)),
                        pltpu.SemaphoreType.DMA((2,)),
                        pltpu.VMEM((rows, chunk_rows), F32),
                        pltpu.VMEM((rows, 1), F32),
                        pltpu.VMEM((rows, 1), F32),
                        pltpu.VMEM((rows, V_DIM), F32)])
    return pl.pallas_call(
        functools.partial(_attn_sample_kernel, n_chunks=n_chunks, n_steps=n_steps,
                          dec_seq=dec_seq),
        grid_spec=grid_spec,
        out_shape=jax.ShapeDtypeStruct((dec_batch * dec_seq, ATTN_WIDTH), F32),
        compiler_params=_cparams(1),
        name="attn_sample",
    )(page_table.reshape(-1), q, k_new, v_new, cache_k_rows, cache_v_rows, *lam_params, subln_g)


def _softplus(x):
    y = jnp.exp(-jnp.abs(x))
    w = 1.0 + y
    log1p = jnp.where(w == 1.0, y, jnp.log(w) * (y / (w - 1.0)))
    return jnp.maximum(x, 0.0) + log1p


def _rglru_kernel(xl_ref, gl_ref, conv0_ref, h0_ref, cw_ref, cb_ref, wg_ref, bg_ref, lam_ref,
                  y_ref, hlast_ref, tail_ref, hc_ref, *, ts):
    s = pl.program_id(1)

    @pl.when(s == 0)
    def _():
        tail_ref[...] = conv0_ref[0]
        hc_ref[...] = h0_ref[0]

    x = xl_ref[...]
    xfull = jnp.concatenate([tail_ref[...], x], axis=0)
    cw = cw_ref[...]
    xc = cb_ref[...] + cw[CONV_W - 1:CONV_W] * x
    for d in range(1, CONV_W):
        shifted = pltpu.roll(xfull, d, 0)[SUBLANES:]
        xc = xc + cw[CONV_W - 1 - d:CONV_W - d] * shifted
    tail_ref[...] = x[ts - SUBLANES:]

    gates = jnp.dot(xc.astype(BF16), wg_ref[...], preferred_element_type=F32) + bg_ref[...]
    r = jax.nn.sigmoid(gates[:, :LRU_WIDTH])
    ig = jax.nn.sigmoid(gates[:, LRU_WIDTH:])
    log_a = -LRU_C * r * _softplus(-lam_ref[...])
    a = jnp.exp(log_a)
    mult = jnp.sqrt(jnp.maximum(-(a * a + 1.0) * jnp.tanh(log_a), 0.0))
    u = mult * ig * xc

    row = lax.broadcasted_iota(I32, a.shape, 0)
    d = 1
    while d < ts:
        keep = row >= d
        a_prev = jnp.where(keep, pltpu.roll(a, d, 0), 1.0)
        u_prev = jnp.where(keep, pltpu.roll(u, d, 0), 0.0)
        u = a * u_prev + u
        a = a * a_prev
        d *= 2
    h = a * hc_ref[...] + u
    h_end = h[ts - 1:ts]
    hc_ref[...] = h_end
    hlast_ref[0] = h_end
    y_ref[...] = (h * jax.nn.gelu(gl_ref[...], approximate=True)).astype(y_ref.dtype)


def _rglru(xl, gl, conv0, h0, conv_w, conv_b, w_gates, b_gates, lam, *, batch, seq, ts):
    assert seq % ts == 0 and ts % SUBLANES == 0
    ns = seq // ts
    tile = pl.BlockSpec((ts, LRU_WIDTH), lambda b, s: (b * ns + s, 0))
    const = lambda shape: pl.BlockSpec(shape, lambda b, s: (0,) * len(shape))
    return pl.pallas_call(
        functools.partial(_rglru_kernel, ts=ts),
        grid=(batch, ns),
        in_specs=[tile, tile,
                  pl.BlockSpec((1, SUBLANES, LRU_WIDTH), lambda b, s: (b, 0, 0)),
                  pl.BlockSpec((1, 1, LRU_WIDTH), lambda b, s: (b, 0, 0)),
                  const((CONV_W, LRU_WIDTH)), const((1, LRU_WIDTH)),
                  const((LRU_WIDTH, 2 * LRU_WIDTH)), const((1, 2 * LRU_WIDTH)),
                  const((1, LRU_WIDTH))],
        out_specs=[tile, pl.BlockSpec((1, 1, LRU_WIDTH), lambda b, s: (b, 0, 0))],
        out_shape=[jax.ShapeDtypeStruct((batch * seq, LRU_WIDTH), BF16 if ts > SUBLANES else F32),
                   jax.ShapeDtypeStruct((batch, 1, LRU_WIDTH), F32)],
        scratch_shapes=[pltpu.VMEM((SUBLANES, LRU_WIDTH), F32),
                        pltpu.VMEM((1, LRU_WIDTH), F32)],
        compiler_params=_cparams(2),
        name="rglru_prompt" if ts > SUBLANES else "rglru_sample",
    )(xl, gl, conv0, h0, conv_w, conv_b, w_gates, b_gates, lam)


def _route(logits):
    lane = lax.broadcasted_iota(I32, logits.shape, 1).astype(F32)
    big = float(ROUTE_LANES)
    neg = -jnp.inf
    is_g = lane < N_GROUPS
    gmax = jnp.max(jnp.where(is_g, logits, neg), axis=-1, keepdims=True)
    gsum = jnp.sum(jnp.where(is_g, jnp.exp(logits - gmax), 0.0), axis=-1, keepdims=True)
    gw = 1.0 / gsum
    gid = jnp.min(jnp.where(is_g & (logits == gmax), lane, big), axis=-1, keepdims=True)
    lo = N_GROUPS + EXPERTS_PER_GROUP * gid
    in_grp = (lane >= lo) & (lane < lo + EXPERTS_PER_GROUP)
    v1 = jnp.max(jnp.where(in_grp, logits, neg), axis=-1, keepdims=True)
    i1 = jnp.min(jnp.where(in_grp & (logits == v1), lane, big), axis=-1, keepdims=True)
    rest = in_grp & (lane != i1)
    v2 = jnp.max(jnp.where(rest, logits, neg), axis=-1, keepdims=True)
    i2 = jnp.min(jnp.where(rest & (logits == v2), lane, big), axis=-1, keepdims=True)
    t = jnp.exp(v2 - v1)
    w1 = gw / (1.0 + t)
    w2 = gw * t / (1.0 + t)
    info = jnp.where(lane == 0, i1 - N_GROUPS,
                     jnp.where(lane == 1, i2 - N_GROUPS,
                               jnp.where(lane == 2, w1, jnp.where(lane == 3, w2, 0.0))))
    return info


def _merge_kernel(xp_ref, xs_ref, op_ref, os_ref, yp_ref, ys_ref, wo_ref, g_ref, b_ref,
                  wr_ref, br_ref, x1_ref, x1t_ref, info_ref, *, n_prompt_tiles):
    i = pl.program_id(0)

    def body(x_ref, o_ref, y_ref):
        m = jnp.dot(o_ref[...].astype(BF16), wo_ref[:ATTN_WIDTH, :], preferred_element_type=F32)
        m = m + jnp.dot(y_ref[...].astype(BF16), wo_ref[ATTN_WIDTH:, :],
                        preferred_element_type=F32)
        x1 = _layer_norm(DEEPNORM_ALPHA * x_ref[...] + m, g_ref[...], b_ref[...])
        x1_ref[...] = x1
        _to_token_tiles(x1t_ref, x1)
        logits = jnp.dot(x1.astype(BF16), wr_ref[...],
                         preferred_element_type=F32) + br_ref[...]
        info_ref[...] = _route(logits)

    @pl.when(i < n_prompt_tiles)
    def _():
        body(xp_ref, op_ref, yp_ref)

    @pl.when(i >= n_prompt_tiles)
    def _():
        body(xs_ref, os_ref, ys_ref)


def _merge(xp, xs, o_p, o_s, y_p, y_s, wo_bf16, g1, b1, wr, br):
    tm = TOK_TM
    tp, ts = xp.shape[0], xs.shape[0]
    assert tp % tm == 0 and ts % tm == 0
    npt, nst = tp // tm, ts // tm
    t_all = tp + ts
    pidx = lambda i: (jnp.minimum(i, npt - 1), 0)
    sidx = lambda i: (jnp.maximum(i - npt, 0), 0)
    const = lambda shape: pl.BlockSpec(shape, lambda i: (0, 0))
    row = lambda i: (i, 0)
    return pl.pallas_call(
        functools.partial(_merge_kernel, n_prompt_tiles=npt),
        grid=(npt + nst,),
        in_specs=[pl.BlockSpec((tm, D_MODEL), pidx), pl.BlockSpec((tm, D_MODEL), sidx),
                  pl.BlockSpec((tm, ATTN_WIDTH), pidx), pl.BlockSpec((tm, ATTN_WIDTH), sidx),
                  pl.BlockSpec((tm, LRU_WIDTH), pidx), pl.BlockSpec((tm, LRU_WIDTH), sidx),
                  const((D_MODEL, D_MODEL)), const((1, D_MODEL)), const((1, D_MODEL)),
                  const((D_MODEL, ROUTE_LANES)), const((1, ROUTE_LANES))],
        out_specs=[pl.BlockSpec((tm, D_MODEL), row),
                   pl.BlockSpec((tm * TILE_ROWS, LANES), row),
                   pl.BlockSpec((tm, ROUTE_LANES), row)],
        out_shape=[jax.ShapeDtypeStruct((t_all, D_MODEL), F32),
                   jax.ShapeDtypeStruct((t_all * TILE_ROWS, LANES), F32),
                   jax.ShapeDtypeStruct((t_all, ROUTE_LANES), F32)],
        compiler_params=_cparams(1),
        name="merge_ln1_route",
    )(xp, xs, o_p, o_s, y_p, y_s, wo_bf16, g1, b1, wr, br)


def _expert_kernel(te_ref, nlive_ref, src_ref, srcn_ref, dstp_ref,
                   x_hbm, wg_ref, wu_ref, wd_ref, out_hbm,
                   xbuf, ybuf, gsem, ssem):
    tm = TOK_TM
    tr = TILE_ROWS
    i = pl.program_id(0)
    slot = i % 2
    other = 1 - slot
    n_live = nlive_ref[0]

    def start_gather(idx_ref, sl):
        for r in range(tm):
            src = pl.ds(pl.multiple_of(idx_ref[0, 0, r], tr), tr)
            pltpu.make_async_copy(x_hbm.at[src], xbuf.at[sl, pl.ds(r * tr, tr)],
                                  gsem.at[sl]).start(priority=r % 2)

    def start_scatter(idx_ref, sl):
        for r in range(tm):
            dst = pl.ds(pl.multiple_of(idx_ref[0, 0, r], tr), tr)
            pltpu.make_async_copy(ybuf.at[sl, pl.ds(r * tr, tr)], out_hbm.at[dst],
                                  ssem.at[sl]).start(priority=r % 2)

    def wait_tile(buf, sem):
        pltpu.make_async_copy(x_hbm.at[pl.ds(0, tm * tr)], buf, sem).wait()

    @pl.when(i == 0)
    def _():
        ybuf[1] = jnp.zeros(ybuf.shape[1:], F32)
        start_gather(src_ref, 0)

    @pl.when((i >= 1) & (i <= n_live))
    def _():
        wait_tile(ybuf.at[slot], ssem.at[slot])

    @pl.when(i <= n_live)
    def _():
        wait_tile(xbuf.at[slot], gsem.at[slot])
        start_gather(srcn_ref, other)
        start_scatter(dstp_ref, other)

        x = _from_token_tiles(xbuf.at[slot], tm).astype(BF16)
        gate = jnp.dot(x, wg_ref[0].astype(BF16), preferred_element_type=F32)
        up = jnp.dot(x, wu_ref[0].astype(BF16), preferred_element_type=F32)
        h = (gate * jax.nn.sigmoid(gate) * up).astype(BF16)
        y = jnp.dot(h, wd_ref[0].astype(BF16), preferred_element_type=F32)
        _to_token_tiles(ybuf.at[slot], y)

    @pl.when(i == n_live)
    def _():
        wait_tile(xbuf.at[other], gsem.at[other])
        wait_tile(ybuf.at[other], ssem.at[other])


def _experts(x1t, tile_expert, n_live, src_tiles, dst_prev_tiles, wg, wu, wd, *, n_rows):
    tm = TOK_TM
    n_tiles = tile_expert.shape[0]
    last = n_tiles - 1
    smem_tile = lambda fn: pl.BlockSpec((1, 1, tm), fn, memory_space=pltpu.SMEM)
    weight = lambda shape: pl.BlockSpec(
        shape, lambda i, te, nl: (te[jnp.minimum(i, last)], 0, 0))
    any_spec = pl.BlockSpec(memory_space=pl.ANY)
    grid_spec = pltpu.PrefetchScalarGridSpec(
        num_scalar_prefetch=2,
        grid=(n_tiles + 1,),
        in_specs=[smem_tile(lambda i, te, nl: (jnp.minimum(i, last), 0, 0)),
                  smem_tile(lambda i, te, nl: (jnp.minimum(i + 1, last), 0, 0)),
                  smem_tile(lambda i, te, nl: (i, 0, 0)),
                  any_spec,
                  weight((1, D_MODEL, D_EXPERT)), weight((1, D_MODEL, D_EXPERT)),
                  weight((1, D_EXPERT, D_MODEL))],
        out_specs=any_spec,
        scratch_shapes=[pltpu.VMEM((2, tm * TILE_ROWS, LANES), F32),
                        pltpu.VMEM((2, tm * TILE_ROWS, LANES), F32),
                        pltpu.SemaphoreType.DMA((2,)),
                        pltpu.SemaphoreType.DMA((2,))])
    return pl.pallas_call(
        _expert_kernel,
        grid_spec=grid_spec,
        out_shape=jax.ShapeDtypeStruct((n_rows * TILE_ROWS, LANES), F32),
        compiler_params=_cparams(1),
        name="expert_mlp",
    )(tile_expert, n_live, src_tiles, src_tiles, dst_prev_tiles, x1t, wg, wu, wd)


def _expert_plan(info, n_tiles):
    tm = TOK_TM
    t_all = info.shape[0]
    n_pairs = 2 * t_all
    ek = jnp.concatenate([info[:, 0], info[:, 1]]).astype(I32)
    _, order = lax.sort((ek, lax.iota(I32, n_pairs)), num_keys=1, is_stable=True)
    experts = jnp.arange(N_EXPERTS, dtype=I32)
    counts = jnp.sum((ek[:, None] == experts[None, :]).astype(I32), axis=0)
    start = jnp.cumsum(counts) - counts
    tiles_e = (counts + tm - 1) // tm
    cum = jnp.cumsum(tiles_e)
    n_live = cum[-1]
    ti = jnp.arange(n_tiles, dtype=I32)
    expert_of = lambda t: jnp.minimum(
        jnp.sum((cum[None, :] <= t[:, None]).astype(I32), axis=1), N_EXPERTS - 1)
    e_i = expert_of(ti)
    e_last = expert_of(n_live[None] - 1)[0]
    j = ti - (cum - tiles_e)[e_i]
    live = ti < n_live
    tile_expert = jnp.where(live, e_i, e_last).astype(I32)
    tile_rows = jnp.where(live, jnp.clip(counts[e_i] - j * tm, 0, tm), 0)
    tile_start = jnp.where(live, start[e_i] + j * tm, 0)
    r = jnp.arange(tm, dtype=I32)[None, :]
    valid = r < tile_rows[:, None]
    pair = order[jnp.minimum(tile_start[:, None] + r, n_pairs - 1)]
    src = jnp.where(valid, pair % t_all, 0) * TILE_ROWS
    dump = (n_pairs + r) * TILE_ROWS
    dst = jnp.where(valid, pair, n_pairs + r) * TILE_ROWS
    dst_prev = jnp.concatenate([dump, dst], axis=0)
    return (tile_expert, n_live[None].astype(I32), src.reshape(n_tiles, 1, tm).astype(I32),
            dst_prev.reshape(n_tiles + 1, 1, tm).astype(I32))


def _combine_kernel(x1_ref, ya_ref, yb_ref, info_ref, g_ref, b_ref, outp_ref, outs_ref,
                    *, n_prompt_tiles):
    tm = TOK_TM
    i = pl.program_id(0)
    info = info_ref[...]
    w1 = info[:, 2:3]
    w2 = info[:, 3:4]
    f = w1 * _from_token_tiles(ya_ref, tm) + w2 * _from_token_tiles(yb_ref, tm)
    out = _layer_norm(DEEPNORM_ALPHA * x1_ref[...] + f, g_ref[...], b_ref[...])

    @pl.when(i < n_prompt_tiles)
    def _():
        outp_ref[...] = out

    @pl.when(i >= n_prompt_tiles)
    def _():
        outs_ref[...] = out


def _combine(x1, y_pairs, info, g2, b2, *, n_prompt, n_sample):
    tm = TOK_TM
    assert n_sample == tm
    npt = n_prompt // tm
    nt = npt + 1
    row = lambda i: (i, 0)
    const = lambda shape: pl.BlockSpec(shape, lambda i: (0, 0))
    return pl.pallas_call(
        functools.partial(_combine_kernel, n_prompt_tiles=npt),
        grid=(nt,),
        in_specs=[pl.BlockSpec((tm, D_MODEL), row),
                  pl.BlockSpec((tm * TILE_ROWS, LANES), row),
                  pl.BlockSpec((tm * TILE_ROWS, LANES), lambda i: (nt + i, 0)),
                  pl.BlockSpec((tm, ROUTE_LANES), row),
                  const((1, D_MODEL)), const((1, D_MODEL))],
        out_specs=[pl.BlockSpec((tm, D_MODEL), lambda i: (jnp.minimum(i, npt - 1), 0)),
                   pl.BlockSpec((tm, D_MODEL), lambda i: (0, 0))],
        out_shape=[jax.ShapeDtypeStruct((n_prompt, D_MODEL), F32),
                   jax.ShapeDtypeStruct((n_sample, D_MODEL), F32)],
        compiler_params=_cparams(1),
        name="combine_ln2",
    )(x1, y_pairs, y_pairs, info, g2, b2)


def _block_diag(w):
    nb, n, _ = w.shape
    eye = jnp.eye(nb, dtype=w.dtype)
    return (eye[:, None, :, None] * w[:, :, None, :]).reshape(nb * n, nb * n)


def kernel(x_prompt, x_sample, cache_k, cache_v, state_h, state_conv, page_table, w_in, lambda_q1, lambda_k1, lambda_q2, lambda_k2, subln_g, conv_w, conv_b, w_rg_a, b_rg_a, w_rg_x, b_rg_x, lru_lambda, w_out, ln1_g, ln1_b, w_group_router, b_group_router, w_expert_router, b_expert_router, w_gate_e, w_up_e, w_down_e, ln2_g, ln2_b):
    bp, sp, _ = x_prompt.shape
    bd, sd, _ = x_sample.shape
    assert w_in.shape[0] == DEPTH == 1
    l = 0
    n_prompt, n_sample = bp * sp, bd * sd
    t_all = n_prompt + n_sample

    xp = x_prompt.reshape(n_prompt, D_MODEL)
    xs = x_sample.reshape(n_sample, D_MODEL)
    w_in_b = w_in[l].astype(BF16)
    w_out_b = w_out[l].astype(BF16)
    lam_params = (lambda_q1[l][None], lambda_k1[l][None], lambda_q2[l][None], lambda_k2[l][None])
    w_gates = jnp.concatenate([_block_diag(w_rg_a[l]), _block_diag(w_rg_x[l])], axis=1).astype(BF16)
    b_gates = jnp.concatenate([b_rg_a[l].reshape(1, -1), b_rg_x[l].reshape(1, -1)], axis=1)
    lru_args = (conv_w[l], conv_b[l][None], w_gates, b_gates, lru_lambda[l][None])

    qt, kh, vt, k_p, v_p, xl_p, gl_p = _inproj(xp, w_in_b, prompt=True, tm=INPROJ_TM)
    o_p = _attn_prompt(qt, kh, vt, lam_params, subln_g[l][:, None], batch=bp, seq=sp)
    y_p, h_p = _rglru(xl_p, gl_p, jnp.zeros((bp, SUBLANES, LRU_WIDTH), F32),
                      jnp.zeros((bp, 1, LRU_WIDTH), F32), *lru_args, batch=bp, seq=sp, ts=LRU_TS)

    q_s, k_s, v_s, xl_s, gl_s = _inproj(xs, w_in_b, prompt=False, tm=n_sample)
    n_phys = cache_k.shape[1]
    ck = cache_k.reshape(DEPTH * n_phys * PAGE_SIZE * N_HEADS, V_DIM)
    cv = cache_v.reshape(DEPTH * n_phys * PAGE_SIZE * N_HEADS, V_DIM)
    o_s = _attn_sample(q_s, k_s, v_s, ck, cv, page_table, lam_params, subln_g[l][None],
                       dec_batch=bd, dec_seq=sd)
    conv0 = jnp.concatenate(
        [jnp.zeros((bd, SUBLANES - (CONV_W - 1), LRU_WIDTH), F32), state_conv[l]], axis=1)
    y_s, h_s = _rglru(xl_s, gl_s, conv0, state_h[l][:, None, :], *lru_args,
                      batch=bd, seq=sd, ts=sd)

    w_router = jnp.concatenate(
        [w_group_router[l], w_expert_router[l],
         jnp.zeros((D_MODEL, ROUTE_LANES - N_GROUPS - N_EXPERTS), F32)], axis=1).astype(BF16)
    b_router = jnp.concatenate(
        [b_group_router[l], b_expert_router[l],
         jnp.zeros((ROUTE_LANES - N_GROUPS - N_EXPERTS,), F32)])[None]
    x1, x1t, info = _merge(xp, xs, o_p, o_s, y_p, y_s, w_out_b, ln1_g[l][None], ln1_b[l][None],
                           w_router, b_router)
    n_tiles = (2 * t_all) // TOK_TM + N_EXPERTS
    tile_expert, n_live, src_tiles, dst_prev_tiles = _expert_plan(info, n_tiles)
    y_pairs = _experts(x1t, tile_expert, n_live, src_tiles, dst_prev_tiles,
                       w_gate_e[l], w_up_e[l], w_down_e[l], n_rows=2 * t_all + TOK_TM)
    out_p, out_s = _combine(x1, y_pairs, info, ln2_g[l][None], ln2_b[l][None],
                            n_prompt=n_prompt, n_sample=n_sample)

    conv_p = xl_p.reshape(bp, sp, LRU_WIDTH)[:, sp - (CONV_W - 1):]
    conv_s = jnp.concatenate([state_conv[l], xl_s.reshape(bd, sd, LRU_WIDTH)],
                             axis=1)[:, -(CONV_W - 1):]
    return (out_p.reshape(bp, sp, D_MODEL),
            out_s.reshape(bd, sd, D_MODEL),
            k_p.reshape(1, bp, sp, N_HEADS, V_DIM),
            v_p.reshape(1, bp, sp, N_HEADS, V_DIM),
            h_p.reshape(1, bp, LRU_WIDTH),
            conv_p[None],
            k_s.reshape(1, bd, sd, N_HEADS, V_DIM),
            v_s.reshape(1, bd, sd, N_HEADS, V_DIM),
            h_s.reshape(1, bd, LRU_WIDTH),
            conv_s[None])
```

```python
import functools
import math

import jax
import jax.numpy as jnp
from jax import lax
from jax.experimental import pallas as pl
from jax.experimental.pallas import tpu as pltpu

F32 = jnp.float32
BF16 = jnp.bfloat16
I32 = jnp.int32

D_MODEL = 1024
ATTN_WIDTH = 512
LRU_WIDTH = 512
N_HEADS = 4
V_DIM = 128
QK_DIM = 64
CONV_W = 4
LRU_C = 8.0
IN_WIDTH = 3 * ATTN_WIDTH + 2 * LRU_WIDTH
N_GROUPS = 4
EXPERTS_PER_GROUP = 8
N_EXPERTS = 32
D_EXPERT = 256
PAGE_SIZE = 128
LN_EPS = 1e-5
DEPTH = 1
DEEPNORM_ALPHA = (2 * DEPTH) ** 0.25
LAM_INIT = 0.8 - 0.6 * math.exp(-0.3 * 0)
QK_SCALE = QK_DIM ** -0.5 * math.log2(math.e)

SUBLANES = 8
LANES = 128
VMEM_LIMIT_BYTES = 48 * 1024 * 1024

INPROJ_TM = 512
ATTN_TQ = 512
ATTN_TK = 512
ATTN_CB = 256
LRU_TS = 512
TOK_TM = 256
PAGES_PER_STEP = 8
ROUTE_LANES = 128
TILE_ROWS = D_MODEL // LANES


def _cparams(n_axes):
    return pltpu.CompilerParams(
        dimension_semantics=("arbitrary",) * n_axes,
        vmem_limit_bytes=VMEM_LIMIT_BYTES)


def _layer_norm(x, g, b):
    mu = jnp.mean(x, axis=-1, keepdims=True)
    xc = x - mu
    var = jnp.mean(xc * xc, axis=-1, keepdims=True)
    return xc * lax.rsqrt(var + LN_EPS) * g + b


def _diff_lambda(lq1, lk1, lq2, lk2):
    s1 = jnp.sum(lq1 * lk1, axis=-1, keepdims=True)
    s2 = jnp.sum(lq2 * lk2, axis=-1, keepdims=True)
    return jnp.exp(s1) - jnp.exp(s2) + LAM_INIT


def _to_token_tiles(ref, x):
    tm = x.shape[0]
    for s in range(TILE_ROWS):
        ref[pl.ds(s, tm, stride=TILE_ROWS), :] = x[:, s * LANES:(s + 1) * LANES]


def _from_token_tiles(ref, tm):
    return jnp.concatenate(
        [ref[pl.ds(s, tm, stride=TILE_ROWS), :] for s in range(TILE_ROWS)], axis=1)


def _inproj_kernel(x_ref, w_ref, *out_refs, prompt):
    x = x_ref[...].astype(BF16)
    tm = x.shape[0]

    def section(idx, width):
        return jnp.dot(x, w_ref[:, idx:idx + width], preferred_element_type=F32)

    q = section(0, ATTN_WIDTH) * QK_SCALE
    k = section(ATTN_WIDTH, ATTN_WIDTH)
    v = section(2 * ATTN_WIDTH, ATTN_WIDTH)
    if prompt:
        qt_ref, kh_ref, vt_ref, k_ref, v_ref, xl_ref, gl_ref = out_refs
        chan = lax.broadcasted_iota(I32, (V_DIM, tm), 0)
        for h in range(N_HEADS):
            sl = slice(h * V_DIM, (h + 1) * V_DIM)
            qt = q[:, sl].T
            qt_ref[h, 0] = jnp.where(chan < QK_DIM, qt, 0.0).astype(BF16)
            qt_ref[h, 1] = jnp.where(chan >= QK_DIM, qt, 0.0).astype(BF16)
            kh_ref[h] = k[:, sl].astype(BF16)
            vt_ref[h, 0] = v[:, sl].T.astype(BF16)
    else:
        q_ref, k_ref, v_ref, xl_ref, gl_ref = out_refs
        q_ref[...] = q
    for h in range(N_HEADS):
        sl = slice(h * V_DIM, (h + 1) * V_DIM)
        k_ref[pl.ds(h, tm, stride=N_HEADS), :] = k[:, sl]
        v_ref[pl.ds(h, tm, stride=N_HEADS), :] = v[:, sl]
    xl_ref[...] = section(3 * ATTN_WIDTH, LRU_WIDTH)
    gl_ref[...] = section(3 * ATTN_WIDTH + LRU_WIDTH, LRU_WIDTH)


def _inproj(x2d, w_bf16, *, prompt, tm):
    t = x2d.shape[0]
    assert t % tm == 0
    row = lambda i: (i, 0)
    spec512 = pl.BlockSpec((tm, ATTN_WIDTH), row)
    head_rows = pl.BlockSpec((tm * N_HEADS, V_DIM), row)
    f32_512 = jax.ShapeDtypeStruct((t, ATTN_WIDTH), F32)
    f32_heads = jax.ShapeDtypeStruct((t * N_HEADS, V_DIM), F32)
    if prompt:
        assert tm == ATTN_TK
        out_shape = [jax.ShapeDtypeStruct((N_HEADS, 2, V_DIM, t), BF16),
                     jax.ShapeDtypeStruct((N_HEADS, t, V_DIM), BF16),
                     jax.ShapeDtypeStruct((N_HEADS, t // tm, V_DIM, tm), BF16)]
        out_specs = [pl.BlockSpec((N_HEADS, 2, V_DIM, tm), lambda i: (0, 0, 0, i)),
                     pl.BlockSpec((N_HEADS, tm, V_DIM), lambda i: (0, i, 0)),
                     pl.BlockSpec((N_HEADS, 1, V_DIM, tm), lambda i: (0, i, 0, 0))]
    else:
        out_shape = [f32_512]
        out_specs = [spec512]
    out_shape += [f32_heads, f32_heads, f32_512, f32_512]
    out_specs += [head_rows, head_rows, spec512, spec512]
    return pl.pallas_call(
        functools.partial(_inproj_kernel, prompt=prompt),
        grid=(t // tm,),
        in_specs=[pl.BlockSpec((tm, D_MODEL), row),
                  pl.BlockSpec((D_MODEL, IN_WIDTH), lambda i: (0, 0))],
        out_specs=out_specs,
        out_shape=out_shape,
        compiler_params=_cparams(1),
        name="inproj_prompt" if prompt else "inproj_sample",
    )(x2d, w_bf16)


def _attn_prompt_kernel(qt_ref, k_ref, vt_ref, lq1_ref, lk1_ref, lq2_ref, lk2_ref, g_ref,
                        o_ref, s_ref, p_ref, a_ref, m_ref, l_ref, acc_ref):
    tq, tk, cb = ATTN_TQ, ATTN_TK, ATTN_CB
    n_cols = 2 * tq
    i = pl.program_id(2)

    m_ref[...] = jnp.full(m_ref.shape, -jnp.inf, F32)
    l_ref[...] = jnp.zeros(l_ref.shape, F32)
    acc_ref[...] = jnp.zeros(acc_ref.shape, F32)
    p_ref[1] = jnp.zeros(p_ref.shape[1:], BF16)
    a_ref[1] = jnp.ones(a_ref.shape[1:], F32)

    def scores(j, slot):
        k = k_ref[0, pl.ds(pl.multiple_of(j * tk, tk), tk), :]
        for mp in range(2):
            s_ref[slot, :, mp * tq:(mp + 1) * tq] = jnp.dot(
                k, qt_ref[0, mp], preferred_element_type=F32)

    def softmax(slot, masked):
        for c in range(n_cols // cb):
            off = (c * cb) % tq
            cs = slice(c * cb, (c + 1) * cb)
            s = s_ref[slot, :, cs]
            if masked:
                key = lax.broadcasted_iota(I32, s.shape, 0)
                qpos = lax.broadcasted_iota(I32, s.shape, 1) + off
                s = jnp.where(key <= qpos, s, -jnp.inf)
            m_prev = m_ref[:, cs]
            m_new = jnp.maximum(m_prev, jnp.max(s, axis=0, keepdims=True))
            alpha = jnp.exp2(m_prev - m_new)
            p = jnp.exp2(s - m_new)
            l_ref[:, cs] = alpha * l_ref[:, cs] + jnp.sum(p, axis=0, keepdims=True)
            m_ref[:, cs] = m_new
            p_ref[slot, :, cs] = p.astype(BF16)
            a_ref[slot, :, cs] = alpha

    def accumulate(j, slot):
        acc_ref[...] = a_ref[slot] * acc_ref[...] + jnp.dot(
            vt_ref[0, j], p_ref[slot], preferred_element_type=F32)

    def stage(j, cur):
        scores(j + 1, 1 - cur)
        softmax(cur, masked=False)
        accumulate(jnp.maximum(j - 1, 0), 1 - cur)

    def diagonal(cur):
        accumulate(jnp.maximum(i - 1, 0), 1 - cur)
        softmax(cur, masked=True)
        accumulate(i, cur)

    scores(0, 0)

    def body(t, carry):
        stage(2 * t, 0)
        stage(2 * t + 1, 1)
        return carry

    lax.fori_loop(0, i // 2, body, 0)

    @pl.when(i % 2 == 0)
    def _():
        diagonal(0)

    @pl.when(i % 2 == 1)
    def _():
        stage(i - 1, 0)
        diagonal(1)

    lam = _diff_lambda(lq1_ref[...], lk1_ref[...], lq2_ref[...], lk2_ref[...])
    o_all = acc_ref[...] / l_ref[...]
    o = o_all[:, :tq] - lam * o_all[:, tq:]
    ms = jnp.mean(o * o, axis=0, keepdims=True)
    o = o * lax.rsqrt(ms + LN_EPS) * g_ref[...] * (1.0 - LAM_INIT)
    o_ref[...] = o.T.astype(o_ref.dtype)


def _attn_prompt(qt, kh, vt, lam_params, subln_g_col, *, batch, seq):
    assert ATTN_TQ == ATTN_TK and seq % ATTN_TQ == 0 and ATTN_TQ % ATTN_CB == 0
    nq = seq // ATTN_TQ
    t = batch * seq
    vec = lambda n: pl.BlockSpec((1, n), lambda b, h, i: (0, 0))
    return pl.pallas_call(
        _attn_prompt_kernel,
        grid=(batch, N_HEADS, nq),
        in_specs=[pl.BlockSpec((1, 2, V_DIM, ATTN_TQ), lambda b, h, i: (h, 0, 0, b * nq + i)),
                  pl.BlockSpec((1, seq, V_DIM), lambda b, h, i: (h, b, 0)),
                  pl.BlockSpec((1, nq, V_DIM, ATTN_TK), lambda b, h, i: (h, b, 0, 0)),
                  vec(QK_DIM), vec(QK_DIM), vec(QK_DIM), vec(QK_DIM),
                  pl.BlockSpec((V_DIM, 1), lambda b, h, i: (0, 0))],
        out_specs=pl.BlockSpec((ATTN_TQ, V_DIM), lambda b, h, i: (b * nq + i, h)),
        out_shape=jax.ShapeDtypeStruct((t, ATTN_WIDTH), BF16),
        scratch_shapes=[pltpu.VMEM((2, ATTN_TK, 2 * ATTN_TQ), F32),
                        pltpu.VMEM((2, ATTN_TK, 2 * ATTN_TQ), BF16),
                        pltpu.VMEM((2, 1, 2 * ATTN_TQ), F32),
                        pltpu.VMEM((1, 2 * ATTN_TQ), F32),
                        pltpu.VMEM((1, 2 * ATTN_TQ), F32),
                        pltpu.VMEM((V_DIM, 2 * ATTN_TQ), F32)],
        compiler_params=_cparams(3),
        name="attn_prompt",
    )(qt, kh, vt, *lam_params, subln_g_col)


def _attn_sample_kernel(pt_ref, q0_ref, q1_ref, kn0_ref, kn1_ref, vn0_ref, vn1_ref,
                        ck_hbm, cv_hbm, lq1_ref, lk1_ref, lq2_ref, lk2_ref, g_ref,
                        o0_ref, o1_ref, kbuf, vbuf, ksem, vsem, bias_ref, m_ref, l_ref, acc_ref,
                        *, n_chunks, n_steps, dec_seq):
    gpp = PAGES_PER_STEP
    page_rows = PAGE_SIZE * N_HEADS
    hrows = 2 * dec_seq
    g = pl.program_id(0)
    c = g % n_chunks
    slot = g % 2
    streams = ((q0_ref, kn0_ref, vn0_ref, o0_ref), (q1_ref, kn1_ref, vn1_ref, o1_ref))

    def page_copies(step, sl):
        copies = []
        for s in range(len(streams)):
            buf = 2 * s + sl
            for p in range(gpp):
                page = pt_ref[(s * n_steps + step) * gpp + p]
                src = pl.ds(pl.multiple_of(page * page_rows, page_rows), page_rows)
                dst = pl.ds(p * page_rows, page_rows)
                copies.append(pltpu.make_async_copy(ck_hbm.at[src], kbuf.at[buf, dst],
                                                    ksem.at[buf]))
                copies.append(pltpu.make_async_copy(cv_hbm.at[src], vbuf.at[buf, dst],
                                                    vsem.at[buf]))
        return copies

    @pl.when(g == 0)
    def _():
        for cp in page_copies(0, 0):
            cp.start()

    @pl.when(g + 1 < n_steps)
    def _():
        for cp in page_copies(g + 1, 1 - slot):
            cp.start()

    def head_bias(shape):
        row_head = lax.broadcasted_iota(I32, shape, 0) // hrows
        col_head = lax.broadcasted_iota(I32, shape, 1) % N_HEADS
        return row_head == col_head

    @pl.when(g == 0)
    def _():
        bias_ref[...] = jnp.where(head_bias(bias_ref.shape), 0.0, -jnp.inf)

    @pl.when(c == 0)
    def _():
        m_ref[...] = jnp.full(m_ref.shape, -jnp.inf, F32)
        l_ref[...] = jnp.zeros(l_ref.shape, F32)
        acc_ref[...] = jnp.zeros(acc_ref.shape, F32)

    lane = lax.broadcasted_iota(I32, (dec_seq, V_DIM), 1)

    def query_block(q_ref):
        q = q_ref[...]
        q_rows = []
        for h in range(N_HEADS):
            qh = q[:, h * V_DIM:(h + 1) * V_DIM]
            q_rows += [jnp.where(lane < QK_DIM, qh, 0.0), jnp.where(lane >= QK_DIM, qh, 0.0)]
        return jnp.concatenate(q_rows, axis=0).astype(BF16)

    def update(s_idx, qblk, k, v, bias):
        s = lax.dot_general(qblk, k, (((1,), (1,)), ((), ())),
                            preferred_element_type=F32) + bias
        m_prev = m_ref[s_idx]
        m_new = jnp.maximum(m_prev, jnp.max(s, axis=-1, keepdims=True))
        alpha = jnp.exp2(m_prev - m_new)
        p = jnp.exp2(s - m_new)
        l_ref[s_idx] = alpha * l_ref[s_idx] + jnp.sum(p, axis=-1, keepdims=True)
        acc_ref[s_idx] = alpha * acc_ref[s_idx] + jnp.dot(p.astype(BF16), v,
                                                          preferred_element_type=F32)
        m_ref[s_idx] = m_new

    for cp in page_copies(g, slot):
        cp.wait()
    qblks = [query_block(refs[0]) for refs in streams]
    for s_idx in range(len(streams)):
        buf = 2 * s_idx + slot
        update(s_idx, qblks[s_idx], kbuf[buf].astype(BF16), vbuf[buf].astype(BF16),
               bias_ref[...])

    @pl.when(c == n_chunks - 1)
    def _():
        new_rows = dec_seq * N_HEADS
        pad = jnp.zeros((LANES - new_rows, V_DIM), F32)
        shape = (N_HEADS * hrows, LANES)
        qi = lax.broadcasted_iota(I32, shape, 0) % dec_seq
        kj = lax.broadcasted_iota(I32, shape, 1) // N_HEADS
        new_bias = jnp.where(head_bias(shape) & (kj <= qi), 0.0, -jnp.inf)
        lam = _diff_lambda(lq1_ref[...], lk1_ref[...], lq2_ref[...], lk2_ref[...])
        for s_idx, (_, kn_ref, vn_ref, o_ref) in enumerate(streams):
            kn = jnp.concatenate([kn_ref[...], pad], axis=0).astype(BF16)
            vn = jnp.concatenate([vn_ref[...], pad], axis=0).astype(BF16)
            update(s_idx, qblks[s_idx], kn, vn, new_bias)
            o_heads = acc_ref[s_idx] / l_ref[s_idx]
            for h in range(N_HEADS):
                o_all = o_heads[h * hrows:(h + 1) * hrows]
                o = o_all[:dec_seq] - lam * o_all[dec_seq:]
                ms = jnp.mean(o * o, axis=-1, keepdims=True)
                o_ref[:, h * V_DIM:(h + 1) * V_DIM] = (
                    o * lax.rsqrt(ms + LN_EPS) * g_ref[...] * (1.0 - LAM_INIT))


def _attn_sample(q, k_new, v_new, cache_k_rows, cache_v_rows, page_table, lam_params, subln_g,
                 *, dec_batch, dec_seq):
    n_pages = page_table.shape[1]
    assert n_pages % PAGES_PER_STEP == 0 and dec_seq % SUBLANES == 0
    n_chunks = n_pages // PAGES_PER_STEP
    n_streams = 2
    assert dec_batch % n_streams == 0
    half = dec_batch // n_streams
    n_steps = half * n_chunks
    rows = 2 * N_HEADS * dec_seq
    chunk_rows = PAGES_PER_STEP * PAGE_SIZE * N_HEADS

    def stream_specs(s):
        tok = pl.BlockSpec((dec_seq, ATTN_WIDTH), lambda g, pt: (s * half + g // n_chunks, 0))
        tok_heads = pl.BlockSpec((dec_seq * N_HEADS, V_DIM),
                                 lambda g, pt: (s * half + g // n_chunks, 0))
        return tok, tok_heads

    (tok0, heads0), (tok1, heads1) = stream_specs(0), stream_specs(1)
    out_spec = pl.BlockSpec((dec_seq, ATTN_WIDTH), lambda g, pt: (g // n_chunks, 0))
    vec = lambda n: pl.BlockSpec((1, n), lambda g, pt: (0, 0))
    any_spec = pl.BlockSpec(memory_space=pl.ANY)
    n_bufs = 2 * n_streams
    grid_spec = pltpu.PrefetchScalarGridSpec(
        num_scalar_prefetch=1,
        grid=(n_steps,),
        in_specs=[tok0, tok1, heads0, heads1, heads0, heads1, any_spec, any_spec,
                  vec(QK_DIM), vec(QK_DIM), vec(QK_DIM), vec(QK_DIM), vec(V_DIM)],
        out_specs=[out_spec, out_spec],
        scratch_shapes=[pltpu.VMEM((n_bufs, chunk_rows, V_DIM), F32),
                        pltpu.VMEM((n_bufs, chunk_rows, V_DIM), F32),
                        pltpu.SemaphoreType.DMA((n_bufs,)),
                        pltpu.SemaphoreType.DMA((n_bufs,)),
                        pltpu.VMEM((rows, chunk_rows), F32),
                        pltpu.VMEM((n_streams, rows, 1), F32),
                        pltpu.VMEM((n_streams, rows, 1), F32),
                        pltpu.VMEM((n_streams, rows, V_DIM), F32)])
    half_out = jax.ShapeDtypeStruct((half * dec_seq, ATTN_WIDTH), F32)
    o0, o1 = pl.pallas_call(
        functools.partial(_attn_sample_kernel, n_chunks=n_chunks, n_steps=n_steps,
                          dec_seq=dec_seq),
        grid_spec=grid_spec,
        out_shape=[half_out, half_out],
        compiler_params=_cparams(1),
        name="attn_sample",
    )(page_table.reshape(-1), q, q, k_new, k_new, v_new, v_new, cache_k_rows, cache_v_rows,
      *lam_params, subln_g)
    return jnp.concatenate([o0, o1], axis=0)


def _softplus(x):
    y = jnp.exp(-jnp.abs(x))
    w = 1.0 + y
    log1p = jnp.where(w == 1.0, y, jnp.log(w) * (y / (w - 1.0)))
    return jnp.maximum(x, 0.0) + log1p


def _rglru_kernel(xl_ref, gl_ref, conv0_ref, h0_ref, cw_ref, cb_ref, wg_ref, bg_ref, lam_ref,
                  y_ref, hlast_ref, tail_ref, hc_ref, *, ts):
    s = pl.program_id(1)

    @pl.when(s == 0)
    def _():
        tail_ref[...] = conv0_ref[0]
        hc_ref[...] = h0_ref[0]

    x = xl_ref[...]
    xfull = jnp.concatenate([tail_ref[...], x], axis=0)
    cw = cw_ref[...]
    xc = cb_ref[...] + cw[CONV_W - 1:CONV_W] * x
    for d in range(1, CONV_W):
        shifted = pltpu.roll(xfull, d, 0)[SUBLANES:]
        xc = xc + cw[CONV_W - 1 - d:CONV_W - d] * shifted
    tail_ref[...] = x[ts - SUBLANES:]

    gates = jnp.dot(xc.astype(BF16), wg_ref[...], preferred_element_type=F32) + bg_ref[...]
    r = jax.nn.sigmoid(gates[:, :LRU_WIDTH])
    ig = jax.nn.sigmoid(gates[:, LRU_WIDTH:])
    log_a = -LRU_C * r * _softplus(-lam_ref[...])
    a = jnp.exp(log_a)
    mult = jnp.sqrt(jnp.maximum(-(a * a + 1.0) * jnp.tanh(log_a), 0.0))
    u = mult * ig * xc

    row = lax.broadcasted_iota(I32, a.shape, 0)
    d = 1
    while d < ts:
        keep = row >= d
        a_prev = jnp.where(keep, pltpu.roll(a, d, 0), 1.0)
        u_prev = jnp.where(keep, pltpu.roll(u, d, 0), 0.0)
        u = a * u_prev + u
        a = a * a_prev
        d *= 2
    h = a * hc_ref[...] + u
    h_end = h[ts - 1:ts]
    hc_ref[...] = h_end
    hlast_ref[0] = h_end
    y_ref[...] = (h * jax.nn.gelu(gl_ref[...], approximate=True)).astype(y_ref.dtype)


def _rglru(xl, gl, conv0, h0, conv_w, conv_b, w_gates, b_gates, lam, *, batch, seq, ts):
    assert seq % ts == 0 and ts % SUBLANES == 0
    ns = seq // ts
    tile = pl.BlockSpec((ts, LRU_WIDTH), lambda b, s: (b * ns + s, 0))
    const = lambda shape: pl.BlockSpec(shape, lambda b, s: (0,) * len(shape))
    return pl.pallas_call(
        functools.partial(_rglru_kernel, ts=ts),
        grid=(batch, ns),
        in_specs=[tile, tile,
                  pl.BlockSpec((1, SUBLANES, LRU_WIDTH), lambda b, s: (b, 0, 0)),
                  pl.BlockSpec((1, 1, LRU_WIDTH), lambda b, s: (b, 0, 0)),
                  const((CONV_W, LRU_WIDTH)), const((1, LRU_WIDTH)),
                  const((LRU_WIDTH, 2 * LRU_WIDTH)), const((1, 2 * LRU_WIDTH)),
                  const((1, LRU_WIDTH))],
        out_specs=[tile, pl.BlockSpec((1, 1, LRU_WIDTH), lambda b, s: (b, 0, 0))],
        out_shape=[jax.ShapeDtypeStruct((batch * seq, LRU_WIDTH), BF16 if ts > SUBLANES else F32),
                   jax.ShapeDtypeStruct((batch, 1, LRU_WIDTH), F32)],
        scratch_shapes=[pltpu.VMEM((SUBLANES, LRU_WIDTH), F32),
                        pltpu.VMEM((1, LRU_WIDTH), F32)],
        compiler_params=_cparams(2),
        name="rglru_prompt" if ts > SUBLANES else "rglru_sample",
    )(xl, gl, conv0, h0, conv_w, conv_b, w_gates, b_gates, lam)


def _route(logits):
    lane = lax.broadcasted_iota(I32, logits.shape, 1).astype(F32)
    big = float(ROUTE_LANES)
    neg = -jnp.inf
    is_g = lane < N_GROUPS
    gmax = jnp.max(jnp.where(is_g, logits, neg), axis=-1, keepdims=True)
    gsum = jnp.sum(jnp.where(is_g, jnp.exp(logits - gmax), 0.0), axis=-1, keepdims=True)
    gw = 1.0 / gsum
    gid = jnp.min(jnp.where(is_g & (logits == gmax), lane, big), axis=-1, keepdims=True)
    lo = N_GROUPS + EXPERTS_PER_GROUP * gid
    in_grp = (lane >= lo) & (lane < lo + EXPERTS_PER_GROUP)
    v1 = jnp.max(jnp.where(in_grp, logits, neg), axis=-1, keepdims=True)
    i1 = jnp.min(jnp.where(in_grp & (logits == v1), lane, big), axis=-1, keepdims=True)
    rest = in_grp & (lane != i1)
    v2 = jnp.max(jnp.where(rest, logits, neg), axis=-1, keepdims=True)
    i2 = jnp.min(jnp.where(rest & (logits == v2), lane, big), axis=-1, keepdims=True)
    t = jnp.exp(v2 - v1)
    w1 = gw / (1.0 + t)
    w2 = gw * t / (1.0 + t)
    info = jnp.where(lane == 0, i1 - N_GROUPS,
                     jnp.where(lane == 1, i2 - N_GROUPS,
                               jnp.where(lane == 2, w1, jnp.where(lane == 3, w2, 0.0))))
    return info


def _merge_kernel(xp_ref, xs_ref, op_ref, os_ref, yp_ref, ys_ref, wo_ref, g_ref, b_ref,
                  wr_ref, br_ref, x1_ref, x1t_ref, info_ref, *, n_prompt_tiles):
    i = pl.program_id(0)

    def body(x_ref, o_ref, y_ref):
        m = jnp.dot(o_ref[...].astype(BF16), wo_ref[:ATTN_WIDTH, :], preferred_element_type=F32)
        m = m + jnp.dot(y_ref[...].astype(BF16), wo_ref[ATTN_WIDTH:, :],
                        preferred_element_type=F32)
        x1 = _layer_norm(DEEPNORM_ALPHA * x_ref[...] + m, g_ref[...], b_ref[...])
        x1_ref[...] = x1
        _to_token_tiles(x1t_ref, x1)
        logits = jnp.dot(x1.astype(BF16), wr_ref[...],
                         preferred_element_type=F32) + br_ref[...]
        info_ref[...] = _route(logits)

    @pl.when(i < n_prompt_tiles)
    def _():
        body(xp_ref, op_ref, yp_ref)

    @pl.when(i >= n_prompt_tiles)
    def _():
        body(xs_ref, os_ref, ys_ref)


def _merge(xp, xs, o_p, o_s, y_p, y_s, wo_bf16, g1, b1, wr, br):
    tm = TOK_TM
    tp, ts = xp.shape[0], xs.shape[0]
    assert tp % tm == 0 and ts % tm == 0
    npt, nst = tp // tm, ts // tm
    t_all = tp + ts
    pidx = lambda i: (jnp.minimum(i, npt - 1), 0)
    sidx = lambda i: (jnp.maximum(i - npt, 0), 0)
    const = lambda shape: pl.BlockSpec(shape, lambda i: (0, 0))
    row = lambda i: (i, 0)
    return pl.pallas_call(
        functools.partial(_merge_kernel, n_prompt_tiles=npt),
        grid=(npt + nst,),
        in_specs=[pl.BlockSpec((tm, D_MODEL), pidx), pl.BlockSpec((tm, D_MODEL), sidx),
                  pl.BlockSpec((tm, ATTN_WIDTH), pidx), pl.BlockSpec((tm, ATTN_WIDTH), sidx),
                  pl.BlockSpec((tm, LRU_WIDTH), pidx), pl.BlockSpec((tm, LRU_WIDTH), sidx),
                  const((D_MODEL, D_MODEL)), const((1, D_MODEL)), const((1, D_MODEL)),
                  const((D_MODEL, ROUTE_LANES)), const((1, ROUTE_LANES))],
        out_specs=[pl.BlockSpec((tm, D_MODEL), row),
                   pl.BlockSpec((tm * TILE_ROWS, LANES), row),
                   pl.BlockSpec((tm, ROUTE_LANES), row)],
        out_shape=[jax.ShapeDtypeStruct((t_all, D_MODEL), F32),
                   jax.ShapeDtypeStruct((t_all * TILE_ROWS, LANES), F32),
                   jax.ShapeDtypeStruct((t_all, ROUTE_LANES), F32)],
        compiler_params=_cparams(1),
        name="merge_ln1_route",
    )(xp, xs, o_p, o_s, y_p, y_s, wo_bf16, g1, b1, wr, br)


def _expert_kernel(te_ref, nlive_ref, src_ref, srcn_ref, dstp_ref,
                   x_hbm, wg_ref, wu_ref, wd_ref, out_hbm,
                   xbuf, ybuf, gsem, ssem):
    tm = TOK_TM
    tr = TILE_ROWS
    i = pl.program_id(0)
    slot = i % 2
    other = 1 - slot
    n_live = nlive_ref[0]

    def start_gather(idx_ref, sl):
        for r in range(tm):
            src = pl.ds(pl.multiple_of(idx_ref[0, 0, r], tr), tr)
            pltpu.make_async_copy(x_hbm.at[src], xbuf.at[sl, pl.ds(r * tr, tr)],
                                  gsem.at[sl]).start(priority=r % 2)

    def start_scatter(idx_ref, sl):
        for r in range(tm):
            dst = pl.ds(pl.multiple_of(idx_ref[0, 0, r], tr), tr)
            pltpu.make_async_copy(ybuf.at[sl, pl.ds(r * tr, tr)], out_hbm.at[dst],
                                  ssem.at[sl]).start(priority=r % 2)

    def wait_tile(buf, sem):
        pltpu.make_async_copy(x_hbm.at[pl.ds(0, tm * tr)], buf, sem).wait()

    @pl.when(i == 0)
    def _():
        ybuf[1] = jnp.zeros(ybuf.shape[1:], F32)
        start_gather(src_ref, 0)

    @pl.when((i >= 1) & (i <= n_live))
    def _():
        wait_tile(ybuf.at[slot], ssem.at[slot])

    @pl.when(i <= n_live)
    def _():
        wait_tile(xbuf.at[slot], gsem.at[slot])
        start_gather(srcn_ref, other)
        start_scatter(dstp_ref, other)

        x = _from_token_tiles(xbuf.at[slot], tm).astype(BF16)
        gate = jnp.dot(x, wg_ref[0].astype(BF16), preferred_element_type=F32)
        up = jnp.dot(x, wu_ref[0].astype(BF16), preferred_element_type=F32)
        h = (gate * jax.nn.sigmoid(gate) * up).astype(BF16)
        y = jnp.dot(h, wd_ref[0].astype(BF16), preferred_element_type=F32)
        _to_token_tiles(ybuf.at[slot], y)

    @pl.when(i == n_live)
    def _():
        wait_tile(xbuf.at[other], gsem.at[other])
        wait_tile(ybuf.at[other], ssem.at[other])


def _experts(x1t, tile_expert, n_live, src_tiles, dst_prev_tiles, wg, wu, wd, *, n_rows):
    tm = TOK_TM
    n_tiles = tile_expert.shape[0]
    last = n_tiles - 1
    smem_tile = lambda fn: pl.BlockSpec((1, 1, tm), fn, memory_space=pltpu.SMEM)
    weight = lambda shape: pl.BlockSpec(
        shape, lambda i, te, nl: (te[jnp.minimum(i, last)], 0, 0))
    any_spec = pl.BlockSpec(memory_space=pl.ANY)
    grid_spec = pltpu.PrefetchScalarGridSpec(
        num_scalar_prefetch=2,
        grid=(n_tiles + 1,),
        in_specs=[smem_tile(lambda i, te, nl: (jnp.minimum(i, last), 0, 0)),
                  smem_tile(lambda i, te, nl: (jnp.minimum(i + 1, last), 0, 0)),
                  smem_tile(lambda i, te, nl: (i, 0, 0)),
                  any_spec,
                  weight((1, D_MODEL, D_EXPERT)), weight((1, D_MODEL, D_EXPERT)),
                  weight((1, D_EXPERT, D_MODEL))],
        out_specs=any_spec,
        scratch_shapes=[pltpu.VMEM((2, tm * TILE_ROWS, LANES), F32),
                        pltpu.VMEM((2, tm * TILE_ROWS, LANES), F32),
                        pltpu.SemaphoreType.DMA((2,)),
                        pltpu.SemaphoreType.DMA((2,))])
    return pl.pallas_call(
        _expert_kernel,
        grid_spec=grid_spec,
        out_shape=jax.ShapeDtypeStruct((n_rows * TILE_ROWS, LANES), F32),
        compiler_params=_cparams(1),
        name="expert_mlp",
    )(tile_expert, n_live, src_tiles, src_tiles, dst_prev_tiles, x1t, wg, wu, wd)


def _expert_plan(info, n_tiles):
    tm = TOK_TM
    t_all = info.shape[0]
    n_pairs = 2 * t_all
    ek = jnp.concatenate([info[:, 0], info[:, 1]]).astype(I32)
    _, order = lax.sort((ek, lax.iota(I32, n_pairs)), num_keys=1, is_stable=True)
    experts = jnp.arange(N_EXPERTS, dtype=I32)
    counts = jnp.sum((ek[:, None] == experts[None, :]).astype(I32), axis=0)
    start = jnp.cumsum(counts) - counts
    tiles_e = (counts + tm - 1) // tm
    cum = jnp.cumsum(tiles_e)
    n_live = cum[-1]
    ti = jnp.arange(n_tiles, dtype=I32)
    expert_of = lambda t: jnp.minimum(
        jnp.sum((cum[None, :] <= t[:, None]).astype(I32), axis=1), N_EXPERTS - 1)
    e_i = expert_of(ti)
    e_last = expert_of(n_live[None] - 1)[0]
    j = ti - (cum - tiles_e)[e_i]
    live = ti < n_live
    tile_expert = jnp.where(live, e_i, e_last).astype(I32)
    tile_rows = jnp.where(live, jnp.clip(counts[e_i] - j * tm, 0, tm), 0)
    tile_start = jnp.where(live, start[e_i] + j * tm, 0)
    r = jnp.arange(tm, dtype=I32)[None, :]
    valid = r < tile_rows[:, None]
    pair = order[jnp.minimum(tile_start[:, None] + r, n_pairs - 1)]
    src = jnp.where(valid, pair % t_all, 0) * TILE_ROWS
    dump = (n_pairs + r) * TILE_ROWS
    dst = jnp.where(valid, pair, n_pairs + r) * TILE_ROWS
    dst_prev = jnp.concatenate([dump, dst], axis=0)
    return (tile_expert, n_live[None].astype(I32), src.reshape(n_tiles, 1, tm).astype(I32),
            dst_prev.reshape(n_tiles + 1, 1, tm).astype(I32))


def _combine_kernel(x1_ref, ya_ref, yb_ref, info_ref, g_ref, b_ref, outp_ref, outs_ref,
                    *, n_prompt_tiles):
    tm = TOK_TM
    i = pl.program_id(0)
    info = info_ref[...]
    w1 = info[:, 2:3]
    w2 = info[:, 3:4]
    f = w1 * _from_token_tiles(ya_ref, tm) + w2 * _from_token_tiles(yb_ref, tm)
    out = _layer_norm(DEEPNORM_ALPHA * x1_ref[...] + f, g_ref[...], b_ref[...])

    @pl.when(i < n_prompt_tiles)
    def _():
        outp_ref[...] = out

    @pl.when(i >= n_prompt_tiles)
    def _():
        outs_ref[...] = out


def _combine(x1, y_pairs, info, g2, b2, *, n_prompt, n_sample):
    tm = TOK_TM
    assert n_sample == tm
    npt = n_prompt // tm
    nt = npt + 1
    row = lambda i: (i, 0)
    const = lambda shape: pl.BlockSpec(shape, lambda i: (0, 0))
    return pl.pallas_call(
        functools.partial(_combine_kernel, n_prompt_tiles=npt),
        grid=(nt,),
        in_specs=[pl.BlockSpec((tm, D_MODEL), row),
                  pl.BlockSpec((tm * TILE_ROWS, LANES), row),
                  pl.BlockSpec((tm * TILE_ROWS, LANES), lambda i: (nt + i, 0)),
                  pl.BlockSpec((tm, ROUTE_LANES), row),
                  const((1, D_MODEL)), const((1, D_MODEL))],
        out_specs=[pl.BlockSpec((tm, D_MODEL), lambda i: (jnp.minimum(i, npt - 1), 0)),
                   pl.BlockSpec((tm, D_MODEL), lambda i: (0, 0))],
        out_shape=[jax.ShapeDtypeStruct((n_prompt, D_MODEL), F32),
                   jax.ShapeDtypeStruct((n_sample, D_MODEL), F32)],
        compiler_params=_cparams(1),
        name="combine_ln2",
    )(x1, y_pairs, y_pairs, info, g2, b2)


def _block_diag(w):
    nb, n, _ = w.shape
    eye = jnp.eye(nb, dtype=w.dtype)
    return (eye[:, None, :, None] * w[:, :, None, :]).reshape(nb * n, nb * n)


def kernel(x_prompt, x_sample, cache_k, cache_v, state_h, state_conv, page_table, w_in, lambda_q1, lambda_k1, lambda_q2, lambda_k2, subln_g, conv_w, conv_b, w_rg_a, b_rg_a, w_rg_x, b_rg_x, lru_lambda, w_out, ln1_g, ln1_b, w_group_router, b_group_router, w_expert_router, b_expert_router, w_gate_e, w_up_e, w_down_e, ln2_g, ln2_b):
    bp, sp, _ = x_prompt.shape
    bd, sd, _ = x_sample.shape
    assert w_in.shape[0] == DEPTH == 1
    l = 0
    n_prompt, n_sample = bp * sp, bd * sd
    t_all = n_prompt + n_sample

    xp = x_prompt.reshape(n_prompt, D_MODEL)
    xs = x_sample.reshape(n_sample, D_MODEL)
    w_in_b = w_in[l].astype(BF16)
    w_out_b = w_out[l].astype(BF16)
    lam_params = (lambda_q1[l][None], lambda_k1[l][None], lambda_q2[l][None], lambda_k2[l][None])
    w_gates = jnp.concatenate([_block_diag(w_rg_a[l]), _block_diag(w_rg_x[l])], axis=1).astype(BF16)
    b_gates = jnp.concatenate([b_rg_a[l].reshape(1, -1), b_rg_x[l].reshape(1, -1)], axis=1)
    lru_args = (conv_w[l], conv_b[l][None], w_gates, b_gates, lru_lambda[l][None])

    qt, kh, vt, k_p, v_p, xl_p, gl_p = _inproj(xp, w_in_b, prompt=True, tm=INPROJ_TM)
    o_p = _attn_prompt(qt, kh, vt, lam_params, subln_g[l][:, None], batch=bp, seq=sp)
    y_p, h_p = _rglru(xl_p, gl_p, jnp.zeros((bp, SUBLANES, LRU_WIDTH), F32),
                      jnp.zeros((bp, 1, LRU_WIDTH), F32), *lru_args, batch=bp, seq=sp, ts=LRU_TS)

    q_s, k_s, v_s, xl_s, gl_s = _inproj(xs, w_in_b, prompt=False, tm=n_sample)
    n_phys = cache_k.shape[1]
    ck = cache_k.reshape(DEPTH * n_phys * PAGE_SIZE * N_HEADS, V_DIM)
    cv = cache_v.reshape(DEPTH * n_phys * PAGE_SIZE * N_HEADS, V_DIM)
    o_s = _attn_sample(q_s, k_s, v_s, ck, cv, page_table, lam_params, subln_g[l][None],
                       dec_batch=bd, dec_seq=sd)
    conv0 = jnp.concatenate(
        [jnp.zeros((bd, SUBLANES - (CONV_W - 1), LRU_WIDTH), F32), state_conv[l]], axis=1)
    y_s, h_s = _rglru(xl_s, gl_s, conv0, state_h[l][:, None, :], *lru_args,
                      batch=bd, seq=sd, ts=sd)

    w_router = jnp.concatenate(
        [w_group_router[l], w_expert_router[l],
         jnp.zeros((D_MODEL, ROUTE_LANES - N_GROUPS - N_EXPERTS), F32)], axis=1).astype(BF16)
    b_router = jnp.concatenate(
        [b_group_router[l], b_expert_router[l],
         jnp.zeros((ROUTE_LANES - N_GROUPS - N_EXPERTS,), F32)])[None]
    x1, x1t, info = _merge(xp, xs, o_p, o_s, y_p, y_s, w_out_b, ln1_g[l][None], ln1_b[l][None],
                           w_router, b_router)
    n_tiles = (2 * t_all) // TOK_TM + N_EXPERTS
    tile_expert, n_live, src_tiles, dst_prev_tiles = _expert_plan(info, n_tiles)
    y_pairs = _experts(x1t, tile_expert, n_live, src_tiles, dst_prev_tiles,
                       w_gate_e[l], w_up_e[l], w_down_e[l], n_rows=2 * t_all + TOK_TM)
    out_p, out_s = _combine(x1, y_pairs, info, ln2_g[l][None], ln2_b[l][None],
                            n_prompt=n_prompt, n_sample=n_sample)

    conv_p = xl_p.reshape(bp, sp, LRU_WIDTH)[:, sp - (CONV_W - 1):]
    conv_s = jnp.concatenate([state_conv[l], xl_s.reshape(bd, sd, LRU_WIDTH)],
                             axis=1)[:, -(CONV_W - 1):]
    return (out_p.reshape(bp, sp, D_MODEL),
            out_s.reshape(bd, sd, D_MODEL),
            k_p.reshape(1, bp, sp, N_HEADS, V_DIM),
            v_p.reshape(1, bp, sp, N_HEADS, V_DIM),
            h_p.reshape(1, bp, LRU_WIDTH),
            conv_p[None],
            k_s.reshape(1, bd, sd, N_HEADS, V_DIM),
            v_s.reshape(1, bd, sd, N_HEADS, V_DIM),
            h_s.reshape(1, bd, LRU_WIDTH),
            conv_s[None])
```

```python
import functools
import math

import jax
import jax.numpy as jnp
from jax import lax
from jax.experimental import pallas as pl
from jax.experimental.pallas import tpu as pltpu

F32 = jnp.float32
BF16 = jnp.bfloat16
I32 = jnp.int32

D_MODEL = 1024
ATTN_WIDTH = 512
LRU_WIDTH = 512
N_HEADS = 4
V_DIM = 128
QK_DIM = 64
CONV_W = 4
LRU_C = 8.0
IN_WIDTH = 3 * ATTN_WIDTH + 2 * LRU_WIDTH
N_GROUPS = 4
EXPERTS_PER_GROUP = 8
N_EXPERTS = 32
D_EXPERT = 256
PAGE_SIZE = 128
LN_EPS = 1e-5
DEPTH = 1
DEEPNORM_ALPHA = (2 * DEPTH) ** 0.25
LAM_INIT = 0.8 - 0.6 * math.exp(-0.3 * 0)
QK_SCALE = QK_DIM ** -0.5 * math.log2(math.e)

SUBLANES = 8
LANES = 128
VMEM_LIMIT_BYTES = 48 * 1024 * 1024

INPROJ_TM = 512
ATTN_TQ = 512
ATTN_TK = 512
ATTN_CB = 256
LRU_TS = 512
TOK_TM = 256
MOE_TM = 1024
MOE_CHUNK = MOE_TM // EXPERTS_PER_GROUP
PAGES_PER_STEP = 8
ROUTE_LANES = 128
TILE_ROWS = D_MODEL // LANES


def _cparams(n_axes):
    return pltpu.CompilerParams(
        dimension_semantics=("arbitrary",) * n_axes,
        vmem_limit_bytes=VMEM_LIMIT_BYTES)


def _layer_norm(x, g, b):
    mu = jnp.mean(x, axis=-1, keepdims=True)
    xc = x - mu
    var = jnp.mean(xc * xc, axis=-1, keepdims=True)
    return xc * lax.rsqrt(var + LN_EPS) * g + b


def _diff_lambda(lq1, lk1, lq2, lk2):
    s1 = jnp.sum(lq1 * lk1, axis=-1, keepdims=True)
    s2 = jnp.sum(lq2 * lk2, axis=-1, keepdims=True)
    return jnp.exp(s1) - jnp.exp(s2) + LAM_INIT


def _to_token_tiles(ref, x):
    tm = x.shape[0]
    for s in range(TILE_ROWS):
        ref[pl.ds(s, tm, stride=TILE_ROWS), :] = x[:, s * LANES:(s + 1) * LANES]


def _from_token_tiles(ref, tm):
    return jnp.concatenate(
        [ref[pl.ds(s, tm, stride=TILE_ROWS), :] for s in range(TILE_ROWS)], axis=1)


def _inproj_kernel(x_ref, w_ref, *out_refs, prompt):
    x = x_ref[...].astype(BF16)
    tm = x.shape[0]

    def section(idx, width):
        return jnp.dot(x, w_ref[:, idx:idx + width], preferred_element_type=F32)

    q = section(0, ATTN_WIDTH) * QK_SCALE
    k = section(ATTN_WIDTH, ATTN_WIDTH)
    v = section(2 * ATTN_WIDTH, ATTN_WIDTH)
    if prompt:
        qt_ref, kh_ref, vt_ref, k_ref, v_ref, xl_ref, gl_ref = out_refs
        chan = lax.broadcasted_iota(I32, (V_DIM, tm), 0)
        for h in range(N_HEADS):
            sl = slice(h * V_DIM, (h + 1) * V_DIM)
            qt = q[:, sl].T
            qt_ref[h, 0] = jnp.where(chan < QK_DIM, qt, 0.0).astype(BF16)
            qt_ref[h, 1] = jnp.where(chan >= QK_DIM, qt, 0.0).astype(BF16)
            kh_ref[h] = k[:, sl].astype(BF16)
            vt_ref[h, 0] = v[:, sl].T.astype(BF16)
    else:
        q_ref, k_ref, v_ref, xl_ref, gl_ref = out_refs
        q_ref[...] = q
    for h in range(N_HEADS):
        sl = slice(h * V_DIM, (h + 1) * V_DIM)
        k_ref[pl.ds(h, tm, stride=N_HEADS), :] = k[:, sl]
        v_ref[pl.ds(h, tm, stride=N_HEADS), :] = v[:, sl]
    xl_ref[...] = section(3 * ATTN_WIDTH, LRU_WIDTH)
    gl_ref[...] = section(3 * ATTN_WIDTH + LRU_WIDTH, LRU_WIDTH)


def _inproj(x2d, w_bf16, *, prompt, tm):
    t = x2d.shape[0]
    assert t % tm == 0
    row = lambda i: (i, 0)
    spec512 = pl.BlockSpec((tm, ATTN_WIDTH), row)
    head_rows = pl.BlockSpec((tm * N_HEADS, V_DIM), row)
    f32_512 = jax.ShapeDtypeStruct((t, ATTN_WIDTH), F32)
    f32_heads = jax.ShapeDtypeStruct((t * N_HEADS, V_DIM), F32)
    if prompt:
        assert tm == ATTN_TK
        out_shape = [jax.ShapeDtypeStruct((N_HEADS, 2, V_DIM, t), BF16),
                     jax.ShapeDtypeStruct((N_HEADS, t, V_DIM), BF16),
                     jax.ShapeDtypeStruct((N_HEADS, t // tm, V_DIM, tm), BF16)]
        out_specs = [pl.BlockSpec((N_HEADS, 2, V_DIM, tm), lambda i: (0, 0, 0, i)),
                     pl.BlockSpec((N_HEADS, tm, V_DIM), lambda i: (0, i, 0)),
                     pl.BlockSpec((N_HEADS, 1, V_DIM, tm), lambda i: (0, i, 0, 0))]
    else:
        out_shape = [f32_512]
        out_specs = [spec512]
    out_shape += [f32_heads, f32_heads, f32_512, f32_512]
    out_specs += [head_rows, head_rows, spec512, spec512]
    return pl.pallas_call(
        functools.partial(_inproj_kernel, prompt=prompt),
        grid=(t // tm,),
        in_specs=[pl.BlockSpec((tm, D_MODEL), row),
                  pl.BlockSpec((D_MODEL, IN_WIDTH), lambda i: (0, 0))],
        out_specs=out_specs,
        out_shape=out_shape,
        compiler_params=_cparams(1),
        name="inproj_prompt" if prompt else "inproj_sample",
    )(x2d, w_bf16)


def _attn_prompt_kernel(qt_ref, k_ref, vt_ref, lq1_ref, lk1_ref, lq2_ref, lk2_ref, g_ref,
                        o_ref, s_ref, p_ref, a_ref, m_ref, l_ref, acc_ref):
    tq, tk, cb = ATTN_TQ, ATTN_TK, ATTN_CB
    n_cols = 2 * tq
    i = pl.program_id(2)

    m_ref[...] = jnp.full(m_ref.shape, -jnp.inf, F32)
    l_ref[...] = jnp.zeros(l_ref.shape, F32)
    acc_ref[...] = jnp.zeros(acc_ref.shape, F32)
    p_ref[1] = jnp.zeros(p_ref.shape[1:], BF16)
    a_ref[1] = jnp.ones(a_ref.shape[1:], F32)

    def scores(j, slot):
        k = k_ref[0, pl.ds(pl.multiple_of(j * tk, tk), tk), :]
        for mp in range(2):
            s_ref[slot, :, mp * tq:(mp + 1) * tq] = jnp.dot(
                k, qt_ref[0, mp], preferred_element_type=F32)

    def softmax(slot, masked):
        for c in range(n_cols // cb):
            off = (c * cb) % tq
            cs = slice(c * cb, (c + 1) * cb)
            s = s_ref[slot, :, cs]
            if masked:
                key = lax.broadcasted_iota(I32, s.shape, 0)
                qpos = lax.broadcasted_iota(I32, s.shape, 1) + off
                s = jnp.where(key <= qpos, s, -jnp.inf)
            m_prev = m_ref[:, cs]
            m_new = jnp.maximum(m_prev, jnp.max(s, axis=0, keepdims=True))
            alpha = jnp.exp2(m_prev - m_new)
            p = jnp.exp2(s - m_new)
            l_ref[:, cs] = alpha * l_ref[:, cs] + jnp.sum(p, axis=0, keepdims=True)
            m_ref[:, cs] = m_new
            p_ref[slot, :, cs] = p.astype(BF16)
            a_ref[slot, :, cs] = alpha

    def accumulate(j, slot):
        acc_ref[...] = a_ref[slot] * acc_ref[...] + jnp.dot(
            vt_ref[0, j], p_ref[slot], preferred_element_type=F32)

    def stage(j, cur):
        scores(j + 1, 1 - cur)
        softmax(cur, masked=False)
        accumulate(jnp.maximum(j - 1, 0), 1 - cur)

    def diagonal(cur):
        accumulate(jnp.maximum(i - 1, 0), 1 - cur)
        softmax(cur, masked=True)
        accumulate(i, cur)

    scores(0, 0)

    def body(t, carry):
        stage(2 * t, 0)
        stage(2 * t + 1, 1)
        return carry

    lax.fori_loop(0, i // 2, body, 0)

    @pl.when(i % 2 == 0)
    def _():
        diagonal(0)

    @pl.when(i % 2 == 1)
    def _():
        stage(i - 1, 0)
        diagonal(1)

    lam = _diff_lambda(lq1_ref[...], lk1_ref[...], lq2_ref[...], lk2_ref[...])
    o_all = acc_ref[...] / l_ref[...]
    o = o_all[:, :tq] - lam * o_all[:, tq:]
    ms = jnp.mean(o * o, axis=0, keepdims=True)
    o = o * lax.rsqrt(ms + LN_EPS) * g_ref[...] * (1.0 - LAM_INIT)
    o_ref[...] = o.T.astype(o_ref.dtype)


def _attn_prompt(qt, kh, vt, lam_params, subln_g_col, *, batch, seq):
    assert ATTN_TQ == ATTN_TK and seq % ATTN_TQ == 0 and ATTN_TQ % ATTN_CB == 0
    nq = seq // ATTN_TQ
    t = batch * seq
    vec = lambda n: pl.BlockSpec((1, n), lambda b, h, i: (0, 0))
    return pl.pallas_call(
        _attn_prompt_kernel,
        grid=(batch, N_HEADS, nq),
        in_specs=[pl.BlockSpec((1, 2, V_DIM, ATTN_TQ), lambda b, h, i: (h, 0, 0, b * nq + i)),
                  pl.BlockSpec((1, seq, V_DIM), lambda b, h, i: (h, b, 0)),
                  pl.BlockSpec((1, nq, V_DIM, ATTN_TK), lambda b, h, i: (h, b, 0, 0)),
                  vec(QK_DIM), vec(QK_DIM), vec(QK_DIM), vec(QK_DIM),
                  pl.BlockSpec((V_DIM, 1), lambda b, h, i: (0, 0))],
        out_specs=pl.BlockSpec((ATTN_TQ, V_DIM), lambda b, h, i: (b * nq + i, h)),
        out_shape=jax.ShapeDtypeStruct((t, ATTN_WIDTH), BF16),
        scratch_shapes=[pltpu.VMEM((2, ATTN_TK, 2 * ATTN_TQ), F32),
                        pltpu.VMEM((2, ATTN_TK, 2 * ATTN_TQ), BF16),
                        pltpu.VMEM((2, 1, 2 * ATTN_TQ), F32),
                        pltpu.VMEM((1, 2 * ATTN_TQ), F32),
                        pltpu.VMEM((1, 2 * ATTN_TQ), F32),
                        pltpu.VMEM((V_DIM, 2 * ATTN_TQ), F32)],
        compiler_params=_cparams(3),
        name="attn_prompt",
    )(qt, kh, vt, *lam_params, subln_g_col)


def _attn_sample_kernel(pt_ref, q0_ref, q1_ref, kn0_ref, kn1_ref, vn0_ref, vn1_ref,
                        ck_hbm, cv_hbm, lq1_ref, lk1_ref, lq2_ref, lk2_ref, g_ref,
                        o0_ref, o1_ref, kbuf, vbuf, ksem, vsem, bias_ref, m_ref, l_ref, acc_ref,
                        *, n_chunks, n_steps, dec_seq):
    gpp = PAGES_PER_STEP
    page_rows = PAGE_SIZE * N_HEADS
    hrows = 2 * dec_seq
    g = pl.program_id(0)
    c = g % n_chunks
    slot = g % 2
    streams = ((q0_ref, kn0_ref, vn0_ref, o0_ref), (q1_ref, kn1_ref, vn1_ref, o1_ref))

    def page_copies(step, sl):
        copies = []
        for s in range(len(streams)):
            buf = 2 * s + sl
            for p in range(gpp):
                page = pt_ref[(s * n_steps + step) * gpp + p]
                src = pl.ds(pl.multiple_of(page * page_rows, page_rows), page_rows)
                dst = pl.ds(p * page_rows, page_rows)
                copies.append(pltpu.make_async_copy(ck_hbm.at[src], kbuf.at[buf, dst],
                                                    ksem.at[buf]))
                copies.append(pltpu.make_async_copy(cv_hbm.at[src], vbuf.at[buf, dst],
                                                    vsem.at[buf]))
        return copies

    @pl.when(g == 0)
    def _():
        for cp in page_copies(0, 0):
            cp.start()

    @pl.when(g + 1 < n_steps)
    def _():
        for cp in page_copies(g + 1, 1 - slot):
            cp.start()

    def head_bias(shape):
        row_head = lax.broadcasted_iota(I32, shape, 0) // hrows
        col_head = lax.broadcasted_iota(I32, shape, 1) % N_HEADS
        return row_head == col_head

    @pl.when(g == 0)
    def _():
        bias_ref[...] = jnp.where(head_bias(bias_ref.shape), 0.0, -jnp.inf)

    @pl.when(c == 0)
    def _():
        m_ref[...] = jnp.full(m_ref.shape, -jnp.inf, F32)
        l_ref[...] = jnp.zeros(l_ref.shape, F32)
        acc_ref[...] = jnp.zeros(acc_ref.shape, F32)

    lane = lax.broadcasted_iota(I32, (dec_seq, V_DIM), 1)

    def query_block(q_ref):
        q = q_ref[...]
        q_rows = []
        for h in range(N_HEADS):
            qh = q[:, h * V_DIM:(h + 1) * V_DIM]
            q_rows += [jnp.where(lane < QK_DIM, qh, 0.0), jnp.where(lane >= QK_DIM, qh, 0.0)]
        return jnp.concatenate(q_rows, axis=0).astype(BF16)

    def update(s_idx, qblk, k, v, bias):
        s = lax.dot_general(qblk, k, (((1,), (1,)), ((), ())),
                            preferred_element_type=F32) + bias
        m_prev = m_ref[s_idx]
        m_new = jnp.maximum(m_prev, jnp.max(s, axis=-1, keepdims=True))
        alpha = jnp.exp2(m_prev - m_new)
        p = jnp.exp2(s - m_new)
        l_ref[s_idx] = alpha * l_ref[s_idx] + jnp.sum(p, axis=-1, keepdims=True)
        acc_ref[s_idx] = alpha * acc_ref[s_idx] + jnp.dot(p.astype(BF16), v,
                                                          preferred_element_type=F32)
        m_ref[s_idx] = m_new

    for cp in page_copies(g, slot):
        cp.wait()
    qblks = [query_block(refs[0]) for refs in streams]
    for s_idx in range(len(streams)):
        buf = 2 * s_idx + slot
        update(s_idx, qblks[s_idx], kbuf[buf].astype(BF16), vbuf[buf].astype(BF16),
               bias_ref[...])

    @pl.when(c == n_chunks - 1)
    def _():
        new_rows = dec_seq * N_HEADS
        pad = jnp.zeros((LANES - new_rows, V_DIM), F32)
        shape = (N_HEADS * hrows, LANES)
        qi = lax.broadcasted_iota(I32, shape, 0) % dec_seq
        kj = lax.broadcasted_iota(I32, shape, 1) // N_HEADS
        new_bias = jnp.where(head_bias(shape) & (kj <= qi), 0.0, -jnp.inf)
        lam = _diff_lambda(lq1_ref[...], lk1_ref[...], lq2_ref[...], lk2_ref[...])
        for s_idx, (_, kn_ref, vn_ref, o_ref) in enumerate(streams):
            kn = jnp.concatenate([kn_ref[...], pad], axis=0).astype(BF16)
            vn = jnp.concatenate([vn_ref[...], pad], axis=0).astype(BF16)
            update(s_idx, qblks[s_idx], kn, vn, new_bias)
            o_heads = acc_ref[s_idx] / l_ref[s_idx]
            for h in range(N_HEADS):
                o_all = o_heads[h * hrows:(h + 1) * hrows]
                o = o_all[:dec_seq] - lam * o_all[dec_seq:]
                ms = jnp.mean(o * o, axis=-1, keepdims=True)
                o_ref[:, h * V_DIM:(h + 1) * V_DIM] = (
                    o * lax.rsqrt(ms + LN_EPS) * g_ref[...] * (1.0 - LAM_INIT))


def _attn_sample(q, k_new, v_new, cache_k_rows, cache_v_rows, page_table, lam_params, subln_g,
                 *, dec_batch, dec_seq):
    n_pages = page_table.shape[1]
    assert n_pages % PAGES_PER_STEP == 0 and dec_seq % SUBLANES == 0
    n_chunks = n_pages // PAGES_PER_STEP
    n_streams = 2
    assert dec_batch % n_streams == 0
    half = dec_batch // n_streams
    n_steps = half * n_chunks
    rows = 2 * N_HEADS * dec_seq
    chunk_rows = PAGES_PER_STEP * PAGE_SIZE * N_HEADS

    def stream_specs(s):
        tok = pl.BlockSpec((dec_seq, ATTN_WIDTH), lambda g, pt: (s * half + g // n_chunks, 0))
        tok_heads = pl.BlockSpec((dec_seq * N_HEADS, V_DIM),
                                 lambda g, pt: (s * half + g // n_chunks, 0))
        return tok, tok_heads

    (tok0, heads0), (tok1, heads1) = stream_specs(0), stream_specs(1)
    out_spec = pl.BlockSpec((dec_seq, ATTN_WIDTH), lambda g, pt: (g // n_chunks, 0))
    vec = lambda n: pl.BlockSpec((1, n), lambda g, pt: (0, 0))
    any_spec = pl.BlockSpec(memory_space=pl.ANY)
    n_bufs = 2 * n_streams
    grid_spec = pltpu.PrefetchScalarGridSpec(
        num_scalar_prefetch=1,
        grid=(n_steps,),
        in_specs=[tok0, tok1, heads0, heads1, heads0, heads1, any_spec, any_spec,
                  vec(QK_DIM), vec(QK_DIM), vec(QK_DIM), vec(QK_DIM), vec(V_DIM)],
        out_specs=[out_spec, out_spec],
        scratch_shapes=[pltpu.VMEM((n_bufs, chunk_rows, V_DIM), F32),
                        pltpu.VMEM((n_bufs, chunk_rows, V_DIM), F32),
                        pltpu.SemaphoreType.DMA((n_bufs,)),
                        pltpu.SemaphoreType.DMA((n_bufs,)),
                        pltpu.VMEM((rows, chunk_rows), F32),
                        pltpu.VMEM((n_streams, rows, 1), F32),
                        pltpu.VMEM((n_streams, rows, 1), F32),
                        pltpu.VMEM((n_streams, rows, V_DIM), F32)])
    half_out = jax.ShapeDtypeStruct((half * dec_seq, ATTN_WIDTH), F32)
    o0, o1 = pl.pallas_call(
        functools.partial(_attn_sample_kernel, n_chunks=n_chunks, n_steps=n_steps,
                          dec_seq=dec_seq),
        grid_spec=grid_spec,
        out_shape=[half_out, half_out],
        compiler_params=_cparams(1),
        name="attn_sample",
    )(page_table.reshape(-1), q, q, k_new, k_new, v_new, v_new, cache_k_rows, cache_v_rows,
      *lam_params, subln_g)
    return jnp.concatenate([o0, o1], axis=0)


def _softplus(x):
    y = jnp.exp(-jnp.abs(x))
    w = 1.0 + y
    log1p = jnp.where(w == 1.0, y, jnp.log(w) * (y / (w - 1.0)))
    return jnp.maximum(x, 0.0) + log1p


def _rglru_kernel(xl_ref, gl_ref, conv0_ref, h0_ref, cw_ref, cb_ref, wg_ref, bg_ref, lam_ref,
                  y_ref, hlast_ref, tail_ref, hc_ref, *, ts):
    s = pl.program_id(1)

    @pl.when(s == 0)
    def _():
        tail_ref[...] = conv0_ref[0]
        hc_ref[...] = h0_ref[0]

    x = xl_ref[...]
    xfull = jnp.concatenate([tail_ref[...], x], axis=0)
    cw = cw_ref[...]
    xc = cb_ref[...] + cw[CONV_W - 1:CONV_W] * x
    for d in range(1, CONV_W):
        shifted = pltpu.roll(xfull, d, 0)[SUBLANES:]
        xc = xc + cw[CONV_W - 1 - d:CONV_W - d] * shifted
    tail_ref[...] = x[ts - SUBLANES:]

    gates = jnp.dot(xc.astype(BF16), wg_ref[...], preferred_element_type=F32) + bg_ref[...]
    r = jax.nn.sigmoid(gates[:, :LRU_WIDTH])
    ig = jax.nn.sigmoid(gates[:, LRU_WIDTH:])
    log_a = -LRU_C * r * _softplus(-lam_ref[...])
    a = jnp.exp(log_a)
    mult = jnp.sqrt(jnp.maximum(-(a * a + 1.0) * jnp.tanh(log_a), 0.0))
    u = mult * ig * xc

    row = lax.broadcasted_iota(I32, a.shape, 0)
    d = 1
    while d < ts:
        if d < SUBLANES:
            keep = row >= d
            a_prev = jnp.where(keep, pltpu.roll(a, d, 0), 1.0)
            u_prev = jnp.where(keep, pltpu.roll(u, d, 0), 0.0)
            u = a * u_prev + u
            a = a * a_prev
        else:
            u = jnp.concatenate([u[:d], a[d:] * u[:ts - d] + u[d:]], axis=0)
            a = jnp.concatenate([a[:d], a[d:] * a[:ts - d]], axis=0)
        d *= 2
    h = a * hc_ref[...] + u
    h_end = h[ts - 1:ts]
    hc_ref[...] = h_end
    hlast_ref[0] = h_end
    y_ref[...] = (h * jax.nn.gelu(gl_ref[...], approximate=True)).astype(y_ref.dtype)


def _rglru(xl, gl, conv0, h0, conv_w, conv_b, w_gates, b_gates, lam, *, batch, seq, ts):
    assert seq % ts == 0 and ts % SUBLANES == 0
    ns = seq // ts
    tile = pl.BlockSpec((ts, LRU_WIDTH), lambda b, s: (b * ns + s, 0))
    const = lambda shape: pl.BlockSpec(shape, lambda b, s: (0,) * len(shape))
    return pl.pallas_call(
        functools.partial(_rglru_kernel, ts=ts),
        grid=(batch, ns),
        in_specs=[tile, tile,
                  pl.BlockSpec((1, SUBLANES, LRU_WIDTH), lambda b, s: (b, 0, 0)),
                  pl.BlockSpec((1, 1, LRU_WIDTH), lambda b, s: (b, 0, 0)),
                  const((CONV_W, LRU_WIDTH)), const((1, LRU_WIDTH)),
                  const((LRU_WIDTH, 2 * LRU_WIDTH)), const((1, 2 * LRU_WIDTH)),
                  const((1, LRU_WIDTH))],
        out_specs=[tile, pl.BlockSpec((1, 1, LRU_WIDTH), lambda b, s: (b, 0, 0))],
        out_shape=[jax.ShapeDtypeStruct((batch * seq, LRU_WIDTH), BF16 if ts > SUBLANES else F32),
                   jax.ShapeDtypeStruct((batch, 1, LRU_WIDTH), F32)],
        scratch_shapes=[pltpu.VMEM((SUBLANES, LRU_WIDTH), F32),
                        pltpu.VMEM((1, LRU_WIDTH), F32)],
        compiler_params=_cparams(2),
        name="rglru_prompt" if ts > SUBLANES else "rglru_sample",
    )(xl, gl, conv0, h0, conv_w, conv_b, w_gates, b_gates, lam)


def _route(logits):
    lane = lax.broadcasted_iota(I32, logits.shape, 1).astype(F32)
    big = float(ROUTE_LANES)
    neg = -jnp.inf
    is_g = lane < N_GROUPS
    gmax = jnp.max(jnp.where(is_g, logits, neg), axis=-1, keepdims=True)
    gsum = jnp.sum(jnp.where(is_g, jnp.exp(logits - gmax), 0.0), axis=-1, keepdims=True)
    gw = 1.0 / gsum
    gid = jnp.min(jnp.where(is_g & (logits == gmax), lane, big), axis=-1, keepdims=True)
    lo = N_GROUPS + EXPERTS_PER_GROUP * gid
    in_grp = (lane >= lo) & (lane < lo + EXPERTS_PER_GROUP)
    v1 = jnp.max(jnp.where(in_grp, logits, neg), axis=-1, keepdims=True)
    i1 = jnp.min(jnp.where(in_grp & (logits == v1), lane, big), axis=-1, keepdims=True)
    rest = in_grp & (lane != i1)
    v2 = jnp.max(jnp.where(rest, logits, neg), axis=-1, keepdims=True)
    i2 = jnp.min(jnp.where(rest & (logits == v2), lane, big), axis=-1, keepdims=True)
    t = jnp.exp(v2 - v1)
    w1 = gw / (1.0 + t)
    w2 = gw * t / (1.0 + t)
    info = jnp.where(lane == 0, i1 - N_GROUPS,
                     jnp.where(lane == 1, i2 - N_GROUPS,
                               jnp.where(lane == 2, w1, jnp.where(lane == 3, w2, 0.0))))
    return info


def _merge_kernel(xp_ref, xs_ref, op_ref, os_ref, yp_ref, ys_ref, wo_ref, g_ref, b_ref,
                  wr_ref, br_ref, x1_ref, x1t_ref, info_ref, *, n_prompt_tiles):
    i = pl.program_id(0)

    def body(x_ref, o_ref, y_ref):
        m = jnp.dot(o_ref[...].astype(BF16), wo_ref[:ATTN_WIDTH, :], preferred_element_type=F32)
        m = m + jnp.dot(y_ref[...].astype(BF16), wo_ref[ATTN_WIDTH:, :],
                        preferred_element_type=F32)
        x1 = _layer_norm(DEEPNORM_ALPHA * x_ref[...] + m, g_ref[...], b_ref[...])
        x1_ref[...] = x1
        _to_token_tiles(x1t_ref, x1)
        logits = jnp.dot(x1.astype(BF16), wr_ref[...],
                         preferred_element_type=F32) + br_ref[...]
        info_ref[...] = _route(logits)

    @pl.when(i < n_prompt_tiles)
    def _():
        body(xp_ref, op_ref, yp_ref)

    @pl.when(i >= n_prompt_tiles)
    def _():
        body(xs_ref, os_ref, ys_ref)


def _merge(xp, xs, o_p, o_s, y_p, y_s, wo_bf16, g1, b1, wr, br):
    tm = TOK_TM
    tp, ts = xp.shape[0], xs.shape[0]
    assert tp % tm == 0 and ts % tm == 0
    npt, nst = tp // tm, ts // tm
    t_all = tp + ts
    pidx = lambda i: (jnp.minimum(i, npt - 1), 0)
    sidx = lambda i: (jnp.maximum(i - npt, 0), 0)
    const = lambda shape: pl.BlockSpec(shape, lambda i: (0, 0))
    row = lambda i: (i, 0)
    return pl.pallas_call(
        functools.partial(_merge_kernel, n_prompt_tiles=npt),
        grid=(npt + nst,),
        in_specs=[pl.BlockSpec((tm, D_MODEL), pidx), pl.BlockSpec((tm, D_MODEL), sidx),
                  pl.BlockSpec((tm, ATTN_WIDTH), pidx), pl.BlockSpec((tm, ATTN_WIDTH), sidx),
                  pl.BlockSpec((tm, LRU_WIDTH), pidx), pl.BlockSpec((tm, LRU_WIDTH), sidx),
                  const((D_MODEL, D_MODEL)), const((1, D_MODEL)), const((1, D_MODEL)),
                  const((D_MODEL, ROUTE_LANES)), const((1, ROUTE_LANES))],
        out_specs=[pl.BlockSpec((tm, D_MODEL), row),
                   pl.BlockSpec((tm * TILE_ROWS, LANES), row),
                   pl.BlockSpec((tm, ROUTE_LANES), row)],
        out_shape=[jax.ShapeDtypeStruct((t_all, D_MODEL), F32),
                   jax.ShapeDtypeStruct((t_all * TILE_ROWS, LANES), F32),
                   jax.ShapeDtypeStruct((t_all, ROUTE_LANES), F32)],
        compiler_params=_cparams(1),
        name="merge_ln1_route",
    )(xp, xs, o_p, o_s, y_p, y_s, wo_bf16, g1, b1, wr, br)


def _tile_group(i, cum_ref):
    t = jnp.minimum(i, cum_ref[N_GROUPS - 1] - 1)
    g = jnp.int32(0)
    for k in range(N_GROUPS - 1):
        g = g + (cum_ref[k] <= t).astype(I32)
    return g


def _expert_kernel(cum_ref, src0_ref, gsrc_ref, sdst_ref,
                   x_hbm, info_ref, wg_ref, wu_ref, wd_ref, out_hbm,
                   xbuf, ybuf, xb_ref, yacc_ref, gsem, ssem):
    tm, ch, tr = MOE_TM, MOE_CHUNK, TILE_ROWS
    last_e = EXPERTS_PER_GROUP - 1
    i = pl.program_id(0)
    e = pl.program_id(1)
    slot = i % 2
    other = 1 - slot
    n_live = cum_ref[N_GROUPS - 1]

    def start_gather(idx_ref, n, base, sl):
        for r in range(n):
            src = pl.ds(pl.multiple_of(idx_ref[0, 0, r], tr), tr)
            dst = pl.ds(pl.multiple_of((base + r) * tr, tr), tr)
            pltpu.make_async_copy(x_hbm.at[src], xbuf.at[sl, dst],
                                  gsem.at[sl]).start(priority=r % 2)

    def start_scatter(idx_ref, n, base, sl):
        for r in range(n):
            src = pl.ds(pl.multiple_of((base + r) * tr, tr), tr)
            dst = pl.ds(pl.multiple_of(idx_ref[0, 0, r], tr), tr)
            pltpu.make_async_copy(ybuf.at[sl, src], out_hbm.at[dst],
                                  ssem.at[sl]).start(priority=r % 2)

    def wait_tile(buf, sem):
        pltpu.make_async_copy(x_hbm.at[pl.ds(0, tm * tr)], buf, sem).wait()

    def move_rows():
        start_gather(gsrc_ref, ch, e * ch, other)
        start_scatter(sdst_ref, ch, e * ch, other)

    @pl.when((i == 0) & (e == 0))
    def _():
        ybuf[1] = jnp.zeros(ybuf.shape[1:], F32)
        yacc_ref[...] = jnp.zeros(yacc_ref.shape, F32)
        start_gather(src0_ref, tm, 0, 0)

    @pl.when((e == 0) & (i <= n_live))
    def _():
        wait_tile(xbuf.at[slot], gsem.at[slot])

    @pl.when((e == last_e) & (i >= 1) & (i <= n_live))
    def _():
        wait_tile(ybuf.at[slot], ssem.at[slot])

    @pl.when(i == n_live)
    def _():
        move_rows()

    @pl.when((e == 0) & (i < n_live))
    def _():
        xb_ref[...] = _from_token_tiles(xbuf.at[slot], tm).astype(BF16)

    @pl.when(i < n_live)
    def _():
        move_rows()
        w_in = jnp.concatenate([wg_ref[0], wu_ref[0]], axis=1).astype(BF16)
        gu = jnp.dot(xb_ref[...], w_in, preferred_element_type=F32)
        gate = gu[:, :D_EXPERT]
        up = gu[:, D_EXPERT:]
        expert = (_tile_group(i, cum_ref) * EXPERTS_PER_GROUP + e).astype(F32)
        info = info_ref[...]
        comb = (jnp.where(info[:, 0:1] == expert, info[:, 2:3], 0.0)
                + jnp.where(info[:, 1:2] == expert, info[:, 3:4], 0.0))
        h = (gate * jax.nn.sigmoid(gate) * up * comb).astype(BF16)
        y = jnp.dot(h, wd_ref[0].astype(BF16), preferred_element_type=F32)
        yacc_ref[...] = y + jnp.where(e > 0, yacc_ref[...], 0.0)

    @pl.when((e == last_e) & (i < n_live))
    def _():
        _to_token_tiles(ybuf.at[slot], yacc_ref[...])

    @pl.when((e == last_e) & (i == n_live))
    def _():
        wait_tile(xbuf.at[other], gsem.at[other])
        wait_tile(ybuf.at[other], ssem.at[other])


def _experts(x1t, cum, src0, gsrc, sdst, info_sorted, wg, wu, wd, *, n_tiles, n_rows):
    tm, ch = MOE_TM, MOE_CHUNK
    ne = EXPERTS_PER_GROUP
    last = n_tiles - 1
    smem = lambda n, fn: pl.BlockSpec((1, 1, n), fn, memory_space=pltpu.SMEM)
    weight = lambda shape: pl.BlockSpec(
        shape, lambda i, e, cum: (_tile_group(i, cum) * ne + e, 0, 0))
    any_spec = pl.BlockSpec(memory_space=pl.ANY)
    grid_spec = pltpu.PrefetchScalarGridSpec(
        num_scalar_prefetch=1,
        grid=(n_tiles + 1, ne),
        in_specs=[smem(tm, lambda i, e, cum: (0, 0, 0)),
                  smem(ch, lambda i, e, cum: (jnp.minimum(i + 1, last) * ne + e, 0, 0)),
                  smem(ch, lambda i, e, cum: (i * ne + e, 0, 0)),
                  any_spec,
                  pl.BlockSpec((tm, ROUTE_LANES), lambda i, e, cum: (jnp.minimum(i, last), 0)),
                  weight((1, D_MODEL, D_EXPERT)), weight((1, D_MODEL, D_EXPERT)),
                  weight((1, D_EXPERT, D_MODEL))],
        out_specs=any_spec,
        scratch_shapes=[pltpu.VMEM((2, tm * TILE_ROWS, LANES), F32),
                        pltpu.VMEM((2, tm * TILE_ROWS, LANES), F32),
                        pltpu.VMEM((tm, D_MODEL), BF16),
                        pltpu.VMEM((tm, D_MODEL), F32),
                        pltpu.SemaphoreType.DMA((2,)),
                        pltpu.SemaphoreType.DMA((2,))])
    return pl.pallas_call(
        _expert_kernel,
        grid_spec=grid_spec,
        out_shape=jax.ShapeDtypeStruct((n_rows * TILE_ROWS, LANES), F32),
        compiler_params=_cparams(2),
        name="expert_mlp",
    )(cum, src0, gsrc, sdst, x1t, info_sorted, wg, wu, wd)


def _expert_plan(info, n_tiles):
    tm, ch = MOE_TM, MOE_CHUNK
    t_all = info.shape[0]
    gid = info[:, 0].astype(I32) // EXPERTS_PER_GROUP
    _, order = lax.sort((gid, lax.iota(I32, t_all)), num_keys=1, is_stable=True)
    groups = jnp.arange(N_GROUPS, dtype=I32)
    counts = jnp.sum((gid[:, None] == groups[None, :]).astype(I32), axis=0)
    start = jnp.cumsum(counts) - counts
    tiles_g = (counts + tm - 1) // tm
    cum = jnp.cumsum(tiles_g).astype(I32)
    n_live = cum[-1]
    ti = jnp.arange(n_tiles, dtype=I32)
    g_i = jnp.minimum(jnp.sum((cum[None, :] <= ti[:, None]).astype(I32), axis=1), N_GROUPS - 1)
    j = ti - (cum - tiles_g)[g_i]
    live = ti < n_live
    tile_rows = jnp.where(live, jnp.clip(counts[g_i] - j * tm, 0, tm), 0)
    tile_start = jnp.where(live, start[g_i] + j * tm, 0)
    r = jnp.arange(tm, dtype=I32)[None, :]
    valid = r < tile_rows[:, None]
    tok = order[jnp.minimum(tile_start[:, None] + r, t_all - 1)]
    src = (jnp.where(valid, tok, 0) * TILE_ROWS).astype(I32)
    dump = ((t_all + r) * TILE_ROWS).astype(I32)
    dst = (jnp.where(valid, tok, t_all + r) * TILE_ROWS).astype(I32)
    dst_prev = jnp.concatenate([dump, dst], axis=0)
    info_sorted = jnp.where(valid.reshape(-1, 1), info[tok.reshape(-1)], 0.0)
    return (cum, src[0].reshape(1, 1, tm), src.reshape(n_tiles * EXPERTS_PER_GROUP, 1, ch),
            dst_prev.reshape((n_tiles + 1) * EXPERTS_PER_GROUP, 1, ch), info_sorted)


def _combine_kernel(x1_ref, f_ref, g_ref, b_ref, outp_ref, outs_ref, *, n_prompt_tiles):
    tm = TOK_TM
    i = pl.program_id(0)
    f = _from_token_tiles(f_ref, tm)
    out = _layer_norm(DEEPNORM_ALPHA * x1_ref[...] + f, g_ref[...], b_ref[...])

    @pl.when(i < n_prompt_tiles)
    def _():
        outp_ref[...] = out

    @pl.when(i >= n_prompt_tiles)
    def _():
        outs_ref[...] = out


def _combine(x1, f_tiles, g2, b2, *, n_prompt, n_sample):
    tm = TOK_TM
    assert n_sample == tm
    npt = n_prompt // tm
    nt = npt + 1
    row = lambda i: (i, 0)
    const = lambda shape: pl.BlockSpec(shape, lambda i: (0, 0))
    return pl.pallas_call(
        functools.partial(_combine_kernel, n_prompt_tiles=npt),
        grid=(nt,),
        in_specs=[pl.BlockSpec((tm, D_MODEL), row),
                  pl.BlockSpec((tm * TILE_ROWS, LANES), row),
                  const((1, D_MODEL)), const((1, D_MODEL))],
        out_specs=[pl.BlockSpec((tm, D_MODEL), lambda i: (jnp.minimum(i, npt - 1), 0)),
                   pl.BlockSpec((tm, D_MODEL), lambda i: (0, 0))],
        out_shape=[jax.ShapeDtypeStruct((n_prompt, D_MODEL), F32),
                   jax.ShapeDtypeStruct((n_sample, D_MODEL), F32)],
        compiler_params=_cparams(1),
        name="combine_ln2",
    )(x1, f_tiles, g2, b2)


def _block_diag(w):
    nb, n, _ = w.shape
    eye = jnp.eye(nb, dtype=w.dtype)
    return (eye[:, None, :, None] * w[:, :, None, :]).reshape(nb * n, nb * n)


def kernel(x_prompt, x_sample, cache_k, cache_v, state_h, state_conv, page_table, w_in, lambda_q1, lambda_k1, lambda_q2, lambda_k2, subln_g, conv_w, conv_b, w_rg_a, b_rg_a, w_rg_x, b_rg_x, lru_lambda, w_out, ln1_g, ln1_b, w_group_router, b_group_router, w_expert_router, b_expert_router, w_gate_e, w_up_e, w_down_e, ln2_g, ln2_b):
    bp, sp, _ = x_prompt.shape
    bd, sd, _ = x_sample.shape
    assert w_in.shape[0] == DEPTH == 1
    l = 0
    n_prompt, n_sample = bp * sp, bd * sd
    t_all = n_prompt + n_sample

    xp = x_prompt.reshape(n_prompt, D_MODEL)
    xs = x_sample.reshape(n_sample, D_MODEL)
    w_in_b = w_in[l].astype(BF16)
    w_out_b = w_out[l].astype(BF16)
    lam_params = (lambda_q1[l][None], lambda_k1[l][None], lambda_q2[l][None], lambda_k2[l][None])
    w_gates = jnp.concatenate([_block_diag(w_rg_a[l]), _block_diag(w_rg_x[l])], axis=1).astype(BF16)
    b_gates = jnp.concatenate([b_rg_a[l].reshape(1, -1), b_rg_x[l].reshape(1, -1)], axis=1)
    lru_args = (conv_w[l], conv_b[l][None], w_gates, b_gates, lru_lambda[l][None])

    qt, kh, vt, k_p, v_p, xl_p, gl_p = _inproj(xp, w_in_b, prompt=True, tm=INPROJ_TM)
    o_p = _attn_prompt(qt, kh, vt, lam_params, subln_g[l][:, None], batch=bp, seq=sp)
    y_p, h_p = _rglru(xl_p, gl_p, jnp.zeros((bp, SUBLANES, LRU_WIDTH), F32),
                      jnp.zeros((bp, 1, LRU_WIDTH), F32), *lru_args, batch=bp, seq=sp, ts=LRU_TS)

    q_s, k_s, v_s, xl_s, gl_s = _inproj(xs, w_in_b, prompt=False, tm=n_sample)
    n_phys = cache_k.shape[1]
    ck = cache_k.reshape(DEPTH * n_phys * PAGE_SIZE * N_HEADS, V_DIM)
    cv = cache_v.reshape(DEPTH * n_phys * PAGE_SIZE * N_HEADS, V_DIM)
    o_s = _attn_sample(q_s, k_s, v_s, ck, cv, page_table, lam_params, subln_g[l][None],
                       dec_batch=bd, dec_seq=sd)
    conv0 = jnp.concatenate(
        [jnp.zeros((bd, SUBLANES - (CONV_W - 1), LRU_WIDTH), F32), state_conv[l]], axis=1)
    y_s, h_s = _rglru(xl_s, gl_s, conv0, state_h[l][:, None, :], *lru_args,
                      batch=bd, seq=sd, ts=sd)

    w_router = jnp.concatenate(
        [w_group_router[l], w_expert_router[l],
         jnp.zeros((D_MODEL, ROUTE_LANES - N_GROUPS - N_EXPERTS), F32)], axis=1).astype(BF16)
    b_router = jnp.concatenate(
        [b_group_router[l], b_expert_router[l],
         jnp.zeros((ROUTE_LANES - N_GROUPS - N_EXPERTS,), F32)])[None]
    x1, x1t, info = _merge(xp, xs, o_p, o_s, y_p, y_s, w_out_b, ln1_g[l][None], ln1_b[l][None],
                           w_router, b_router)
    n_tiles = -(-t_all // MOE_TM) + N_GROUPS
    cum, src0, gsrc, sdst, info_sorted = _expert_plan(info, n_tiles)
    f_tiles = _experts(x1t, cum, src0, gsrc, sdst, info_sorted,
                       w_gate_e[l], w_up_e[l], w_down_e[l],
                       n_tiles=n_tiles, n_rows=t_all + MOE_TM)
    out_p, out_s = _combine(x1, f_tiles, ln2_g[l][None], ln2_b[l][None],
                            n_prompt=n_prompt, n_sample=n_sample)

    conv_p = xl_p.reshape(bp, sp, LRU_WIDTH)[:, sp - (CONV_W - 1):]
    conv_s = jnp.concatenate([state_conv[l], xl_s.reshape(bd, sd, LRU_WIDTH)],
                             axis=1)[:, -(CONV_W - 1):]
    return (out_p.reshape(bp, sp, D_MODEL),
            out_s.reshape(bd, sd, D_MODEL),
            k_p.reshape(1, bp, sp, N_HEADS, V_DIM),
            v_p.reshape(1, bp, sp, N_HEADS, V_DIM),
            h_p.reshape(1, bp, LRU_WIDTH),
            conv_p[None],
            k_s.reshape(1, bd, sd, N_HEADS, V_DIM),
            v_s.reshape(1, bd, sd, N_HEADS, V_DIM),
            h_s.reshape(1, bd, LRU_WIDTH),
            conv_s[None])
```

```python
import functools
import math

import jax
import jax.numpy as jnp
from jax import lax
from jax.experimental import pallas as pl
from jax.experimental.pallas import tpu as pltpu

F32 = jnp.float32
BF16 = jnp.bfloat16
I32 = jnp.int32

D_MODEL = 1024
ATTN_WIDTH = 512
LRU_WIDTH = 512
N_HEADS = 4
V_DIM = 128
QK_DIM = 64
CONV_W = 4
LRU_C = 8.0
IN_WIDTH = 3 * ATTN_WIDTH + 2 * LRU_WIDTH
N_GROUPS = 4
EXPERTS_PER_GROUP = 8
N_EXPERTS = 32
D_EXPERT = 256
PAGE_SIZE = 128
LN_EPS = 1e-5
DEPTH = 1
DEEPNORM_ALPHA = (2 * DEPTH) ** 0.25
LAM_INIT = 0.8 - 0.6 * math.exp(-0.3 * 0)
QK_SCALE = QK_DIM ** -0.5 * math.log2(math.e)

SUBLANES = 8
LANES = 128
VMEM_LIMIT_BYTES = 48 * 1024 * 1024

INPROJ_TM = 512
ATTN_TQ = 512
ATTN_TK = 512
ATTN_CB = 256
ATTN_HEADS_PER_STEP = 2
LRU_TS = 512
TOK_TM = 256
MOE_TM = 1024
MOE_CHUNK = MOE_TM // EXPERTS_PER_GROUP
PAGES_PER_STEP = 8
ROUTE_LANES = 128
TILE_ROWS = D_MODEL // LANES


def _cparams(n_axes):
    return pltpu.CompilerParams(
        dimension_semantics=("arbitrary",) * n_axes,
        vmem_limit_bytes=VMEM_LIMIT_BYTES)


def _layer_norm(x, g, b):
    mu = jnp.mean(x, axis=-1, keepdims=True)
    xc = x - mu
    var = jnp.mean(xc * xc, axis=-1, keepdims=True)
    return xc * lax.rsqrt(var + LN_EPS) * g + b


def _diff_lambda(lq1, lk1, lq2, lk2):
    s1 = jnp.sum(lq1 * lk1, axis=-1, keepdims=True)
    s2 = jnp.sum(lq2 * lk2, axis=-1, keepdims=True)
    return jnp.exp(s1) - jnp.exp(s2) + LAM_INIT


def _to_token_tiles(ref, x):
    tm = x.shape[0]
    for s in range(TILE_ROWS):
        ref[pl.ds(s, tm, stride=TILE_ROWS), :] = x[:, s * LANES:(s + 1) * LANES]


def _from_token_tiles(ref, tm):
    return jnp.concatenate(
        [ref[pl.ds(s, tm, stride=TILE_ROWS), :] for s in range(TILE_ROWS)], axis=1)


def _inproj_kernel(x_ref, w_ref, *out_refs, prompt):
    x = x_ref[...].astype(BF16)
    tm = x.shape[0]

    def section(idx, width):
        return jnp.dot(x, w_ref[:, idx:idx + width], preferred_element_type=F32)

    q = section(0, ATTN_WIDTH) * QK_SCALE
    k = section(ATTN_WIDTH, ATTN_WIDTH)
    v = section(2 * ATTN_WIDTH, ATTN_WIDTH)
    if prompt:
        qt_ref, kh_ref, vt_ref, k_ref, v_ref, xl_ref, gl_ref = out_refs
        chan = lax.broadcasted_iota(I32, (V_DIM, tm), 0)
        for h in range(N_HEADS):
            sl = slice(h * V_DIM, (h + 1) * V_DIM)
            qt = q[:, sl].T
            qt_ref[h, 0] = jnp.where(chan < QK_DIM, qt, 0.0).astype(BF16)
            qt_ref[h, 1] = jnp.where(chan >= QK_DIM, qt, 0.0).astype(BF16)
            kh_ref[h] = k[:, sl].astype(BF16)
            vt_ref[h, 0] = v[:, sl].T.astype(BF16)
    else:
        q_ref, k_ref, v_ref, xl_ref, gl_ref = out_refs
        q_ref[...] = q
    for h in range(N_HEADS):
        sl = slice(h * V_DIM, (h + 1) * V_DIM)
        k_ref[pl.ds(h, tm, stride=N_HEADS), :] = k[:, sl]
        v_ref[pl.ds(h, tm, stride=N_HEADS), :] = v[:, sl]
    xl_ref[...] = section(3 * ATTN_WIDTH, LRU_WIDTH)
    gl_ref[...] = section(3 * ATTN_WIDTH + LRU_WIDTH, LRU_WIDTH)


def _inproj(x2d, w_bf16, *, prompt, tm):
    t = x2d.shape[0]
    assert t % tm == 0
    row = lambda i: (i, 0)
    spec512 = pl.BlockSpec((tm, ATTN_WIDTH), row)
    head_rows = pl.BlockSpec((tm * N_HEADS, V_DIM), row)
    f32_512 = jax.ShapeDtypeStruct((t, ATTN_WIDTH), F32)
    f32_heads = jax.ShapeDtypeStruct((t * N_HEADS, V_DIM), F32)
    if prompt:
        assert tm == ATTN_TK
        out_shape = [jax.ShapeDtypeStruct((N_HEADS, 2, V_DIM, t), BF16),
                     jax.ShapeDtypeStruct((N_HEADS, t, V_DIM), BF16),
                     jax.ShapeDtypeStruct((N_HEADS, t // tm, V_DIM, tm), BF16)]
        out_specs = [pl.BlockSpec((N_HEADS, 2, V_DIM, tm), lambda i: (0, 0, 0, i)),
                     pl.BlockSpec((N_HEADS, tm, V_DIM), lambda i: (0, i, 0)),
                     pl.BlockSpec((N_HEADS, 1, V_DIM, tm), lambda i: (0, i, 0, 0))]
    else:
        out_shape = [f32_512]
        out_specs = [spec512]
    out_shape += [f32_heads, f32_heads, f32_512, f32_512]
    out_specs += [head_rows, head_rows, spec512, spec512]
    return pl.pallas_call(
        functools.partial(_inproj_kernel, prompt=prompt),
        grid=(t // tm,),
        in_specs=[pl.BlockSpec((tm, D_MODEL), row),
                  pl.BlockSpec((D_MODEL, IN_WIDTH), lambda i: (0, 0))],
        out_specs=out_specs,
        out_shape=out_shape,
        compiler_params=_cparams(1),
        name="inproj_prompt" if prompt else "inproj_sample",
    )(x2d, w_bf16)


def _attn_prompt_kernel(qt_ref, k_ref, vt_ref, lq1_ref, lk1_ref, lq2_ref, lk2_ref, g_ref,
                        o_ref, s_ref, p_ref, a_ref, m_ref, l_ref, acc_ref):
    tq, tk, cb = ATTN_TQ, ATTN_TK, ATTN_CB
    heads = range(ATTN_HEADS_PER_STEP)
    n_cols = 2 * tq
    i = pl.program_id(2)

    m_ref[...] = jnp.full(m_ref.shape, -jnp.inf, F32)
    l_ref[...] = jnp.zeros(l_ref.shape, F32)
    acc_ref[...] = jnp.zeros(acc_ref.shape, F32)
    for hd in heads:
        p_ref[hd, 1] = jnp.zeros(p_ref.shape[2:], BF16)
        a_ref[hd, 1] = jnp.ones(a_ref.shape[2:], F32)

    def scores(j, slot):
        for hd in heads:
            k = k_ref[hd, pl.ds(pl.multiple_of(j * tk, tk), tk), :]
            for mp in range(2):
                s_ref[hd, slot, :, mp * tq:(mp + 1) * tq] = jnp.dot(
                    k, qt_ref[hd, mp], preferred_element_type=F32)

    def softmax(slot, masked):
        for hd in heads:
            for c in range(n_cols // cb):
                off = (c * cb) % tq
                cs = slice(c * cb, (c + 1) * cb)
                s = s_ref[hd, slot, :, cs]
                if masked:
                    key = lax.broadcasted_iota(I32, s.shape, 0)
                    qpos = lax.broadcasted_iota(I32, s.shape, 1) + off
                    s = jnp.where(key <= qpos, s, -jnp.inf)
                m_prev = m_ref[hd, :, cs]
                m_new = jnp.maximum(m_prev, jnp.max(s, axis=0, keepdims=True))
                alpha = jnp.exp2(m_prev - m_new)
                p = jnp.exp2(s - m_new)
                l_ref[hd, :, cs] = alpha * l_ref[hd, :, cs] + jnp.sum(p, axis=0, keepdims=True)
                m_ref[hd, :, cs] = m_new
                p_ref[hd, slot, :, cs] = p.astype(BF16)
                a_ref[hd, slot, :, cs] = alpha

    def accumulate(j, slot):
        for hd in heads:
            acc_ref[hd] = a_ref[hd, slot] * acc_ref[hd] + jnp.dot(
                vt_ref[hd, j], p_ref[hd, slot], preferred_element_type=F32)

    def stage(j, cur):
        scores(j + 1, 1 - cur)
        softmax(cur, masked=False)
        accumulate(jnp.maximum(j - 1, 0), 1 - cur)

    def diagonal(cur):
        accumulate(jnp.maximum(i - 1, 0), 1 - cur)
        softmax(cur, masked=True)
        accumulate(i, cur)

    scores(0, 0)

    def body(t, carry):
        stage(2 * t, 0)
        stage(2 * t + 1, 1)
        return carry

    lax.fori_loop(0, i // 2, body, 0)

    @pl.when(i % 2 == 0)
    def _():
        diagonal(0)

    @pl.when(i % 2 == 1)
    def _():
        stage(i - 1, 0)
        diagonal(1)

    lam = _diff_lambda(lq1_ref[...], lk1_ref[...], lq2_ref[...], lk2_ref[...])
    for hd in heads:
        o_all = acc_ref[hd] / l_ref[hd]
        o = o_all[:, :tq] - lam * o_all[:, tq:]
        ms = jnp.mean(o * o, axis=0, keepdims=True)
        o = o * lax.rsqrt(ms + LN_EPS) * g_ref[...] * (1.0 - LAM_INIT)
        o_ref[:, hd * V_DIM:(hd + 1) * V_DIM] = o.T.astype(o_ref.dtype)


def _attn_prompt(qt, kh, vt, lam_params, subln_g_col, *, batch, seq):
    assert ATTN_TQ == ATTN_TK and seq % ATTN_TQ == 0 and ATTN_TQ % ATTN_CB == 0
    hp = ATTN_HEADS_PER_STEP
    assert N_HEADS % hp == 0
    nq = seq // ATTN_TQ
    t = batch * seq
    vec = lambda n: pl.BlockSpec((1, n), lambda b, h, i: (0, 0))
    return pl.pallas_call(
        _attn_prompt_kernel,
        grid=(batch, N_HEADS // hp, nq),
        in_specs=[pl.BlockSpec((hp, 2, V_DIM, ATTN_TQ), lambda b, h, i: (h, 0, 0, b * nq + i)),
                  pl.BlockSpec((hp, seq, V_DIM), lambda b, h, i: (h, b, 0)),
                  pl.BlockSpec((hp, nq, V_DIM, ATTN_TK), lambda b, h, i: (h, b, 0, 0)),
                  vec(QK_DIM), vec(QK_DIM), vec(QK_DIM), vec(QK_DIM),
                  pl.BlockSpec((V_DIM, 1), lambda b, h, i: (0, 0))],
        out_specs=pl.BlockSpec((ATTN_TQ, hp * V_DIM), lambda b, h, i: (b * nq + i, h)),
        out_shape=jax.ShapeDtypeStruct((t, ATTN_WIDTH), BF16),
        scratch_shapes=[pltpu.VMEM((hp, 2, ATTN_TK, 2 * ATTN_TQ), F32),
                        pltpu.VMEM((hp, 2, ATTN_TK, 2 * ATTN_TQ), BF16),
                        pltpu.VMEM((hp, 2, 1, 2 * ATTN_TQ), F32),
                        pltpu.VMEM((hp, 1, 2 * ATTN_TQ), F32),
                        pltpu.VMEM((hp, 1, 2 * ATTN_TQ), F32),
                        pltpu.VMEM((hp, V_DIM, 2 * ATTN_TQ), F32)],
        compiler_params=_cparams(3),
        name="attn_prompt",
    )(qt, kh, vt, *lam_params, subln_g_col)


def _attn_sample_kernel(pt_ref, q0_ref, q1_ref, kn0_ref, kn1_ref, vn0_ref, vn1_ref,
                        ck_hbm, cv_hbm, lq1_ref, lk1_ref, lq2_ref, lk2_ref, g_ref,
                        o0_ref, o1_ref, kbuf, vbuf, ksem, vsem, bias_ref, m_ref, l_ref, acc_ref,
                        *, n_chunks, n_steps, dec_seq):
    gpp = PAGES_PER_STEP
    page_rows = PAGE_SIZE * N_HEADS
    hrows = 2 * dec_seq
    g = pl.program_id(0)
    c = g % n_chunks
    slot = g % 2
    streams = ((q0_ref, kn0_ref, vn0_ref, o0_ref), (q1_ref, kn1_ref, vn1_ref, o1_ref))

    def page_copies(step, sl):
        copies = []
        for s in range(len(streams)):
            buf = 2 * s + sl
            for p in range(gpp):
                page = pt_ref[(s * n_steps + step) * gpp + p]
                src = pl.ds(pl.multiple_of(page * page_rows, page_rows), page_rows)
                dst = pl.ds(p * page_rows, page_rows)
                copies.append(pltpu.make_async_copy(ck_hbm.at[src], kbuf.at[buf, dst],
                                                    ksem.at[buf]))
                copies.append(pltpu.make_async_copy(cv_hbm.at[src], vbuf.at[buf, dst],
                                                    vsem.at[buf]))
        return copies

    @pl.when(g == 0)
    def _():
        for cp in page_copies(0, 0):
            cp.start()

    @pl.when(g + 1 < n_steps)
    def _():
        for cp in page_copies(g + 1, 1 - slot):
            cp.start()

    def head_bias(shape):
        row_head = lax.broadcasted_iota(I32, shape, 0) // hrows
        col_head = lax.broadcasted_iota(I32, shape, 1) % N_HEADS
        return row_head == col_head

    @pl.when(g == 0)
    def _():
        bias_ref[...] = jnp.where(head_bias(bias_ref.shape), 0.0, -jnp.inf)

    @pl.when(c == 0)
    def _():
        m_ref[...] = jnp.full(m_ref.shape, -jnp.inf, F32)
        l_ref[...] = jnp.zeros(l_ref.shape, F32)
        acc_ref[...] = jnp.zeros(acc_ref.shape, F32)

    lane = lax.broadcasted_iota(I32, (dec_seq, V_DIM), 1)

    def query_block(q_ref):
        q = q_ref[...]
        q_rows = []
        for h in range(N_HEADS):
            qh = q[:, h * V_DIM:(h + 1) * V_DIM]
            q_rows += [jnp.where(lane < QK_DIM, qh, 0.0), jnp.where(lane >= QK_DIM, qh, 0.0)]
        return jnp.concatenate(q_rows, axis=0).astype(BF16)

    def update(s_idx, qblk, k, v, bias):
        s = lax.dot_general(qblk, k, (((1,), (1,)), ((), ())),
                            preferred_element_type=F32) + bias
        m_prev = m_ref[s_idx]
        m_new = jnp.maximum(m_prev, jnp.max(s, axis=-1, keepdims=True))
        alpha = jnp.exp2(m_prev - m_new)
        p = jnp.exp2(s - m_new)
        l_ref[s_idx] = alpha * l_ref[s_idx] + jnp.sum(p, axis=-1, keepdims=True)
        acc_ref[s_idx] = alpha * acc_ref[s_idx] + jnp.dot(p.astype(BF16), v,
                                                          preferred_element_type=F32)
        m_ref[s_idx] = m_new

    for cp in page_copies(g, slot):
        cp.wait()
    qblks = [query_block(refs[0]) for refs in streams]
    for s_idx in range(len(streams)):
        buf = 2 * s_idx + slot
        update(s_idx, qblks[s_idx], kbuf[buf].astype(BF16), vbuf[buf].astype(BF16),
               bias_ref[...])

    @pl.when(c == n_chunks - 1)
    def _():
        new_rows = dec_seq * N_HEADS
        pad = jnp.zeros((LANES - new_rows, V_DIM), F32)
        shape = (N_HEADS * hrows, LANES)
        qi = lax.broadcasted_iota(I32, shape, 0) % dec_seq
        kj = lax.broadcasted_iota(I32, shape, 1) // N_HEADS
        new_bias = jnp.where(head_bias(shape) & (kj <= qi), 0.0, -jnp.inf)
        lam = _diff_lambda(lq1_ref[...], lk1_ref[...], lq2_ref[...], lk2_ref[...])
        for s_idx, (_, kn_ref, vn_ref, o_ref) in enumerate(streams):
            kn = jnp.concatenate([kn_ref[...], pad], axis=0).astype(BF16)
            vn = jnp.concatenate([vn_ref[...], pad], axis=0).astype(BF16)
            update(s_idx, qblks[s_idx], kn, vn, new_bias)
            o_heads = acc_ref[s_idx] / l_ref[s_idx]
            for h in range(N_HEADS):
                o_all = o_heads[h * hrows:(h + 1) * hrows]
                o = o_all[:dec_seq] - lam * o_all[dec_seq:]
                ms = jnp.mean(o * o, axis=-1, keepdims=True)
                o_ref[:, h * V_DIM:(h + 1) * V_DIM] = (
                    o * lax.rsqrt(ms + LN_EPS) * g_ref[...] * (1.0 - LAM_INIT))


def _attn_sample(q, k_new, v_new, cache_k_rows, cache_v_rows, page_table, lam_params, subln_g,
                 *, dec_batch, dec_seq):
    n_pages = page_table.shape[1]
    assert n_pages % PAGES_PER_STEP == 0 and dec_seq % SUBLANES == 0
    n_chunks = n_pages // PAGES_PER_STEP
    n_streams = 2
    assert dec_batch % n_streams == 0
    half = dec_batch // n_streams
    n_steps = half * n_chunks
    rows = 2 * N_HEADS * dec_seq
    chunk_rows = PAGES_PER_STEP * PAGE_SIZE * N_HEADS

    def stream_specs(s):
        tok = pl.BlockSpec((dec_seq, ATTN_WIDTH), lambda g, pt: (s * half + g // n_chunks, 0))
        tok_heads = pl.BlockSpec((dec_seq * N_HEADS, V_DIM),
                                 lambda g, pt: (s * half + g // n_chunks, 0))
        return tok, tok_heads

    (tok0, heads0), (tok1, heads1) = stream_specs(0), stream_specs(1)
    out_spec = pl.BlockSpec((dec_seq, ATTN_WIDTH), lambda g, pt: (g // n_chunks, 0))
    vec = lambda n: pl.BlockSpec((1, n), lambda g, pt: (0, 0))
    any_spec = pl.BlockSpec(memory_space=pl.ANY)
    n_bufs = 2 * n_streams
    grid_spec = pltpu.PrefetchScalarGridSpec(
        num_scalar_prefetch=1,
        grid=(n_steps,),
        in_specs=[tok0, tok1, heads0, heads1, heads0, heads1, any_spec, any_spec,
                  vec(QK_DIM), vec(QK_DIM), vec(QK_DIM), vec(QK_DIM), vec(V_DIM)],
        out_specs=[out_spec, out_spec],
        scratch_shapes=[pltpu.VMEM((n_bufs, chunk_rows, V_DIM), F32),
                        pltpu.VMEM((n_bufs, chunk_rows, V_DIM), F32),
                        pltpu.SemaphoreType.DMA((n_bufs,)),
                        pltpu.SemaphoreType.DMA((n_bufs,)),
                        pltpu.VMEM((rows, chunk_rows), F32),
                        pltpu.VMEM((n_streams, rows, 1), F32),
                        pltpu.VMEM((n_streams, rows, 1), F32),
                        pltpu.VMEM((n_streams, rows, V_DIM), F32)])
    half_out = jax.ShapeDtypeStruct((half * dec_seq, ATTN_WIDTH), F32)
    o0, o1 = pl.pallas_call(
        functools.partial(_attn_sample_kernel, n_chunks=n_chunks, n_steps=n_steps,
                          dec_seq=dec_seq),
        grid_spec=grid_spec,
        out_shape=[half_out, half_out],
        compiler_params=_cparams(1),
        name="attn_sample",
    )(page_table.reshape(-1), q, q, k_new, k_new, v_new, v_new, cache_k_rows, cache_v_rows,
      *lam_params, subln_g)
    return jnp.concatenate([o0, o1], axis=0)


def _softplus(x):
    y = jnp.exp(-jnp.abs(x))
    w = 1.0 + y
    log1p = jnp.where(w == 1.0, y, jnp.log(w) * (y / (w - 1.0)))
    return jnp.maximum(x, 0.0) + log1p


def _rglru_kernel(xl_ref, gl_ref, conv0_ref, h0_ref, cw_ref, cb_ref, wg_ref, bg_ref, lam_ref,
                  y_ref, hlast_ref, tail_ref, hc_ref, *, ts):
    s = pl.program_id(1)

    @pl.when(s == 0)
    def _():
        tail_ref[...] = conv0_ref[0]
        hc_ref[...] = h0_ref[0]

    x = xl_ref[...]
    xfull = jnp.concatenate([tail_ref[...], x], axis=0)
    cw = cw_ref[...]
    xc = cb_ref[...] + cw[CONV_W - 1:CONV_W] * x
    for d in range(1, CONV_W):
        shifted = pltpu.roll(xfull, d, 0)[SUBLANES:]
        xc = xc + cw[CONV_W - 1 - d:CONV_W - d] * shifted
    tail_ref[...] = x[ts - SUBLANES:]

    gates = jnp.dot(xc.astype(BF16), wg_ref[...], preferred_element_type=F32) + bg_ref[...]
    r = jax.nn.sigmoid(gates[:, :LRU_WIDTH])
    ig = jax.nn.sigmoid(gates[:, LRU_WIDTH:])
    log_a = -LRU_C * r * _softplus(-lam_ref[...])
    a = jnp.exp(log_a)
    mult = jnp.sqrt(jnp.maximum(-(a * a + 1.0) * jnp.tanh(log_a), 0.0))
    u = mult * ig * xc

    row = lax.broadcasted_iota(I32, a.shape, 0)
    d = 1
    while d < ts:
        if d < SUBLANES:
            keep = row >= d
            a_prev = jnp.where(keep, pltpu.roll(a, d, 0), 1.0)
            u_prev = jnp.where(keep, pltpu.roll(u, d, 0), 0.0)
            u = a * u_prev + u
            a = a * a_prev
        else:
            u = jnp.concatenate([u[:d], a[d:] * u[:ts - d] + u[d:]], axis=0)
            a = jnp.concatenate([a[:d], a[d:] * a[:ts - d]], axis=0)
        d *= 2
    h = a * hc_ref[...] + u
    h_end = h[ts - 1:ts]
    hc_ref[...] = h_end
    hlast_ref[0] = h_end
    y_ref[...] = (h * jax.nn.gelu(gl_ref[...], approximate=True)).astype(y_ref.dtype)


def _rglru(xl, gl, conv0, h0, conv_w, conv_b, w_gates, b_gates, lam, *, batch, seq, ts):
    assert seq % ts == 0 and ts % SUBLANES == 0
    ns = seq // ts
    tile = pl.BlockSpec((ts, LRU_WIDTH), lambda b, s: (b * ns + s, 0))
    const = lambda shape: pl.BlockSpec(shape, lambda b, s: (0,) * len(shape))
    return pl.pallas_call(
        functools.partial(_rglru_kernel, ts=ts),
        grid=(batch, ns),
        in_specs=[tile, tile,
                  pl.BlockSpec((1, SUBLANES, LRU_WIDTH), lambda b, s: (b, 0, 0)),
                  pl.BlockSpec((1, 1, LRU_WIDTH), lambda b, s: (b, 0, 0)),
                  const((CONV_W, LRU_WIDTH)), const((1, LRU_WIDTH)),
                  const((LRU_WIDTH, 2 * LRU_WIDTH)), const((1, 2 * LRU_WIDTH)),
                  const((1, LRU_WIDTH))],
        out_specs=[tile, pl.BlockSpec((1, 1, LRU_WIDTH), lambda b, s: (b, 0, 0))],
        out_shape=[jax.ShapeDtypeStruct((batch * seq, LRU_WIDTH), BF16 if ts > SUBLANES else F32),
                   jax.ShapeDtypeStruct((batch, 1, LRU_WIDTH), F32)],
        scratch_shapes=[pltpu.VMEM((SUBLANES, LRU_WIDTH), F32),
                        pltpu.VMEM((1, LRU_WIDTH), F32)],
        compiler_params=_cparams(2),
        name="rglru_prompt" if ts > SUBLANES else "rglru_sample",
    )(xl, gl, conv0, h0, conv_w, conv_b, w_gates, b_gates, lam)


def _route(logits):
    lane = lax.broadcasted_iota(I32, logits.shape, 1).astype(F32)
    big = float(ROUTE_LANES)
    neg = -jnp.inf
    is_g = lane < N_GROUPS
    gmax = jnp.max(jnp.where(is_g, logits, neg), axis=-1, keepdims=True)
    gsum = jnp.sum(jnp.where(is_g, jnp.exp(logits - gmax), 0.0), axis=-1, keepdims=True)
    gw = 1.0 / gsum
    gid = jnp.min(jnp.where(is_g & (logits == gmax), lane, big), axis=-1, keepdims=True)
    lo = N_GROUPS + EXPERTS_PER_GROUP * gid
    in_grp = (lane >= lo) & (lane < lo + EXPERTS_PER_GROUP)
    v1 = jnp.max(jnp.where(in_grp, logits, neg), axis=-1, keepdims=True)
    i1 = jnp.min(jnp.where(in_grp & (logits == v1), lane, big), axis=-1, keepdims=True)
    rest = in_grp & (lane != i1)
    v2 = jnp.max(jnp.where(rest, logits, neg), axis=-1, keepdims=True)
    i2 = jnp.min(jnp.where(rest & (logits == v2), lane, big), axis=-1, keepdims=True)
    t = jnp.exp(v2 - v1)
    w1 = gw / (1.0 + t)
    w2 = gw * t / (1.0 + t)
    info = jnp.where(lane == 0, i1 - N_GROUPS,
                     jnp.where(lane == 1, i2 - N_GROUPS,
                               jnp.where(lane == 2, w1, jnp.where(lane == 3, w2, 0.0))))
    return info


def _merge_kernel(xp_ref, xs_ref, op_ref, os_ref, yp_ref, ys_ref, wo_ref, g_ref, b_ref,
                  wr_ref, br_ref, x1_ref, x1t_ref, info_ref, *, n_prompt_tiles):
    i = pl.program_id(0)

    def body(x_ref, o_ref, y_ref):
        m = jnp.dot(o_ref[...].astype(BF16), wo_ref[:ATTN_WIDTH, :], preferred_element_type=F32)
        m = m + jnp.dot(y_ref[...].astype(BF16), wo_ref[ATTN_WIDTH:, :],
                        preferred_element_type=F32)
        x1 = _layer_norm(DEEPNORM_ALPHA * x_ref[...] + m, g_ref[...], b_ref[...])
        x1_ref[...] = x1
        _to_token_tiles(x1t_ref, x1)
        logits = jnp.dot(x1.astype(BF16), wr_ref[...],
                         preferred_element_type=F32) + br_ref[...]
        info_ref[...] = _route(logits)

    @pl.when(i < n_prompt_tiles)
    def _():
        body(xp_ref, op_ref, yp_ref)

    @pl.when(i >= n_prompt_tiles)
    def _():
        body(xs_ref, os_ref, ys_ref)


def _merge(xp, xs, o_p, o_s, y_p, y_s, wo_bf16, g1, b1, wr, br):
    tm = TOK_TM
    tp, ts = xp.shape[0], xs.shape[0]
    assert tp % tm == 0 and ts % tm == 0
    npt, nst = tp // tm, ts // tm
    t_all = tp + ts
    pidx = lambda i: (jnp.minimum(i, npt - 1), 0)
    sidx = lambda i: (jnp.maximum(i - npt, 0), 0)
    const = lambda shape: pl.BlockSpec(shape, lambda i: (0, 0))
    row = lambda i: (i, 0)
    return pl.pallas_call(
        functools.partial(_merge_kernel, n_prompt_tiles=npt),
        grid=(npt + nst,),
        in_specs=[pl.BlockSpec((tm, D_MODEL), pidx), pl.BlockSpec((tm, D_MODEL), sidx),
                  pl.BlockSpec((tm, ATTN_WIDTH), pidx), pl.BlockSpec((tm, ATTN_WIDTH), sidx),
                  pl.BlockSpec((tm, LRU_WIDTH), pidx), pl.BlockSpec((tm, LRU_WIDTH), sidx),
                  const((D_MODEL, D_MODEL)), const((1, D_MODEL)), const((1, D_MODEL)),
                  const((D_MODEL, ROUTE_LANES)), const((1, ROUTE_LANES))],
        out_specs=[pl.BlockSpec((tm, D_MODEL), row),
                   pl.BlockSpec((tm * TILE_ROWS, LANES), row),
                   pl.BlockSpec((tm, ROUTE_LANES), row)],
        out_shape=[jax.ShapeDtypeStruct((t_all, D_MODEL), F32),
                   jax.ShapeDtypeStruct((t_all * TILE_ROWS, LANES), F32),
                   jax.ShapeDtypeStruct((t_all, ROUTE_LANES), F32)],
        compiler_params=_cparams(1),
        name="merge_ln1_route",
    )(xp, xs, o_p, o_s, y_p, y_s, wo_bf16, g1, b1, wr, br)


def _tile_group(i, cum_ref):
    t = jnp.minimum(i, cum_ref[N_GROUPS - 1] - 1)
    g = jnp.int32(0)
    for k in range(N_GROUPS - 1):
        g = g + (cum_ref[k] <= t).astype(I32)
    return g


def _expert_kernel(cum_ref, src0_ref, gsrc_ref, sdst_ref,
                   x_hbm, info_ref, wg_ref, wu_ref, wd_ref, out_hbm,
                   xbuf, ybuf, xb_ref, yacc_ref, gsem, ssem):
    tm, ch, tr = MOE_TM, MOE_CHUNK, TILE_ROWS
    last_e = EXPERTS_PER_GROUP - 1
    i = pl.program_id(0)
    e = pl.program_id(1)
    slot = i % 2
    other = 1 - slot
    n_live = cum_ref[N_GROUPS - 1]

    def start_gather(idx_ref, rows, base, sl):
        for r in rows:
            src = pl.ds(pl.multiple_of(idx_ref[0, 0, r], tr), tr)
            dst = pl.ds(pl.multiple_of((base + r) * tr, tr), tr)
            pltpu.make_async_copy(x_hbm.at[src], xbuf.at[sl, dst],
                                  gsem.at[sl]).start(priority=r % 2)

    def start_scatter(idx_ref, rows, base, sl):
        for r in rows:
            src = pl.ds(pl.multiple_of((base + r) * tr, tr), tr)
            dst = pl.ds(pl.multiple_of(idx_ref[0, 0, r], tr), tr)
            pltpu.make_async_copy(ybuf.at[sl, src], out_hbm.at[dst],
                                  ssem.at[sl]).start(priority=r % 2)

    def wait_tile(buf, sem):
        pltpu.make_async_copy(x_hbm.at[pl.ds(0, tm * tr)], buf, sem).wait()

    def move_rows():
        start_gather(gsrc_ref, range(ch), e * ch, other)
        start_scatter(sdst_ref, range(ch), e * ch, other)

    @pl.when((i == 0) & (e == 0))
    def _():
        ybuf[1] = jnp.zeros(ybuf.shape[1:], F32)
        yacc_ref[...] = jnp.zeros(yacc_ref.shape, F32)
        start_gather(src0_ref, range(tm), 0, 0)

    @pl.when((e == 0) & (i <= n_live))
    def _():
        wait_tile(xbuf.at[slot], gsem.at[slot])

    @pl.when((e == last_e) & (i >= 1) & (i <= n_live))
    def _():
        wait_tile(ybuf.at[slot], ssem.at[slot])

    @pl.when(i == n_live)
    def _():
        move_rows()

    @pl.when((e == 0) & (i < n_live))
    def _():
        xb_ref[...] = _from_token_tiles(xbuf.at[slot], tm).astype(BF16)

    @pl.when(i < n_live)
    def _():
        move_rows()
        w_in = jnp.concatenate([wg_ref[0], wu_ref[0]], axis=1).astype(BF16)
        gu = jnp.dot(xb_ref[...], w_in, preferred_element_type=F32)
        gate = gu[:, :D_EXPERT]
        up = gu[:, D_EXPERT:]
        expert = (_tile_group(i, cum_ref) * EXPERTS_PER_GROUP + e).astype(F32)
        info = info_ref[0].T
        comb = (jnp.where(info[:, 0:1] == expert, info[:, 2:3], 0.0)
                + jnp.where(info[:, 1:2] == expert, info[:, 3:4], 0.0))
        h = (gate * jax.nn.sigmoid(gate) * up * comb).astype(BF16)
        y = jnp.dot(h, wd_ref[0].astype(BF16), preferred_element_type=F32)
        yacc_ref[...] = y + jnp.where(e > 0, yacc_ref[...], 0.0)

    @pl.when((e == last_e) & (i < n_live))
    def _():
        _to_token_tiles(ybuf.at[slot], yacc_ref[...])

    @pl.when((e == last_e) & (i == n_live))
    def _():
        wait_tile(xbuf.at[other], gsem.at[other])
        wait_tile(ybuf.at[other], ssem.at[other])


def _experts(x1t, cum, src0, gsrc, sdst, info_sorted, wg, wu, wd, *, n_tiles, n_rows):
    tm, ch = MOE_TM, MOE_CHUNK
    ne = EXPERTS_PER_GROUP
    last = n_tiles - 1
    smem = lambda n, fn: pl.BlockSpec((1, 1, n), fn, memory_space=pltpu.SMEM)
    weight = lambda shape: pl.BlockSpec(
        shape, lambda i, e, cum: (_tile_group(i, cum) * ne + e, 0, 0))
    any_spec = pl.BlockSpec(memory_space=pl.ANY)
    grid_spec = pltpu.PrefetchScalarGridSpec(
        num_scalar_prefetch=1,
        grid=(n_tiles + 1, ne),
        in_specs=[smem(tm, lambda i, e, cum: (0, 0, 0)),
                  smem(ch, lambda i, e, cum: (jnp.minimum(i + 1, last) * ne + e, 0, 0)),
                  smem(ch, lambda i, e, cum: (i * ne + e, 0, 0)),
                  any_spec,
                  pl.BlockSpec((1, SUBLANES, tm), lambda i, e, cum: (jnp.minimum(i, last), 0, 0)),
                  weight((1, D_MODEL, D_EXPERT)), weight((1, D_MODEL, D_EXPERT)),
                  weight((1, D_EXPERT, D_MODEL))],
        out_specs=any_spec,
        scratch_shapes=[pltpu.VMEM((2, tm * TILE_ROWS, LANES), F32),
                        pltpu.VMEM((2, tm * TILE_ROWS, LANES), F32),
                        pltpu.VMEM((tm, D_MODEL), BF16),
                        pltpu.VMEM((tm, D_MODEL), F32),
                        pltpu.SemaphoreType.DMA((2,)),
                        pltpu.SemaphoreType.DMA((2,))])
    return pl.pallas_call(
        _expert_kernel,
        grid_spec=grid_spec,
        out_shape=jax.ShapeDtypeStruct((n_rows * TILE_ROWS, LANES), F32),
        compiler_params=_cparams(2),
        name="expert_mlp",
    )(cum, src0, gsrc, sdst, x1t, info_sorted, wg, wu, wd)


def _expert_plan(info, n_tiles):
    tm, ch = MOE_TM, MOE_CHUNK
    t_all = info.shape[0]
    gid = info[:, 0].astype(I32) // EXPERTS_PER_GROUP
    _, order = lax.sort((gid, lax.iota(I32, t_all)), num_keys=1, is_stable=True)
    groups = jnp.arange(N_GROUPS, dtype=I32)
    counts = jnp.sum((gid[:, None] == groups[None, :]).astype(I32), axis=0)
    start = jnp.cumsum(counts) - counts
    tiles_g = (counts + tm - 1) // tm
    cum = jnp.cumsum(tiles_g).astype(I32)
    n_live = cum[-1]
    ti = jnp.arange(n_tiles, dtype=I32)
    g_i = jnp.minimum(jnp.sum((cum[None, :] <= ti[:, None]).astype(I32), axis=1), N_GROUPS - 1)
    j = ti - (cum - tiles_g)[g_i]
    live = ti < n_live
    tile_rows = jnp.where(live, jnp.clip(counts[g_i] - j * tm, 0, tm), 0)
    tile_start = jnp.where(live, start[g_i] + j * tm, 0)
    r = jnp.arange(tm, dtype=I32)[None, :]
    valid = r < tile_rows[:, None]
    tok = order[jnp.minimum(tile_start[:, None] + r, t_all - 1)]
    src = (jnp.where(valid, tok, 0) * TILE_ROWS).astype(I32)
    dump = ((t_all + r) * TILE_ROWS).astype(I32)
    dst = (jnp.where(valid, tok, t_all + r) * TILE_ROWS).astype(I32)
    dst_prev = jnp.concatenate([dump, dst], axis=0)
    info_rows = jnp.where(valid.reshape(-1, 1), info[:, :SUBLANES][tok.reshape(-1)], 0.0)
    info_sorted = info_rows.reshape(n_tiles, tm, SUBLANES).transpose(0, 2, 1)
    return (cum, src[0].reshape(1, 1, tm), src.reshape(n_tiles * EXPERTS_PER_GROUP, 1, ch),
            dst_prev.reshape((n_tiles + 1) * EXPERTS_PER_GROUP, 1, ch), info_sorted)


def _combine_kernel(x1_ref, f_ref, g_ref, b_ref, outp_ref, outs_ref, *, n_prompt_tiles):
    tm = TOK_TM
    i = pl.program_id(0)
    f = _from_token_tiles(f_ref, tm)
    out = _layer_norm(DEEPNORM_ALPHA * x1_ref[...] + f, g_ref[...], b_ref[...])

    @pl.when(i < n_prompt_tiles)
    def _():
        outp_ref[...] = out

    @pl.when(i >= n_prompt_tiles)
    def _():
        outs_ref[...] = out


def _combine(x1, f_tiles, g2, b2, *, n_prompt, n_sample):
    tm = TOK_TM
    assert n_sample == tm
    npt = n_prompt // tm
    nt = npt + 1
    row = lambda i: (i, 0)
    const = lambda shape: pl.BlockSpec(shape, lambda i: (0, 0))
    return pl.pallas_call(
        functools.partial(_combine_kernel, n_prompt_tiles=npt),
        grid=(nt,),
        in_specs=[pl.BlockSpec((tm, D_MODEL), row),
                  pl.BlockSpec((tm * TILE_ROWS, LANES), row),
                  const((1, D_MODEL)), const((1, D_MODEL))],
        out_specs=[pl.BlockSpec((tm, D_MODEL), lambda i: (jnp.minimum(i, npt - 1), 0)),
                   pl.BlockSpec((tm, D_MODEL), lambda i: (0, 0))],
        out_shape=[jax.ShapeDtypeStruct((n_prompt, D_MODEL), F32),
                   jax.ShapeDtypeStruct((n_sample, D_MODEL), F32)],
        compiler_params=_cparams(1),
        name="combine_ln2",
    )(x1, f_tiles, g2, b2)


def _block_diag(w):
    nb, n, _ = w.shape
    eye = jnp.eye(nb, dtype=w.dtype)
    return (eye[:, None, :, None] * w[:, :, None, :]).reshape(nb * n, nb * n)


def kernel(x_prompt, x_sample, cache_k, cache_v, state_h, state_conv, page_table, w_in, lambda_q1, lambda_k1, lambda_q2, lambda_k2, subln_g, conv_w, conv_b, w_rg_a, b_rg_a, w_rg_x, b_rg_x, lru_lambda, w_out, ln1_g, ln1_b, w_group_router, b_group_router, w_expert_router, b_expert_router, w_gate_e, w_up_e, w_down_e, ln2_g, ln2_b):
    bp, sp, _ = x_prompt.shape
    bd, sd, _ = x_sample.shape
    assert w_in.shape[0] == DEPTH == 1
    l = 0
    n_prompt, n_sample = bp * sp, bd * sd
    t_all = n_prompt + n_sample

    xp = x_prompt.reshape(n_prompt, D_MODEL)
    xs = x_sample.reshape(n_sample, D_MODEL)
    w_in_b = w_in[l].astype(BF16)
    w_out_b = w_out[l].astype(BF16)
    lam_params = (lambda_q1[l][None], lambda_k1[l][None], lambda_q2[l][None], lambda_k2[l][None])
    w_gates = jnp.concatenate([_block_diag(w_rg_a[l]), _block_diag(w_rg_x[l])], axis=1).astype(BF16)
    b_gates = jnp.concatenate([b_rg_a[l].reshape(1, -1), b_rg_x[l].reshape(1, -1)], axis=1)
    lru_args = (conv_w[l], conv_b[l][None], w_gates, b_gates, lru_lambda[l][None])

    qt, kh, vt, k_p, v_p, xl_p, gl_p = _inproj(xp, w_in_b, prompt=True, tm=INPROJ_TM)
    o_p = _attn_prompt(qt, kh, vt, lam_params, subln_g[l][:, None], batch=bp, seq=sp)
    y_p, h_p = _rglru(xl_p, gl_p, jnp.zeros((bp, SUBLANES, LRU_WIDTH), F32),
                      jnp.zeros((bp, 1, LRU_WIDTH), F32), *lru_args, batch=bp, seq=sp, ts=LRU_TS)

    q_s, k_s, v_s, xl_s, gl_s = _inproj(xs, w_in_b, prompt=False, tm=n_sample)
    n_phys = cache_k.shape[1]
    ck = cache_k.reshape(DEPTH * n_phys * PAGE_SIZE * N_HEADS, V_DIM)
    cv = cache_v.reshape(DEPTH * n_phys * PAGE_SIZE * N_HEADS, V_DIM)
    o_s = _attn_sample(q_s, k_s, v_s, ck, cv, page_table, lam_params, subln_g[l][None],
                       dec_batch=bd, dec_seq=sd)
    conv0 = jnp.concatenate(
        [jnp.zeros((bd, SUBLANES - (CONV_W - 1), LRU_WIDTH), F32), state_conv[l]], axis=1)
    y_s, h_s = _rglru(xl_s, gl_s, conv0, state_h[l][:, None, :], *lru_args,
                      batch=bd, seq=sd, ts=sd)

    w_router = jnp.concatenate(
        [w_group_router[l], w_expert_router[l],
         jnp.zeros((D_MODEL, ROUTE_LANES - N_GROUPS - N_EXPERTS), F32)], axis=1).astype(BF16)
    b_router = jnp.concatenate(
        [b_group_router[l], b_expert_router[l],
         jnp.zeros((ROUTE_LANES - N_GROUPS - N_EXPERTS,), F32)])[None]
    x1, x1t, info = _merge(xp, xs, o_p, o_s, y_p, y_s, w_out_b, ln1_g[l][None], ln1_b[l][None],
                           w_router, b_router)
    n_tiles = -(-t_all // MOE_TM) + N_GROUPS
    cum, src0, gsrc, sdst, info_sorted = _expert_plan(info, n_tiles)
    f_tiles = _experts(x1t, cum, src0, gsrc, sdst, info_sorted,
                       w_gate_e[l], w_up_e[l], w_down_e[l],
                       n_tiles=n_tiles, n_rows=t_all + MOE_TM)
    out_p, out_s = _combine(x1, f_tiles, ln2_g[l][None], ln2_b[l][None],
                            n_prompt=n_prompt, n_sample=n_sample)

    conv_p = xl_p.reshape(bp, sp, LRU_WIDTH)[:, sp - (CONV_W - 1):]
    conv_s = jnp.concatenate([state_conv[l], xl_s.reshape(bd, sd, LRU_WIDTH)],
                             axis=1)[:, -(CONV_W - 1):]
    return (out_p.reshape(bp, sp, D_MODEL),
            out_s.reshape(bd, sd, D_MODEL),
            k_p.reshape(1, bp, sp, N_HEADS, V_DIM),
            v_p.reshape(1, bp, sp, N_HEADS, V_DIM),
            h_p.reshape(1, bp, LRU_WIDTH),
            conv_p[None],
            k_s.reshape(1, bd, sd, N_HEADS, V_DIM),
            v_s.reshape(1, bd, sd, N_HEADS, V_DIM),
            h_s.reshape(1, bd, LRU_WIDTH),
            conv_s[None])
```

```python
import functools
import math

import jax
import jax.numpy as jnp
from jax import lax
from jax.experimental import pallas as pl
from jax.experimental.pallas import tpu as pltpu

F32 = jnp.float32
BF16 = jnp.bfloat16
I32 = jnp.int32

D_MODEL = 1024
ATTN_WIDTH = 512
LRU_WIDTH = 512
N_HEADS = 4
V_DIM = 128
QK_DIM = 64
CONV_W = 4
LRU_C = 8.0
IN_WIDTH = 3 * ATTN_WIDTH + 2 * LRU_WIDTH
N_GROUPS = 4
EXPERTS_PER_GROUP = 8
N_EXPERTS = 32
D_EXPERT = 256
PAGE_SIZE = 128
LN_EPS = 1e-5
DEPTH = 1
DEEPNORM_ALPHA = (2 * DEPTH) ** 0.25
LAM_INIT = 0.8 - 0.6 * math.exp(-0.3 * 0)
QK_SCALE = QK_DIM ** -0.5 * math.log2(math.e)

SUBLANES = 8
LANES = 128
VMEM_LIMIT_BYTES = 48 * 1024 * 1024

INPROJ_TM = 512
ATTN_TQ = 512
ATTN_TK = 512
ATTN_CB = 256
ATTN_HEADS_PER_STEP = 2
TOK_TM = 256
MOE_TM = 1024
MOE_CHUNK = MOE_TM // EXPERTS_PER_GROUP
PAGES_PER_STEP = 8
ROUTE_LANES = 128
TILE_ROWS = D_MODEL // LANES


def _cparams(n_axes):
    return pltpu.CompilerParams(
        dimension_semantics=("arbitrary",) * n_axes,
        vmem_limit_bytes=VMEM_LIMIT_BYTES)


def _layer_norm(x, g, b):
    mu = jnp.mean(x, axis=-1, keepdims=True)
    xc = x - mu
    var = jnp.mean(xc * xc, axis=-1, keepdims=True)
    return xc * lax.rsqrt(var + LN_EPS) * g + b


def _diff_lambda(lq1, lk1, lq2, lk2):
    s1 = jnp.sum(lq1 * lk1, axis=-1, keepdims=True)
    s2 = jnp.sum(lq2 * lk2, axis=-1, keepdims=True)
    return jnp.exp(s1) - jnp.exp(s2) + LAM_INIT


def _to_token_tiles(ref, x):
    tm = x.shape[0]
    for s in range(TILE_ROWS):
        ref[pl.ds(s, tm, stride=TILE_ROWS), :] = x[:, s * LANES:(s + 1) * LANES]


def _from_token_tiles(ref, tm):
    return jnp.concatenate(
        [ref[pl.ds(s, tm, stride=TILE_ROWS), :] for s in range(TILE_ROWS)], axis=1)


def _inproj_kernel(x_ref, w_ref, *refs, prompt, tiles_per_seq):
    x = x_ref[...].astype(BF16)
    tm = x.shape[0]

    def section(idx, width):
        return jnp.dot(x, w_ref[:, idx:idx + width], preferred_element_type=F32)

    if prompt:
        (cw_ref, cb_ref, wgate_ref, bg_ref, lam_ref, qt_ref, kh_ref, vt_ref, k_ref, v_ref,
         y_ref, hlast_ref, xtail_ref, tail_ref, hc_ref) = refs

        @pl.when(pl.program_id(0) % tiles_per_seq == 0)
        def _():
            tail_ref[...] = jnp.zeros(tail_ref.shape, F32)
            hc_ref[...] = jnp.zeros(hc_ref.shape, F32)

        xl = section(3 * ATTN_WIDTH, LRU_WIDTH)
        gl = section(3 * ATTN_WIDTH + LRU_WIDTH, LRU_WIDTH)
        y, h_end = _rglru_tile(xl, gl, tail_ref, hc_ref, cw_ref, cb_ref, wgate_ref, bg_ref,
                               lam_ref)
        y_ref[...] = y.astype(y_ref.dtype)
        hlast_ref[0] = h_end
        xtail_ref[0] = xl[tm - SUBLANES:]

    q = section(0, ATTN_WIDTH) * QK_SCALE
    k = section(ATTN_WIDTH, ATTN_WIDTH)
    v = section(2 * ATTN_WIDTH, ATTN_WIDTH)
    if prompt:
        chan = lax.broadcasted_iota(I32, (V_DIM, tm), 0)
        for h in range(N_HEADS):
            sl = slice(h * V_DIM, (h + 1) * V_DIM)
            qt = q[:, sl].T
            qt_ref[h, 0] = jnp.where(chan < QK_DIM, qt, 0.0).astype(BF16)
            qt_ref[h, 1] = jnp.where(chan >= QK_DIM, qt, 0.0).astype(BF16)
            kh_ref[h] = k[:, sl].astype(BF16)
            vt_ref[h, 0] = v[:, sl].T.astype(BF16)
    else:
        q_ref, k_ref, v_ref, xl_ref, gl_ref = refs
        q_ref[...] = q
        xl_ref[...] = section(3 * ATTN_WIDTH, LRU_WIDTH)
        gl_ref[...] = section(3 * ATTN_WIDTH + LRU_WIDTH, LRU_WIDTH)
    for h in range(N_HEADS):
        sl = slice(h * V_DIM, (h + 1) * V_DIM)
        k_ref[pl.ds(h, tm, stride=N_HEADS), :] = k[:, sl]
        v_ref[pl.ds(h, tm, stride=N_HEADS), :] = v[:, sl]


def _inproj(x2d, w_bf16, lru_args=None, *, prompt, tm, seq=None):
    t = x2d.shape[0]
    assert t % tm == 0
    row = lambda i: (i, 0)
    const = lambda shape: pl.BlockSpec(shape, lambda i: (0,) * len(shape))
    spec512 = pl.BlockSpec((tm, ATTN_WIDTH), row)
    head_rows = pl.BlockSpec((tm * N_HEADS, V_DIM), row)
    f32_512 = jax.ShapeDtypeStruct((t, ATTN_WIDTH), F32)
    f32_heads = jax.ShapeDtypeStruct((t * N_HEADS, V_DIM), F32)
    in_specs = [pl.BlockSpec((tm, D_MODEL), row), const((D_MODEL, IN_WIDTH))]
    operands = [x2d, w_bf16]
    scratch = []
    tiles_per_seq = None
    if prompt:
        assert tm == ATTN_TK and seq % tm == 0
        tiles_per_seq = seq // tm
        batch = t // seq
        per_seq = lambda n: pl.BlockSpec((1, n, LRU_WIDTH), lambda i: (i // tiles_per_seq, 0, 0))
        in_specs += [const((CONV_W, LRU_WIDTH)), const((1, LRU_WIDTH)),
                     const((LRU_WIDTH, 2 * LRU_WIDTH)), const((1, 2 * LRU_WIDTH)),
                     const((1, LRU_WIDTH))]
        operands += list(lru_args)
        out_shape = [jax.ShapeDtypeStruct((N_HEADS, 2, V_DIM, t), BF16),
                     jax.ShapeDtypeStruct((N_HEADS, t, V_DIM), BF16),
                     jax.ShapeDtypeStruct((N_HEADS, t // tm, V_DIM, tm), BF16),
                     f32_heads, f32_heads,
                     jax.ShapeDtypeStruct((t, LRU_WIDTH), BF16),
                     jax.ShapeDtypeStruct((batch, 1, LRU_WIDTH), F32),
                     jax.ShapeDtypeStruct((batch, SUBLANES, LRU_WIDTH), F32)]
        out_specs = [pl.BlockSpec((N_HEADS, 2, V_DIM, tm), lambda i: (0, 0, 0, i)),
                     pl.BlockSpec((N_HEADS, tm, V_DIM), lambda i: (0, i, 0)),
                     pl.BlockSpec((N_HEADS, 1, V_DIM, tm), lambda i: (0, i, 0, 0)),
                     head_rows, head_rows, spec512, per_seq(1), per_seq(SUBLANES)]
        scratch = [pltpu.VMEM((SUBLANES, LRU_WIDTH), F32), pltpu.VMEM((1, LRU_WIDTH), F32)]
    else:
        out_shape = [f32_512, f32_heads, f32_heads, f32_512, f32_512]
        out_specs = [spec512, head_rows, head_rows, spec512, spec512]
    return pl.pallas_call(
        functools.partial(_inproj_kernel, prompt=prompt, tiles_per_seq=tiles_per_seq),
        grid=(t // tm,),
        in_specs=in_specs,
        out_specs=out_specs,
        out_shape=out_shape,
        scratch_shapes=scratch,
        compiler_params=_cparams(1),
        name="inproj_rglru_prompt" if prompt else "inproj_sample",
    )(*operands)


def _attn_prompt_kernel(qt_ref, k_ref, vt_ref, lq1_ref, lk1_ref, lq2_ref, lk2_ref, g_ref,
                        o_ref, s_ref, p_ref, a_ref, m_ref, l_ref, acc_ref):
    tq, tk, cb = ATTN_TQ, ATTN_TK, ATTN_CB
    heads = range(ATTN_HEADS_PER_STEP)
    n_cols = 2 * tq
    i = pl.program_id(2)

    m_ref[...] = jnp.full(m_ref.shape, -jnp.inf, F32)
    l_ref[...] = jnp.zeros(l_ref.shape, F32)
    acc_ref[...] = jnp.zeros(acc_ref.shape, F32)
    for hd in heads:
        p_ref[hd, 1] = jnp.zeros(p_ref.shape[2:], BF16)
        a_ref[hd, 1] = jnp.ones(a_ref.shape[2:], F32)

    def scores(j, slot):
        for hd in heads:
            k = k_ref[hd, pl.ds(pl.multiple_of(j * tk, tk), tk), :]
            for mp in range(2):
                s_ref[hd, slot, :, mp * tq:(mp + 1) * tq] = jnp.dot(
                    k, qt_ref[hd, mp], preferred_element_type=F32)

    def softmax(slot, masked):
        for hd in heads:
            for c in range(n_cols // cb):
                off = (c * cb) % tq
                cs = slice(c * cb, (c + 1) * cb)
                s = s_ref[hd, slot, :, cs]
                if masked:
                    key = lax.broadcasted_iota(I32, s.shape, 0)
                    qpos = lax.broadcasted_iota(I32, s.shape, 1) + off
                    s = jnp.where(key <= qpos, s, -jnp.inf)
                m_prev = m_ref[hd, :, cs]
                m_new = jnp.maximum(m_prev, jnp.max(s, axis=0, keepdims=True))
                alpha = jnp.exp2(m_prev - m_new)
                p = jnp.exp2(s - m_new)
                l_ref[hd, :, cs] = alpha * l_ref[hd, :, cs] + jnp.sum(p, axis=0, keepdims=True)
                m_ref[hd, :, cs] = m_new
                p_ref[hd, slot, :, cs] = p.astype(BF16)
                a_ref[hd, slot, :, cs] = alpha

    def accumulate(j, slot):
        for hd in heads:
            acc_ref[hd] = a_ref[hd, slot] * acc_ref[hd] + jnp.dot(
                vt_ref[hd, j], p_ref[hd, slot], preferred_element_type=F32)

    def stage(j, cur):
        scores(j + 1, 1 - cur)
        softmax(cur, masked=False)
        accumulate(jnp.maximum(j - 1, 0), 1 - cur)

    def diagonal(cur):
        accumulate(jnp.maximum(i - 1, 0), 1 - cur)
        softmax(cur, masked=True)
        accumulate(i, cur)

    scores(0, 0)

    def body(t, carry):
        stage(2 * t, 0)
        stage(2 * t + 1, 1)
        return carry

    lax.fori_loop(0, i // 2, body, 0)

    @pl.when(i % 2 == 0)
    def _():
        diagonal(0)

    @pl.when(i % 2 == 1)
    def _():
        stage(i - 1, 0)
        diagonal(1)

    lam = _diff_lambda(lq1_ref[...], lk1_ref[...], lq2_ref[...], lk2_ref[...])
    for hd in heads:
        o_all = acc_ref[hd] / l_ref[hd]
        o = o_all[:, :tq] - lam * o_all[:, tq:]
        ms = jnp.mean(o * o, axis=0, keepdims=True)
        o = o * lax.rsqrt(ms + LN_EPS) * g_ref[...] * (1.0 - LAM_INIT)
        o_ref[:, hd * V_DIM:(hd + 1) * V_DIM] = o.T.astype(o_ref.dtype)


def _attn_prompt(qt, kh, vt, lam_params, subln_g_col, *, batch, seq):
    assert ATTN_TQ == ATTN_TK and seq % ATTN_TQ == 0 and ATTN_TQ % ATTN_CB == 0
    hp = ATTN_HEADS_PER_STEP
    assert N_HEADS % hp == 0
    nq = seq // ATTN_TQ
    t = batch * seq
    vec = lambda n: pl.BlockSpec((1, n), lambda b, h, i: (0, 0))
    return pl.pallas_call(
        _attn_prompt_kernel,
        grid=(batch, N_HEADS // hp, nq),
        in_specs=[pl.BlockSpec((hp, 2, V_DIM, ATTN_TQ), lambda b, h, i: (h, 0, 0, b * nq + i)),
                  pl.BlockSpec((hp, seq, V_DIM), lambda b, h, i: (h, b, 0)),
                  pl.BlockSpec((hp, nq, V_DIM, ATTN_TK), lambda b, h, i: (h, b, 0, 0)),
                  vec(QK_DIM), vec(QK_DIM), vec(QK_DIM), vec(QK_DIM),
                  pl.BlockSpec((V_DIM, 1), lambda b, h, i: (0, 0))],
        out_specs=pl.BlockSpec((ATTN_TQ, hp * V_DIM), lambda b, h, i: (b * nq + i, h)),
        out_shape=jax.ShapeDtypeStruct((t, ATTN_WIDTH), BF16),
        scratch_shapes=[pltpu.VMEM((hp, 2, ATTN_TK, 2 * ATTN_TQ), F32),
                        pltpu.VMEM((hp, 2, ATTN_TK, 2 * ATTN_TQ), BF16),
                        pltpu.VMEM((hp, 2, 1, 2 * ATTN_TQ), F32),
                        pltpu.VMEM((hp, 1, 2 * ATTN_TQ), F32),
                        pltpu.VMEM((hp, 1, 2 * ATTN_TQ), F32),
                        pltpu.VMEM((hp, V_DIM, 2 * ATTN_TQ), F32)],
        compiler_params=_cparams(3),
        name="attn_prompt",
    )(qt, kh, vt, *lam_params, subln_g_col)


def _attn_sample_kernel(pt_ref, q0_ref, q1_ref, kn0_ref, kn1_ref, vn0_ref, vn1_ref,
                        ck_hbm, cv_hbm, lq1_ref, lk1_ref, lq2_ref, lk2_ref, g_ref,
                        o0_ref, o1_ref, kbuf, vbuf, ksem, vsem, bias_ref, m_ref, l_ref, acc_ref,
                        *, n_chunks, n_steps, dec_seq):
    gpp = PAGES_PER_STEP
    page_rows = PAGE_SIZE * N_HEADS
    hrows = 2 * dec_seq
    g = pl.program_id(0)
    c = g % n_chunks
    slot = g % 2
    streams = ((q0_ref, kn0_ref, vn0_ref, o0_ref), (q1_ref, kn1_ref, vn1_ref, o1_ref))

    def page_copies(step, sl):
        copies = []
        for s in range(len(streams)):
            buf = 2 * s + sl
            for p in range(gpp):
                page = pt_ref[(s * n_steps + step) * gpp + p]
                src = pl.ds(pl.multiple_of(page * page_rows, page_rows), page_rows)
                dst = pl.ds(p * page_rows, page_rows)
                copies.append(pltpu.make_async_copy(ck_hbm.at[src], kbuf.at[buf, dst],
                                                    ksem.at[buf]))
                copies.append(pltpu.make_async_copy(cv_hbm.at[src], vbuf.at[buf, dst],
                                                    vsem.at[buf]))
        return copies

    @pl.when(g == 0)
    def _():
        for cp in page_copies(0, 0):
            cp.start()

    @pl.when(g + 1 < n_steps)
    def _():
        for cp in page_copies(g + 1, 1 - slot):
            cp.start()

    def head_bias(shape):
        row_head = lax.broadcasted_iota(I32, shape, 0) // hrows
        col_head = lax.broadcasted_iota(I32, shape, 1) % N_HEADS
        return row_head == col_head

    @pl.when(g == 0)
    def _():
        bias_ref[...] = jnp.where(head_bias(bias_ref.shape), 0.0, -jnp.inf)

    @pl.when(c == 0)
    def _():
        m_ref[...] = jnp.full(m_ref.shape, -jnp.inf, F32)
        l_ref[...] = jnp.zeros(l_ref.shape, F32)
        acc_ref[...] = jnp.zeros(acc_ref.shape, F32)

    lane = lax.broadcasted_iota(I32, (dec_seq, V_DIM), 1)

    def query_block(q_ref):
        q = q_ref[...]
        q_rows = []
        for h in range(N_HEADS):
            qh = q[:, h * V_DIM:(h + 1) * V_DIM]
            q_rows += [jnp.where(lane < QK_DIM, qh, 0.0), jnp.where(lane >= QK_DIM, qh, 0.0)]
        return jnp.concatenate(q_rows, axis=0).astype(BF16)

    def update(s_idx, qblk, k, v, bias):
        s = lax.dot_general(qblk, k, (((1,), (1,)), ((), ())),
                            preferred_element_type=F32) + bias
        m_prev = m_ref[s_idx]
        m_new = jnp.maximum(m_prev, jnp.max(s, axis=-1, keepdims=True))
        alpha = jnp.exp2(m_prev - m_new)
        p = jnp.exp2(s - m_new)
        l_ref[s_idx] = alpha * l_ref[s_idx] + jnp.sum(p, axis=-1, keepdims=True)
        acc_ref[s_idx] = alpha * acc_ref[s_idx] + jnp.dot(p.astype(BF16), v,
                                                          preferred_element_type=F32)
        m_ref[s_idx] = m_new

    for cp in page_copies(g, slot):
        cp.wait()
    qblks = [query_block(refs[0]) for refs in streams]
    for s_idx in range(len(streams)):
        buf = 2 * s_idx + slot
        update(s_idx, qblks[s_idx], kbuf[buf].astype(BF16), vbuf[buf].astype(BF16),
               bias_ref[...])

    @pl.when(c == n_chunks - 1)
    def _():
        new_rows = dec_seq * N_HEADS
        pad = jnp.zeros((LANES - new_rows, V_DIM), F32)
        shape = (N_HEADS * hrows, LANES)
        qi = lax.broadcasted_iota(I32, shape, 0) % dec_seq
        kj = lax.broadcasted_iota(I32, shape, 1) // N_HEADS
        new_bias = jnp.where(head_bias(shape) & (kj <= qi), 0.0, -jnp.inf)
        lam = _diff_lambda(lq1_ref[...], lk1_ref[...], lq2_ref[...], lk2_ref[...])
        for s_idx, (_, kn_ref, vn_ref, o_ref) in enumerate(streams):
            kn = jnp.concatenate([kn_ref[...], pad], axis=0).astype(BF16)
            vn = jnp.concatenate([vn_ref[...], pad], axis=0).astype(BF16)
            update(s_idx, qblks[s_idx], kn, vn, new_bias)
            o_heads = acc_ref[s_idx] / l_ref[s_idx]
            for h in range(N_HEADS):
                o_all = o_heads[h * hrows:(h + 1) * hrows]
                o = o_all[:dec_seq] - lam * o_all[dec_seq:]
                ms = jnp.mean(o * o, axis=-1, keepdims=True)
                o_ref[:, h * V_DIM:(h + 1) * V_DIM] = (
                    o * lax.rsqrt(ms + LN_EPS) * g_ref[...] * (1.0 - LAM_INIT))


def _attn_sample(q, k_new, v_new, cache_k_rows, cache_v_rows, page_table, lam_params, subln_g,
                 *, dec_batch, dec_seq):
    n_pages = page_table.shape[1]
    assert n_pages % PAGES_PER_STEP == 0 and dec_seq % SUBLANES == 0
    n_chunks = n_pages // PAGES_PER_STEP
    n_streams = 2
    assert dec_batch % n_streams == 0
    half = dec_batch // n_streams
    n_steps = half * n_chunks
    rows = 2 * N_HEADS * dec_seq
    chunk_rows = PAGES_PER_STEP * PAGE_SIZE * N_HEADS

    def stream_specs(s):
        tok = pl.BlockSpec((dec_seq, ATTN_WIDTH), lambda g, pt: (s * half + g // n_chunks, 0))
        tok_heads = pl.BlockSpec((dec_seq * N_HEADS, V_DIM),
                                 lambda g, pt: (s * half + g // n_chunks, 0))
        return tok, tok_heads

    (tok0, heads0), (tok1, heads1) = stream_specs(0), stream_specs(1)
    out_spec = pl.BlockSpec((dec_seq, ATTN_WIDTH), lambda g, pt: (g // n_chunks, 0))
    vec = lambda n: pl.BlockSpec((1, n), lambda g, pt: (0, 0))
    any_spec = pl.BlockSpec(memory_space=pl.ANY)
    n_bufs = 2 * n_streams
    grid_spec = pltpu.PrefetchScalarGridSpec(
        num_scalar_prefetch=1,
        grid=(n_steps,),
        in_specs=[tok0, tok1, heads0, heads1, heads0, heads1, any_spec, any_spec,
                  vec(QK_DIM), vec(QK_DIM), vec(QK_DIM), vec(QK_DIM), vec(V_DIM)],
        out_specs=[out_spec, out_spec],
        scratch_shapes=[pltpu.VMEM((n_bufs, chunk_rows, V_DIM), F32),
                        pltpu.VMEM((n_bufs, chunk_rows, V_DIM), F32),
                        pltpu.SemaphoreType.DMA((n_bufs,)),
                        pltpu.SemaphoreType.DMA((n_bufs,)),
                        pltpu.VMEM((rows, chunk_rows), F32),
                        pltpu.VMEM((n_streams, rows, 1), F32),
                        pltpu.VMEM((n_streams, rows, 1), F32),
                        pltpu.VMEM((n_streams, rows, V_DIM), F32)])
    half_out = jax.ShapeDtypeStruct((half * dec_seq, ATTN_WIDTH), F32)
    o0, o1 = pl.pallas_call(
        functools.partial(_attn_sample_kernel, n_chunks=n_chunks, n_steps=n_steps,
                          dec_seq=dec_seq),
        grid_spec=grid_spec,
        out_shape=[half_out, half_out],
        compiler_params=_cparams(1),
        name="attn_sample",
    )(page_table.reshape(-1), q, q, k_new, k_new, v_new, v_new, cache_k_rows, cache_v_rows,
      *lam_params, subln_g)
    return jnp.concatenate([o0, o1], axis=0)


def _softplus(x):
    y = jnp.exp(-jnp.abs(x))
    w = 1.0 + y
    log1p = jnp.where(w == 1.0, y, jnp.log(w) * (y / (w - 1.0)))
    return jnp.maximum(x, 0.0) + log1p


def _rglru_tile(x, gl, tail_ref, hc_ref, cw_ref, cb_ref, wg_ref, bg_ref, lam_ref):
    ts = x.shape[0]
    xfull = jnp.concatenate([tail_ref[...], x], axis=0)
    cw = cw_ref[...]
    xc = cb_ref[...] + cw[CONV_W - 1:CONV_W] * x
    for d in range(1, CONV_W):
        shifted = pltpu.roll(xfull, d, 0)[SUBLANES:]
        xc = xc + cw[CONV_W - 1 - d:CONV_W - d] * shifted
    tail_ref[...] = x[ts - SUBLANES:]

    gates = jnp.dot(xc.astype(BF16), wg_ref[...], preferred_element_type=F32) + bg_ref[...]
    r = jax.nn.sigmoid(gates[:, :LRU_WIDTH])
    ig = jax.nn.sigmoid(gates[:, LRU_WIDTH:])
    log_a = -LRU_C * r * _softplus(-lam_ref[...])
    a = jnp.exp(log_a)
    mult = jnp.sqrt(jnp.maximum(-(a * a + 1.0) * jnp.tanh(log_a), 0.0))
    u = mult * ig * xc

    row = lax.broadcasted_iota(I32, a.shape, 0)
    d = 1
    while d < ts:
        if d < SUBLANES:
            keep = row >= d
            a_prev = jnp.where(keep, pltpu.roll(a, d, 0), 1.0)
            u_prev = jnp.where(keep, pltpu.roll(u, d, 0), 0.0)
            u = a * u_prev + u
            a = a * a_prev
        else:
            u = jnp.concatenate([u[:d], a[d:] * u[:ts - d] + u[d:]], axis=0)
            a = jnp.concatenate([a[:d], a[d:] * a[:ts - d]], axis=0)
        d *= 2
    h = a * hc_ref[...] + u
    h_end = h[ts - 1:ts]
    hc_ref[...] = h_end
    return h * jax.nn.gelu(gl, approximate=True), h_end


def _rglru_kernel(xl_ref, gl_ref, conv0_ref, h0_ref, cw_ref, cb_ref, wg_ref, bg_ref, lam_ref,
                  y_ref, hlast_ref, tail_ref, hc_ref):
    s = pl.program_id(1)

    @pl.when(s == 0)
    def _():
        tail_ref[...] = conv0_ref[0]
        hc_ref[...] = h0_ref[0]

    y, h_end = _rglru_tile(xl_ref[...], gl_ref[...], tail_ref, hc_ref,
                           cw_ref, cb_ref, wg_ref, bg_ref, lam_ref)
    hlast_ref[0] = h_end
    y_ref[...] = y.astype(y_ref.dtype)


def _rglru(xl, gl, conv0, h0, conv_w, conv_b, w_gates, b_gates, lam, *, batch, seq, ts):
    assert seq % ts == 0 and ts % SUBLANES == 0
    ns = seq // ts
    tile = pl.BlockSpec((ts, LRU_WIDTH), lambda b, s: (b * ns + s, 0))
    const = lambda shape: pl.BlockSpec(shape, lambda b, s: (0,) * len(shape))
    return pl.pallas_call(
        _rglru_kernel,
        grid=(batch, ns),
        in_specs=[tile, tile,
                  pl.BlockSpec((1, SUBLANES, LRU_WIDTH), lambda b, s: (b, 0, 0)),
                  pl.BlockSpec((1, 1, LRU_WIDTH), lambda b, s: (b, 0, 0)),
                  const((CONV_W, LRU_WIDTH)), const((1, LRU_WIDTH)),
                  const((LRU_WIDTH, 2 * LRU_WIDTH)), const((1, 2 * LRU_WIDTH)),
                  const((1, LRU_WIDTH))],
        out_specs=[tile, pl.BlockSpec((1, 1, LRU_WIDTH), lambda b, s: (b, 0, 0))],
        out_shape=[jax.ShapeDtypeStruct((batch * seq, LRU_WIDTH), BF16 if ts > SUBLANES else F32),
                   jax.ShapeDtypeStruct((batch, 1, LRU_WIDTH), F32)],
        scratch_shapes=[pltpu.VMEM((SUBLANES, LRU_WIDTH), F32),
                        pltpu.VMEM((1, LRU_WIDTH), F32)],
        compiler_params=_cparams(2),
        name="rglru_prompt" if ts > SUBLANES else "rglru_sample",
    )(xl, gl, conv0, h0, conv_w, conv_b, w_gates, b_gates, lam)


def _route(logits):
    lane = lax.broadcasted_iota(I32, logits.shape, 1).astype(F32)
    big = float(ROUTE_LANES)
    neg = -jnp.inf
    is_g = lane < N_GROUPS
    gmax = jnp.max(jnp.where(is_g, logits, neg), axis=-1, keepdims=True)
    gsum = jnp.sum(jnp.where(is_g, jnp.exp(logits - gmax), 0.0), axis=-1, keepdims=True)
    gw = 1.0 / gsum
    gid = jnp.min(jnp.where(is_g & (logits == gmax), lane, big), axis=-1, keepdims=True)
    lo = N_GROUPS + EXPERTS_PER_GROUP * gid
    in_grp = (lane >= lo) & (lane < lo + EXPERTS_PER_GROUP)
    v1 = jnp.max(jnp.where(in_grp, logits, neg), axis=-1, keepdims=True)
    i1 = jnp.min(jnp.where(in_grp & (logits == v1), lane, big), axis=-1, keepdims=True)
    rest = in_grp & (lane != i1)
    v2 = jnp.max(jnp.where(rest, logits, neg), axis=-1, keepdims=True)
    i2 = jnp.min(jnp.where(rest & (logits == v2), lane, big), axis=-1, keepdims=True)
    t = jnp.exp(v2 - v1)
    w1 = gw / (1.0 + t)
    w2 = gw * t / (1.0 + t)
    info = jnp.where(lane == 0, i1 - N_GROUPS,
                     jnp.where(lane == 1, i2 - N_GROUPS,
                               jnp.where(lane == 2, w1, jnp.where(lane == 3, w2, 0.0))))
    return info


def _merge_kernel(xp_ref, xs_ref, op_ref, os_ref, yp_ref, ys_ref, wo_ref, g_ref, b_ref,
                  wr_ref, br_ref, x1_ref, x1t_ref, info_ref, *, n_prompt_tiles):
    i = pl.program_id(0)

    def body(x_ref, o_ref, y_ref):
        m = jnp.dot(o_ref[...].astype(BF16), wo_ref[:ATTN_WIDTH, :], preferred_element_type=F32)
        m = m + jnp.dot(y_ref[...].astype(BF16), wo_ref[ATTN_WIDTH:, :],
                        preferred_element_type=F32)
        x1 = _layer_norm(DEEPNORM_ALPHA * x_ref[...] + m, g_ref[...], b_ref[...])
        x1_ref[...] = x1
        _to_token_tiles(x1t_ref, x1)
        logits = jnp.dot(x1.astype(BF16), wr_ref[...],
                         preferred_element_type=F32) + br_ref[...]
        info_ref[...] = _route(logits)

    @pl.when(i < n_prompt_tiles)
    def _():
        body(xp_ref, op_ref, yp_ref)

    @pl.when(i >= n_prompt_tiles)
    def _():
        body(xs_ref, os_ref, ys_ref)


def _merge(xp, xs, o_p, o_s, y_p, y_s, wo_bf16, g1, b1, wr, br):
    tm = TOK_TM
    tp, ts = xp.shape[0], xs.shape[0]
    assert tp % tm == 0 and ts % tm == 0
    npt, nst = tp // tm, ts // tm
    t_all = tp + ts
    pidx = lambda i: (jnp.minimum(i, npt - 1), 0)
    sidx = lambda i: (jnp.maximum(i - npt, 0), 0)
    const = lambda shape: pl.BlockSpec(shape, lambda i: (0, 0))
    row = lambda i: (i, 0)
    return pl.pallas_call(
        functools.partial(_merge_kernel, n_prompt_tiles=npt),
        grid=(npt + nst,),
        in_specs=[pl.BlockSpec((tm, D_MODEL), pidx), pl.BlockSpec((tm, D_MODEL), sidx),
                  pl.BlockSpec((tm, ATTN_WIDTH), pidx), pl.BlockSpec((tm, ATTN_WIDTH), sidx),
                  pl.BlockSpec((tm, LRU_WIDTH), pidx), pl.BlockSpec((tm, LRU_WIDTH), sidx),
                  const((D_MODEL, D_MODEL)), const((1, D_MODEL)), const((1, D_MODEL)),
                  const((D_MODEL, ROUTE_LANES)), const((1, ROUTE_LANES))],
        out_specs=[pl.BlockSpec((tm, D_MODEL), row),
                   pl.BlockSpec((tm * TILE_ROWS, LANES), row),
                   pl.BlockSpec((tm, ROUTE_LANES), row)],
        out_shape=[jax.ShapeDtypeStruct((t_all, D_MODEL), F32),
                   jax.ShapeDtypeStruct((t_all * TILE_ROWS, LANES), F32),
                   jax.ShapeDtypeStruct((t_all, ROUTE_LANES), F32)],
        compiler_params=_cparams(1),
        name="merge_ln1_route",
    )(xp, xs, o_p, o_s, y_p, y_s, wo_bf16, g1, b1, wr, br)


def _tile_group(i, cum_ref):
    t = jnp.minimum(i, cum_ref[N_GROUPS - 1] - 1)
    g = jnp.int32(0)
    for k in range(N_GROUPS - 1):
        g = g + (cum_ref[k] <= t).astype(I32)
    return g


def _expert_kernel(cum_ref, src0_ref, gsrc_ref, sdst_ref,
                   x_hbm, info_ref, wg_ref, wu_ref, wd_ref, out_hbm,
                   xbuf, ybuf, xb_ref, infot_ref, yacc_ref, gsem, ssem):
    tm, ch, tr = MOE_TM, MOE_CHUNK, TILE_ROWS
    last_e = EXPERTS_PER_GROUP - 1
    i = pl.program_id(0)
    e = pl.program_id(1)
    slot = i % 2
    other = 1 - slot
    n_live = cum_ref[N_GROUPS - 1]

    def start_gather(idx_ref, rows, base, sl):
        for r in rows:
            src = pl.ds(pl.multiple_of(idx_ref[0, 0, r], tr), tr)
            dst = pl.ds(pl.multiple_of((base + r) * tr, tr), tr)
            pltpu.make_async_copy(x_hbm.at[src], xbuf.at[sl, dst],
                                  gsem.at[sl]).start(priority=r % 2)

    def start_scatter(idx_ref, rows, base, sl):
        for r in rows:
            src = pl.ds(pl.multiple_of((base + r) * tr, tr), tr)
            dst = pl.ds(pl.multiple_of(idx_ref[0, 0, r], tr), tr)
            pltpu.make_async_copy(ybuf.at[sl, src], out_hbm.at[dst],
                                  ssem.at[sl]).start(priority=r % 2)

    def wait_tile(buf, sem):
        pltpu.make_async_copy(x_hbm.at[pl.ds(0, tm * tr)], buf, sem).wait()

    def move_rows():
        start_gather(gsrc_ref, range(ch), e * ch, other)
        start_scatter(sdst_ref, range(ch), e * ch, other)

    @pl.when((i == 0) & (e == 0))
    def _():
        ybuf[1] = jnp.zeros(ybuf.shape[1:], F32)
        yacc_ref[...] = jnp.zeros(yacc_ref.shape, F32)
        start_gather(src0_ref, range(tm), 0, 0)

    @pl.when((e == 0) & (i <= n_live))
    def _():
        wait_tile(xbuf.at[slot], gsem.at[slot])

    @pl.when((e == last_e) & (i >= 1) & (i <= n_live))
    def _():
        wait_tile(ybuf.at[slot], ssem.at[slot])

    @pl.when(i == n_live)
    def _():
        move_rows()

    @pl.when((e == 0) & (i < n_live))
    def _():
        xb_ref[...] = _from_token_tiles(xbuf.at[slot], tm).astype(BF16)
        infot_ref[...] = info_ref[0].T

    @pl.when(i < n_live)
    def _():
        move_rows()
        w_in = jnp.concatenate([wg_ref[0], wu_ref[0]], axis=1).astype(BF16)
        gu = jnp.dot(xb_ref[...], w_in, preferred_element_type=F32)
        gate = gu[:, :D_EXPERT]
        up = gu[:, D_EXPERT:]
        expert = (_tile_group(i, cum_ref) * EXPERTS_PER_GROUP + e).astype(F32)
        info = infot_ref[...]
        comb =(jnp.where(info[:, 0:1] == expert, info[:, 2:3], 0.0)
                + jnp.where(info[:, 1:2] == expert, info[:, 3:4], 0.0))
        h = (gate * jax.nn.sigmoid(gate) * up * comb).astype(BF16)
        y = jnp.dot(h, wd_ref[0].astype(BF16), preferred_element_type=F32)
        yacc_ref[...] = y + jnp.where(e > 0, yacc_ref[...], 0.0)

    @pl.when((e == last_e) & (i < n_live))
    def _():
        _to_token_tiles(ybuf.at[slot], yacc_ref[...])

    @pl.when((e == last_e) & (i == n_live))
    def _():
        wait_tile(xbuf.at[other], gsem.at[other])
        wait_tile(ybuf.at[other], ssem.at[other])


def _experts(x1t, cum, src0, gsrc, sdst, info_sorted, wg, wu, wd, *, n_tiles, n_rows):
    tm, ch = MOE_TM, MOE_CHUNK
    ne = EXPERTS_PER_GROUP
    last = n_tiles - 1
    smem = lambda n, fn: pl.BlockSpec((1, 1, n), fn, memory_space=pltpu.SMEM)
    weight = lambda shape: pl.BlockSpec(
        shape, lambda i, e, cum: (_tile_group(i, cum) * ne + e, 0, 0))
    any_spec = pl.BlockSpec(memory_space=pl.ANY)
    grid_spec = pltpu.PrefetchScalarGridSpec(
        num_scalar_prefetch=1,
        grid=(n_tiles + 1, ne),
        in_specs=[smem(tm, lambda i, e, cum: (0, 0, 0)),
                  smem(ch, lambda i, e, cum: (jnp.minimum(i + 1, last) * ne + e, 0, 0)),
                  smem(ch, lambda i, e, cum: (i * ne + e, 0, 0)),
                  any_spec,
                  pl.BlockSpec((1, SUBLANES, tm), lambda i, e, cum: (jnp.minimum(i, last), 0, 0)),
                  weight((1, D_MODEL, D_EXPERT)), weight((1, D_MODEL, D_EXPERT)),
                  weight((1, D_EXPERT, D_MODEL))],
        out_specs=any_spec,
        scratch_shapes=[pltpu.VMEM((2, tm * TILE_ROWS, LANES), F32),
                        pltpu.VMEM((2, tm * TILE_ROWS, LANES), F32),
                        pltpu.VMEM((tm, D_MODEL), BF16),
                        pltpu.VMEM((tm, SUBLANES), F32),
                        pltpu.VMEM((tm, D_MODEL), F32),
                        pltpu.SemaphoreType.DMA((2,)),
                        pltpu.SemaphoreType.DMA((2,))])
    return pl.pallas_call(
        _expert_kernel,
        grid_spec=grid_spec,
        out_shape=jax.ShapeDtypeStruct((n_rows * TILE_ROWS, LANES), F32),
        compiler_params=_cparams(2),
        name="expert_mlp",
    )(cum, src0, gsrc, sdst, x1t, info_sorted, wg, wu, wd)


def _expert_plan(info, n_tiles):
    tm, ch = MOE_TM, MOE_CHUNK
    t_all = info.shape[0]
    gid = info[:, 0].astype(I32) // EXPERTS_PER_GROUP
    _, order, *rec = lax.sort((gid, lax.iota(I32, t_all)) + tuple(info[:, c] for c in range(4)),
                              num_keys=1, is_stable=True)
    groups = jnp.arange(N_GROUPS, dtype=I32)
    counts = jnp.sum((gid[:, None] == groups[None, :]).astype(I32), axis=0)
    start = jnp.cumsum(counts) - counts
    tiles_g = (counts + tm - 1) // tm
    cum = jnp.cumsum(tiles_g).astype(I32)
    n_live = cum[-1]
    ti = jnp.arange(n_tiles, dtype=I32)
    g_i = jnp.minimum(jnp.sum((cum[None, :] <= ti[:, None]).astype(I32), axis=1), N_GROUPS - 1)
    j = ti - (cum - tiles_g)[g_i]
    live = ti < n_live
    tile_rows = jnp.where(live, jnp.clip(counts[g_i] - j * tm, 0, tm), 0)
    tile_start = jnp.where(live, start[g_i] + j * tm, 0)
    r = jnp.arange(tm, dtype=I32)[None, :]
    valid = r < tile_rows[:, None]
    window = jax.vmap(lambda a, s: lax.dynamic_slice(a, (0, s), (a.shape[0], tm)), (None, 0))
    tok = window(jnp.pad(order[None], ((0, 0), (0, tm))), tile_start)[:, 0]
    src = (jnp.where(valid, tok, 0) * TILE_ROWS).astype(I32)
    dump = ((t_all + r) * TILE_ROWS).astype(I32)
    dst = (jnp.where(valid, tok, t_all + r) * TILE_ROWS).astype(I32)
    dst_prev = jnp.concatenate([dump, dst], axis=0)
    rec = jnp.stack(rec + [jnp.zeros((t_all,), F32)] * (SUBLANES - len(rec)))
    info_sorted = jnp.where(valid[:, None, :],
                            window(jnp.pad(rec, ((0, 0), (0, tm))), tile_start), 0.0)
    return (cum, src[0].reshape(1, 1, tm), src.reshape(n_tiles * EXPERTS_PER_GROUP, 1, ch),
            dst_prev.reshape((n_tiles + 1) * EXPERTS_PER_GROUP, 1, ch), info_sorted)


def _combine_kernel(x1_ref, f_ref, g_ref, b_ref, outp_ref, outs_ref, *, n_prompt_tiles):
    tm = TOK_TM
    i = pl.program_id(0)
    f = _from_token_tiles(f_ref, tm)
    out = _layer_norm(DEEPNORM_ALPHA * x1_ref[...] + f, g_ref[...], b_ref[...])

    @pl.when(i < n_prompt_tiles)
    def _():
        outp_ref[...] = out

    @pl.when(i >= n_prompt_tiles)
    def _():
        outs_ref[...] = out


def _combine(x1, f_tiles, g2, b2, *, n_prompt, n_sample):
    tm = TOK_TM
    assert n_sample == tm
    npt = n_prompt // tm
    nt = npt + 1
    row = lambda i: (i, 0)
    const = lambda shape: pl.BlockSpec(shape, lambda i: (0, 0))
    return pl.pallas_call(
        functools.partial(_combine_kernel, n_prompt_tiles=npt),
        grid=(nt,),
        in_specs=[pl.BlockSpec((tm, D_MODEL), row),
                  pl.BlockSpec((tm * TILE_ROWS, LANES), row),
                  const((1, D_MODEL)), const((1, D_MODEL))],
        out_specs=[pl.BlockSpec((tm, D_MODEL), lambda i: (jnp.minimum(i, npt - 1), 0)),
                   pl.BlockSpec((tm, D_MODEL), lambda i: (0, 0))],
        out_shape=[jax.ShapeDtypeStruct((n_prompt, D_MODEL), F32),
                   jax.ShapeDtypeStruct((n_sample, D_MODEL), F32)],
        compiler_params=_cparams(1),
        name="combine_ln2",
    )(x1, f_tiles, g2, b2)


def _block_diag(w):
    nb, n, _ = w.shape
    eye = jnp.eye(nb, dtype=w.dtype)
    return (eye[:, None, :, None] * w[:, :, None, :]).reshape(nb * n, nb * n)


def kernel(x_prompt, x_sample, cache_k, cache_v, state_h, state_conv, page_table, w_in, lambda_q1, lambda_k1, lambda_q2, lambda_k2, subln_g, conv_w, conv_b, w_rg_a, b_rg_a, w_rg_x, b_rg_x, lru_lambda, w_out, ln1_g, ln1_b, w_group_router, b_group_router, w_expert_router, b_expert_router, w_gate_e, w_up_e, w_down_e, ln2_g, ln2_b):
    bp, sp, _ = x_prompt.shape
    bd, sd, _ = x_sample.shape
    assert w_in.shape[0] == DEPTH == 1
    l = 0
    n_prompt, n_sample = bp * sp, bd * sd
    t_all = n_prompt + n_sample

    xp = x_prompt.reshape(n_prompt, D_MODEL)
    xs = x_sample.reshape(n_sample, D_MODEL)
    w_in_b = w_in[l].astype(BF16)
    w_out_b = w_out[l].astype(BF16)
    lam_params = (lambda_q1[l][None], lambda_k1[l][None], lambda_q2[l][None], lambda_k2[l][None])
    w_gates = jnp.concatenate([_block_diag(w_rg_a[l]), _block_diag(w_rg_x[l])], axis=1).astype(BF16)
    b_gates = jnp.concatenate([b_rg_a[l].reshape(1, -1), b_rg_x[l].reshape(1, -1)], axis=1)
    lru_args = (conv_w[l], conv_b[l][None], w_gates, b_gates, lru_lambda[l][None])

    qt, kh, vt, k_p, v_p, y_p, h_p, xl_tail = _inproj(xp, w_in_b, lru_args, prompt=True,
                                                      tm=INPROJ_TM, seq=sp)
    o_p = _attn_prompt(qt, kh, vt, lam_params, subln_g[l][:, None], batch=bp, seq=sp)

    q_s, k_s, v_s, xl_s, gl_s = _inproj(xs, w_in_b, prompt=False, tm=n_sample)
    n_phys = cache_k.shape[1]
    ck = cache_k.reshape(DEPTH * n_phys * PAGE_SIZE * N_HEADS, V_DIM)
    cv = cache_v.reshape(DEPTH * n_phys * PAGE_SIZE * N_HEADS, V_DIM)
    o_s = _attn_sample(q_s, k_s, v_s, ck, cv, page_table, lam_params, subln_g[l][None],
                       dec_batch=bd, dec_seq=sd)
    conv0 = jnp.concatenate(
        [jnp.zeros((bd, SUBLANES - (CONV_W - 1), LRU_WIDTH), F32), state_conv[l]], axis=1)
    y_s, h_s = _rglru(xl_s, gl_s, conv0, state_h[l][:, None, :], *lru_args,
                      batch=bd, seq=sd, ts=sd)

    w_router = jnp.concatenate(
        [w_group_router[l], w_expert_router[l],
         jnp.zeros((D_MODEL, ROUTE_LANES - N_GROUPS - N_EXPERTS), F32)], axis=1).astype(BF16)
    b_router = jnp.concatenate(
        [b_group_router[l], b_expert_router[l],
         jnp.zeros((ROUTE_LANES - N_GROUPS - N_EXPERTS,), F32)])[None]
    x1, x1t, info = _merge(xp, xs, o_p, o_s, y_p, y_s, w_out_b, ln1_g[l][None], ln1_b[l][None],
                           w_router, b_router)
    n_tiles = -(-t_all // MOE_TM) + N_GROUPS
    cum, src0, gsrc, sdst, info_sorted = _expert_plan(info, n_tiles)
    f_tiles = _experts(x1t, cum, src0, gsrc, sdst, info_sorted,
                       w_gate_e[l], w_up_e[l], w_down_e[l],
                       n_tiles=n_tiles, n_rows=t_all + MOE_TM)
    out_p, out_s = _combine(x1, f_tiles, ln2_g[l][None], ln2_b[l][None],
                            n_prompt=n_prompt, n_sample=n_sample)

    conv_p = xl_tail[:, SUBLANES - (CONV_W - 1):]
    conv_s = jnp.concatenate([state_conv[l], xl_s.reshape(bd, sd, LRU_WIDTH)],
                             axis=1)[:, -(CONV_W - 1):]
    return (out_p.reshape(bp, sp, D_MODEL),
            out_s.reshape(bd, sd, D_MODEL),
            k_p.reshape(1, bp, sp, N_HEADS, V_DIM),
            v_p.reshape(1, bp, sp, N_HEADS, V_DIM),
            h_p.reshape(1, bp, LRU_WIDTH),
            conv_p[None],
            k_s.reshape(1, bd, sd, N_HEADS, V_DIM),
            v_s.reshape(1, bd, sd, N_HEADS, V_DIM),
            h_s.reshape(1, bd, LRU_WIDTH),
            conv_s[None])
```

```python
import functools
import math

import jax
import jax.numpy as jnp
from jax import lax
from jax.experimental import pallas as pl
from jax.experimental.pallas import tpu as pltpu

F32 = jnp.float32
BF16 = jnp.bfloat16
I32 = jnp.int32

D_MODEL = 1024
ATTN_WIDTH = 512
LRU_WIDTH = 512
N_HEADS = 4
V_DIM = 128
QK_DIM = 64
CONV_W = 4
LRU_C = 8.0
IN_WIDTH = 3 * ATTN_WIDTH + 2 * LRU_WIDTH
N_GROUPS = 4
EXPERTS_PER_GROUP = 8
N_EXPERTS = 32
D_EXPERT = 256
PAGE_SIZE = 128
LN_EPS = 1e-5
DEPTH = 1
DEEPNORM_ALPHA = (2 * DEPTH) ** 0.25
LAM_INIT = 0.8 - 0.6 * math.exp(-0.3 * 0)
QK_SCALE = QK_DIM ** -0.5 * math.log2(math.e)

SUBLANES = 8
LANES = 128
VMEM_LIMIT_BYTES = 48 * 1024 * 1024

INPROJ_TM = 512
ATTN_TQ = 512
ATTN_TK = 512
ATTN_CB = 256
ATTN_HEADS_PER_STEP = 2
TOK_TM = 256
MOE_TM = 1024
MOE_CHUNK = MOE_TM // EXPERTS_PER_GROUP
PAGES_PER_STEP = 8
ROUTE_LANES = 128
TILE_ROWS = D_MODEL // LANES


def _cparams(n_axes):
    return pltpu.CompilerParams(
        dimension_semantics=("arbitrary",) * n_axes,
        vmem_limit_bytes=VMEM_LIMIT_BYTES)


def _layer_norm(x, g, b):
    mu = jnp.mean(x, axis=-1, keepdims=True)
    xc = x - mu
    var = jnp.mean(xc * xc, axis=-1, keepdims=True)
    return xc * lax.rsqrt(var + LN_EPS) * g + b


def _diff_lambda(lq1, lk1, lq2, lk2):
    s1 = jnp.sum(lq1 * lk1, axis=-1, keepdims=True)
    s2 = jnp.sum(lq2 * lk2, axis=-1, keepdims=True)
    return jnp.exp(s1) - jnp.exp(s2) + LAM_INIT


def _to_token_tiles(ref, x):
    tm = x.shape[0]
    for s in range(TILE_ROWS):
        ref[pl.ds(s, tm, stride=TILE_ROWS), :] = x[:, s * LANES:(s + 1) * LANES]


def _from_token_tiles(ref, tm):
    return jnp.concatenate(
        [ref[pl.ds(s, tm, stride=TILE_ROWS), :] for s in range(TILE_ROWS)], axis=1)


def _inproj_kernel(x_ref, w_ref, *refs, prompt, tiles_per_seq):
    x = x_ref[...].astype(BF16)
    tm = x.shape[0]

    def section(idx, width):
        return jnp.dot(x, w_ref[:, idx:idx + width], preferred_element_type=F32)

    heads = [slice(h * V_DIM, (h + 1) * V_DIM) for h in range(N_HEADS)]

    def to_head_rows(ref, val):
        for h, sl in enumerate(heads):
            ref[pl.ds(h, tm, stride=N_HEADS), :] = val[:, sl]

    if not prompt:
        q_ref, k_ref, v_ref, xl_ref, gl_ref = refs
        q_ref[...] = section(0, ATTN_WIDTH) * QK_SCALE
        to_head_rows(k_ref, section(ATTN_WIDTH, ATTN_WIDTH))
        to_head_rows(v_ref, section(2 * ATTN_WIDTH, ATTN_WIDTH))
        xl_ref[...] = section(3 * ATTN_WIDTH, LRU_WIDTH)
        gl_ref[...] = section(3 * ATTN_WIDTH + LRU_WIDTH, LRU_WIDTH)
        return

    (cw_ref, cb_ref, wgate_ref, bg_ref, lam_ref, qt_ref, kh_ref, vt_ref, k_ref, v_ref,
     y_ref, hlast_ref, xtail_ref, tail_ref, hc_ref) = refs

    @pl.when(pl.program_id(0) % tiles_per_seq == 0)
    def _():
        tail_ref[...] = jnp.zeros(tail_ref.shape, F32)
        hc_ref[...] = jnp.zeros(hc_ref.shape, F32)

    def emit_q():
        q = section(0, ATTN_WIDTH) * QK_SCALE
        chan = lax.broadcasted_iota(I32, (V_DIM, tm), 0)
        for h, sl in enumerate(heads):
            qt = q[:, sl].T
            qt_ref[h, 0] = jnp.where(chan < QK_DIM, qt, 0.0).astype(BF16)
            qt_ref[h, 1] = jnp.where(chan >= QK_DIM, qt, 0.0).astype(BF16)

    def emit_k():
        k = section(ATTN_WIDTH, ATTN_WIDTH)
        for h, sl in enumerate(heads):
            kh_ref[h] = k[:, sl].astype(BF16)
        to_head_rows(k_ref, k)

    def emit_v():
        v = section(2 * ATTN_WIDTH, ATTN_WIDTH)
        for h, sl in enumerate(heads):
            vt_ref[h, 0] = v[:, sl].T.astype(BF16)
        to_head_rows(v_ref, v)

    xl = section(3 * ATTN_WIDTH, LRU_WIDTH)
    gl = section(3 * ATTN_WIDTH + LRU_WIDTH, LRU_WIDTH)
    xtail_ref[0] = xl[tm - SUBLANES:]
    chunks = _rglru_chunks(xl, gl, y_ref, tail_ref, hc_ref, cw_ref, cb_ref, wgate_ref, bg_ref,
                           lam_ref)
    h_ends = [next(chunks)]
    for emit in (emit_q, emit_k, emit_v):
        emit()
        h_ends.append(next(chunks))
    hlast_ref[0] = jnp.concatenate(h_ends, axis=1)


def _inproj(x2d, w_bf16, lru_args=None, *, prompt, tm, seq=None):
    t = x2d.shape[0]
    assert t % tm == 0
    row = lambda i: (i, 0)
    const = lambda shape: pl.BlockSpec(shape, lambda i: (0,) * len(shape))
    spec512 = pl.BlockSpec((tm, ATTN_WIDTH), row)
    head_rows = pl.BlockSpec((tm * N_HEADS, V_DIM), row)
    f32_512 = jax.ShapeDtypeStruct((t, ATTN_WIDTH), F32)
    f32_heads = jax.ShapeDtypeStruct((t * N_HEADS, V_DIM), F32)
    in_specs = [pl.BlockSpec((tm, D_MODEL), row), const((D_MODEL, IN_WIDTH))]
    operands = [x2d, w_bf16]
    scratch = []
    tiles_per_seq = None
    if prompt:
        assert tm == ATTN_TK and seq % tm == 0
        tiles_per_seq = seq // tm
        batch = t // seq
        per_seq = lambda n: pl.BlockSpec((1, n, LRU_WIDTH), lambda i: (i // tiles_per_seq, 0, 0))
        in_specs += [const((CONV_W, LRU_WIDTH)), const((1, LRU_WIDTH)),
                     const((LRU_WIDTH, 2 * LRU_WIDTH)), const((1, 2 * LRU_WIDTH)),
                     const((1, LRU_WIDTH))]
        operands += list(lru_args)
        out_shape = [jax.ShapeDtypeStruct((N_HEADS, 2, V_DIM, t), BF16),
                     jax.ShapeDtypeStruct((N_HEADS, t, V_DIM), BF16),
                     jax.ShapeDtypeStruct((N_HEADS, t // tm, V_DIM, tm), BF16),
                     f32_heads, f32_heads,
                     jax.ShapeDtypeStruct((t, LRU_WIDTH), BF16),
                     jax.ShapeDtypeStruct((batch, 1, LRU_WIDTH), F32),
                     jax.ShapeDtypeStruct((batch, SUBLANES, LRU_WIDTH), F32)]
        out_specs = [pl.BlockSpec((N_HEADS, 2, V_DIM, tm), lambda i: (0, 0, 0, i)),
                     pl.BlockSpec((N_HEADS, tm, V_DIM), lambda i: (0, i, 0)),
                     pl.BlockSpec((N_HEADS, 1, V_DIM, tm), lambda i: (0, i, 0, 0)),
                     head_rows, head_rows, spec512, per_seq(1), per_seq(SUBLANES)]
        scratch = [pltpu.VMEM((SUBLANES, LRU_WIDTH), F32), pltpu.VMEM((1, LRU_WIDTH), F32)]
    else:
        out_shape = [f32_512, f32_heads, f32_heads, f32_512, f32_512]
        out_specs = [spec512, head_rows, head_rows, spec512, spec512]
    return pl.pallas_call(
        functools.partial(_inproj_kernel, prompt=prompt, tiles_per_seq=tiles_per_seq),
        grid=(t // tm,),
        in_specs=in_specs,
        out_specs=out_specs,
        out_shape=out_shape,
        scratch_shapes=scratch,
        compiler_params=_cparams(1),
        name="inproj_rglru_prompt" if prompt else "inproj_sample",
    )(*operands)


def _attn_prompt_kernel(qt_ref, k_ref, vt_ref, lq1_ref, lk1_ref, lq2_ref, lk2_ref, g_ref,
                        o_ref, s_ref, p_ref, a_ref, m_ref, l_ref, acc_ref):
    tq, tk, cb = ATTN_TQ, ATTN_TK, ATTN_CB
    heads = range(ATTN_HEADS_PER_STEP)
    n_cols = 2 * tq
    i = pl.program_id(2)

    m_ref[...] = jnp.full(m_ref.shape, -jnp.inf, F32)
    l_ref[...] = jnp.zeros(l_ref.shape, F32)
    acc_ref[...] = jnp.zeros(acc_ref.shape, F32)
    for hd in heads:
        p_ref[hd, 1] = jnp.zeros(p_ref.shape[2:], BF16)
        a_ref[hd, 1] = jnp.ones(a_ref.shape[2:], F32)

    def scores(j, slot):
        for hd in heads:
            k = k_ref[hd, pl.ds(pl.multiple_of(j * tk, tk), tk), :]
            for mp in range(2):
                s_ref[hd, slot, :, mp * tq:(mp + 1) * tq] = jnp.dot(
                    k, qt_ref[hd, mp], preferred_element_type=F32)

    def softmax(slot, masked):
        for hd in heads:
            for c in range(n_cols // cb):
                off = (c * cb) % tq
                cs = slice(c * cb, (c + 1) * cb)
                s = s_ref[hd, slot, :, cs]
                if masked:
                    key = lax.broadcasted_iota(I32, s.shape, 0)
                    qpos = lax.broadcasted_iota(I32, s.shape, 1) + off
                    s = jnp.where(key <= qpos, s, -jnp.inf)
                m_prev = m_ref[hd, :, cs]
                m_new = jnp.maximum(m_prev, jnp.max(s, axis=0, keepdims=True))
                alpha = jnp.exp2(m_prev - m_new)
                p = jnp.exp2(s - m_new)
                l_ref[hd, :, cs] = alpha * l_ref[hd, :, cs] + jnp.sum(p, axis=0, keepdims=True)
                m_ref[hd, :, cs] = m_new
                p_ref[hd, slot, :, cs] = p.astype(BF16)
                a_ref[hd, slot, :, cs] = alpha

    def accumulate(j, slot):
        for hd in heads:
            acc_ref[hd] = a_ref[hd, slot] * acc_ref[hd] + jnp.dot(
                vt_ref[hd, j], p_ref[hd, slot], preferred_element_type=F32)

    def stage(j, cur):
        scores(j + 1, 1 - cur)
        softmax(cur, masked=False)
        accumulate(jnp.maximum(j - 1, 0), 1 - cur)

    def diagonal(cur):
        accumulate(jnp.maximum(i - 1, 0), 1 - cur)
        softmax(cur, masked=True)
        accumulate(i, cur)

    scores(0, 0)

    def body(t, carry):
        stage(2 * t, 0)
        stage(2 * t + 1, 1)
        return carry

    lax.fori_loop(0, i // 2, body, 0)

    @pl.when(i % 2 == 0)
    def _():
        diagonal(0)

    @pl.when(i % 2 == 1)
    def _():
        stage(i - 1, 0)
        diagonal(1)

    lam = _diff_lambda(lq1_ref[...], lk1_ref[...], lq2_ref[...], lk2_ref[...])
    for hd in heads:
        o_all = acc_ref[hd] / l_ref[hd]
        o = o_all[:, :tq] - lam * o_all[:, tq:]
        ms = jnp.mean(o * o, axis=0, keepdims=True)
        o = o * lax.rsqrt(ms + LN_EPS) * g_ref[...] * (1.0 - LAM_INIT)
        o_ref[:, hd * V_DIM:(hd + 1) * V_DIM] = o.T.astype(o_ref.dtype)


def _attn_prompt(qt, kh, vt, lam_params, subln_g_col, *, batch, seq):
    assert ATTN_TQ == ATTN_TK and seq % ATTN_TQ == 0 and ATTN_TQ % ATTN_CB == 0
    hp = ATTN_HEADS_PER_STEP
    assert N_HEADS % hp == 0
    nq = seq // ATTN_TQ
    t = batch * seq
    vec = lambda n: pl.BlockSpec((1, n), lambda b, h, i: (0, 0))
    return pl.pallas_call(
        _attn_prompt_kernel,
        grid=(batch, N_HEADS // hp, nq),
        in_specs=[pl.BlockSpec((hp, 2, V_DIM, ATTN_TQ), lambda b, h, i: (h, 0, 0, b * nq + i)),
                  pl.BlockSpec((hp, seq, V_DIM), lambda b, h, i: (h, b, 0)),
                  pl.BlockSpec((hp, nq, V_DIM, ATTN_TK), lambda b, h, i: (h, b, 0, 0)),
                  vec(QK_DIM), vec(QK_DIM), vec(QK_DIM), vec(QK_DIM),
                  pl.BlockSpec((V_DIM, 1), lambda b, h, i: (0, 0))],
        out_specs=pl.BlockSpec((ATTN_TQ, hp * V_DIM), lambda b, h, i: (b * nq + i, h)),
        out_shape=jax.ShapeDtypeStruct((t, ATTN_WIDTH), BF16),
        scratch_shapes=[pltpu.VMEM((hp, 2, ATTN_TK, 2 * ATTN_TQ), F32),
                        pltpu.VMEM((hp, 2, ATTN_TK, 2 * ATTN_TQ), BF16),
                        pltpu.VMEM((hp, 2, 1, 2 * ATTN_TQ), F32),
                        pltpu.VMEM((hp, 1, 2 * ATTN_TQ), F32),
                        pltpu.VMEM((hp, 1, 2 * ATTN_TQ), F32),
                        pltpu.VMEM((hp, V_DIM, 2 * ATTN_TQ), F32)],
        compiler_params=_cparams(3),
        name="attn_prompt",
    )(qt, kh, vt, *lam_params, subln_g_col)


def _attn_sample_kernel(pt_ref, q0_ref, q1_ref, kn0_ref, kn1_ref, vn0_ref, vn1_ref,
                        ck_hbm, cv_hbm, lq1_ref, lk1_ref, lq2_ref, lk2_ref, g_ref,
                        o0_ref, o1_ref, kbuf, vbuf, ksem, vsem, bias_ref, m_ref, l_ref, acc_ref,
                        *, n_chunks, n_steps, dec_seq):
    gpp = PAGES_PER_STEP
    page_rows = PAGE_SIZE * N_HEADS
    hrows = 2 * dec_seq
    g = pl.program_id(0)
    c = g % n_chunks
    slot = g % 2
    streams = ((q0_ref, kn0_ref, vn0_ref, o0_ref), (q1_ref, kn1_ref, vn1_ref, o1_ref))

    def page_copies(step, sl):
        copies = []
        for s in range(len(streams)):
            buf = 2 * s + sl
            for p in range(gpp):
                page = pt_ref[(s * n_steps + step) * gpp + p]
                src = pl.ds(pl.multiple_of(page * page_rows, page_rows), page_rows)
                dst = pl.ds(p * page_rows, page_rows)
                copies.append(pltpu.make_async_copy(ck_hbm.at[src], kbuf.at[buf, dst],
                                                    ksem.at[buf]))
                copies.append(pltpu.make_async_copy(cv_hbm.at[src], vbuf.at[buf, dst],
                                                    vsem.at[buf]))
        return copies

    @pl.when(g == 0)
    def _():
        for cp in page_copies(0, 0):
            cp.start()

    @pl.when(g + 1 < n_steps)
    def _():
        for cp in page_copies(g + 1, 1 - slot):
            cp.start()

    def head_bias(shape):
        row_head = lax.broadcasted_iota(I32, shape, 0) // hrows
        col_head = lax.broadcasted_iota(I32, shape, 1) % N_HEADS
        return row_head == col_head

    @pl.when(g == 0)
    def _():
        bias_ref[...] = jnp.where(head_bias(bias_ref.shape), 0.0, -jnp.inf)

    @pl.when(c == 0)
    def _():
        m_ref[...] = jnp.full(m_ref.shape, -jnp.inf, F32)
        l_ref[...] = jnp.zeros(l_ref.shape, F32)
        acc_ref[...] = jnp.zeros(acc_ref.shape, F32)

    lane = lax.broadcasted_iota(I32, (dec_seq, V_DIM), 1)

    def query_block(q_ref):
        q = q_ref[...]
        q_rows = []
        for h in range(N_HEADS):
            qh = q[:, h * V_DIM:(h + 1) * V_DIM]
            q_rows += [jnp.where(lane < QK_DIM, qh, 0.0), jnp.where(lane >= QK_DIM, qh, 0.0)]
        return jnp.concatenate(q_rows, axis=0).astype(BF16)

    def update(s_idx, qblk, k, v, bias):
        s = lax.dot_general(qblk, k, (((1,), (1,)), ((), ())),
                            preferred_element_type=F32) + bias
        m_prev = m_ref[s_idx]
        m_new = jnp.maximum(m_prev, jnp.max(s, axis=-1, keepdims=True))
        alpha = jnp.exp2(m_prev - m_new)
        p = jnp.exp2(s - m_new)
        l_ref[s_idx] = alpha * l_ref[s_idx] + jnp.sum(p, axis=-1, keepdims=True)
        acc_ref[s_idx] = alpha * acc_ref[s_idx] + jnp.dot(p.astype(BF16), v,
                                                          preferred_element_type=F32)
        m_ref[s_idx] = m_new

    for cp in page_copies(g, slot):
        cp.wait()
    qblks = [query_block(refs[0]) for refs in streams]
    for s_idx in range(len(streams)):
        buf = 2 * s_idx + slot
        update(s_idx, qblks[s_idx], kbuf[buf].astype(BF16), vbuf[buf].astype(BF16),
               bias_ref[...])

    @pl.when(c == n_chunks - 1)
    def _():
        new_rows = dec_seq * N_HEADS
        pad = jnp.zeros((LANES - new_rows, V_DIM), F32)
        shape = (N_HEADS * hrows, LANES)
        qi = lax.broadcasted_iota(I32, shape, 0) % dec_seq
        kj = lax.broadcasted_iota(I32, shape, 1) // N_HEADS
        new_bias = jnp.where(head_bias(shape) & (kj <= qi), 0.0, -jnp.inf)
        lam = _diff_lambda(lq1_ref[...], lk1_ref[...], lq2_ref[...], lk2_ref[...])
        for s_idx, (_, kn_ref, vn_ref, o_ref) in enumerate(streams):
            kn = jnp.concatenate([kn_ref[...], pad], axis=0).astype(BF16)
            vn = jnp.concatenate([vn_ref[...], pad], axis=0).astype(BF16)
            update(s_idx, qblks[s_idx], kn, vn, new_bias)
            o_heads = acc_ref[s_idx] / l_ref[s_idx]
            for h in range(N_HEADS):
                o_all = o_heads[h * hrows:(h + 1) * hrows]
                o = o_all[:dec_seq] - lam * o_all[dec_seq:]
                ms = jnp.mean(o * o, axis=-1, keepdims=True)
                o_ref[:, h * V_DIM:(h + 1) * V_DIM] = (
                    o * lax.rsqrt(ms + LN_EPS) * g_ref[...] * (1.0 - LAM_INIT))


def _attn_sample(q, k_new, v_new, cache_k_rows, cache_v_rows, page_table, lam_params, subln_g,
                 *, dec_batch, dec_seq):
    n_pages = page_table.shape[1]
    assert n_pages % PAGES_PER_STEP == 0 and dec_seq % SUBLANES == 0
    n_chunks = n_pages // PAGES_PER_STEP
    n_streams = 2
    assert dec_batch % n_streams == 0
    half = dec_batch // n_streams
    n_steps = half * n_chunks
    rows = 2 * N_HEADS * dec_seq
    chunk_rows = PAGES_PER_STEP * PAGE_SIZE * N_HEADS

    def stream_specs(s):
        tok = pl.BlockSpec((dec_seq, ATTN_WIDTH), lambda g, pt: (s * half + g // n_chunks, 0))
        tok_heads = pl.BlockSpec((dec_seq * N_HEADS, V_DIM),
                                 lambda g, pt: (s * half + g // n_chunks, 0))
        return tok, tok_heads

    (tok0, heads0), (tok1, heads1) = stream_specs(0), stream_specs(1)
    out_spec = pl.BlockSpec((dec_seq, ATTN_WIDTH), lambda g, pt: (g // n_chunks, 0))
    vec = lambda n: pl.BlockSpec((1, n), lambda g, pt: (0, 0))
    any_spec = pl.BlockSpec(memory_space=pl.ANY)
    n_bufs = 2 * n_streams
    grid_spec = pltpu.PrefetchScalarGridSpec(
        num_scalar_prefetch=1,
        grid=(n_steps,),
        in_specs=[tok0, tok1, heads0, heads1, heads0, heads1, any_spec, any_spec,
                  vec(QK_DIM), vec(QK_DIM), vec(QK_DIM), vec(QK_DIM), vec(V_DIM)],
        out_specs=[out_spec, out_spec],
        scratch_shapes=[pltpu.VMEM((n_bufs, chunk_rows, V_DIM), F32),
                        pltpu.VMEM((n_bufs, chunk_rows, V_DIM), F32),
                        pltpu.SemaphoreType.DMA((n_bufs,)),
                        pltpu.SemaphoreType.DMA((n_bufs,)),
                        pltpu.VMEM((rows, chunk_rows), F32),
                        pltpu.VMEM((n_streams, rows, 1), F32),
                        pltpu.VMEM((n_streams, rows, 1), F32),
                        pltpu.VMEM((n_streams, rows, V_DIM), F32)])
    half_out = jax.ShapeDtypeStruct((half * dec_seq, ATTN_WIDTH), F32)
    o0, o1 = pl.pallas_call(
        functools.partial(_attn_sample_kernel, n_chunks=n_chunks, n_steps=n_steps,
                          dec_seq=dec_seq),
        grid_spec=grid_spec,
        out_shape=[half_out, half_out],
        compiler_params=_cparams(1),
        name="attn_sample",
    )(page_table.reshape(-1), q, q, k_new, k_new, v_new, v_new, cache_k_rows, cache_v_rows,
      *lam_params, subln_g)
    return jnp.concatenate([o0, o1], axis=0)


def _softplus(x):
    y = jnp.exp(-jnp.abs(x))
    w = 1.0 + y
    log1p = jnp.where(w == 1.0, y, jnp.log(w) * (y / (w - 1.0)))
    return jnp.maximum(x, 0.0) + log1p


def _rglru_tile(*args):
    return jnp.concatenate(list(_rglru_chunks(*args)), axis=1)


def _rglru_chunks(x, gl, y_ref, tail_ref, hc_ref, cw_ref, cb_ref, wg_ref, bg_ref, lam_ref):
    for c in range(LRU_WIDTH // LANES):
        cs = slice(c * LANES, (c + 1) * LANES)
        gate_w = jnp.concatenate(
            [wg_ref[cs, cs], wg_ref[cs, LRU_WIDTH + c * LANES:LRU_WIDTH + (c + 1) * LANES]], axis=1)
        gate_b = jnp.concatenate(
            [bg_ref[:, cs], bg_ref[:, LRU_WIDTH + c * LANES:LRU_WIDTH + (c + 1) * LANES]], axis=1)
        xs = x[:, cs]
        y, h_end = _rglru_chunk(xs, gl[:, cs], tail_ref[:, cs], hc_ref[:, cs],
                                cw_ref[:, cs], cb_ref[:, cs], gate_w, gate_b, lam_ref[:, cs])
        tail_ref[:, cs] = xs[xs.shape[0] - SUBLANES:]
        hc_ref[:, cs] = h_end
        y_ref[:, cs] = y.astype(y_ref.dtype)
        yield h_end


def _rglru_chunk(x, gl, tail, h_in, cw, cb, gate_w, gate_b, lam):
    ts, n = x.shape
    xfull = jnp.concatenate([tail, x], axis=0)
    xc = cb + cw[CONV_W - 1:CONV_W] * x
    for d in range(1, CONV_W):
        shifted = pltpu.roll(xfull, d, 0)[SUBLANES:]
        xc = xc + cw[CONV_W - 1 - d:CONV_W - d] * shifted

    gates = jnp.dot(xc.astype(BF16), gate_w, preferred_element_type=F32) + gate_b
    r = jax.nn.sigmoid(gates[:, :n])
    ig = jax.nn.sigmoid(gates[:, n:])
    log_a = -LRU_C * r * _softplus(-lam)
    a = jnp.exp(log_a)
    mult = jnp.sqrt(jnp.maximum(-(a * a + 1.0) * jnp.tanh(log_a), 0.0))
    u = mult * ig * xc

    row = lax.broadcasted_iota(I32, a.shape, 0)
    d = 1
    while d < ts:
        if d < SUBLANES:
            keep = row >= d
            a_prev = jnp.where(keep, pltpu.roll(a, d, 0), 1.0)
            u_prev = jnp.where(keep, pltpu.roll(u, d, 0), 0.0)
            u = a * u_prev + u
            a = a * a_prev
        else:
            u = jnp.concatenate([u[:d], a[d:] * u[:ts - d] + u[d:]], axis=0)
            a = jnp.concatenate([a[:d], a[d:] * a[:ts - d]], axis=0)
        d *= 2
    h = a * h_in + u
    return h * jax.nn.gelu(gl, approximate=True), h[ts - 1:ts]


def _rglru_kernel(xl_ref, gl_ref, conv0_ref, h0_ref, cw_ref, cb_ref, wg_ref, bg_ref, lam_ref,
                  y_ref, hlast_ref, tail_ref, hc_ref):
    s = pl.program_id(1)

    @pl.when(s == 0)
    def _():
        tail_ref[...] = conv0_ref[0]
        hc_ref[...] = h0_ref[0]

    hlast_ref[0] = _rglru_tile(xl_ref[...], gl_ref[...], y_ref, tail_ref, hc_ref,
                               cw_ref, cb_ref, wg_ref, bg_ref, lam_ref)


def _rglru(xl, gl, conv0, h0, conv_w, conv_b, w_gates, b_gates, lam, *, batch, seq, ts):
    assert seq % ts == 0 and ts % SUBLANES == 0
    ns = seq // ts
    tile = pl.BlockSpec((ts, LRU_WIDTH), lambda b, s: (b * ns + s, 0))
    const = lambda shape: pl.BlockSpec(shape, lambda b, s: (0,) * len(shape))
    return pl.pallas_call(
        _rglru_kernel,
        grid=(batch, ns),
        in_specs=[tile, tile,
                  pl.BlockSpec((1, SUBLANES, LRU_WIDTH), lambda b, s: (b, 0, 0)),
                  pl.BlockSpec((1, 1, LRU_WIDTH), lambda b, s: (b, 0, 0)),
                  const((CONV_W, LRU_WIDTH)), const((1, LRU_WIDTH)),
                  const((LRU_WIDTH, 2 * LRU_WIDTH)), const((1, 2 * LRU_WIDTH)),
                  const((1, LRU_WIDTH))],
        out_specs=[tile, pl.BlockSpec((1, 1, LRU_WIDTH), lambda b, s: (b, 0, 0))],
        out_shape=[jax.ShapeDtypeStruct((batch * seq, LRU_WIDTH), BF16 if ts > SUBLANES else F32),
                   jax.ShapeDtypeStruct((batch, 1, LRU_WIDTH), F32)],
        scratch_shapes=[pltpu.VMEM((SUBLANES, LRU_WIDTH), F32),
                        pltpu.VMEM((1, LRU_WIDTH), F32)],
        compiler_params=_cparams(2),
        name="rglru_prompt" if ts > SUBLANES else "rglru_sample",
    )(xl, gl, conv0, h0, conv_w, conv_b, w_gates, b_gates, lam)


def _route(logits):
    lane = lax.broadcasted_iota(I32, logits.shape, 1).astype(F32)
    big = float(ROUTE_LANES)
    neg = -jnp.inf
    is_g = lane < N_GROUPS
    gmax = jnp.max(jnp.where(is_g, logits, neg), axis=-1, keepdims=True)
    gsum = jnp.sum(jnp.where(is_g, jnp.exp(logits - gmax), 0.0), axis=-1, keepdims=True)
    gw = 1.0 / gsum
    gid = jnp.min(jnp.where(is_g & (logits == gmax), lane, big), axis=-1, keepdims=True)
    lo = N_GROUPS + EXPERTS_PER_GROUP * gid
    in_grp = (lane >= lo) & (lane < lo + EXPERTS_PER_GROUP)
    v1 = jnp.max(jnp.where(in_grp, logits, neg), axis=-1, keepdims=True)
    i1 = jnp.min(jnp.where(in_grp & (logits == v1), lane, big), axis=-1, keepdims=True)
    rest = in_grp & (lane != i1)
    v2 = jnp.max(jnp.where(rest, logits, neg), axis=-1, keepdims=True)
    i2 = jnp.min(jnp.where(rest & (logits == v2), lane, big), axis=-1, keepdims=True)
    t = jnp.exp(v2 - v1)
    w1 = gw / (1.0 + t)
    w2 = gw * t / (1.0 + t)
    info = jnp.where(lane == 0, i1 - N_GROUPS,
                     jnp.where(lane == 1, i2 - N_GROUPS,
                               jnp.where(lane == 2, w1, jnp.where(lane == 3, w2, 0.0))))
    return info


def _merge_kernel(xp_ref, xs_ref, op_ref, os_ref, yp_ref, ys_ref, wo_ref, g_ref, b_ref,
                  wr_ref, br_ref, x1_ref, x1t_ref, info_ref, *, n_prompt_tiles):
    i = pl.program_id(0)

    def body(x_ref, o_ref, y_ref):
        m = jnp.dot(o_ref[...].astype(BF16), wo_ref[:ATTN_WIDTH, :], preferred_element_type=F32)
        m = m + jnp.dot(y_ref[...].astype(BF16), wo_ref[ATTN_WIDTH:, :],
                        preferred_element_type=F32)
        x1 = _layer_norm(DEEPNORM_ALPHA * x_ref[...] + m, g_ref[...], b_ref[...])
        x1_ref[...] = x1
        _to_token_tiles(x1t_ref, x1)
        logits = jnp.dot(x1.astype(BF16), wr_ref[...],
                         preferred_element_type=F32) + br_ref[...]
        info_ref[...] = _route(logits)

    @pl.when(i < n_prompt_tiles)
    def _():
        body(xp_ref, op_ref, yp_ref)

    @pl.when(i >= n_prompt_tiles)
    def _():
        body(xs_ref, os_ref, ys_ref)


def _merge(xp, xs, o_p, o_s, y_p, y_s, wo_bf16, g1, b1, wr, br):
    tm = TOK_TM
    tp, ts = xp.shape[0], xs.shape[0]
    assert tp % tm == 0 and ts % tm == 0
    npt, nst = tp // tm, ts // tm
    t_all = tp + ts
    pidx = lambda i: (jnp.minimum(i, npt - 1), 0)
    sidx = lambda i: (jnp.maximum(i - npt, 0), 0)
    const = lambda shape: pl.BlockSpec(shape, lambda i: (0, 0))
    row = lambda i: (i, 0)
    return pl.pallas_call(
        functools.partial(_merge_kernel, n_prompt_tiles=npt),
        grid=(npt + nst,),
        in_specs=[pl.BlockSpec((tm, D_MODEL), pidx), pl.BlockSpec((tm, D_MODEL), sidx),
                  pl.BlockSpec((tm, ATTN_WIDTH), pidx), pl.BlockSpec((tm, ATTN_WIDTH), sidx),
                  pl.BlockSpec((tm, LRU_WIDTH), pidx), pl.BlockSpec((tm, LRU_WIDTH), sidx),
                  const((D_MODEL, D_MODEL)), const((1, D_MODEL)), const((1, D_MODEL)),
                  const((D_MODEL, ROUTE_LANES)), const((1, ROUTE_LANES))],
        out_specs=[pl.BlockSpec((tm, D_MODEL), row),
                   pl.BlockSpec((tm * TILE_ROWS, LANES), row),
                   pl.BlockSpec((tm, ROUTE_LANES), row)],
        out_shape=[jax.ShapeDtypeStruct((t_all, D_MODEL), F32),
                   jax.ShapeDtypeStruct((t_all * TILE_ROWS, LANES), F32),
                   jax.ShapeDtypeStruct((t_all, ROUTE_LANES), F32)],
        compiler_params=_cparams(1),
        name="merge_ln1_route",
    )(xp, xs, o_p, o_s, y_p, y_s, wo_bf16, g1, b1, wr, br)


def _tile_group(i, cum_ref):
    t = jnp.minimum(i, cum_ref[N_GROUPS - 1] - 1)
    g = jnp.int32(0)
    for k in range(N_GROUPS - 1):
        g = g + (cum_ref[k] <= t).astype(I32)
    return g


def _expert_kernel(cum_ref, src0_ref, gsrc_ref, sdst_ref,
                   x_hbm, info_ref, wg_ref, wu_ref, wd_ref, out_hbm,
                   xbuf, ybuf, xb_ref, infot_ref, yacc_ref, gsem, ssem):
    tm, ch, tr = MOE_TM, MOE_CHUNK, TILE_ROWS
    last_e = EXPERTS_PER_GROUP - 1
    i = pl.program_id(0)
    e = pl.program_id(1)
    slot = i % 2
    other = 1 - slot
    n_live = cum_ref[N_GROUPS - 1]

    def start_gather(idx_ref, rows, base, sl):
        for r in rows:
            src = pl.ds(pl.multiple_of(idx_ref[0, 0, r], tr), tr)
            dst = pl.ds(pl.multiple_of((base + r) * tr, tr), tr)
            pltpu.make_async_copy(x_hbm.at[src], xbuf.at[sl, dst],
                                  gsem.at[sl]).start(priority=r % 2)

    def start_scatter(idx_ref, rows, base, sl):
        for r in rows:
            src = pl.ds(pl.multiple_of((base + r) * tr, tr), tr)
            dst = pl.ds(pl.multiple_of(idx_ref[0, 0, r], tr), tr)
            pltpu.make_async_copy(ybuf.at[sl, src], out_hbm.at[dst],
                                  ssem.at[sl]).start(priority=r % 2)

    def wait_tile(buf, sem):
        pltpu.make_async_copy(x_hbm.at[pl.ds(0, tm * tr)], buf, sem).wait()

    def move_rows():
        start_gather(gsrc_ref, range(ch), e * ch, other)
        start_scatter(sdst_ref, range(ch), e * ch, other)

    @pl.when((i == 0) & (e == 0))
    def _():
        ybuf[1] = jnp.zeros(ybuf.shape[1:], F32)
        yacc_ref[...] = jnp.zeros(yacc_ref.shape, F32)
        start_gather(src0_ref, range(tm), 0, 0)

    @pl.when((e == 0) & (i <= n_live))
    def _():
        wait_tile(xbuf.at[slot], gsem.at[slot])

    @pl.when((e == last_e) & (i >= 1) & (i <= n_live))
    def _():
        wait_tile(ybuf.at[slot], ssem.at[slot])

    @pl.when(i == n_live)
    def _():
        move_rows()

    @pl.when((e == 0) & (i < n_live))
    def _():
        xb_ref[...] = _from_token_tiles(xbuf.at[slot], tm).astype(BF16)
        infot_ref[...] = info_ref[0].T

    @pl.when(i < n_live)
    def _():
        move_rows()
        w_in = jnp.concatenate([wg_ref[0], wu_ref[0]], axis=1).astype(BF16)
        gu = jnp.dot(xb_ref[...], w_in, preferred_element_type=F32)
        gate = gu[:, :D_EXPERT]
        up = gu[:, D_EXPERT:]
        expert = (_tile_group(i, cum_ref) * EXPERTS_PER_GROUP + e).astype(F32)
        info = infot_ref[...]
        comb =(jnp.where(info[:, 0:1] == expert, info[:, 2:3], 0.0)
                + jnp.where(info[:, 1:2] == expert, info[:, 3:4], 0.0))
        h = (gate * jax.nn.sigmoid(gate) * up * comb).astype(BF16)
        y = jnp.dot(h, wd_ref[0].astype(BF16), preferred_element_type=F32)
        yacc_ref[...] = y + jnp.where(e > 0, yacc_ref[...], 0.0)

    @pl.when((e == last_e) & (i < n_live))
    def _():
        _to_token_tiles(ybuf.at[slot], yacc_ref[...])

    @pl.when((e == last_e) & (i == n_live))
    def _():
        wait_tile(xbuf.at[other], gsem.at[other])
        wait_tile(ybuf.at[other], ssem.at[other])


def _experts(x1t, cum, src0, gsrc, sdst, info_sorted, wg, wu, wd, *, n_tiles, n_rows):
    tm, ch = MOE_TM, MOE_CHUNK
    ne = EXPERTS_PER_GROUP
    last = n_tiles - 1
    smem = lambda n, fn: pl.BlockSpec((1, 1, n), fn, memory_space=pltpu.SMEM)
    weight = lambda shape: pl.BlockSpec(
        shape, lambda i, e, cum: (_tile_group(i, cum) * ne + e, 0, 0))
    any_spec = pl.BlockSpec(memory_space=pl.ANY)
    grid_spec = pltpu.PrefetchScalarGridSpec(
        num_scalar_prefetch=1,
        grid=(n_tiles + 1, ne),
        in_specs=[smem(tm, lambda i, e, cum: (0, 0, 0)),
                  smem(ch, lambda i, e, cum: (jnp.minimum(i + 1, last) * ne + e, 0, 0)),
                  smem(ch, lambda i, e, cum: (i * ne + e, 0, 0)),
                  any_spec,
                  pl.BlockSpec((1, SUBLANES, tm), lambda i, e, cum: (jnp.minimum(i, last), 0, 0)),
                  weight((1, D_MODEL, D_EXPERT)), weight((1, D_MODEL, D_EXPERT)),
                  weight((1, D_EXPERT, D_MODEL))],
        out_specs=any_spec,
        scratch_shapes=[pltpu.VMEM((2, tm * TILE_ROWS, LANES), F32),
                        pltpu.VMEM((2, tm * TILE_ROWS, LANES), F32),
                        pltpu.VMEM((tm, D_MODEL), BF16),
                        pltpu.VMEM((tm, SUBLANES), F32),
                        pltpu.VMEM((tm, D_MODEL), F32),
                        pltpu.SemaphoreType.DMA((2,)),
                        pltpu.SemaphoreType.DMA((2,))])
    return pl.pallas_call(
        _expert_kernel,
        grid_spec=grid_spec,
        out_shape=jax.ShapeDtypeStruct((n_rows * TILE_ROWS, LANES), F32),
        compiler_params=_cparams(2),
        name="expert_mlp",
    )(cum, src0, gsrc, sdst, x1t, info_sorted, wg, wu, wd)


def _expert_plan(info, n_tiles):
    tm, ch = MOE_TM, MOE_CHUNK
    t_all = info.shape[0]
    gid = info[:, 0].astype(I32) // EXPERTS_PER_GROUP
    _, order = lax.sort((gid, lax.iota(I32, t_all)), num_keys=1, is_stable=True)
    groups = jnp.arange(N_GROUPS, dtype=I32)
    counts = jnp.sum((gid[:, None] == groups[None, :]).astype(I32), axis=0)
    start = jnp.cumsum(counts) - counts
    tiles_g = (counts + tm - 1) // tm
    cum = jnp.cumsum(tiles_g).astype(I32)
    n_live = cum[-1]
    ti = jnp.arange(n_tiles, dtype=I32)
    g_i = jnp.minimum(jnp.sum((cum[None, :] <= ti[:, None]).astype(I32), axis=1), N_GROUPS - 1)
    j = ti - (cum - tiles_g)[g_i]
    live = ti < n_live
    tile_rows = jnp.where(live, jnp.clip(counts[g_i] - j * tm, 0, tm), 0)
    tile_start = jnp.where(live, start[g_i] + j * tm, 0)
    r = jnp.arange(tm, dtype=I32)[None, :]
    valid = r < tile_rows[:, None]
    tok = order[jnp.minimum(tile_start[:, None] + r, t_all - 1)]
    src = (jnp.where(valid, tok, 0) * TILE_ROWS).astype(I32)
    dump = ((t_all + r) * TILE_ROWS).astype(I32)
    dst = (jnp.where(valid, tok, t_all + r) * TILE_ROWS).astype(I32)
    dst_prev = jnp.concatenate([dump, dst], axis=0)
    rec = [jnp.where(valid, info[:, c][tok], 0.0) for c in range(4)]
    rec += [jnp.zeros((n_tiles, tm), F32)] * (SUBLANES - len(rec))
    info_sorted = jnp.stack(rec, axis=1)
    return (cum, src[0].reshape(1, 1, tm), src.reshape(n_tiles * EXPERTS_PER_GROUP, 1, ch),
            dst_prev.reshape((n_tiles + 1) * EXPERTS_PER_GROUP, 1, ch), info_sorted)


def _combine_kernel(x1_ref, f_ref, g_ref, b_ref, outp_ref, outs_ref, *, n_prompt_tiles):
    tm = TOK_TM
    i = pl.program_id(0)
    f = _from_token_tiles(f_ref, tm)
    out = _layer_norm(DEEPNORM_ALPHA * x1_ref[...] + f, g_ref[...], b_ref[...])

    @pl.when(i < n_prompt_tiles)
    def _():
        outp_ref[...] = out

    @pl.when(i >= n_prompt_tiles)
    def _():
        outs_ref[...] = out


def _combine(x1, f_tiles, g2, b2, *, n_prompt, n_sample):
    tm = TOK_TM
    assert n_sample == tm
    npt = n_prompt // tm
    nt = npt + 1
    row = lambda i: (i, 0)
    const = lambda shape: pl.BlockSpec(shape, lambda i: (0, 0))
    return pl.pallas_call(
        functools.partial(_combine_kernel, n_prompt_tiles=npt),
        grid=(nt,),
        in_specs=[pl.BlockSpec((tm, D_MODEL), row),
                  pl.BlockSpec((tm * TILE_ROWS, LANES), row),
                  const((1, D_MODEL)), const((1, D_MODEL))],
        out_specs=[pl.BlockSpec((tm, D_MODEL), lambda i: (jnp.minimum(i, npt - 1), 0)),
                   pl.BlockSpec((tm, D_MODEL), lambda i: (0, 0))],
        out_shape=[jax.ShapeDtypeStruct((n_prompt, D_MODEL), F32),
                   jax.ShapeDtypeStruct((n_sample, D_MODEL), F32)],
        compiler_params=_cparams(1),
        name="combine_ln2",
    )(x1, f_tiles, g2, b2)


def _block_diag(w):
    nb, n, _ = w.shape
    eye = jnp.eye(nb, dtype=w.dtype)
    return (eye[:, None, :, None] * w[:, :, None, :]).reshape(nb * n, nb * n)


def kernel(x_prompt, x_sample, cache_k, cache_v, state_h, state_conv, page_table, w_in, lambda_q1, lambda_k1, lambda_q2, lambda_k2, subln_g, conv_w, conv_b, w_rg_a, b_rg_a, w_rg_x, b_rg_x, lru_lambda, w_out, ln1_g, ln1_b, w_group_router, b_group_router, w_expert_router, b_expert_router, w_gate_e, w_up_e, w_down_e, ln2_g, ln2_b):
    bp, sp, _ = x_prompt.shape
    bd, sd, _ = x_sample.shape
    assert w_in.shape[0] == DEPTH == 1
    l = 0
    n_prompt, n_sample = bp * sp, bd * sd
    t_all = n_prompt + n_sample

    xp = x_prompt.reshape(n_prompt, D_MODEL)
    xs = x_sample.reshape(n_sample, D_MODEL)
    w_in_b = w_in[l].astype(BF16)
    w_out_b = w_out[l].astype(BF16)
    lam_params = (lambda_q1[l][None], lambda_k1[l][None], lambda_q2[l][None], lambda_k2[l][None])
    w_gates = jnp.concatenate([_block_diag(w_rg_a[l]), _block_diag(w_rg_x[l])], axis=1).astype(BF16)
    b_gates = jnp.concatenate([b_rg_a[l].reshape(1, -1), b_rg_x[l].reshape(1, -1)], axis=1)
    lru_args = (conv_w[l], conv_b[l][None], w_gates, b_gates, lru_lambda[l][None])

    qt, kh, vt, k_p, v_p, y_p, h_p, xl_tail = _inproj(xp, w_in_b, lru_args, prompt=True,
                                                      tm=INPROJ_TM, seq=sp)
    o_p = _attn_prompt(qt, kh, vt, lam_params, subln_g[l][:, None], batch=bp, seq=sp)

    q_s, k_s, v_s, xl_s, gl_s = _inproj(xs, w_in_b, prompt=False, tm=n_sample)
    n_phys = cache_k.shape[1]
    ck = cache_k.reshape(DEPTH * n_phys * PAGE_SIZE * N_HEADS, V_DIM)
    cv = cache_v.reshape(DEPTH * n_phys * PAGE_SIZE * N_HEADS, V_DIM)
    o_s = _attn_sample(q_s, k_s, v_s, ck, cv, page_table, lam_params, subln_g[l][None],
                       dec_batch=bd, dec_seq=sd)
    conv0 = jnp.concatenate(
        [jnp.zeros((bd, SUBLANES - (CONV_W - 1), LRU_WIDTH), F32), state_conv[l]], axis=1)
    y_s, h_s = _rglru(xl_s, gl_s, conv0, state_h[l][:, None, :], *lru_args,
                      batch=bd, seq=sd, ts=sd)

    w_router = jnp.concatenate(
        [w_group_router[l], w_expert_router[l],
         jnp.zeros((D_MODEL, ROUTE_LANES - N_GROUPS - N_EXPERTS), F32)], axis=1).astype(BF16)
    b_router = jnp.concatenate(
        [b_group_router[l], b_expert_router[l],
         jnp.zeros((ROUTE_LANES - N_GROUPS - N_EXPERTS,), F32)])[None]
    x1, x1t, info = _merge(xp, xs, o_p, o_s, y_p, y_s, w_out_b, ln1_g[l][None], ln1_b[l][None],
                           w_router, b_router)
    n_tiles = -(-t_all // MOE_TM) + N_GROUPS
    cum, src0, gsrc, sdst, info_sorted = _expert_plan(info, n_tiles)
    f_tiles = _experts(x1t, cum, src0, gsrc, sdst, info_sorted,
                       w_gate_e[l], w_up_e[l], w_down_e[l],
                       n_tiles=n_tiles, n_rows=t_all + MOE_TM)
    out_p, out_s = _combine(x1, f_tiles, ln2_g[l][None], ln2_b[l][None],
                            n_prompt=n_prompt, n_sample=n_sample)

    conv_p = xl_tail[:, SUBLANES - (CONV_W - 1):]
    conv_s = jnp.concatenate([state_conv[l], xl_s.reshape(bd, sd, LRU_WIDTH)],
                             axis=1)[:, -(CONV_W - 1):]
    return (out_p.reshape(bp, sp, D_MODEL),
            out_s.reshape(bd, sd, D_MODEL),
            k_p.reshape(1, bp, sp, N_HEADS, V_DIM),
            v_p.reshape(1, bp, sp, N_HEADS, V_DIM),
            h_p.reshape(1, bp, LRU_WIDTH),
            conv_p[None],
            k_s.reshape(1, bd, sd, N_HEADS, V_DIM),
            v_s.reshape(1, bd, sd, N_HEADS, V_DIM),
            h_s.reshape(1, bd, LRU_WIDTH),
            conv_s[None])
```

```python
import functools
import math

import jax
import jax.numpy as jnp
from jax import lax
from jax.experimental import pallas as pl
from jax.experimental.pallas import tpu as pltpu

F32 = jnp.float32
BF16 = jnp.bfloat16
I32 = jnp.int32

D_MODEL = 1024
ATTN_WIDTH = 512
LRU_WIDTH = 512
N_HEADS = 4
V_DIM = 128
QK_DIM = 64
CONV_W = 4
LRU_C = 8.0
IN_WIDTH = 3 * ATTN_WIDTH + 2 * LRU_WIDTH
N_GROUPS = 4
EXPERTS_PER_GROUP = 8
N_EXPERTS = 32
D_EXPERT = 256
PAGE_SIZE = 128
LN_EPS = 1e-5
DEPTH = 1
DEEPNORM_ALPHA = (2 * DEPTH) ** 0.25
LAM_INIT = 0.8 - 0.6 * math.exp(-0.3 * 0)
QK_SCALE = QK_DIM ** -0.5 * math.log2(math.e)

SUBLANES = 8
LANES = 128
VMEM_LIMIT_BYTES = 48 * 1024 * 1024

INPROJ_TM = 512
ATTN_TQ = 512
ATTN_TK = 512
ATTN_CB = 256
ATTN_HEADS_PER_STEP = 2
TOK_TM = 256
MOE_TM = 1024
MOE_CHUNK = MOE_TM // EXPERTS_PER_GROUP
PAGES_PER_STEP = 16
ROUTE_LANES = 128
TILE_ROWS = D_MODEL // LANES


def _cparams(n_axes):
    return pltpu.CompilerParams(
        dimension_semantics=("arbitrary",) * n_axes,
        vmem_limit_bytes=VMEM_LIMIT_BYTES)


def _layer_norm(x, g, b):
    mu = jnp.mean(x, axis=-1, keepdims=True)
    xc = x - mu
    var = jnp.mean(xc * xc, axis=-1, keepdims=True)
    return xc * lax.rsqrt(var + LN_EPS) * g + b


def _diff_lambda(lq1, lk1, lq2, lk2):
    s1 = jnp.sum(lq1 * lk1, axis=-1, keepdims=True)
    s2 = jnp.sum(lq2 * lk2, axis=-1, keepdims=True)
    return jnp.exp(s1) - jnp.exp(s2) + LAM_INIT


def _to_token_tiles(ref, x):
    tm = x.shape[0]
    for s in range(TILE_ROWS):
        ref[pl.ds(s, tm, stride=TILE_ROWS), :] = x[:, s * LANES:(s + 1) * LANES]


def _from_token_tiles(ref, tm):
    return jnp.concatenate(
        [ref[pl.ds(s, tm, stride=TILE_ROWS), :] for s in range(TILE_ROWS)], axis=1)


def _inproj_kernel(x_ref, w_ref, *refs, prompt, tiles_per_seq):
    x = x_ref[...].astype(BF16)
    tm = x.shape[0]

    def section(idx, width):
        return jnp.dot(x, w_ref[:, idx:idx + width], preferred_element_type=F32)

    heads = [slice(h * V_DIM, (h + 1) * V_DIM) for h in range(N_HEADS)]

    def to_head_rows(ref, val):
        for h, sl in enumerate(heads):
            ref[pl.ds(h, tm, stride=N_HEADS), :] = val[:, sl]

    if not prompt:
        q_ref, k_ref, v_ref, xl_ref, gl_ref = refs
        q_ref[...] = section(0, ATTN_WIDTH) * QK_SCALE
        to_head_rows(k_ref, section(ATTN_WIDTH, ATTN_WIDTH))
        to_head_rows(v_ref, section(2 * ATTN_WIDTH, ATTN_WIDTH))
        xl_ref[...] = section(3 * ATTN_WIDTH, LRU_WIDTH)
        gl_ref[...] = section(3 * ATTN_WIDTH + LRU_WIDTH, LRU_WIDTH)
        return

    (cw_ref, cb_ref, wgate_ref, bg_ref, lam_ref, qt_ref, kh_ref, vt_ref, k_ref, v_ref,
     y_ref, hlast_ref, xtail_ref, tail_ref, hc_ref) = refs

    @pl.when(pl.program_id(0) % tiles_per_seq == 0)
    def _():
        tail_ref[...] = jnp.zeros(tail_ref.shape, F32)
        hc_ref[...] = jnp.zeros(hc_ref.shape, F32)

    def emit_q():
        q = section(0, ATTN_WIDTH) * QK_SCALE
        chan = lax.broadcasted_iota(I32, (V_DIM, tm), 0)
        for h, sl in enumerate(heads):
            qt = q[:, sl].T
            qt_ref[h, 0] = jnp.where(chan < QK_DIM, qt, 0.0).astype(BF16)
            qt_ref[h, 1] = jnp.where(chan >= QK_DIM, qt, 0.0).astype(BF16)

    def emit_k():
        k = section(ATTN_WIDTH, ATTN_WIDTH)
        for h, sl in enumerate(heads):
            kh_ref[h] = k[:, sl].astype(BF16)
        to_head_rows(k_ref, k)

    def emit_v():
        v = section(2 * ATTN_WIDTH, ATTN_WIDTH)
        for h, sl in enumerate(heads):
            vt_ref[h, 0] = v[:, sl].T.astype(BF16)
        to_head_rows(v_ref, v)

    xl = section(3 * ATTN_WIDTH, LRU_WIDTH)
    gl = section(3 * ATTN_WIDTH + LRU_WIDTH, LRU_WIDTH)
    xtail_ref[0] = xl[tm - SUBLANES:]
    chunks = _rglru_chunks(xl, gl, y_ref, tail_ref, hc_ref, cw_ref, cb_ref, wgate_ref, bg_ref,
                           lam_ref)
    h_ends = [next(chunks)]
    for emit in (emit_q, emit_k, emit_v):
        emit()
        h_ends.append(next(chunks))
    hlast_ref[0] = jnp.concatenate(h_ends, axis=1)


def _inproj(x2d, w_bf16, lru_args=None, *, prompt, tm, seq=None):
    t = x2d.shape[0]
    assert t % tm == 0
    row = lambda i: (i, 0)
    const = lambda shape: pl.BlockSpec(shape, lambda i: (0,) * len(shape))
    spec512 = pl.BlockSpec((tm, ATTN_WIDTH), row)
    head_rows = pl.BlockSpec((tm * N_HEADS, V_DIM), row)
    f32_512 = jax.ShapeDtypeStruct((t, ATTN_WIDTH), F32)
    f32_heads = jax.ShapeDtypeStruct((t * N_HEADS, V_DIM), F32)
    in_specs = [pl.BlockSpec((tm, D_MODEL), row), const((D_MODEL, IN_WIDTH))]
    operands = [x2d, w_bf16]
    scratch = []
    tiles_per_seq = None
    if prompt:
        assert tm == ATTN_TK and seq % tm == 0
        tiles_per_seq = seq // tm
        batch = t // seq
        per_seq = lambda n: pl.BlockSpec((1, n, LRU_WIDTH), lambda i: (i // tiles_per_seq, 0, 0))
        in_specs += [const((CONV_W, LRU_WIDTH)), const((1, LRU_WIDTH)),
                     const((LRU_WIDTH, 2 * LRU_WIDTH)), const((1, 2 * LRU_WIDTH)),
                     const((1, LRU_WIDTH))]
        operands += list(lru_args)
        out_shape = [jax.ShapeDtypeStruct((N_HEADS, 2, V_DIM, t), BF16),
                     jax.ShapeDtypeStruct((N_HEADS, t, V_DIM), BF16),
                     jax.ShapeDtypeStruct((N_HEADS, t // tm, V_DIM, tm), BF16),
                     f32_heads, f32_heads,
                     jax.ShapeDtypeStruct((t, LRU_WIDTH), BF16),
                     jax.ShapeDtypeStruct((batch, 1, LRU_WIDTH), F32),
                     jax.ShapeDtypeStruct((batch, SUBLANES, LRU_WIDTH), F32)]
        out_specs = [pl.BlockSpec((N_HEADS, 2, V_DIM, tm), lambda i: (0, 0, 0, i)),
                     pl.BlockSpec((N_HEADS, tm, V_DIM), lambda i: (0, i, 0)),
                     pl.BlockSpec((N_HEADS, 1, V_DIM, tm), lambda i: (0, i, 0, 0)),
                     head_rows, head_rows, spec512, per_seq(1), per_seq(SUBLANES)]
        scratch = [pltpu.VMEM((SUBLANES, LRU_WIDTH), F32), pltpu.VMEM((1, LRU_WIDTH), F32)]
    else:
        out_shape = [f32_512, f32_heads, f32_heads, f32_512, f32_512]
        out_specs = [spec512, head_rows, head_rows, spec512, spec512]
    return pl.pallas_call(
        functools.partial(_inproj_kernel, prompt=prompt, tiles_per_seq=tiles_per_seq),
        grid=(t // tm,),
        in_specs=in_specs,
        out_specs=out_specs,
        out_shape=out_shape,
        scratch_shapes=scratch,
        compiler_params=_cparams(1),
        name="inproj_rglru_prompt" if prompt else "inproj_sample",
    )(*operands)


def _attn_prompt_kernel(qt_ref, k_ref, vt_ref, lq1_ref, lk1_ref, lq2_ref, lk2_ref, g_ref,
                        o_ref, s_ref, p_ref, a_ref, m_ref, l_ref, acc_ref):
    tq, tk, cb = ATTN_TQ, ATTN_TK, ATTN_CB
    heads = range(ATTN_HEADS_PER_STEP)
    n_cols = 2 * tq
    i = pl.program_id(2)

    m_ref[...] = jnp.full(m_ref.shape, -jnp.inf, F32)
    l_ref[...] = jnp.zeros(l_ref.shape, F32)
    acc_ref[...] = jnp.zeros(acc_ref.shape, F32)
    for hd in heads:
        p_ref[hd, 1] = jnp.zeros(p_ref.shape[2:], BF16)
        a_ref[hd, 1] = jnp.ones(a_ref.shape[2:], F32)

    def scores(j, slot):
        for hd in heads:
            k = k_ref[hd, pl.ds(pl.multiple_of(j * tk, tk), tk), :]
            for mp in range(2):
                s_ref[hd, slot, :, mp * tq:(mp + 1) * tq] = jnp.dot(
                    k, qt_ref[hd, mp], preferred_element_type=F32)

    def softmax(slot, masked):
        for hd in heads:
            for c in range(n_cols // cb):
                off = (c * cb) % tq
                cs = slice(c * cb, (c + 1) * cb)
                s = s_ref[hd, slot, :, cs]
                if masked:
                    key = lax.broadcasted_iota(I32, s.shape, 0)
                    qpos = lax.broadcasted_iota(I32, s.shape, 1) + off
                    s = jnp.where(key <= qpos, s, -jnp.inf)
                m_prev = m_ref[hd, :, cs]
                m_new = jnp.maximum(m_prev, jnp.max(s, axis=0, keepdims=True))
                alpha = jnp.exp2(m_prev - m_new)
                p = jnp.exp2(s - m_new)
                l_ref[hd, :, cs] = alpha * l_ref[hd, :, cs] + jnp.sum(p, axis=0, keepdims=True)
                m_ref[hd, :, cs] = m_new
                p_ref[hd, slot, :, cs] = p.astype(BF16)
                a_ref[hd, slot, :, cs] = alpha

    def accumulate(j, slot):
        for hd in heads:
            acc_ref[hd] = a_ref[hd, slot] * acc_ref[hd] + jnp.dot(
                vt_ref[hd, j], p_ref[hd, slot], preferred_element_type=F32)

    def stage(j, cur):
        scores(j + 1, 1 - cur)
        softmax(cur, masked=False)
        accumulate(jnp.maximum(j - 1, 0), 1 - cur)

    def diagonal(cur):
        accumulate(jnp.maximum(i - 1, 0), 1 - cur)
        softmax(cur, masked=True)
        accumulate(i, cur)

    scores(0, 0)

    def body(t, carry):
        stage(2 * t, 0)
        stage(2 * t + 1, 1)
        return carry

    lax.fori_loop(0, i // 2, body, 0)

    @pl.when(i % 2 == 0)
    def _():
        diagonal(0)

    @pl.when(i % 2 == 1)
    def _():
        stage(i - 1, 0)
        diagonal(1)

    lam = _diff_lambda(lq1_ref[...], lk1_ref[...], lq2_ref[...], lk2_ref[...])
    for hd in heads:
        o_all = acc_ref[hd] / l_ref[hd]
        o = o_all[:, :tq] - lam * o_all[:, tq:]
        ms = jnp.mean(o * o, axis=0, keepdims=True)
        o = o * lax.rsqrt(ms + LN_EPS) * g_ref[...] * (1.0 - LAM_INIT)
        o_ref[:, hd * V_DIM:(hd + 1) * V_DIM] = o.T.astype(o_ref.dtype)


def _attn_prompt(qt, kh, vt, lam_params, subln_g_col, *, batch, seq):
    assert ATTN_TQ == ATTN_TK and seq % ATTN_TQ == 0 and ATTN_TQ % ATTN_CB == 0
    hp = ATTN_HEADS_PER_STEP
    assert N_HEADS % hp == 0
    nq = seq // ATTN_TQ
    t = batch * seq
    vec = lambda n: pl.BlockSpec((1, n), lambda b, h, i: (0, 0))
    return pl.pallas_call(
        _attn_prompt_kernel,
        grid=(batch, N_HEADS // hp, nq),
        in_specs=[pl.BlockSpec((hp, 2, V_DIM, ATTN_TQ), lambda b, h, i: (h, 0, 0, b * nq + i)),
                  pl.BlockSpec((hp, seq, V_DIM), lambda b, h, i: (h, b, 0)),
                  pl.BlockSpec((hp, nq, V_DIM, ATTN_TK), lambda b, h, i: (h, b, 0, 0)),
                  vec(QK_DIM), vec(QK_DIM), vec(QK_DIM), vec(QK_DIM),
                  pl.BlockSpec((V_DIM, 1), lambda b, h, i: (0, 0))],
        out_specs=pl.BlockSpec((ATTN_TQ, hp * V_DIM), lambda b, h, i: (b * nq + i, h)),
        out_shape=jax.ShapeDtypeStruct((t, ATTN_WIDTH), BF16),
        scratch_shapes=[pltpu.VMEM((hp, 2, ATTN_TK, 2 * ATTN_TQ), F32),
                        pltpu.VMEM((hp, 2, ATTN_TK, 2 * ATTN_TQ), BF16),
                        pltpu.VMEM((hp, 2, 1, 2 * ATTN_TQ), F32),
                        pltpu.VMEM((hp, 1, 2 * ATTN_TQ), F32),
                        pltpu.VMEM((hp, 1, 2 * ATTN_TQ), F32),
                        pltpu.VMEM((hp, V_DIM, 2 * ATTN_TQ), F32)],
        compiler_params=_cparams(3),
        name="attn_prompt",
    )(qt, kh, vt, *lam_params, subln_g_col)


def _attn_sample_kernel(pt_ref, q0_ref, q1_ref, kn0_ref, kn1_ref, vn0_ref, vn1_ref,
                        ck_hbm, cv_hbm, lq1_ref, lk1_ref, lq2_ref, lk2_ref, g_ref,
                        o0_ref, o1_ref, kbuf, vbuf, ksem, vsem, bias_ref, m_ref, l_ref, acc_ref,
                        *, n_chunks, n_steps, dec_seq):
    gpp = PAGES_PER_STEP
    page_rows = PAGE_SIZE * N_HEADS
    hrows = 2 * dec_seq
    g = pl.program_id(0)
    c = g % n_chunks
    slot = g % 2
    streams = ((q0_ref, kn0_ref, vn0_ref, o0_ref), (q1_ref, kn1_ref, vn1_ref, o1_ref))

    def page_copies(step, sl):
        copies = []
        for s in range(len(streams)):
            buf = 2 * s + sl
            for p in range(gpp):
                page = pt_ref[(s * n_steps + step) * gpp + p]
                src = pl.ds(pl.multiple_of(page * page_rows, page_rows), page_rows)
                dst = pl.ds(p * page_rows, page_rows)
                copies.append(pltpu.make_async_copy(ck_hbm.at[src], kbuf.at[buf, dst],
                                                    ksem.at[buf]))
                copies.append(pltpu.make_async_copy(cv_hbm.at[src], vbuf.at[buf, dst],
                                                    vsem.at[buf]))
        return copies

    @pl.when(g == 0)
    def _():
        for cp in page_copies(0, 0):
            cp.start()

    @pl.when(g + 1 < n_steps)
    def _():
        for cp in page_copies(g + 1, 1 - slot):
            cp.start()

    def head_bias(shape):
        row_head = lax.broadcasted_iota(I32, shape, 0) // hrows
        col_head = lax.broadcasted_iota(I32, shape, 1) % N_HEADS
        return row_head == col_head

    @pl.when(g == 0)
    def _():
        bias_ref[...] = jnp.where(head_bias(bias_ref.shape), 0.0, -jnp.inf)

    @pl.when(c == 0)
    def _():
        m_ref[...] = jnp.full(m_ref.shape, -jnp.inf, F32)
        l_ref[...] = jnp.zeros(l_ref.shape, F32)
        acc_ref[...] = jnp.zeros(acc_ref.shape, F32)

    lane = lax.broadcasted_iota(I32, (dec_seq, V_DIM), 1)

    def query_block(q_ref):
        q = q_ref[...]
        q_rows = []
        for h in range(N_HEADS):
            qh = q[:, h * V_DIM:(h + 1) * V_DIM]
            q_rows += [jnp.where(lane < QK_DIM, qh, 0.0), jnp.where(lane >= QK_DIM, qh, 0.0)]
        return jnp.concatenate(q_rows, axis=0).astype(BF16)

    def update(s_idx, qblk, k, v, bias):
        s = lax.dot_general(qblk, k, (((1,), (1,)), ((), ())),
                            preferred_element_type=F32) + bias
        m_prev = m_ref[s_idx]
        m_new = jnp.maximum(m_prev, jnp.max(s, axis=-1, keepdims=True))
        alpha = jnp.exp2(m_prev - m_new)
        p = jnp.exp2(s - m_new)
        l_ref[s_idx] = alpha * l_ref[s_idx] + jnp.sum(p, axis=-1, keepdims=True)
        acc_ref[s_idx] = alpha * acc_ref[s_idx] + jnp.dot(p.astype(BF16), v,
                                                          preferred_element_type=F32)
        m_ref[s_idx] = m_new

    for cp in page_copies(g, slot):
        cp.wait()
    qblks = [query_block(refs[0]) for refs in streams]
    for s_idx in range(len(streams)):
        buf = 2 * s_idx + slot
        update(s_idx, qblks[s_idx], kbuf[buf].astype(BF16), vbuf[buf].astype(BF16),
               bias_ref[...])

    @pl.when(c == n_chunks - 1)
    def _():
        new_rows = dec_seq * N_HEADS
        pad = jnp.zeros((LANES - new_rows, V_DIM), F32)
        shape = (N_HEADS * hrows, LANES)
        qi = lax.broadcasted_iota(I32, shape, 0) % dec_seq
        kj = lax.broadcasted_iota(I32, shape, 1) // N_HEADS
        new_bias = jnp.where(head_bias(shape) & (kj <= qi), 0.0, -jnp.inf)
        lam = _diff_lambda(lq1_ref[...], lk1_ref[...], lq2_ref[...], lk2_ref[...])
        for s_idx, (_, kn_ref, vn_ref, o_ref) in enumerate(streams):
            kn = jnp.concatenate([kn_ref[...], pad], axis=0).astype(BF16)
            vn = jnp.concatenate([vn_ref[...], pad], axis=0).astype(BF16)
            update(s_idx, qblks[s_idx], kn, vn, new_bias)
            o_heads = acc_ref[s_idx] / l_ref[s_idx]
            for h in range(N_HEADS):
                o_all = o_heads[h * hrows:(h + 1) * hrows]
                o = o_all[:dec_seq] - lam * o_all[dec_seq:]
                ms = jnp.mean(o * o, axis=-1, keepdims=True)
                o_ref[:, h * V_DIM:(h + 1) * V_DIM] = (
                    o * lax.rsqrt(ms + LN_EPS) * g_ref[...] * (1.0 - LAM_INIT))


def _attn_sample(q, k_new, v_new, cache_k_rows, cache_v_rows, page_table, lam_params, subln_g,
                 *, dec_batch, dec_seq):
    n_pages = page_table.shape[1]
    assert n_pages % PAGES_PER_STEP == 0 and dec_seq % SUBLANES == 0
    n_chunks = n_pages // PAGES_PER_STEP
    n_streams = 2
    assert dec_batch % n_streams == 0
    half = dec_batch // n_streams
    n_steps = half * n_chunks
    rows = 2 * N_HEADS * dec_seq
    chunk_rows = PAGES_PER_STEP * PAGE_SIZE * N_HEADS

    def stream_specs(s):
        tok = pl.BlockSpec((dec_seq, ATTN_WIDTH), lambda g, pt: (s * half + g // n_chunks, 0))
        tok_heads = pl.BlockSpec((dec_seq * N_HEADS, V_DIM),
                                 lambda g, pt: (s * half + g // n_chunks, 0))
        return tok, tok_heads

    (tok0, heads0), (tok1, heads1) = stream_specs(0), stream_specs(1)
    out_spec = pl.BlockSpec((dec_seq, ATTN_WIDTH), lambda g, pt: (g // n_chunks, 0))
    vec = lambda n: pl.BlockSpec((1, n), lambda g, pt: (0, 0))
    any_spec = pl.BlockSpec(memory_space=pl.ANY)
    n_bufs = 2 * n_streams
    grid_spec = pltpu.PrefetchScalarGridSpec(
        num_scalar_prefetch=1,
        grid=(n_steps,),
        in_specs=[tok0, tok1, heads0, heads1, heads0, heads1, any_spec, any_spec,
                  vec(QK_DIM), vec(QK_DIM), vec(QK_DIM), vec(QK_DIM), vec(V_DIM)],
        out_specs=[out_spec, out_spec],
        scratch_shapes=[pltpu.VMEM((n_bufs, chunk_rows, V_DIM), F32),
                        pltpu.VMEM((n_bufs, chunk_rows, V_DIM), F32),
                        pltpu.SemaphoreType.DMA((n_bufs,)),
                        pltpu.SemaphoreType.DMA((n_bufs,)),
                        pltpu.VMEM((rows, chunk_rows), F32),
                        pltpu.VMEM((n_streams, rows, 1), F32),
                        pltpu.VMEM((n_streams, rows, 1), F32),
                        pltpu.VMEM((n_streams, rows, V_DIM), F32)])
    half_out = jax.ShapeDtypeStruct((half * dec_seq, ATTN_WIDTH), F32)
    o0, o1 = pl.pallas_call(
        functools.partial(_attn_sample_kernel, n_chunks=n_chunks, n_steps=n_steps,
                          dec_seq=dec_seq),
        grid_spec=grid_spec,
        out_shape=[half_out, half_out],
        compiler_params=_cparams(1),
        name="attn_sample",
    )(page_table.reshape(-1), q, q, k_new, k_new, v_new, v_new, cache_k_rows, cache_v_rows,
      *lam_params, subln_g)
    return jnp.concatenate([o0, o1], axis=0)


def _softplus(x):
    y = jnp.exp(-jnp.abs(x))
    w = 1.0 + y
    log1p = jnp.where(w == 1.0, y, jnp.log(w) * (y / (w - 1.0)))
    return jnp.maximum(x, 0.0) + log1p


def _rglru_tile(*args):
    return jnp.concatenate(list(_rglru_chunks(*args)), axis=1)


def _rglru_chunks(x, gl, y_ref, tail_ref, hc_ref, cw_ref, cb_ref, wg_ref, bg_ref, lam_ref):
    for c in range(LRU_WIDTH // LANES):
        cs = slice(c * LANES, (c + 1) * LANES)
        gate_w = jnp.concatenate(
            [wg_ref[cs, cs], wg_ref[cs, LRU_WIDTH + c * LANES:LRU_WIDTH + (c + 1) * LANES]], axis=1)
        gate_b = jnp.concatenate(
            [bg_ref[:, cs], bg_ref[:, LRU_WIDTH + c * LANES:LRU_WIDTH + (c + 1) * LANES]], axis=1)
        xs = x[:, cs]
        y, h_end = _rglru_chunk(xs, gl[:, cs], tail_ref[:, cs], hc_ref[:, cs],
                                cw_ref[:, cs], cb_ref[:, cs], gate_w, gate_b, lam_ref[:, cs])
        tail_ref[:, cs] = xs[xs.shape[0] - SUBLANES:]
        hc_ref[:, cs] = h_end
        y_ref[:, cs] = y.astype(y_ref.dtype)
        yield h_end


def _rglru_chunk(x, gl, tail, h_in, cw, cb, gate_w, gate_b, lam):
    ts, n = x.shape
    xfull = jnp.concatenate([tail, x], axis=0)
    xc = cb + cw[CONV_W - 1:CONV_W] * x
    for d in range(1, CONV_W):
        shifted = pltpu.roll(xfull, d, 0)[SUBLANES:]
        xc = xc + cw[CONV_W - 1 - d:CONV_W - d] * shifted

    gates = jnp.dot(xc.astype(BF16), gate_w, preferred_element_type=F32) + gate_b
    r = jax.nn.sigmoid(gates[:, :n])
    ig = jax.nn.sigmoid(gates[:, n:])
    log_a = -LRU_C * r * _softplus(-lam)
    a = jnp.exp(log_a)
    mult = jnp.sqrt(jnp.maximum(-(a * a + 1.0) * jnp.tanh(log_a), 0.0))
    u = mult * ig * xc

    row = lax.broadcasted_iota(I32, a.shape, 0)
    d = 1
    while d < ts:
        if d < SUBLANES:
            keep = row >= d
            a_prev = jnp.where(keep, pltpu.roll(a, d, 0), 1.0)
            u_prev = jnp.where(keep, pltpu.roll(u, d, 0), 0.0)
            u = a * u_prev + u
            a = a * a_prev
        else:
            u = jnp.concatenate([u[:d], a[d:] * u[:ts - d] + u[d:]], axis=0)
            a = jnp.concatenate([a[:d], a[d:] * a[:ts - d]], axis=0)
        d *= 2
    h = a * h_in + u
    return h * jax.nn.gelu(gl, approximate=True), h[ts - 1:ts]


def _rglru_kernel(xl_ref, gl_ref, conv0_ref, h0_ref, cw_ref, cb_ref, wg_ref, bg_ref, lam_ref,
                  y_ref, hlast_ref, tail_ref, hc_ref):
    s = pl.program_id(1)

    @pl.when(s == 0)
    def _():
        tail_ref[...] = conv0_ref[0]
        hc_ref[...] = h0_ref[0]

    hlast_ref[0] = _rglru_tile(xl_ref[...], gl_ref[...], y_ref, tail_ref, hc_ref,
                               cw_ref, cb_ref, wg_ref, bg_ref, lam_ref)


def _rglru(xl, gl, conv0, h0, conv_w, conv_b, w_gates, b_gates, lam, *, batch, seq, ts):
    assert seq % ts == 0 and ts % SUBLANES == 0
    ns = seq // ts
    tile = pl.BlockSpec((ts, LRU_WIDTH), lambda b, s: (b * ns + s, 0))
    const = lambda shape: pl.BlockSpec(shape, lambda b, s: (0,) * len(shape))
    return pl.pallas_call(
        _rglru_kernel,
        grid=(batch, ns),
        in_specs=[tile, tile,
                  pl.BlockSpec((1, SUBLANES, LRU_WIDTH), lambda b, s: (b, 0, 0)),
                  pl.BlockSpec((1, 1, LRU_WIDTH), lambda b, s: (b, 0, 0)),
                  const((CONV_W, LRU_WIDTH)), const((1, LRU_WIDTH)),
                  const((LRU_WIDTH, 2 * LRU_WIDTH)), const((1, 2 * LRU_WIDTH)),
                  const((1, LRU_WIDTH))],
        out_specs=[tile, pl.BlockSpec((1, 1, LRU_WIDTH), lambda b, s: (b, 0, 0))],
        out_shape=[jax.ShapeDtypeStruct((batch * seq, LRU_WIDTH), BF16 if ts > SUBLANES else F32),
                   jax.ShapeDtypeStruct((batch, 1, LRU_WIDTH), F32)],
        scratch_shapes=[pltpu.VMEM((SUBLANES, LRU_WIDTH), F32),
                        pltpu.VMEM((1, LRU_WIDTH), F32)],
        compiler_params=_cparams(2),
        name="rglru_prompt" if ts > SUBLANES else "rglru_sample",
    )(xl, gl, conv0, h0, conv_w, conv_b, w_gates, b_gates, lam)


def _route(logits):
    lane = lax.broadcasted_iota(I32, logits.shape, 1).astype(F32)
    big = float(ROUTE_LANES)
    neg = -jnp.inf
    is_g = lane < N_GROUPS
    gmax = jnp.max(jnp.where(is_g, logits, neg), axis=-1, keepdims=True)
    gsum = jnp.sum(jnp.where(is_g, jnp.exp(logits - gmax), 0.0), axis=-1, keepdims=True)
    gw = 1.0 / gsum
    gid = jnp.min(jnp.where(is_g & (logits == gmax), lane, big), axis=-1, keepdims=True)
    lo = N_GROUPS + EXPERTS_PER_GROUP * gid
    in_grp = (lane >= lo) & (lane < lo + EXPERTS_PER_GROUP)
    v1 = jnp.max(jnp.where(in_grp, logits, neg), axis=-1, keepdims=True)
    i1 = jnp.min(jnp.where(in_grp & (logits == v1), lane, big), axis=-1, keepdims=True)
    rest = in_grp & (lane != i1)
    v2 = jnp.max(jnp.where(rest, logits, neg), axis=-1, keepdims=True)
    i2 = jnp.min(jnp.where(rest & (logits == v2), lane, big), axis=-1, keepdims=True)
    t = jnp.exp(v2 - v1)
    w1 = gw / (1.0 + t)
    w2 = gw * t / (1.0 + t)
    info = jnp.where(lane == 0, i1 - N_GROUPS,
                     jnp.where(lane == 1, i2 - N_GROUPS,
                               jnp.where(lane == 2, w1, jnp.where(lane == 3, w2, 0.0))))
    return info


def _merge_kernel(xp_ref, xs_ref, op_ref, os_ref, yp_ref, ys_ref, wo_ref, g_ref, b_ref,
                  wr_ref, br_ref, x1_ref, x1t_ref, info_ref, *, n_prompt_tiles):
    i = pl.program_id(0)

    def body(x_ref, o_ref, y_ref):
        dot = functools.partial(jnp.dot, preferred_element_type=F32)
        m = dot(o_ref[...].astype(BF16), wo_ref[:ATTN_WIDTH, :])
        m = m + dot(y_ref[...].astype(BF16), wo_ref[ATTN_WIDTH:, :])
        x1 = _layer_norm(DEEPNORM_ALPHA * x_ref[...] + m, g_ref[...], b_ref[...])
        x1_ref[...] = x1
        _to_token_tiles(x1t_ref, x1)
        info_ref[...] = _route(dot(x1.astype(BF16), wr_ref[...]) + br_ref[...])

    @pl.when(i < n_prompt_tiles)
    def _():
        body(xp_ref, op_ref, yp_ref)

    @pl.when(i >= n_prompt_tiles)
    def _():
        body(xs_ref, os_ref, ys_ref)


def _merge(xp, xs, o_p, o_s, y_p, y_s, wo, g1, b1, wr, br):
    tm = TOK_TM
    tp, ts = xp.shape[0], xs.shape[0]
    assert tp % tm == 0 and ts % tm == 0
    npt, nst = tp // tm, ts // tm
    t_all = tp + ts
    pidx = lambda i: (jnp.minimum(i, npt - 1), 0)
    sidx = lambda i: (jnp.maximum(i - npt, 0), 0)
    const = lambda shape: pl.BlockSpec(shape, lambda i: (0, 0))
    row = lambda i: (i, 0)
    return pl.pallas_call(
        functools.partial(_merge_kernel, n_prompt_tiles=npt),
        grid=(npt + nst,),
        in_specs=[pl.BlockSpec((tm, D_MODEL), pidx), pl.BlockSpec((tm, D_MODEL), sidx),
                  pl.BlockSpec((tm, ATTN_WIDTH), pidx), pl.BlockSpec((tm, ATTN_WIDTH), sidx),
                  pl.BlockSpec((tm, LRU_WIDTH), pidx), pl.BlockSpec((tm, LRU_WIDTH), sidx),
                  const((D_MODEL, D_MODEL)), const((1, D_MODEL)), const((1, D_MODEL)),
                  const((D_MODEL, ROUTE_LANES)), const((1, ROUTE_LANES))],
        out_specs=[pl.BlockSpec((tm, D_MODEL), row),
                   pl.BlockSpec((tm * TILE_ROWS, LANES), row),
                   pl.BlockSpec((tm, ROUTE_LANES), row)],
        out_shape=[jax.ShapeDtypeStruct((t_all, D_MODEL), F32),
                   jax.ShapeDtypeStruct((t_all * TILE_ROWS, LANES), F32),
                   jax.ShapeDtypeStruct((t_all, ROUTE_LANES), F32)],
        compiler_params=_cparams(1),
        name="merge_ln1_route",
    )(xp, xs, o_p, o_s, y_p, y_s, wo.astype(BF16), g1, b1, wr.astype(BF16), br)


def _tile_group(i, cum_ref):
    t = jnp.minimum(i, cum_ref[N_GROUPS - 1] - 1)
    g = jnp.int32(0)
    for k in range(N_GROUPS - 1):
        g = g + (cum_ref[k] <= t).astype(I32)
    return g


def _expert_kernel(cum_ref, src0_ref, gsrc_ref, sdst_ref,
                   x_hbm, info_ref, wg_ref, wu_ref, wd_ref, out_hbm,
                   xbuf, ybuf, xb_ref, infot_ref, yacc_ref, gsem, ssem):
    tm, ch, tr = MOE_TM, MOE_CHUNK, TILE_ROWS
    last_e = EXPERTS_PER_GROUP - 1
    i = pl.program_id(0)
    e = pl.program_id(1)
    slot = i % 2
    other = 1 - slot
    n_live = cum_ref[N_GROUPS - 1]

    def start_gather(idx_ref, rows, base, sl):
        for r in rows:
            src = pl.ds(pl.multiple_of(idx_ref[0, 0, r], tr), tr)
            dst = pl.ds(pl.multiple_of((base + r) * tr, tr), tr)
            pltpu.make_async_copy(x_hbm.at[src], xbuf.at[sl, dst],
                                  gsem.at[sl]).start(priority=r % 2)

    def start_scatter(idx_ref, rows, base, sl):
        for r in rows:
            src = pl.ds(pl.multiple_of((base + r) * tr, tr), tr)
            dst = pl.ds(pl.multiple_of(idx_ref[0, 0, r], tr), tr)
            pltpu.make_async_copy(ybuf.at[sl, src], out_hbm.at[dst],
                                  ssem.at[sl]).start(priority=r % 2)

    def wait_tile(buf, sem):
        pltpu.make_async_copy(x_hbm.at[pl.ds(0, tm * tr)], buf, sem).wait()

    def move_rows():
        start_gather(gsrc_ref, range(ch), e * ch, other)
        start_scatter(sdst_ref, range(ch), e * ch, other)

    @pl.when((i == 0) & (e == 0))
    def _():
        ybuf[1] = jnp.zeros(ybuf.shape[1:], F32)
        yacc_ref[...] = jnp.zeros(yacc_ref.shape, F32)
        start_gather(src0_ref, range(tm), 0, 0)

    @pl.when((e == 0) & (i <= n_live))
    def _():
        wait_tile(xbuf.at[slot], gsem.at[slot])

    @pl.when((e == last_e) & (i >= 1) & (i <= n_live))
    def _():
        wait_tile(ybuf.at[slot], ssem.at[slot])

    @pl.when(i == n_live)
    def _():
        move_rows()

    @pl.when((e == 0) & (i < n_live))
    def _():
        xb_ref[...] = _from_token_tiles(xbuf.at[slot], tm).astype(BF16)
        infot_ref[...] = info_ref[0].T

    @pl.when(i < n_live)
    def _():
        move_rows()
        w_in = jnp.concatenate([wg_ref[0], wu_ref[0]], axis=1).astype(BF16)
        gu = jnp.dot(xb_ref[...], w_in, preferred_element_type=F32)
        gate = gu[:, :D_EXPERT]
        up = gu[:, D_EXPERT:]
        expert = (_tile_group(i, cum_ref) * EXPERTS_PER_GROUP + e).astype(F32)
        info = infot_ref[...]
        comb =(jnp.where(info[:, 0:1] == expert, info[:, 2:3], 0.0)
                + jnp.where(info[:, 1:2] == expert, info[:, 3:4], 0.0))
        h = (gate * jax.nn.sigmoid(gate) * up * comb).astype(BF16)
        y = jnp.dot(h, wd_ref[0].astype(BF16), preferred_element_type=F32)
        yacc_ref[...] = y + jnp.where(e > 0, yacc_ref[...], 0.0)

    @pl.when((e == last_e) & (i < n_live))
    def _():
        _to_token_tiles(ybuf.at[slot], yacc_ref[...])

    @pl.when((e == last_e) & (i == n_live))
    def _():
        wait_tile(xbuf.at[other], gsem.at[other])
        wait_tile(ybuf.at[other], ssem.at[other])


def _experts(x1t, cum, src0, gsrc, sdst, info_sorted, wg, wu, wd, *, n_tiles, n_rows):
    tm, ch = MOE_TM, MOE_CHUNK
    ne = EXPERTS_PER_GROUP
    last = n_tiles - 1
    smem = lambda n, fn: pl.BlockSpec((1, 1, n), fn, memory_space=pltpu.SMEM)
    weight = lambda shape: pl.BlockSpec(
        shape, lambda i, e, cum: (_tile_group(i, cum) * ne + e, 0, 0))
    any_spec = pl.BlockSpec(memory_space=pl.ANY)
    grid_spec = pltpu.PrefetchScalarGridSpec(
        num_scalar_prefetch=1,
        grid=(n_tiles + 1, ne),
        in_specs=[smem(tm, lambda i, e, cum: (0, 0, 0)),
                  smem(ch, lambda i, e, cum: (jnp.minimum(i + 1, last) * ne + e, 0, 0)),
                  smem(ch, lambda i, e, cum: (i * ne + e, 0, 0)),
                  any_spec,
                  pl.BlockSpec((1, SUBLANES, tm), lambda i, e, cum: (jnp.minimum(i, last), 0, 0)),
                  weight((1, D_MODEL, D_EXPERT)), weight((1, D_MODEL, D_EXPERT)),
                  weight((1, D_EXPERT, D_MODEL))],
        out_specs=any_spec,
        scratch_shapes=[pltpu.VMEM((2, tm * TILE_ROWS, LANES), F32),
                        pltpu.VMEM((2, tm * TILE_ROWS, LANES), F32),
                        pltpu.VMEM((tm, D_MODEL), BF16),
                        pltpu.VMEM((tm, SUBLANES), F32),
                        pltpu.VMEM((tm, D_MODEL), F32),
                        pltpu.SemaphoreType.DMA((2,)),
                        pltpu.SemaphoreType.DMA((2,))])
    return pl.pallas_call(
        _expert_kernel,
        grid_spec=grid_spec,
        out_shape=jax.ShapeDtypeStruct((n_rows * TILE_ROWS, LANES), F32),
        compiler_params=_cparams(2),
        name="expert_mlp",
    )(cum, src0, gsrc, sdst, x1t, info_sorted, wg, wu, wd)


def _expert_plan(info, n_tiles):
    tm, ch = MOE_TM, MOE_CHUNK
    t_all = info.shape[0]
    gid = info[:, 0].astype(I32) // EXPERTS_PER_GROUP
    _, order = lax.sort((gid, lax.iota(I32, t_all)), num_keys=1, is_stable=True)
    groups = jnp.arange(N_GROUPS, dtype=I32)
    counts = jnp.sum((gid[:, None] == groups[None, :]).astype(I32), axis=0)
    start = jnp.cumsum(counts) - counts
    tiles_g = (counts + tm - 1) // tm
    cum = jnp.cumsum(tiles_g).astype(I32)
    n_live = cum[-1]
    ti = jnp.arange(n_tiles, dtype=I32)
    g_i = jnp.minimum(jnp.sum((cum[None, :] <= ti[:, None]).astype(I32), axis=1), N_GROUPS - 1)
    j = ti - (cum - tiles_g)[g_i]
    live = ti < n_live
    tile_rows = jnp.where(live, jnp.clip(counts[g_i] - j * tm, 0, tm), 0)
    tile_start = jnp.where(live, start[g_i] + j * tm, 0)
    r = jnp.arange(tm, dtype=I32)[None, :]
    valid = r < tile_rows[:, None]
    tok = order[jnp.minimum(tile_start[:, None] + r, t_all - 1)]
    src = (jnp.where(valid, tok, 0) * TILE_ROWS).astype(I32)
    dump = ((t_all + r) * TILE_ROWS).astype(I32)
    dst = (jnp.where(valid, tok, t_all + r) * TILE_ROWS).astype(I32)
    dst_prev = jnp.concatenate([dump, dst], axis=0)
    rec = [jnp.where(valid, info[:, c][tok], 0.0) for c in range(4)]
    rec += [jnp.zeros((n_tiles, tm), F32)] * (SUBLANES - len(rec))
    info_sorted = jnp.stack(rec, axis=1)
    return (cum, src[0].reshape(1, 1, tm), src.reshape(n_tiles * EXPERTS_PER_GROUP, 1, ch),
            dst_prev.reshape((n_tiles + 1) * EXPERTS_PER_GROUP, 1, ch), info_sorted)


def _combine_kernel(x1_ref, f_ref, g_ref, b_ref, outp_ref, outs_ref, *, n_prompt_tiles):
    tm = TOK_TM
    i = pl.program_id(0)
    f = _from_token_tiles(f_ref, tm)
    out = _layer_norm(DEEPNORM_ALPHA * x1_ref[...] + f, g_ref[...], b_ref[...])

    @pl.when(i < n_prompt_tiles)
    def _():
        outp_ref[...] = out

    @pl.when(i >= n_prompt_tiles)
    def _():
        outs_ref[...] = out


def _combine(x1, f_tiles, g2, b2, *, n_prompt, n_sample):
    tm = TOK_TM
    assert n_sample == tm
    npt = n_prompt // tm
    nt = npt + 1
    row = lambda i: (i, 0)
    const = lambda shape: pl.BlockSpec(shape, lambda i: (0, 0))
    return pl.pallas_call(
        functools.partial(_combine_kernel, n_prompt_tiles=npt),
        grid=(nt,),
        in_specs=[pl.BlockSpec((tm, D_MODEL), row),
                  pl.BlockSpec((tm * TILE_ROWS, LANES), row),
                  const((1, D_MODEL)), const((1, D_MODEL))],
        out_specs=[pl.BlockSpec((tm, D_MODEL), lambda i: (jnp.minimum(i, npt - 1), 0)),
                   pl.BlockSpec((tm, D_MODEL), lambda i: (0, 0))],
        out_shape=[jax.ShapeDtypeStruct((n_prompt, D_MODEL), F32),
                   jax.ShapeDtypeStruct((n_sample, D_MODEL), F32)],
        compiler_params=_cparams(1),
        name="combine_ln2",
    )(x1, f_tiles, g2, b2)


def _block_diag(w):
    nb, n, _ = w.shape
    eye = jnp.eye(nb, dtype=w.dtype)
    return (eye[:, None, :, None] * w[:, :, None, :]).reshape(nb * n, nb * n)


def kernel(x_prompt, x_sample, cache_k, cache_v, state_h, state_conv, page_table, w_in, lambda_q1, lambda_k1, lambda_q2, lambda_k2, subln_g, conv_w, conv_b, w_rg_a, b_rg_a, w_rg_x, b_rg_x, lru_lambda, w_out, ln1_g, ln1_b, w_group_router, b_group_router, w_expert_router, b_expert_router, w_gate_e, w_up_e, w_down_e, ln2_g, ln2_b):
    bp, sp, _ = x_prompt.shape
    bd, sd, _ = x_sample.shape
    assert w_in.shape[0] == DEPTH == 1
    l = 0
    n_prompt, n_sample = bp * sp, bd * sd
    t_all = n_prompt + n_sample

    xp = x_prompt.reshape(n_prompt, D_MODEL)
    xs = x_sample.reshape(n_sample, D_MODEL)
    w_in_b = w_in[l].astype(BF16)
    lam_params = (lambda_q1[l][None], lambda_k1[l][None], lambda_q2[l][None], lambda_k2[l][None])
    w_gates = jnp.concatenate([_block_diag(w_rg_a[l]), _block_diag(w_rg_x[l])], axis=1).astype(BF16)
    b_gates = jnp.concatenate([b_rg_a[l].reshape(1, -1), b_rg_x[l].reshape(1, -1)], axis=1)
    lru_args = (conv_w[l], conv_b[l][None], w_gates, b_gates, lru_lambda[l][None])

    qt, kh, vt, k_p, v_p, y_p, h_p, xl_tail = _inproj(xp, w_in_b, lru_args, prompt=True,
                                                      tm=INPROJ_TM, seq=sp)
    o_p = _attn_prompt(qt, kh, vt, lam_params, subln_g[l][:, None], batch=bp, seq=sp)

    q_s, k_s, v_s, xl_s, gl_s = _inproj(xs, w_in_b, prompt=False, tm=n_sample)
    n_phys = cache_k.shape[1]
    ck = cache_k.reshape(DEPTH * n_phys * PAGE_SIZE * N_HEADS, V_DIM)
    cv = cache_v.reshape(DEPTH * n_phys * PAGE_SIZE * N_HEADS, V_DIM)
    o_s = _attn_sample(q_s, k_s, v_s, ck, cv, page_table, lam_params, subln_g[l][None],
                       dec_batch=bd, dec_seq=sd)
    conv0 = jnp.concatenate(
        [jnp.zeros((bd, SUBLANES - (CONV_W - 1), LRU_WIDTH), F32), state_conv[l]], axis=1)
    y_s, h_s = _rglru(xl_s, gl_s, conv0, state_h[l][:, None, :], *lru_args,
                      batch=bd, seq=sd, ts=sd)

    w_router = jnp.concatenate(
        [w_group_router[l], w_expert_router[l],
         jnp.zeros((D_MODEL, ROUTE_LANES - N_GROUPS - N_EXPERTS), F32)], axis=1)
    b_router = jnp.concatenate(
        [b_group_router[l], b_expert_router[l],
         jnp.zeros((ROUTE_LANES - N_GROUPS - N_EXPERTS,), F32)])[None]
    x1, x1t, info = _merge(xp, xs, o_p, o_s, y_p, y_s, w_out[l], ln1_g[l][None], ln1_b[l][None],
                           w_router, b_router)
    n_tiles = -(-t_all // MOE_TM) + N_GROUPS
    cum, src0, gsrc, sdst, info_sorted = _expert_plan(info, n_tiles)
    f_tiles = _experts(x1t, cum, src0, gsrc, sdst, info_sorted,
                       w_gate_e[l], w_up_e[l], w_down_e[l],
                       n_tiles=n_tiles, n_rows=t_all + MOE_TM)
    out_p, out_s = _combine(x1, f_tiles, ln2_g[l][None], ln2_b[l][None],
                            n_prompt=n_prompt, n_sample=n_sample)

    conv_p = xl_tail[:, SUBLANES - (CONV_W - 1):]
    conv_s = jnp.concatenate([state_conv[l], xl_s.reshape(bd, sd, LRU_WIDTH)],
                             axis=1)[:, -(CONV_W - 1):]
    return (out_p.reshape(bp, sp, D_MODEL),
            out_s.reshape(bd, sd, D_MODEL),
            k_p.reshape(1, bp, sp, N_HEADS, V_DIM),
            v_p.reshape(1, bp, sp, N_HEADS, V_DIM),
            h_p.reshape(1, bp, LRU_WIDTH),
            conv_p[None],
            k_s.reshape(1, bd, sd, N_HEADS, V_DIM),
            v_s.reshape(1, bd, sd, N_HEADS, V_DIM),
            h_s.reshape(1, bd, LRU_WIDTH),
            conv_s[None])
```

```python
import functools
import math

import jax
import jax.numpy as jnp
from jax import lax
from jax.experimental import pallas as pl
from jax.experimental.pallas import tpu as pltpu

F32 = jnp.float32
BF16 = jnp.bfloat16
I32 = jnp.int32

D_MODEL = 1024
ATTN_WIDTH = 512
LRU_WIDTH = 512
N_HEADS = 4
V_DIM = 128
QK_DIM = 64
CONV_W = 4
LRU_C = 8.0
IN_WIDTH = 3 * ATTN_WIDTH + 2 * LRU_WIDTH
N_GROUPS = 4
EXPERTS_PER_GROUP = 8
N_EXPERTS = 32
D_EXPERT = 256
PAGE_SIZE = 128
LN_EPS = 1e-5
DEPTH = 1
DEEPNORM_ALPHA = (2 * DEPTH) ** 0.25
LAM_INIT = 0.8 - 0.6 * math.exp(-0.3 * 0)
QK_SCALE = QK_DIM ** -0.5 * math.log2(math.e)

SUBLANES = 8
LANES = 128
VMEM_LIMIT_BYTES = 48 * 1024 * 1024

INPROJ_TM = 512
ATTN_TQ = 512
ATTN_TK = 512
ATTN_CB = 256
ATTN_HEADS_PER_STEP = 2
TOK_TM = 256
MOE_TM = 1024
MOE_CHUNK = MOE_TM // EXPERTS_PER_GROUP
PAGES_PER_STEP = 16
ROUTE_LANES = 128
TILE_ROWS = D_MODEL // LANES


def _cparams(n_axes):
    return pltpu.CompilerParams(
        dimension_semantics=("arbitrary",) * n_axes,
        vmem_limit_bytes=VMEM_LIMIT_BYTES)


def _layer_norm(x, g, b):
    mu = jnp.mean(x, axis=-1, keepdims=True)
    xc = x - mu
    var = jnp.mean(xc * xc, axis=-1, keepdims=True)
    return xc * lax.rsqrt(var + LN_EPS) * g + b


def _diff_lambda(lq1, lk1, lq2, lk2):
    s1 = jnp.sum(lq1 * lk1, axis=-1, keepdims=True)
    s2 = jnp.sum(lq2 * lk2, axis=-1, keepdims=True)
    return jnp.exp(s1) - jnp.exp(s2) + LAM_INIT


def _to_token_tiles(ref, x):
    tm = x.shape[0]
    for s in range(TILE_ROWS):
        ref[pl.ds(s, tm, stride=TILE_ROWS), :] = x[:, s * LANES:(s + 1) * LANES]


def _from_token_tiles(ref, tm):
    return jnp.concatenate(
        [ref[pl.ds(s, tm, stride=TILE_ROWS), :] for s in range(TILE_ROWS)], axis=1)


def _inproj_kernel(x_ref, w_ref, *refs, prompt, tiles_per_seq):
    x = x_ref[...].astype(BF16)
    tm = x.shape[0]

    def section(idx, width):
        return jnp.dot(x, w_ref[:, idx:idx + width], preferred_element_type=F32)

    heads = [slice(h * V_DIM, (h + 1) * V_DIM) for h in range(N_HEADS)]

    def to_head_rows(ref, val):
        for h, sl in enumerate(heads):
            ref[pl.ds(h, tm, stride=N_HEADS), :] = val[:, sl]

    if not prompt:
        q_ref, k_ref, v_ref, xl_ref, gl_ref = refs
        q_ref[...] = section(0, ATTN_WIDTH) * QK_SCALE
        to_head_rows(k_ref, section(ATTN_WIDTH, ATTN_WIDTH))
        to_head_rows(v_ref, section(2 * ATTN_WIDTH, ATTN_WIDTH))
        xl_ref[...] = section(3 * ATTN_WIDTH, LRU_WIDTH)
        gl_ref[...] = section(3 * ATTN_WIDTH + LRU_WIDTH, LRU_WIDTH)
        return

    (cw_ref, cb_ref, wgate_ref, bg_ref, lam_ref, qt_ref, kh_ref, vt_ref, k_ref, v_ref,
     y_ref, hlast_ref, xtail_ref, tail_ref, hc_ref) = refs

    @pl.when(pl.program_id(0) % tiles_per_seq == 0)
    def _():
        tail_ref[...] = jnp.zeros(tail_ref.shape, F32)
        hc_ref[...] = jnp.zeros(hc_ref.shape, F32)

    def emit_q():
        q = section(0, ATTN_WIDTH) * QK_SCALE
        chan = lax.broadcasted_iota(I32, (V_DIM, tm), 0)
        for h, sl in enumerate(heads):
            qt = q[:, sl].T
            qt_ref[h, 0] = jnp.where(chan < QK_DIM, qt, 0.0).astype(BF16)
            qt_ref[h, 1] = jnp.where(chan >= QK_DIM, qt, 0.0).astype(BF16)

    def emit_k():
        k = section(ATTN_WIDTH, ATTN_WIDTH)
        for h, sl in enumerate(heads):
            kh_ref[h] = k[:, sl].astype(BF16)
        to_head_rows(k_ref, k)

    def emit_v():
        v = section(2 * ATTN_WIDTH, ATTN_WIDTH)
        for h, sl in enumerate(heads):
            vt_ref[h, 0] = v[:, sl].T.astype(BF16)
        to_head_rows(v_ref, v)

    xl = section(3 * ATTN_WIDTH, LRU_WIDTH)
    gl = section(3 * ATTN_WIDTH + LRU_WIDTH, LRU_WIDTH)
    xtail_ref[0] = xl[tm - SUBLANES:]
    chunks = _rglru_chunks(xl, gl, y_ref, tail_ref, hc_ref, cw_ref, cb_ref, wgate_ref, bg_ref,
                           lam_ref)
    h_ends = [next(chunks)]
    for emit in (emit_q, emit_k, emit_v):
        emit()
        h_ends.append(next(chunks))
    hlast_ref[0] = jnp.concatenate(h_ends, axis=1)


def _inproj(x2d, w_bf16, lru_args=None, *, prompt, tm, seq=None):
    t = x2d.shape[0]
    assert t % tm == 0
    row = lambda i: (i, 0)
    const = lambda shape: pl.BlockSpec(shape, lambda i: (0,) * len(shape))
    spec512 = pl.BlockSpec((tm, ATTN_WIDTH), row)
    head_rows = pl.BlockSpec((tm * N_HEADS, V_DIM), row)
    f32_512 = jax.ShapeDtypeStruct((t, ATTN_WIDTH), F32)
    f32_heads = jax.ShapeDtypeStruct((t * N_HEADS, V_DIM), F32)
    in_specs = [pl.BlockSpec((tm, D_MODEL), row), const((D_MODEL, IN_WIDTH))]
    operands = [x2d, w_bf16]
    scratch = []
    tiles_per_seq = None
    if prompt:
        assert tm == ATTN_TK and seq % tm == 0
        tiles_per_seq = seq // tm
        batch = t // seq
        per_seq = lambda n: pl.BlockSpec((1, n, LRU_WIDTH), lambda i: (i // tiles_per_seq, 0, 0))
        in_specs += [const((CONV_W, LRU_WIDTH)), const((1, LRU_WIDTH)),
                     const((LRU_WIDTH, 2 * LRU_WIDTH)), const((1, 2 * LRU_WIDTH)),
                     const((1, LRU_WIDTH))]
        operands += list(lru_args)
        out_shape = [jax.ShapeDtypeStruct((N_HEADS, 2, V_DIM, t), BF16),
                     jax.ShapeDtypeStruct((N_HEADS, t, V_DIM), BF16),
                     jax.ShapeDtypeStruct((N_HEADS, t // tm, V_DIM, tm), BF16),
                     f32_heads, f32_heads,
                     jax.ShapeDtypeStruct((t, LRU_WIDTH), BF16),
                     jax.ShapeDtypeStruct((batch, 1, LRU_WIDTH), F32),
                     jax.ShapeDtypeStruct((batch, SUBLANES, LRU_WIDTH), F32)]
        out_specs = [pl.BlockSpec((N_HEADS, 2, V_DIM, tm), lambda i: (0, 0, 0, i)),
                     pl.BlockSpec((N_HEADS, tm, V_DIM), lambda i: (0, i, 0)),
                     pl.BlockSpec((N_HEADS, 1, V_DIM, tm), lambda i: (0, i, 0, 0)),
                     head_rows, head_rows, spec512, per_seq(1), per_seq(SUBLANES)]
        scratch = [pltpu.VMEM((SUBLANES, LRU_WIDTH), F32), pltpu.VMEM((1, LRU_WIDTH), F32)]
    else:
        out_shape = [f32_512, f32_heads, f32_heads, f32_512, f32_512]
        out_specs = [spec512, head_rows, head_rows, spec512, spec512]
    return pl.pallas_call(
        functools.partial(_inproj_kernel, prompt=prompt, tiles_per_seq=tiles_per_seq),
        grid=(t // tm,),
        in_specs=in_specs,
        out_specs=out_specs,
        out_shape=out_shape,
        scratch_shapes=scratch,
        compiler_params=_cparams(1),
        name="inproj_rglru_prompt" if prompt else "inproj_sample",
    )(*operands)


def _attn_prompt_kernel(qt_ref, k_ref, vt_ref, lq1_ref, lk1_ref, lq2_ref, lk2_ref, g_ref,
                        o_ref, s_ref, p_ref, a_ref, m_ref, l_ref, acc_ref):
    tq, tk, cb = ATTN_TQ, ATTN_TK, ATTN_CB
    heads = range(ATTN_HEADS_PER_STEP)
    n_cols = 2 * tq
    i = pl.program_id(2)

    m_ref[...] = jnp.full(m_ref.shape, -jnp.inf, F32)
    l_ref[...] = jnp.zeros(l_ref.shape, F32)
    acc_ref[...] = jnp.zeros(acc_ref.shape, F32)
    for hd in heads:
        p_ref[hd, 1] = jnp.zeros(p_ref.shape[2:], BF16)
        a_ref[hd, 1] = jnp.ones(a_ref.shape[2:], F32)

    def scores(j, slot):
        for hd in heads:
            k = k_ref[hd, pl.ds(pl.multiple_of(j * tk, tk), tk), :]
            for mp in range(2):
                s_ref[hd, slot, :, mp * tq:(mp + 1) * tq] = jnp.dot(
                    k, qt_ref[hd, mp], preferred_element_type=F32)

    def softmax(slot, masked):
        for hd in heads:
            for c in range(n_cols // cb):
                off = (c * cb) % tq
                cs = slice(c * cb, (c + 1) * cb)
                s = s_ref[hd, slot, :, cs]
                if masked:
                    key = lax.broadcasted_iota(I32, s.shape, 0)
                    qpos = lax.broadcasted_iota(I32, s.shape, 1) + off
                    s = jnp.where(key <= qpos, s, -jnp.inf)
                m_prev = m_ref[hd, :, cs]
                m_new = jnp.maximum(m_prev, jnp.max(s, axis=0, keepdims=True))
                alpha = jnp.exp2(m_prev - m_new)
                p = jnp.exp2(s - m_new)
                l_ref[hd, :, cs] = alpha * l_ref[hd, :, cs] + jnp.sum(p, axis=0, keepdims=True)
                m_ref[hd, :, cs] = m_new
                p_ref[hd, slot, :, cs] = p.astype(BF16)
                a_ref[hd, slot, :, cs] = alpha

    def accumulate(j, slot):
        for hd in heads:
            acc_ref[hd] = a_ref[hd, slot] * acc_ref[hd] + jnp.dot(
                vt_ref[hd, j], p_ref[hd, slot], preferred_element_type=F32)

    def stage(j, cur):
        scores(j + 1, 1 - cur)
        softmax(cur, masked=False)
        accumulate(jnp.maximum(j - 1, 0), 1 - cur)

    def diagonal(cur):
        accumulate(jnp.maximum(i - 1, 0), 1 - cur)
        softmax(cur, masked=True)
        accumulate(i, cur)

    scores(0, 0)

    def body(t, carry):
        stage(2 * t, 0)
        stage(2 * t + 1, 1)
        return carry

    lax.fori_loop(0, i // 2, body, 0)

    @pl.when(i % 2 == 0)
    def _():
        diagonal(0)

    @pl.when(i % 2 == 1)
    def _():
        stage(i - 1, 0)
        diagonal(1)

    lam = _diff_lambda(lq1_ref[...], lk1_ref[...], lq2_ref[...], lk2_ref[...])
    for hd in heads:
        o_all = acc_ref[hd] / l_ref[hd]
        o = o_all[:, :tq] - lam * o_all[:, tq:]
        ms = jnp.mean(o * o, axis=0, keepdims=True)
        o = o * lax.rsqrt(ms + LN_EPS) * g_ref[...] * (1.0 - LAM_INIT)
        o_ref[:, hd * V_DIM:(hd + 1) * V_DIM] = o.T.astype(o_ref.dtype)


def _attn_prompt(qt, kh, vt, lam_params, subln_g_col, *, batch, seq):
    assert ATTN_TQ == ATTN_TK and seq % ATTN_TQ == 0 and ATTN_TQ % ATTN_CB == 0
    hp = ATTN_HEADS_PER_STEP
    assert N_HEADS % hp == 0
    nq = seq // ATTN_TQ
    t = batch * seq
    vec = lambda n: pl.BlockSpec((1, n), lambda b, h, i: (0, 0))
    return pl.pallas_call(
        _attn_prompt_kernel,
        grid=(batch, N_HEADS // hp, nq),
        in_specs=[pl.BlockSpec((hp, 2, V_DIM, ATTN_TQ), lambda b, h, i: (h, 0, 0, b * nq + i)),
                  pl.BlockSpec((hp, seq, V_DIM), lambda b, h, i: (h, b, 0)),
                  pl.BlockSpec((hp, nq, V_DIM, ATTN_TK), lambda b, h, i: (h, b, 0, 0)),
                  vec(QK_DIM), vec(QK_DIM), vec(QK_DIM), vec(QK_DIM),
                  pl.BlockSpec((V_DIM, 1), lambda b, h, i: (0, 0))],
        out_specs=pl.BlockSpec((ATTN_TQ, hp * V_DIM), lambda b, h, i: (b * nq + i, h)),
        out_shape=jax.ShapeDtypeStruct((t, ATTN_WIDTH), BF16),
        scratch_shapes=[pltpu.VMEM((hp, 2, ATTN_TK, 2 * ATTN_TQ), F32),
                        pltpu.VMEM((hp, 2, ATTN_TK, 2 * ATTN_TQ), BF16),
                        pltpu.VMEM((hp, 2, 1, 2 * ATTN_TQ), F32),
                        pltpu.VMEM((hp, 1, 2 * ATTN_TQ), F32),
                        pltpu.VMEM((hp, 1, 2 * ATTN_TQ), F32),
                        pltpu.VMEM((hp, V_DIM, 2 * ATTN_TQ), F32)],
        compiler_params=_cparams(3),
        name="attn_prompt",
    )(qt, kh, vt, *lam_params, subln_g_col)


def _attn_sample_kernel(pt_ref, q0_ref, q1_ref, kn0_ref, kn1_ref, vn0_ref, vn1_ref,
                        ck_hbm, cv_hbm, lq1_ref, lk1_ref, lq2_ref, lk2_ref, g_ref,
                        o0_ref, o1_ref, kbuf, vbuf, ksem, vsem, bias_ref, m_ref, l_ref, acc_ref,
                        *, n_chunks, n_steps, dec_seq):
    gpp = PAGES_PER_STEP
    page_rows = PAGE_SIZE * N_HEADS
    hrows = 2 * dec_seq
    g = pl.program_id(0)
    c = g % n_chunks
    slot = g % 2
    streams = ((q0_ref, kn0_ref, vn0_ref, o0_ref), (q1_ref, kn1_ref, vn1_ref, o1_ref))

    def page_copies(step, sl):
        copies = []
        for s in range(len(streams)):
            buf = 2 * s + sl
            for p in range(gpp):
                page = pt_ref[(s * n_steps + step) * gpp + p]
                src = pl.ds(pl.multiple_of(page * page_rows, page_rows), page_rows)
                dst = pl.ds(p * page_rows, page_rows)
                copies.append(pltpu.make_async_copy(ck_hbm.at[src], kbuf.at[buf, dst],
                                                    ksem.at[buf]))
                copies.append(pltpu.make_async_copy(cv_hbm.at[src], vbuf.at[buf, dst],
                                                    vsem.at[buf]))
        return copies

    @pl.when(g == 0)
    def _():
        for cp in page_copies(0, 0):
            cp.start()

    @pl.when(g + 1 < n_steps)
    def _():
        for cp in page_copies(g + 1, 1 - slot):
            cp.start()

    def head_bias(shape):
        row_head = lax.broadcasted_iota(I32, shape, 0) // hrows
        col_head = lax.broadcasted_iota(I32, shape, 1) % N_HEADS
        return row_head == col_head

    @pl.when(g == 0)
    def _():
        bias_ref[...] = jnp.where(head_bias(bias_ref.shape), 0.0, -jnp.inf)

    @pl.when(c == 0)
    def _():
        m_ref[...] = jnp.full(m_ref.shape, -jnp.inf, F32)
        l_ref[...] = jnp.zeros(l_ref.shape, F32)
        acc_ref[...] = jnp.zeros(acc_ref.shape, F32)

    lane = lax.broadcasted_iota(I32, (dec_seq, V_DIM), 1)

    def query_block(q_ref):
        q = q_ref[...]
        q_rows = []
        for h in range(N_HEADS):
            qh = q[:, h * V_DIM:(h + 1) * V_DIM]
            q_rows += [jnp.where(lane < QK_DIM, qh, 0.0), jnp.where(lane >= QK_DIM, qh, 0.0)]
        return jnp.concatenate(q_rows, axis=0).astype(BF16)

    def update(s_idx, qblk, k, v, bias):
        s = lax.dot_general(qblk, k, (((1,), (1,)), ((), ())),
                            preferred_element_type=F32) + bias
        m_prev = m_ref[s_idx]
        m_new = jnp.maximum(m_prev, jnp.max(s, axis=-1, keepdims=True))
        alpha = jnp.exp2(m_prev - m_new)
        p = jnp.exp2(s - m_new)
        l_ref[s_idx] = alpha * l_ref[s_idx] + jnp.sum(p, axis=-1, keepdims=True)
        acc_ref[s_idx] = alpha * acc_ref[s_idx] + jnp.dot(p.astype(BF16), v,
                                                          preferred_element_type=F32)
        m_ref[s_idx] = m_new

    for cp in page_copies(g, slot):
        cp.wait()
    qblks = [query_block(refs[0]) for refs in streams]
    for s_idx in range(len(streams)):
        buf = 2 * s_idx + slot
        update(s_idx, qblks[s_idx], kbuf[buf].astype(BF16), vbuf[buf].astype(BF16),
               bias_ref[...])

    @pl.when(c == n_chunks - 1)
    def _():
        new_rows = dec_seq * N_HEADS
        pad = jnp.zeros((LANES - new_rows, V_DIM), F32)
        shape = (N_HEADS * hrows, LANES)
        qi = lax.broadcasted_iota(I32, shape, 0) % dec_seq
        kj = lax.broadcasted_iota(I32, shape, 1) // N_HEADS
        new_bias = jnp.where(head_bias(shape) & (kj <= qi), 0.0, -jnp.inf)
        lam = _diff_lambda(lq1_ref[...], lk1_ref[...], lq2_ref[...], lk2_ref[...])
        for s_idx, (_, kn_ref, vn_ref, o_ref) in enumerate(streams):
            kn = jnp.concatenate([kn_ref[...], pad], axis=0).astype(BF16)
            vn = jnp.concatenate([vn_ref[...], pad], axis=0).astype(BF16)
            update(s_idx, qblks[s_idx], kn, vn, new_bias)
            o_heads = acc_ref[s_idx] / l_ref[s_idx]
            for h in range(N_HEADS):
                o_all = o_heads[h * hrows:(h + 1) * hrows]
                o = o_all[:dec_seq] - lam * o_all[dec_seq:]
                ms = jnp.mean(o * o, axis=-1, keepdims=True)
                o_ref[:, h * V_DIM:(h + 1) * V_DIM] = (
                    o * lax.rsqrt(ms + LN_EPS) * g_ref[...] * (1.0 - LAM_INIT))


def _attn_sample(q, k_new, v_new, cache_k_rows, cache_v_rows, page_table, lam_params, subln_g,
                 *, dec_batch, dec_seq):
    n_pages = page_table.shape[1]
    assert n_pages % PAGES_PER_STEP == 0 and dec_seq % SUBLANES == 0
    n_chunks = n_pages // PAGES_PER_STEP
    n_streams = 2
    assert dec_batch % n_streams == 0
    half = dec_batch // n_streams
    n_steps = half * n_chunks
    rows = 2 * N_HEADS * dec_seq
    chunk_rows = PAGES_PER_STEP * PAGE_SIZE * N_HEADS

    def stream_specs(s):
        tok = pl.BlockSpec((dec_seq, ATTN_WIDTH), lambda g, pt: (s * half + g // n_chunks, 0))
        tok_heads = pl.BlockSpec((dec_seq * N_HEADS, V_DIM),
                                 lambda g, pt: (s * half + g // n_chunks, 0))
        return tok, tok_heads

    (tok0, heads0), (tok1, heads1) = stream_specs(0), stream_specs(1)
    out_spec = pl.BlockSpec((dec_seq, ATTN_WIDTH), lambda g, pt: (g // n_chunks, 0))
    vec = lambda n: pl.BlockSpec((1, n), lambda g, pt: (0, 0))
    any_spec = pl.BlockSpec(memory_space=pl.ANY)
    n_bufs = 2 * n_streams
    grid_spec = pltpu.PrefetchScalarGridSpec(
        num_scalar_prefetch=1,
        grid=(n_steps,),
        in_specs=[tok0, tok1, heads0, heads1, heads0, heads1, any_spec, any_spec,
                  vec(QK_DIM), vec(QK_DIM), vec(QK_DIM), vec(QK_DIM), vec(V_DIM)],
        out_specs=[out_spec, out_spec],
        scratch_shapes=[pltpu.VMEM((n_bufs, chunk_rows, V_DIM), F32),
                        pltpu.VMEM((n_bufs, chunk_rows, V_DIM), F32),
                        pltpu.SemaphoreType.DMA((n_bufs,)),
                        pltpu.SemaphoreType.DMA((n_bufs,)),
                        pltpu.VMEM((rows, chunk_rows), F32),
                        pltpu.VMEM((n_streams, rows, 1), F32),
                        pltpu.VMEM((n_streams, rows, 1), F32),
                        pltpu.VMEM((n_streams, rows, V_DIM), F32)])
    half_out = jax.ShapeDtypeStruct((half * dec_seq, ATTN_WIDTH), F32)
    o0, o1 = pl.pallas_call(
        functools.partial(_attn_sample_kernel, n_chunks=n_chunks, n_steps=n_steps,
                          dec_seq=dec_seq),
        grid_spec=grid_spec,
        out_shape=[half_out, half_out],
        compiler_params=_cparams(1),
        name="attn_sample",
    )(page_table.reshape(-1), q, q, k_new, k_new, v_new, v_new, cache_k_rows, cache_v_rows,
      *lam_params, subln_g)
    return jnp.concatenate([o0, o1], axis=0)


def _softplus(x):
    y = jnp.exp(-jnp.abs(x))
    w = 1.0 + y
    log1p = jnp.where(w == 1.0, y, jnp.log(w) * (y / (w - 1.0)))
    return jnp.maximum(x, 0.0) + log1p


def _rglru_tile(*args):
    return jnp.concatenate(list(_rglru_chunks(*args)), axis=1)


def _rglru_chunks(x, gl, y_ref, tail_ref, hc_ref, cw_ref, cb_ref, wg_ref, bg_ref, lam_ref):
    for c in range(LRU_WIDTH // LANES):
        cs = slice(c * LANES, (c + 1) * LANES)
        gate_w = jnp.concatenate(
            [wg_ref[cs, cs], wg_ref[cs, LRU_WIDTH + c * LANES:LRU_WIDTH + (c + 1) * LANES]], axis=1)
        gate_b = jnp.concatenate(
            [bg_ref[:, cs], bg_ref[:, LRU_WIDTH + c * LANES:LRU_WIDTH + (c + 1) * LANES]], axis=1)
        xs = x[:, cs]
        y, h_end = _rglru_chunk(xs, gl[:, cs], tail_ref[:, cs], hc_ref[:, cs],
                                cw_ref[:, cs], cb_ref[:, cs], gate_w, gate_b, lam_ref[:, cs])
        tail_ref[:, cs] = xs[xs.shape[0] - SUBLANES:]
        hc_ref[:, cs] = h_end
        y_ref[:, cs] = y.astype(y_ref.dtype)
        yield h_end


def _rglru_chunk(x, gl, tail, h_in, cw, cb, gate_w, gate_b, lam):
    ts, n = x.shape
    xfull = jnp.concatenate([tail, x], axis=0)
    xc = cb + cw[CONV_W - 1:CONV_W] * x
    for d in range(1, CONV_W):
        shifted = pltpu.roll(xfull, d, 0)[SUBLANES:]
        xc = xc + cw[CONV_W - 1 - d:CONV_W - d] * shifted

    gates = jnp.dot(xc.astype(BF16), gate_w, preferred_element_type=F32) + gate_b
    r = jax.nn.sigmoid(gates[:, :n])
    ig = jax.nn.sigmoid(gates[:, n:])
    log_a = -LRU_C * r * _softplus(-lam)
    a = jnp.exp(log_a)
    mult = jnp.sqrt(jnp.maximum(-(a * a + 1.0) * jnp.tanh(log_a), 0.0))
    u = mult * ig * xc

    row = lax.broadcasted_iota(I32, a.shape, 0)
    d = 1
    while d < ts:
        if d < SUBLANES:
            keep = row >= d
            a_prev = jnp.where(keep, pltpu.roll(a, d, 0), 1.0)
            u_prev = jnp.where(keep, pltpu.roll(u, d, 0), 0.0)
            u = a * u_prev + u
            a = a * a_prev
        else:
            u = jnp.concatenate([u[:d], a[d:] * u[:ts - d] + u[d:]], axis=0)
            a = jnp.concatenate([a[:d], a[d:] * a[:ts - d]], axis=0)
        d *= 2
    h = a * h_in + u
    return h * jax.nn.gelu(gl, approximate=True), h[ts - 1:ts]


def _rglru_kernel(xl_ref, gl_ref, conv0_ref, h0_ref, cw_ref, cb_ref, wg_ref, bg_ref, lam_ref,
                  y_ref, hlast_ref, tail_ref, hc_ref):
    s = pl.program_id(1)

    @pl.when(s == 0)
    def _():
        tail_ref[...] = conv0_ref[0]
        hc_ref[...] = h0_ref[0]

    hlast_ref[0] = _rglru_tile(xl_ref[...], gl_ref[...], y_ref, tail_ref, hc_ref,
                               cw_ref, cb_ref, wg_ref, bg_ref, lam_ref)


def _rglru(xl, gl, conv0, h0, conv_w, conv_b, w_gates, b_gates, lam, *, batch, seq, ts):
    assert seq % ts == 0 and ts % SUBLANES == 0
    ns = seq // ts
    tile = pl.BlockSpec((ts, LRU_WIDTH), lambda b, s: (b * ns + s, 0))
    const = lambda shape: pl.BlockSpec(shape, lambda b, s: (0,) * len(shape))
    return pl.pallas_call(
        _rglru_kernel,
        grid=(batch, ns),
        in_specs=[tile, tile,
                  pl.BlockSpec((1, SUBLANES, LRU_WIDTH), lambda b, s: (b, 0, 0)),
                  pl.BlockSpec((1, 1, LRU_WIDTH), lambda b, s: (b, 0, 0)),
                  const((CONV_W, LRU_WIDTH)), const((1, LRU_WIDTH)),
                  const((LRU_WIDTH, 2 * LRU_WIDTH)), const((1, 2 * LRU_WIDTH)),
                  const((1, LRU_WIDTH))],
        out_specs=[tile, pl.BlockSpec((1, 1, LRU_WIDTH), lambda b, s: (b, 0, 0))],
        out_shape=[jax.ShapeDtypeStruct((batch * seq, LRU_WIDTH), BF16 if ts > SUBLANES else F32),
                   jax.ShapeDtypeStruct((batch, 1, LRU_WIDTH), F32)],
        scratch_shapes=[pltpu.VMEM((SUBLANES, LRU_WIDTH), F32),
                        pltpu.VMEM((1, LRU_WIDTH), F32)],
        compiler_params=_cparams(2),
        name="rglru_prompt" if ts > SUBLANES else "rglru_sample",
    )(xl, gl, conv0, h0, conv_w, conv_b, w_gates, b_gates, lam)


def _route(logits):
    lane = lax.broadcasted_iota(I32, logits.shape, 1).astype(F32)
    big = float(ROUTE_LANES)
    neg = -jnp.inf
    is_g = lane < N_GROUPS
    gmax = jnp.max(jnp.where(is_g, logits, neg), axis=-1, keepdims=True)
    gsum = jnp.sum(jnp.where(is_g, jnp.exp(logits - gmax), 0.0), axis=-1, keepdims=True)
    gw = 1.0 / gsum
    gid = jnp.min(jnp.where(is_g & (logits == gmax), lane, big), axis=-1, keepdims=True)
    lo = N_GROUPS + EXPERTS_PER_GROUP * gid
    in_grp = (lane >= lo) & (lane < lo + EXPERTS_PER_GROUP)
    v1 = jnp.max(jnp.where(in_grp, logits, neg), axis=-1, keepdims=True)
    i1 = jnp.min(jnp.where(in_grp & (logits == v1), lane, big), axis=-1, keepdims=True)
    rest = in_grp & (lane != i1)
    v2 = jnp.max(jnp.where(rest, logits, neg), axis=-1, keepdims=True)
    i2 = jnp.min(jnp.where(rest & (logits == v2), lane, big), axis=-1, keepdims=True)
    t = jnp.exp(v2 - v1)
    w1 = gw / (1.0 + t)
    w2 = gw * t / (1.0 + t)
    info = jnp.where(lane == 0, i1 - N_GROUPS,
                     jnp.where(lane == 1, i2 - N_GROUPS,
                               jnp.where(lane == 2, w1, jnp.where(lane == 3, w2, 0.0))))
    return info


def _merge_kernel(xp_ref, xs_ref, op_ref, os_ref, yp_ref, ys_ref, wo_ref, g_ref, b_ref,
                  wr_ref, br_ref, x1_ref, x1t_ref, info_ref, *, n_prompt_tiles):
    i = pl.program_id(0)

    def body(x_ref, o_ref, y_ref):
        dot = functools.partial(jnp.dot, preferred_element_type=F32)
        m = dot(o_ref[...].astype(BF16), wo_ref[:ATTN_WIDTH, :])
        m = m + dot(y_ref[...].astype(BF16), wo_ref[ATTN_WIDTH:, :])
        x1 = _layer_norm(DEEPNORM_ALPHA * x_ref[...] + m, g_ref[...], b_ref[...])
        x1_ref[...] = x1
        _to_token_tiles(x1t_ref, x1)
        info_ref[...] = _route(dot(x1.astype(BF16), wr_ref[...]) + br_ref[...]).T[:SUBLANES]

    @pl.when(i < n_prompt_tiles)
    def _():
        body(xp_ref, op_ref, yp_ref)

    @pl.when(i >= n_prompt_tiles)
    def _():
        body(xs_ref, os_ref, ys_ref)


def _merge(xp, xs, o_p, o_s, y_p, y_s, wo, g1, b1, wr, br):
    tm = TOK_TM
    tp, ts = xp.shape[0], xs.shape[0]
    assert tp % tm == 0 and ts % tm == 0
    npt, nst = tp // tm, ts // tm
    t_all = tp + ts
    pidx = lambda i: (jnp.minimum(i, npt - 1), 0)
    sidx = lambda i: (jnp.maximum(i - npt, 0), 0)
    const = lambda shape: pl.BlockSpec(shape, lambda i: (0, 0))
    row = lambda i: (i, 0)
    return pl.pallas_call(
        functools.partial(_merge_kernel, n_prompt_tiles=npt),
        grid=(npt + nst,),
        in_specs=[pl.BlockSpec((tm, D_MODEL), pidx), pl.BlockSpec((tm, D_MODEL), sidx),
                  pl.BlockSpec((tm, ATTN_WIDTH), pidx), pl.BlockSpec((tm, ATTN_WIDTH), sidx),
                  pl.BlockSpec((tm, LRU_WIDTH), pidx), pl.BlockSpec((tm, LRU_WIDTH), sidx),
                  const((D_MODEL, D_MODEL)), const((1, D_MODEL)), const((1, D_MODEL)),
                  const((D_MODEL, ROUTE_LANES)), const((1, ROUTE_LANES))],
        out_specs=[pl.BlockSpec((tm, D_MODEL), row),
                   pl.BlockSpec((tm * TILE_ROWS, LANES), row),
                   pl.BlockSpec((SUBLANES, tm), lambda i: (0, i))],
        out_shape=[jax.ShapeDtypeStruct((t_all, D_MODEL), F32),
                   jax.ShapeDtypeStruct((t_all * TILE_ROWS, LANES), F32),
                   jax.ShapeDtypeStruct((SUBLANES, t_all), F32)],
        compiler_params=_cparams(1),
        name="merge_ln1_route",
    )(xp, xs, o_p, o_s, y_p, y_s, wo.astype(BF16), g1, b1, wr.astype(BF16), br)


def _tile_group(i, cum_ref):
    t = jnp.minimum(i, cum_ref[N_GROUPS - 1] - 1)
    g = jnp.int32(0)
    for k in range(N_GROUPS - 1):
        g = g + (cum_ref[k] <= t).astype(I32)
    return g


def _expert_kernel(cum_ref, src0_ref, gsrc_ref, sdst_ref,
                   x_hbm, info_ref, wg_ref, wu_ref, wd_ref, out_hbm,
                   xbuf, ybuf, xb_ref, infot_ref, yacc_ref, gsem, ssem):
    tm, ch, tr = MOE_TM, MOE_CHUNK, TILE_ROWS
    last_e = EXPERTS_PER_GROUP - 1
    i = pl.program_id(0)
    e = pl.program_id(1)
    slot = i % 2
    other = 1 - slot
    n_live = cum_ref[N_GROUPS - 1]

    def start_gather(idx_ref, rows, base, sl):
        for r in rows:
            src = pl.ds(pl.multiple_of(idx_ref[0, 0, r], tr), tr)
            dst = pl.ds(pl.multiple_of((base + r) * tr, tr), tr)
            pltpu.make_async_copy(x_hbm.at[src], xbuf.at[sl, dst],
                                  gsem.at[sl]).start(priority=r % 2)

    def start_scatter(idx_ref, rows, base, sl):
        for r in rows:
            src = pl.ds(pl.multiple_of((base + r) * tr, tr), tr)
            dst = pl.ds(pl.multiple_of(idx_ref[0, 0, r], tr), tr)
            pltpu.make_async_copy(ybuf.at[sl, src], out_hbm.at[dst],
                                  ssem.at[sl]).start(priority=r % 2)

    def wait_tile(buf, sem):
        pltpu.make_async_copy(x_hbm.at[pl.ds(0, tm * tr)], buf, sem).wait()

    def move_rows():
        start_gather(gsrc_ref, range(ch), e * ch, other)
        start_scatter(sdst_ref, range(ch), e * ch, other)

    @pl.when((i == 0) & (e == 0))
    def _():
        ybuf[1] = jnp.zeros(ybuf.shape[1:], F32)
        yacc_ref[...] = jnp.zeros(yacc_ref.shape, F32)
        start_gather(src0_ref, range(tm), 0, 0)

    @pl.when((e == 0) & (i <= n_live))
    def _():
        wait_tile(xbuf.at[slot], gsem.at[slot])

    @pl.when((e == last_e) & (i >= 1) & (i <= n_live))
    def _():
        wait_tile(ybuf.at[slot], ssem.at[slot])

    @pl.when(i == n_live)
    def _():
        move_rows()

    @pl.when((e == 0) & (i < n_live))
    def _():
        xb_ref[...] = _from_token_tiles(xbuf.at[slot], tm).astype(BF16)
        infot_ref[...] = info_ref[0].T

    @pl.when(i < n_live)
    def _():
        move_rows()
        w_in = jnp.concatenate([wg_ref[0], wu_ref[0]], axis=1).astype(BF16)
        gu = jnp.dot(xb_ref[...], w_in, preferred_element_type=F32)
        gate = gu[:, :D_EXPERT]
        up = gu[:, D_EXPERT:]
        expert = (_tile_group(i, cum_ref) * EXPERTS_PER_GROUP + e).astype(F32)
        info = infot_ref[...]
        comb = (jnp.where(info[:, 0:1] == expert, info[:, 2:3], 0.0)
                + jnp.where(info[:, 1:2] == expert, info[:, 3:4], 0.0))
        h = (gate * jax.nn.sigmoid(gate) * up * comb).astype(BF16)
        y = jnp.dot(h, wd_ref[0].astype(BF16), preferred_element_type=F32)
        yacc_ref[...] = y + jnp.where(e > 0, yacc_ref[...], 0.0)

    @pl.when((e == last_e) & (i < n_live))
    def _():
        _to_token_tiles(ybuf.at[slot], yacc_ref[...])

    @pl.when((e == last_e) & (i == n_live))
    def _():
        wait_tile(xbuf.at[other], gsem.at[other])
        wait_tile(ybuf.at[other], ssem.at[other])


def _experts(x1t, cum, src0, gsrc, sdst, info_sorted, wg, wu, wd, *, n_tiles, n_rows):
    tm, ch = MOE_TM, MOE_CHUNK
    ne = EXPERTS_PER_GROUP
    last = n_tiles - 1
    smem = lambda n, fn: pl.BlockSpec((1, 1, n), fn, memory_space=pltpu.SMEM)
    weight = lambda shape: pl.BlockSpec(
        shape, lambda i, e, cum: (_tile_group(i, cum) * ne + e, 0, 0))
    any_spec = pl.BlockSpec(memory_space=pl.ANY)
    grid_spec = pltpu.PrefetchScalarGridSpec(
        num_scalar_prefetch=1,
        grid=(n_tiles + 1, ne),
        in_specs=[smem(tm, lambda i, e, cum: (0, 0, 0)),
                  smem(ch, lambda i, e, cum: (jnp.minimum(i + 1, last) * ne + e, 0, 0)),
                  smem(ch, lambda i, e, cum: (i * ne + e, 0, 0)),
                  any_spec,
                  pl.BlockSpec((1, SUBLANES, tm), lambda i, e, cum: (jnp.minimum(i, last), 0, 0)),
                  weight((1, D_MODEL, D_EXPERT)), weight((1, D_MODEL, D_EXPERT)),
                  weight((1, D_EXPERT, D_MODEL))],
        out_specs=any_spec,
        scratch_shapes=[pltpu.VMEM((2, tm * TILE_ROWS, LANES), F32),
                        pltpu.VMEM((2, tm * TILE_ROWS, LANES), F32),
                        pltpu.VMEM((tm, D_MODEL), BF16),
                        pltpu.VMEM((tm, SUBLANES), F32),
                        pltpu.VMEM((tm, D_MODEL), F32),
                        pltpu.SemaphoreType.DMA((2,)),
                        pltpu.SemaphoreType.DMA((2,))])
    return pl.pallas_call(
        _expert_kernel,
        grid_spec=grid_spec,
        out_shape=jax.ShapeDtypeStruct((n_rows * TILE_ROWS, LANES), F32),
        compiler_params=_cparams(2),
        name="expert_mlp",
    )(cum, src0, gsrc, sdst, x1t, info_sorted, wg, wu, wd)


def _expert_plan(info, n_tiles):
    tm, ch = MOE_TM, MOE_CHUNK
    t_all = info.shape[1]
    gid = info[0].astype(I32) // EXPERTS_PER_GROUP
    _, order = lax.sort((gid, lax.iota(I32, t_all)), num_keys=1, is_stable=True)
    groups = jnp.arange(N_GROUPS, dtype=I32)
    counts = jnp.sum((gid[:, None] == groups[None, :]).astype(I32), axis=0)
    start = jnp.cumsum(counts) - counts
    tiles_g = (counts + tm - 1) // tm
    cum = jnp.cumsum(tiles_g).astype(I32)
    n_live = cum[-1]
    ti = jnp.arange(n_tiles, dtype=I32)
    g_i = jnp.minimum(jnp.sum((cum[None, :] <= ti[:, None]).astype(I32), axis=1), N_GROUPS - 1)
    j = ti - (cum - tiles_g)[g_i]
    live = ti < n_live
    tile_rows = jnp.where(live, jnp.clip(counts[g_i] - j * tm, 0, tm), 0)
    tile_start = jnp.where(live, start[g_i] + j * tm, 0)
    r = jnp.arange(tm, dtype=I32)[None, :]
    valid = r < tile_rows[:, None]
    tok = order[jnp.minimum(tile_start[:, None] + r, t_all - 1)]
    src = (jnp.where(valid, tok, 0) * TILE_ROWS).astype(I32)
    dump = ((t_all + r) * TILE_ROWS).astype(I32)
    dst = (jnp.where(valid, tok, t_all + r) * TILE_ROWS).astype(I32)
    dst_prev = jnp.concatenate([dump, dst], axis=0)
    rec = [jnp.where(valid, info[c][tok], 0.0) for c in range(4)]
    rec += [jnp.zeros((n_tiles, tm), F32)] * (SUBLANES - len(rec))
    info_sorted = jnp.stack(rec, axis=1)
    return (cum, src[0].reshape(1, 1, tm), src.reshape(n_tiles * EXPERTS_PER_GROUP, 1, ch),
            dst_prev.reshape((n_tiles + 1) * EXPERTS_PER_GROUP, 1, ch), info_sorted)


def _combine_kernel(x1_ref, f_ref, g_ref, b_ref, outp_ref, outs_ref, *, n_prompt_tiles):
    tm = TOK_TM
    i = pl.program_id(0)
    f = _from_token_tiles(f_ref, tm)
    out = _layer_norm(DEEPNORM_ALPHA * x1_ref[...] + f, g_ref[...], b_ref[...])

    @pl.when(i < n_prompt_tiles)
    def _():
        outp_ref[...] = out

    @pl.when(i >= n_prompt_tiles)
    def _():
        outs_ref[...] = out


def _combine(x1, f_tiles, g2, b2, *, n_prompt, n_sample):
    tm = TOK_TM
    assert n_sample == tm
    npt = n_prompt // tm
    nt = npt + 1
    row = lambda i: (i, 0)
    const = lambda shape: pl.BlockSpec(shape, lambda i: (0, 0))
    return pl.pallas_call(
        functools.partial(_combine_kernel, n_prompt_tiles=npt),
        grid=(nt,),
        in_specs=[pl.BlockSpec((tm, D_MODEL), row),
                  pl.BlockSpec((tm * TILE_ROWS, LANES), row),
                  const((1, D_MODEL)), const((1, D_MODEL))],
        out_specs=[pl.BlockSpec((tm, D_MODEL), lambda i: (jnp.minimum(i, npt - 1), 0)),
                   pl.BlockSpec((tm, D_MODEL), lambda i: (0, 0))],
        out_shape=[jax.ShapeDtypeStruct((n_prompt, D_MODEL), F32),
                   jax.ShapeDtypeStruct((n_sample, D_MODEL), F32)],
        compiler_params=_cparams(1),
        name="combine_ln2",
    )(x1, f_tiles, g2, b2)


def _block_diag(w):
    nb, n, _ = w.shape
    eye = jnp.eye(nb, dtype=w.dtype)
    return (eye[:, None, :, None] * w[:, :, None, :]).reshape(nb * n, nb * n)


def kernel(x_prompt, x_sample, cache_k, cache_v, state_h, state_conv, page_table, w_in, lambda_q1, lambda_k1, lambda_q2, lambda_k2, subln_g, conv_w, conv_b, w_rg_a, b_rg_a, w_rg_x, b_rg_x, lru_lambda, w_out, ln1_g, ln1_b, w_group_router, b_group_router, w_expert_router, b_expert_router, w_gate_e, w_up_e, w_down_e, ln2_g, ln2_b):
    bp, sp, _ = x_prompt.shape
    bd, sd, _ = x_sample.shape
    assert w_in.shape[0] == DEPTH == 1
    l = 0
    n_prompt, n_sample = bp * sp, bd * sd
    t_all = n_prompt + n_sample

    xp = x_prompt.reshape(n_prompt, D_MODEL)
    xs = x_sample.reshape(n_sample, D_MODEL)
    w_in_b = w_in[l].astype(BF16)
    lam_params = (lambda_q1[l][None], lambda_k1[l][None], lambda_q2[l][None], lambda_k2[l][None])
    w_gates = jnp.concatenate([_block_diag(w_rg_a[l]), _block_diag(w_rg_x[l])], axis=1).astype(BF16)
    b_gates = jnp.concatenate([b_rg_a[l].reshape(1, -1), b_rg_x[l].reshape(1, -1)], axis=1)
    lru_args = (conv_w[l], conv_b[l][None], w_gates, b_gates, lru_lambda[l][None])

    qt, kh, vt, k_p, v_p, y_p, h_p, xl_tail = _inproj(xp, w_in_b, lru_args, prompt=True,
                                                      tm=INPROJ_TM, seq=sp)
    o_p = _attn_prompt(qt, kh, vt, lam_params, subln_g[l][:, None], batch=bp, seq=sp)

    q_s, k_s, v_s, xl_s, gl_s = _inproj(xs, w_in_b, prompt=False, tm=n_sample)
    n_phys = cache_k.shape[1]
    ck = cache_k.reshape(DEPTH * n_phys * PAGE_SIZE * N_HEADS, V_DIM)
    cv = cache_v.reshape(DEPTH * n_phys * PAGE_SIZE * N_HEADS, V_DIM)
    o_s = _attn_sample(q_s, k_s, v_s, ck, cv, page_table, lam_params, subln_g[l][None],
                       dec_batch=bd, dec_seq=sd)
    conv0 = jnp.concatenate(
        [jnp.zeros((bd, SUBLANES - (CONV_W - 1), LRU_WIDTH), F32), state_conv[l]], axis=1)
    y_s, h_s = _rglru(xl_s, gl_s, conv0, state_h[l][:, None, :], *lru_args,
                      batch=bd, seq=sd, ts=sd)

    w_router = jnp.concatenate(
        [w_group_router[l], w_expert_router[l],
         jnp.zeros((D_MODEL, ROUTE_LANES - N_GROUPS - N_EXPERTS), F32)], axis=1)
    b_router = jnp.concatenate(
        [b_group_router[l], b_expert_router[l],
         jnp.zeros((ROUTE_LANES - N_GROUPS - N_EXPERTS,), F32)])[None]
    x1, x1t, info = _merge(xp, xs, o_p, o_s, y_p, y_s, w_out[l], ln1_g[l][None], ln1_b[l][None],
                           w_router, b_router)
    n_tiles = -(-t_all // MOE_TM) + N_GROUPS
    cum, src0, gsrc, sdst, info_sorted = _expert_plan(info, n_tiles)
    f_tiles = _experts(x1t, cum, src0, gsrc, sdst, info_sorted,
                       w_gate_e[l], w_up_e[l], w_down_e[l],
                       n_tiles=n_tiles, n_rows=t_all + MOE_TM)
    out_p, out_s = _combine(x1, f_tiles, ln2_g[l][None], ln2_b[l][None],
                            n_prompt=n_prompt, n_sample=n_sample)

    conv_p = xl_tail[:, SUBLANES - (CONV_W - 1):]
    conv_s = jnp.concatenate([state_conv[l], xl_s.reshape(bd, sd, LRU_WIDTH)],
                             axis=1)[:, -(CONV_W - 1):]
    return (out_p.reshape(bp, sp, D_MODEL),
            out_s.reshape(bd, sd, D_MODEL),
            k_p.reshape(1, bp, sp, N_HEADS, V_DIM),
            v_p.reshape(1, bp, sp, N_HEADS, V_DIM),
            h_p.reshape(1, bp, LRU_WIDTH),
            conv_p[None],
            k_s.reshape(1, bd, sd, N_HEADS, V_DIM),
            v_s.reshape(1, bd, sd, N_HEADS, V_DIM),
            h_s.reshape(1, bd, LRU_WIDTH),
            conv_s[None])
```

```python
import functools
import math

import jax
import jax.numpy as jnp
from jax import lax
from jax.experimental import pallas as pl
from jax.experimental.pallas import tpu as pltpu

F32 = jnp.float32
BF16 = jnp.bfloat16
I32 = jnp.int32

D_MODEL = 1024
ATTN_WIDTH = 512
LRU_WIDTH = 512
N_HEADS = 4
V_DIM = 128
QK_DIM = 64
CONV_W = 4
LRU_C = 8.0
IN_WIDTH = 3 * ATTN_WIDTH + 2 * LRU_WIDTH
N_GROUPS = 4
EXPERTS_PER_GROUP = 8
N_EXPERTS = 32
D_EXPERT = 256
PAGE_SIZE = 128
LN_EPS = 1e-5
DEPTH = 1
DEEPNORM_ALPHA = (2 * DEPTH) ** 0.25
LAM_INIT = 0.8 - 0.6 * math.exp(-0.3 * 0)
QK_SCALE = QK_DIM ** -0.5 * math.log2(math.e)

SUBLANES = 8
LANES = 128
VMEM_LIMIT_BYTES = 48 * 1024 * 1024

INPROJ_TM = 512
ATTN_TQ = 512
ATTN_TK = 512
ATTN_CB = 256
ATTN_HEADS_PER_STEP = 2
TOK_TM = 256
MOE_TM = 1024
MOE_CHUNK = MOE_TM // EXPERTS_PER_GROUP
PAGES_PER_STEP = 16
ROUTE_LANES = 128
REC_PAIR, REC_W1, REC_W2 = 0, 1, 2
REC_FIELDS = 3
TILE_ROWS = D_MODEL // LANES


def _cparams(n_axes):
    return pltpu.CompilerParams(
        dimension_semantics=("arbitrary",) * n_axes,
        vmem_limit_bytes=VMEM_LIMIT_BYTES)


def _layer_norm(x, g, b):
    mu = jnp.mean(x, axis=-1, keepdims=True)
    xc = x - mu
    var = jnp.mean(xc * xc, axis=-1, keepdims=True)
    return xc * lax.rsqrt(var + LN_EPS) * g + b


def _diff_lambda(lq1, lk1, lq2, lk2):
    s1 = jnp.sum(lq1 * lk1, axis=-1, keepdims=True)
    s2 = jnp.sum(lq2 * lk2, axis=-1, keepdims=True)
    return jnp.exp(s1) - jnp.exp(s2) + LAM_INIT


def _to_token_tiles(ref, x):
    tm = x.shape[0]
    for s in range(TILE_ROWS):
        ref[pl.ds(s, tm, stride=TILE_ROWS), :] = x[:, s * LANES:(s + 1) * LANES]


def _from_token_tiles(ref, tm):
    return jnp.concatenate(
        [ref[pl.ds(s, tm, stride=TILE_ROWS), :] for s in range(TILE_ROWS)], axis=1)


def _inproj_kernel(x_ref, w_ref, *refs, prompt, tiles_per_seq):
    x = x_ref[...].astype(BF16)
    tm = x.shape[0]

    def section(idx, width):
        return jnp.dot(x, w_ref[:, idx:idx + width], preferred_element_type=F32)

    heads = [slice(h * V_DIM, (h + 1) * V_DIM) for h in range(N_HEADS)]

    def to_head_rows(ref, val):
        for h, sl in enumerate(heads):
            ref[pl.ds(h, tm, stride=N_HEADS), :] = val[:, sl]

    if not prompt:
        q_ref, k_ref, v_ref, xl_ref, gl_ref = refs
        q_ref[...] = section(0, ATTN_WIDTH) * QK_SCALE
        to_head_rows(k_ref, section(ATTN_WIDTH, ATTN_WIDTH))
        to_head_rows(v_ref, section(2 * ATTN_WIDTH, ATTN_WIDTH))
        xl_ref[...] = section(3 * ATTN_WIDTH, LRU_WIDTH)
        gl_ref[...] = section(3 * ATTN_WIDTH + LRU_WIDTH, LRU_WIDTH)
        return

    (cw_ref, cb_ref, wgate_ref, bg_ref, lam_ref, qt_ref, kh_ref, vt_ref, k_ref, v_ref,
     y_ref, hlast_ref, xtail_ref, tail_ref, hc_ref) = refs

    @pl.when(pl.program_id(0) % tiles_per_seq == 0)
    def _():
        tail_ref[...] = jnp.zeros(tail_ref.shape, F32)
        hc_ref[...] = jnp.zeros(hc_ref.shape, F32)

    def emit_q():
        q = section(0, ATTN_WIDTH) * QK_SCALE
        chan = lax.broadcasted_iota(I32, (V_DIM, tm), 0)
        for h, sl in enumerate(heads):
            qt = q[:, sl].T
            qt_ref[h, 0] = jnp.where(chan < QK_DIM, qt, 0.0).astype(BF16)
            qt_ref[h, 1] = jnp.where(chan >= QK_DIM, qt, 0.0).astype(BF16)

    def emit_k():
        k = section(ATTN_WIDTH, ATTN_WIDTH)
        for h, sl in enumerate(heads):
            kh_ref[h] = k[:, sl].astype(BF16)
        to_head_rows(k_ref, k)

    def emit_v():
        v = section(2 * ATTN_WIDTH, ATTN_WIDTH)
        for h, sl in enumerate(heads):
            vt_ref[h, 0] = v[:, sl].T.astype(BF16)
        to_head_rows(v_ref, v)

    xl = section(3 * ATTN_WIDTH, LRU_WIDTH)
    gl = section(3 * ATTN_WIDTH + LRU_WIDTH, LRU_WIDTH)
    xtail_ref[0] = xl[tm - SUBLANES:]
    chunks = _rglru_chunks(xl, gl, y_ref, tail_ref, hc_ref, cw_ref, cb_ref, wgate_ref, bg_ref,
                           lam_ref)
    h_ends = [next(chunks)]
    for emit in (emit_q, emit_k, emit_v):
        emit()
        h_ends.append(next(chunks))
    hlast_ref[0] = jnp.concatenate(h_ends, axis=1)


def _inproj(x2d, w_bf16, lru_args=None, *, prompt, tm, seq=None):
    t = x2d.shape[0]
    assert t % tm == 0
    row = lambda i: (i, 0)
    const = lambda shape: pl.BlockSpec(shape, lambda i: (0,) * len(shape))
    spec512 = pl.BlockSpec((tm, ATTN_WIDTH), row)
    head_rows = pl.BlockSpec((tm * N_HEADS, V_DIM), row)
    f32_512 = jax.ShapeDtypeStruct((t, ATTN_WIDTH), F32)
    f32_heads = jax.ShapeDtypeStruct((t * N_HEADS, V_DIM), F32)
    in_specs = [pl.BlockSpec((tm, D_MODEL), row), const((D_MODEL, IN_WIDTH))]
    operands = [x2d, w_bf16]
    scratch = []
    tiles_per_seq = None
    if prompt:
        assert tm == ATTN_TK and seq % tm == 0
        tiles_per_seq = seq // tm
        batch = t // seq
        per_seq = lambda n: pl.BlockSpec((1, n, LRU_WIDTH), lambda i: (i // tiles_per_seq, 0, 0))
        in_specs += [const((CONV_W, LRU_WIDTH)), const((1, LRU_WIDTH)),
                     const((LRU_WIDTH, 2 * LRU_WIDTH)), const((1, 2 * LRU_WIDTH)),
                     const((1, LRU_WIDTH))]
        operands += list(lru_args)
        out_shape = [jax.ShapeDtypeStruct((N_HEADS, 2, V_DIM, t), BF16),
                     jax.ShapeDtypeStruct((N_HEADS, t, V_DIM), BF16),
                     jax.ShapeDtypeStruct((N_HEADS, t // tm, V_DIM, tm), BF16),
                     f32_heads, f32_heads,
                     jax.ShapeDtypeStruct((t, LRU_WIDTH), BF16),
                     jax.ShapeDtypeStruct((batch, 1, LRU_WIDTH), F32),
                     jax.ShapeDtypeStruct((batch, SUBLANES, LRU_WIDTH), F32)]
        out_specs = [pl.BlockSpec((N_HEADS, 2, V_DIM, tm), lambda i: (0, 0, 0, i)),
                     pl.BlockSpec((N_HEADS, tm, V_DIM), lambda i: (0, i, 0)),
                     pl.BlockSpec((N_HEADS, 1, V_DIM, tm), lambda i: (0, i, 0, 0)),
                     head_rows, head_rows, spec512, per_seq(1), per_seq(SUBLANES)]
        scratch = [pltpu.VMEM((SUBLANES, LRU_WIDTH), F32), pltpu.VMEM((1, LRU_WIDTH), F32)]
    else:
        out_shape = [f32_512, f32_heads, f32_heads, f32_512, f32_512]
        out_specs = [spec512, head_rows, head_rows, spec512, spec512]
    return pl.pallas_call(
        functools.partial(_inproj_kernel, prompt=prompt, tiles_per_seq=tiles_per_seq),
        grid=(t // tm,),
        in_specs=in_specs,
        out_specs=out_specs,
        out_shape=out_shape,
        scratch_shapes=scratch,
        compiler_params=_cparams(1),
        name="inproj_rglru_prompt" if prompt else "inproj_sample",
    )(*operands)


def _attn_prompt_kernel(qt_ref, k_ref, vt_ref, lq1_ref, lk1_ref, lq2_ref, lk2_ref, g_ref,
                        o_ref, s_ref, p_ref, a_ref, m_ref, l_ref, acc_ref):
    tq, tk, cb = ATTN_TQ, ATTN_TK, ATTN_CB
    heads = range(ATTN_HEADS_PER_STEP)
    n_cols = 2 * tq
    i = pl.program_id(2)

    m_ref[...] = jnp.full(m_ref.shape, -jnp.inf, F32)
    l_ref[...] = jnp.zeros(l_ref.shape, F32)
    acc_ref[...] = jnp.zeros(acc_ref.shape, F32)
    for hd in heads:
        p_ref[hd, 1] = jnp.zeros(p_ref.shape[2:], BF16)
        a_ref[hd, 1] = jnp.ones(a_ref.shape[2:], F32)

    def scores(j, slot):
        for hd in heads:
            k = k_ref[hd, pl.ds(pl.multiple_of(j * tk, tk), tk), :]
            for mp in range(2):
                s_ref[hd, slot, :, mp * tq:(mp + 1) * tq] = jnp.dot(
                    k, qt_ref[hd, mp], preferred_element_type=F32)

    def softmax(slot, masked):
        for hd in heads:
            for c in range(n_cols // cb):
                off = (c * cb) % tq
                cs = slice(c * cb, (c + 1) * cb)
                s = s_ref[hd, slot, :, cs]
                if masked:
                    key = lax.broadcasted_iota(I32, s.shape, 0)
                    qpos = lax.broadcasted_iota(I32, s.shape, 1) + off
                    s = jnp.where(key <= qpos, s, -jnp.inf)
                m_prev = m_ref[hd, :, cs]
                m_new = jnp.maximum(m_prev, jnp.max(s, axis=0, keepdims=True))
                alpha = jnp.exp2(m_prev - m_new)
                p = jnp.exp2(s - m_new)
                l_ref[hd, :, cs] = alpha * l_ref[hd, :, cs] + jnp.sum(p, axis=0, keepdims=True)
                m_ref[hd, :, cs] = m_new
                p_ref[hd, slot, :, cs] = p.astype(BF16)
                a_ref[hd, slot, :, cs] = alpha

    def accumulate(j, slot):
        for hd in heads:
            acc_ref[hd] = a_ref[hd, slot] * acc_ref[hd] + jnp.dot(
                vt_ref[hd, j], p_ref[hd, slot], preferred_element_type=F32)

    def stage(j, cur):
        scores(j + 1, 1 - cur)
        softmax(cur, masked=False)
        accumulate(jnp.maximum(j - 1, 0), 1 - cur)

    def diagonal(cur):
        accumulate(jnp.maximum(i - 1, 0), 1 - cur)
        softmax(cur, masked=True)
        accumulate(i, cur)

    scores(0, 0)

    def body(t, carry):
        stage(2 * t, 0)
        stage(2 * t + 1, 1)
        return carry

    lax.fori_loop(0, i // 2, body, 0)

    @pl.when(i % 2 == 0)
    def _():
        diagonal(0)

    @pl.when(i % 2 == 1)
    def _():
        stage(i - 1, 0)
        diagonal(1)

    lam = _diff_lambda(lq1_ref[...], lk1_ref[...], lq2_ref[...], lk2_ref[...])
    for hd in heads:
        o_all = acc_ref[hd] / l_ref[hd]
        o = o_all[:, :tq] - lam * o_all[:, tq:]
        ms = jnp.mean(o * o, axis=0, keepdims=True)
        o = o * lax.rsqrt(ms + LN_EPS) * g_ref[...] * (1.0 - LAM_INIT)
        o_ref[:, hd * V_DIM:(hd + 1) * V_DIM] = o.T.astype(o_ref.dtype)


def _attn_prompt(qt, kh, vt, lam_params, subln_g_col, *, batch, seq):
    assert ATTN_TQ == ATTN_TK and seq % ATTN_TQ == 0 and ATTN_TQ % ATTN_CB == 0
    hp = ATTN_HEADS_PER_STEP
    assert N_HEADS % hp == 0
    nq = seq // ATTN_TQ
    t = batch * seq
    vec = lambda n: pl.BlockSpec((1, n), lambda b, h, i: (0, 0))
    return pl.pallas_call(
        _attn_prompt_kernel,
        grid=(batch, N_HEADS // hp, nq),
        in_specs=[pl.BlockSpec((hp, 2, V_DIM, ATTN_TQ), lambda b, h, i: (h, 0, 0, b * nq + i)),
                  pl.BlockSpec((hp, seq, V_DIM), lambda b, h, i: (h, b, 0)),
                  pl.BlockSpec((hp, nq, V_DIM, ATTN_TK), lambda b, h, i: (h, b, 0, 0)),
                  vec(QK_DIM), vec(QK_DIM), vec(QK_DIM), vec(QK_DIM),
                  pl.BlockSpec((V_DIM, 1), lambda b, h, i: (0, 0))],
        out_specs=pl.BlockSpec((ATTN_TQ, hp * V_DIM), lambda b, h, i: (b * nq + i, h)),
        out_shape=jax.ShapeDtypeStruct((t, ATTN_WIDTH), BF16),
        scratch_shapes=[pltpu.VMEM((hp, 2, ATTN_TK, 2 * ATTN_TQ), F32),
                        pltpu.VMEM((hp, 2, ATTN_TK, 2 * ATTN_TQ), BF16),
                        pltpu.VMEM((hp, 2, 1, 2 * ATTN_TQ), F32),
                        pltpu.VMEM((hp, 1, 2 * ATTN_TQ), F32),
                        pltpu.VMEM((hp, 1, 2 * ATTN_TQ), F32),
                        pltpu.VMEM((hp, V_DIM, 2 * ATTN_TQ), F32)],
        compiler_params=_cparams(3),
        name="attn_prompt",
    )(qt, kh, vt, *lam_params, subln_g_col)


def _attn_sample_kernel(pt_ref, q0_ref, q1_ref, kn0_ref, kn1_ref, vn0_ref, vn1_ref,
                        ck_hbm, cv_hbm, lq1_ref, lk1_ref, lq2_ref, lk2_ref, g_ref,
                        o0_ref, o1_ref, kbuf, vbuf, ksem, vsem, bias_ref, m_ref, l_ref, acc_ref,
                        *, n_chunks, n_steps, dec_seq):
    gpp = PAGES_PER_STEP
    page_rows = PAGE_SIZE * N_HEADS
    hrows = 2 * dec_seq
    g = pl.program_id(0)
    c = g % n_chunks
    slot = g % 2
    streams = ((q0_ref, kn0_ref, vn0_ref, o0_ref), (q1_ref, kn1_ref, vn1_ref, o1_ref))

    def page_copies(step, sl):
        copies = []
        for s in range(len(streams)):
            buf = 2 * s + sl
            for p in range(gpp):
                page = pt_ref[(s * n_steps + step) * gpp + p]
                src = pl.ds(pl.multiple_of(page * page_rows, page_rows), page_rows)
                dst = pl.ds(p * page_rows, page_rows)
                copies.append(pltpu.make_async_copy(ck_hbm.at[src], kbuf.at[buf, dst],
                                                    ksem.at[buf]))
                copies.append(pltpu.make_async_copy(cv_hbm.at[src], vbuf.at[buf, dst],
                                                    vsem.at[buf]))
        return copies

    @pl.when(g == 0)
    def _():
        for cp in page_copies(0, 0):
            cp.start()

    @pl.when(g + 1 < n_steps)
    def _():
        for cp in page_copies(g + 1, 1 - slot):
            cp.start()

    def head_bias(shape):
        row_head = lax.broadcasted_iota(I32, shape, 0) // hrows
        col_head = lax.broadcasted_iota(I32, shape, 1) % N_HEADS
        return row_head == col_head

    @pl.when(g == 0)
    def _():
        bias_ref[...] = jnp.where(head_bias(bias_ref.shape), 0.0, -jnp.inf)

    @pl.when(c == 0)
    def _():
        m_ref[...] = jnp.full(m_ref.shape, -jnp.inf, F32)
        l_ref[...] = jnp.zeros(l_ref.shape, F32)
        acc_ref[...] = jnp.zeros(acc_ref.shape, F32)

    lane = lax.broadcasted_iota(I32, (dec_seq, V_DIM), 1)

    def query_block(q_ref):
        q = q_ref[...]
        q_rows = []
        for h in range(N_HEADS):
            qh = q[:, h * V_DIM:(h + 1) * V_DIM]
            q_rows += [jnp.where(lane < QK_DIM, qh, 0.0), jnp.where(lane >= QK_DIM, qh, 0.0)]
        return jnp.concatenate(q_rows, axis=0).astype(BF16)

    def update(s_idx, qblk, k, v, bias):
        s = lax.dot_general(qblk, k, (((1,), (1,)), ((), ())),
                            preferred_element_type=F32) + bias
        m_prev = m_ref[s_idx]
        m_new = jnp.maximum(m_prev, jnp.max(s, axis=-1, keepdims=True))
        alpha = jnp.exp2(m_prev - m_new)
        p = jnp.exp2(s - m_new)
        l_ref[s_idx] = alpha * l_ref[s_idx] + jnp.sum(p, axis=-1, keepdims=True)
        acc_ref[s_idx] = alpha * acc_ref[s_idx] + jnp.dot(p.astype(BF16), v,
                                                          preferred_element_type=F32)
        m_ref[s_idx] = m_new

    for cp in page_copies(g, slot):
        cp.wait()
    qblks = [query_block(refs[0]) for refs in streams]
    for s_idx in range(len(streams)):
        buf = 2 * s_idx + slot
        update(s_idx, qblks[s_idx], kbuf[buf].astype(BF16), vbuf[buf].astype(BF16),
               bias_ref[...])

    @pl.when(c == n_chunks - 1)
    def _():
        new_rows = dec_seq * N_HEADS
        pad = jnp.zeros((LANES - new_rows, V_DIM), F32)
        shape = (N_HEADS * hrows, LANES)
        qi = lax.broadcasted_iota(I32, shape, 0) % dec_seq
        kj = lax.broadcasted_iota(I32, shape, 1) // N_HEADS
        new_bias = jnp.where(head_bias(shape) & (kj <= qi), 0.0, -jnp.inf)
        lam = _diff_lambda(lq1_ref[...], lk1_ref[...], lq2_ref[...], lk2_ref[...])
        for s_idx, (_, kn_ref, vn_ref, o_ref) in enumerate(streams):
            kn = jnp.concatenate([kn_ref[...], pad], axis=0).astype(BF16)
            vn = jnp.concatenate([vn_ref[...], pad], axis=0).astype(BF16)
            update(s_idx, qblks[s_idx], kn, vn, new_bias)
            o_heads = acc_ref[s_idx] / l_ref[s_idx]
            for h in range(N_HEADS):
                o_all = o_heads[h * hrows:(h + 1) * hrows]
                o = o_all[:dec_seq] - lam * o_all[dec_seq:]
                ms = jnp.mean(o * o, axis=-1, keepdims=True)
                o_ref[:, h * V_DIM:(h + 1) * V_DIM] = (
                    o * lax.rsqrt(ms + LN_EPS) * g_ref[...] * (1.0 - LAM_INIT))


def _attn_sample(q, k_new, v_new, cache_k_rows, cache_v_rows, page_table, lam_params, subln_g,
                 *, dec_batch, dec_seq):
    n_pages = page_table.shape[1]
    assert n_pages % PAGES_PER_STEP == 0 and dec_seq % SUBLANES == 0
    n_chunks = n_pages // PAGES_PER_STEP
    n_streams = 2
    assert dec_batch % n_streams == 0
    half = dec_batch // n_streams
    n_steps = half * n_chunks
    rows = 2 * N_HEADS * dec_seq
    chunk_rows = PAGES_PER_STEP * PAGE_SIZE * N_HEADS

    def stream_specs(s):
        tok = pl.BlockSpec((dec_seq, ATTN_WIDTH), lambda g, pt: (s * half + g // n_chunks, 0))
        tok_heads = pl.BlockSpec((dec_seq * N_HEADS, V_DIM),
                                 lambda g, pt: (s * half + g // n_chunks, 0))
        return tok, tok_heads

    (tok0, heads0), (tok1, heads1) = stream_specs(0), stream_specs(1)
    out_spec = pl.BlockSpec((dec_seq, ATTN_WIDTH), lambda g, pt: (g // n_chunks, 0))
    vec = lambda n: pl.BlockSpec((1, n), lambda g, pt: (0, 0))
    any_spec = pl.BlockSpec(memory_space=pl.ANY)
    n_bufs = 2 * n_streams
    grid_spec = pltpu.PrefetchScalarGridSpec(
        num_scalar_prefetch=1,
        grid=(n_steps,),
        in_specs=[tok0, tok1, heads0, heads1, heads0, heads1, any_spec, any_spec,
                  vec(QK_DIM), vec(QK_DIM), vec(QK_DIM), vec(QK_DIM), vec(V_DIM)],
        out_specs=[out_spec, out_spec],
        scratch_shapes=[pltpu.VMEM((n_bufs, chunk_rows, V_DIM), F32),
                        pltpu.VMEM((n_bufs, chunk_rows, V_DIM), F32),
                        pltpu.SemaphoreType.DMA((n_bufs,)),
                        pltpu.SemaphoreType.DMA((n_bufs,)),
                        pltpu.VMEM((rows, chunk_rows), F32),
                        pltpu.VMEM((n_streams, rows, 1), F32),
                        pltpu.VMEM((n_streams, rows, 1), F32),
                        pltpu.VMEM((n_streams, rows, V_DIM), F32)])
    half_out = jax.ShapeDtypeStruct((half * dec_seq, ATTN_WIDTH), F32)
    o0, o1 = pl.pallas_call(
        functools.partial(_attn_sample_kernel, n_chunks=n_chunks, n_steps=n_steps,
                          dec_seq=dec_seq),
        grid_spec=grid_spec,
        out_shape=[half_out, half_out],
        compiler_params=_cparams(1),
        name="attn_sample",
    )(page_table.reshape(-1), q, q, k_new, k_new, v_new, v_new, cache_k_rows, cache_v_rows,
      *lam_params, subln_g)
    return jnp.concatenate([o0, o1], axis=0)


def _softplus(x):
    y = jnp.exp(-jnp.abs(x))
    w = 1.0 + y
    log1p = jnp.where(w == 1.0, y, jnp.log(w) * (y / (w - 1.0)))
    return jnp.maximum(x, 0.0) + log1p


def _rglru_tile(*args):
    return jnp.concatenate(list(_rglru_chunks(*args)), axis=1)


def _rglru_chunks(x, gl, y_ref, tail_ref, hc_ref, cw_ref, cb_ref, wg_ref, bg_ref, lam_ref):
    for c in range(LRU_WIDTH // LANES):
        cs = slice(c * LANES, (c + 1) * LANES)
        gate_w = jnp.concatenate(
            [wg_ref[cs, cs], wg_ref[cs, LRU_WIDTH + c * LANES:LRU_WIDTH + (c + 1) * LANES]], axis=1)
        gate_b = jnp.concatenate(
            [bg_ref[:, cs], bg_ref[:, LRU_WIDTH + c * LANES:LRU_WIDTH + (c + 1) * LANES]], axis=1)
        xs = x[:, cs]
        y, h_end = _rglru_chunk(xs, gl[:, cs], tail_ref[:, cs], hc_ref[:, cs],
                                cw_ref[:, cs], cb_ref[:, cs], gate_w, gate_b, lam_ref[:, cs])
        tail_ref[:, cs] = xs[xs.shape[0] - SUBLANES:]
        hc_ref[:, cs] = h_end
        y_ref[:, cs] = y.astype(y_ref.dtype)
        yield h_end


def _rglru_chunk(x, gl, tail, h_in, cw, cb, gate_w, gate_b, lam):
    ts, n = x.shape
    xfull = jnp.concatenate([tail, x], axis=0)
    xc = cb + cw[CONV_W - 1:CONV_W] * x
    for d in range(1, CONV_W):
        shifted = pltpu.roll(xfull, d, 0)[SUBLANES:]
        xc = xc + cw[CONV_W - 1 - d:CONV_W - d] * shifted

    gates = jnp.dot(xc.astype(BF16), gate_w, preferred_element_type=F32) + gate_b
    r = jax.nn.sigmoid(gates[:, :n])
    ig = jax.nn.sigmoid(gates[:, n:])
    log_a = -LRU_C * r * _softplus(-lam)
    a = jnp.exp(log_a)
    mult = jnp.sqrt(jnp.maximum(-(a * a + 1.0) * jnp.tanh(log_a), 0.0))
    u = mult * ig * xc

    row = lax.broadcasted_iota(I32, a.shape, 0)
    d = 1
    while d < ts:
        if d < SUBLANES:
            keep = row >= d
            a_prev = jnp.where(keep, pltpu.roll(a, d, 0), 1.0)
            u_prev = jnp.where(keep, pltpu.roll(u, d, 0), 0.0)
            u = a * u_prev + u
            a = a * a_prev
        else:
            u = jnp.concatenate([u[:d], a[d:] * u[:ts - d] + u[d:]], axis=0)
            a = jnp.concatenate([a[:d], a[d:] * a[:ts - d]], axis=0)
        d *= 2
    h = a * h_in + u
    return h * jax.nn.gelu(gl, approximate=True), h[ts - 1:ts]


def _rglru_kernel(xl_ref, gl_ref, conv0_ref, h0_ref, cw_ref, cb_ref, wg_ref, bg_ref, lam_ref,
                  y_ref, hlast_ref, tail_ref, hc_ref):
    s = pl.program_id(1)

    @pl.when(s == 0)
    def _():
        tail_ref[...] = conv0_ref[0]
        hc_ref[...] = h0_ref[0]

    hlast_ref[0] = _rglru_tile(xl_ref[...], gl_ref[...], y_ref, tail_ref, hc_ref,
                               cw_ref, cb_ref, wg_ref, bg_ref, lam_ref)


def _rglru(xl, gl, conv0, h0, conv_w, conv_b, w_gates, b_gates, lam, *, batch, seq, ts):
    assert seq % ts == 0 and ts % SUBLANES == 0
    ns = seq // ts
    tile = pl.BlockSpec((ts, LRU_WIDTH), lambda b, s: (b * ns + s, 0))
    const = lambda shape: pl.BlockSpec(shape, lambda b, s: (0,) * len(shape))
    return pl.pallas_call(
        _rglru_kernel,
        grid=(batch, ns),
        in_specs=[tile, tile,
                  pl.BlockSpec((1, SUBLANES, LRU_WIDTH), lambda b, s: (b, 0, 0)),
                  pl.BlockSpec((1, 1, LRU_WIDTH), lambda b, s: (b, 0, 0)),
                  const((CONV_W, LRU_WIDTH)), const((1, LRU_WIDTH)),
                  const((LRU_WIDTH, 2 * LRU_WIDTH)), const((1, 2 * LRU_WIDTH)),
                  const((1, LRU_WIDTH))],
        out_specs=[tile, pl.BlockSpec((1, 1, LRU_WIDTH), lambda b, s: (b, 0, 0))],
        out_shape=[jax.ShapeDtypeStruct((batch * seq, LRU_WIDTH), BF16 if ts > SUBLANES else F32),
                   jax.ShapeDtypeStruct((batch, 1, LRU_WIDTH), F32)],
        scratch_shapes=[pltpu.VMEM((SUBLANES, LRU_WIDTH), F32),
                        pltpu.VMEM((1, LRU_WIDTH), F32)],
        compiler_params=_cparams(2),
        name="rglru_prompt" if ts > SUBLANES else "rglru_sample",
    )(xl, gl, conv0, h0, conv_w, conv_b, w_gates, b_gates, lam)


def _route(logits):
    lane = lax.broadcasted_iota(I32, logits.shape, 1).astype(F32)
    big = float(ROUTE_LANES)
    neg = -jnp.inf
    is_g = lane < N_GROUPS
    gmax = jnp.max(jnp.where(is_g, logits, neg), axis=-1, keepdims=True)
    gsum = jnp.sum(jnp.where(is_g, jnp.exp(logits - gmax), 0.0), axis=-1, keepdims=True)
    gw = 1.0 / gsum
    gid = jnp.min(jnp.where(is_g & (logits == gmax), lane, big), axis=-1, keepdims=True)
    lo = N_GROUPS + EXPERTS_PER_GROUP * gid
    in_grp = (lane >= lo) & (lane < lo + EXPERTS_PER_GROUP)
    v1 = jnp.max(jnp.where(in_grp, logits, neg), axis=-1, keepdims=True)
    i1 = jnp.min(jnp.where(in_grp & (logits == v1), lane, big), axis=-1, keepdims=True)
    rest = in_grp & (lane != i1)
    v2 = jnp.max(jnp.where(rest, logits, neg), axis=-1, keepdims=True)
    i2 = jnp.min(jnp.where(rest & (logits == v2), lane, big), axis=-1, keepdims=True)
    t = jnp.exp(v2 - v1)
    w1 = gw / (1.0 + t)
    w2 = gw * t / (1.0 + t)
    pair = (i1 - N_GROUPS) * N_EXPERTS + (i2 - N_GROUPS)
    return jnp.where(lane == REC_PAIR, pair,
                     jnp.where(lane == REC_W1, w1, jnp.where(lane == REC_W2, w2, 0.0)))


def _merge_kernel(xp_ref, xs_ref, op_ref, os_ref, yp_ref, ys_ref, wo_ref, g_ref, b_ref,
                  wr_ref, br_ref, x1_ref, x1t_ref, info_ref, *, n_prompt_tiles):
    i = pl.program_id(0)

    def body(x_ref, o_ref, y_ref):
        dot = functools.partial(jnp.dot, preferred_element_type=F32)
        m = dot(o_ref[...].astype(BF16), wo_ref[:ATTN_WIDTH, :])
        m = m + dot(y_ref[...].astype(BF16), wo_ref[ATTN_WIDTH:, :])
        x1 = _layer_norm(DEEPNORM_ALPHA * x_ref[...] + m, g_ref[...], b_ref[...])
        x1_ref[...] = x1
        _to_token_tiles(x1t_ref, x1)
        info_ref[...] = _route(dot(x1.astype(BF16), wr_ref[...]) + br_ref[...]).T[:SUBLANES]

    @pl.when(i < n_prompt_tiles)
    def _():
        body(xp_ref, op_ref, yp_ref)

    @pl.when(i >= n_prompt_tiles)
    def _():
        body(xs_ref, os_ref, ys_ref)


def _merge(xp, xs, o_p, o_s, y_p, y_s, wo, g1, b1, wr, br):
    tm = TOK_TM
    tp, ts = xp.shape[0], xs.shape[0]
    assert tp % tm == 0 and ts % tm == 0
    npt, nst = tp // tm, ts // tm
    t_all = tp + ts
    pidx = lambda i: (jnp.minimum(i, npt - 1), 0)
    sidx = lambda i: (jnp.maximum(i - npt, 0), 0)
    const = lambda shape: pl.BlockSpec(shape, lambda i: (0, 0))
    row = lambda i: (i, 0)
    return pl.pallas_call(
        functools.partial(_merge_kernel, n_prompt_tiles=npt),
        grid=(npt + nst,),
        in_specs=[pl.BlockSpec((tm, D_MODEL), pidx), pl.BlockSpec((tm, D_MODEL), sidx),
                  pl.BlockSpec((tm, ATTN_WIDTH), pidx), pl.BlockSpec((tm, ATTN_WIDTH), sidx),
                  pl.BlockSpec((tm, LRU_WIDTH), pidx), pl.BlockSpec((tm, LRU_WIDTH), sidx),
                  const((D_MODEL, D_MODEL)), const((1, D_MODEL)), const((1, D_MODEL)),
                  const((D_MODEL, ROUTE_LANES)), const((1, ROUTE_LANES))],
        out_specs=[pl.BlockSpec((tm, D_MODEL), row),
                   pl.BlockSpec((tm * TILE_ROWS, LANES), row),
                   pl.BlockSpec((SUBLANES, tm), lambda i: (0, i))],
        out_shape=[jax.ShapeDtypeStruct((t_all, D_MODEL), F32),
                   jax.ShapeDtypeStruct((t_all * TILE_ROWS, LANES), F32),
                   jax.ShapeDtypeStruct((SUBLANES, t_all), F32)],
        compiler_params=_cparams(1),
        name="merge_ln1_route",
    )(xp, xs, o_p, o_s, y_p, y_s, wo.astype(BF16), g1, b1, wr.astype(BF16), br)


def _tile_group(i, cum_ref):
    t = jnp.minimum(i, cum_ref[N_GROUPS - 1] - 1)
    g = jnp.int32(0)
    for k in range(N_GROUPS - 1):
        g = g + (cum_ref[k] <= t).astype(I32)
    return g


def _expert_kernel(cum_ref, src0_ref, gsrc_ref, sdst_ref,
                   x_hbm, info_ref, wg_ref, wu_ref, wd_ref, out_hbm,
                   xbuf, ybuf, xb_ref, infot_ref, yacc_ref, gsem, ssem):
    tm, ch, tr = MOE_TM, MOE_CHUNK, TILE_ROWS
    last_e = EXPERTS_PER_GROUP - 1
    i = pl.program_id(0)
    e = pl.program_id(1)
    slot = i % 2
    other = 1 - slot
    n_live = cum_ref[N_GROUPS - 1]

    def start_gather(idx_ref, rows, base, sl):
        for r in rows:
            src = pl.ds(pl.multiple_of(idx_ref[0, 0, r], tr), tr)
            dst = pl.ds(pl.multiple_of((base + r) * tr, tr), tr)
            pltpu.make_async_copy(x_hbm.at[src], xbuf.at[sl, dst],
                                  gsem.at[sl]).start(priority=r % 2)

    def start_scatter(idx_ref, rows, base, sl):
        for r in rows:
            src = pl.ds(pl.multiple_of((base + r) * tr, tr), tr)
            dst = pl.ds(pl.multiple_of(idx_ref[0, 0, r], tr), tr)
            pltpu.make_async_copy(ybuf.at[sl, src], out_hbm.at[dst],
                                  ssem.at[sl]).start(priority=r % 2)

    def wait_tile(buf, sem):
        pltpu.make_async_copy(x_hbm.at[pl.ds(0, tm * tr)], buf, sem).wait()

    def move_rows():
        start_gather(gsrc_ref, range(ch), e * ch, other)
        start_scatter(sdst_ref, range(ch), e * ch, other)

    @pl.when((i == 0) & (e == 0))
    def _():
        ybuf[1] = jnp.zeros(ybuf.shape[1:], F32)
        yacc_ref[...] = jnp.zeros(yacc_ref.shape, F32)
        start_gather(src0_ref, range(tm), 0, 0)

    @pl.when((e == 0) & (i <= n_live))
    def _():
        wait_tile(xbuf.at[slot], gsem.at[slot])

    @pl.when((e == last_e) & (i >= 1) & (i <= n_live))
    def _():
        wait_tile(ybuf.at[slot], ssem.at[slot])

    @pl.when(i == n_live)
    def _():
        move_rows()

    @pl.when((e == 0) & (i < n_live))
    def _():
        xb_ref[...] = _from_token_tiles(xbuf.at[slot], tm).astype(BF16)
        infot_ref[...] = info_ref[0].T

    @pl.when(i < n_live)
    def _():
        move_rows()
        w_in = jnp.concatenate([wg_ref[0], wu_ref[0]], axis=1).astype(BF16)
        gu = jnp.dot(xb_ref[...], w_in, preferred_element_type=F32)
        gate = gu[:, :D_EXPERT]
        up = gu[:, D_EXPERT:]
        expert = (_tile_group(i, cum_ref) * EXPERTS_PER_GROUP + e).astype(F32)
        info = infot_ref[...]
        pair = info[:, REC_PAIR:REC_PAIR + 1]
        first = jnp.floor(pair * (1.0 / N_EXPERTS))
        second = pair - first * N_EXPERTS
        comb = (jnp.where(first == expert, info[:, REC_W1:REC_W1 + 1], 0.0)
                + jnp.where(second == expert, info[:, REC_W2:REC_W2 + 1], 0.0))
        h = (gate * jax.nn.sigmoid(gate) * up * comb).astype(BF16)
        y = jnp.dot(h, wd_ref[0].astype(BF16), preferred_element_type=F32)
        yacc_ref[...] = y + jnp.where(e > 0, yacc_ref[...], 0.0)

    @pl.when((e == last_e) & (i < n_live))
    def _():
        _to_token_tiles(ybuf.at[slot], yacc_ref[...])

    @pl.when((e == last_e) & (i == n_live))
    def _():
        wait_tile(xbuf.at[other], gsem.at[other])
        wait_tile(ybuf.at[other], ssem.at[other])


def _experts(x1t, cum, src0, gsrc, sdst, info_sorted, wg, wu, wd, *, n_tiles, n_rows):
    tm, ch = MOE_TM, MOE_CHUNK
    ne = EXPERTS_PER_GROUP
    last = n_tiles - 1
    smem = lambda n, fn: pl.BlockSpec((1, 1, n), fn, memory_space=pltpu.SMEM)
    weight = lambda shape: pl.BlockSpec(
        shape, lambda i, e, cum: (_tile_group(i, cum) * ne + e, 0, 0))
    any_spec = pl.BlockSpec(memory_space=pl.ANY)
    grid_spec = pltpu.PrefetchScalarGridSpec(
        num_scalar_prefetch=1,
        grid=(n_tiles + 1, ne),
        in_specs=[smem(tm, lambda i, e, cum: (0, 0, 0)),
                  smem(ch, lambda i, e, cum: (jnp.minimum(i + 1, last) * ne + e, 0, 0)),
                  smem(ch, lambda i, e, cum: (i * ne + e, 0, 0)),
                  any_spec,
                  pl.BlockSpec((1, SUBLANES, tm), lambda i, e, cum: (jnp.minimum(i, last), 0, 0)),
                  weight((1, D_MODEL, D_EXPERT)), weight((1, D_MODEL, D_EXPERT)),
                  weight((1, D_EXPERT, D_MODEL))],
        out_specs=any_spec,
        scratch_shapes=[pltpu.VMEM((2, tm * TILE_ROWS, LANES), F32),
                        pltpu.VMEM((2, tm * TILE_ROWS, LANES), F32),
                        pltpu.VMEM((tm, D_MODEL), BF16),
                        pltpu.VMEM((tm, SUBLANES), F32),
                        pltpu.VMEM((tm, D_MODEL), F32),
                        pltpu.SemaphoreType.DMA((2,)),
                        pltpu.SemaphoreType.DMA((2,))])
    return pl.pallas_call(
        _expert_kernel,
        grid_spec=grid_spec,
        out_shape=jax.ShapeDtypeStruct((n_rows * TILE_ROWS, LANES), F32),
        compiler_params=_cparams(2),
        name="expert_mlp",
    )(cum, src0, gsrc, sdst, x1t, info_sorted, wg, wu, wd)


def _expert_plan(info, n_tiles):
    tm, ch = MOE_TM, MOE_CHUNK
    t_all = info.shape[1]
    gid = info[REC_PAIR].astype(I32) // (N_EXPERTS * EXPERTS_PER_GROUP)
    _, order = lax.sort((gid, lax.iota(I32, t_all)), num_keys=1, is_stable=True)
    groups = jnp.arange(N_GROUPS, dtype=I32)
    counts = jnp.sum((gid[:, None] == groups[None, :]).astype(I32), axis=0)
    start = jnp.cumsum(counts) - counts
    tiles_g = (counts + tm - 1) // tm
    cum = jnp.cumsum(tiles_g).astype(I32)
    n_live = cum[-1]
    ti = jnp.arange(n_tiles, dtype=I32)
    g_i = jnp.minimum(jnp.sum((cum[None, :] <= ti[:, None]).astype(I32), axis=1), N_GROUPS - 1)
    j = ti - (cum - tiles_g)[g_i]
    live = ti < n_live
    tile_rows = jnp.where(live, jnp.clip(counts[g_i] - j * tm, 0, tm), 0)
    tile_start = jnp.where(live, start[g_i] + j * tm, 0)
    r = jnp.arange(tm, dtype=I32)[None, :]
    valid = r < tile_rows[:, None]
    tok = order[jnp.minimum(tile_start[:, None] + r, t_all - 1)]
    src = (jnp.where(valid, tok, 0) * TILE_ROWS).astype(I32)
    dump = ((t_all + r) * TILE_ROWS).astype(I32)
    dst = (jnp.where(valid, tok, t_all + r) * TILE_ROWS).astype(I32)
    dst_prev = jnp.concatenate([dump, dst], axis=0)
    rec = [jnp.where(valid, info[c][tok], 0.0) for c in range(REC_FIELDS)]
    rec += [jnp.zeros((n_tiles, tm), F32)] * (SUBLANES - len(rec))
    info_sorted = jnp.stack(rec, axis=1)
    return (cum, src[0].reshape(1, 1, tm), src.reshape(n_tiles * EXPERTS_PER_GROUP, 1, ch),
            dst_prev.reshape((n_tiles + 1) * EXPERTS_PER_GROUP, 1, ch), info_sorted)


def _combine_kernel(x1_ref, f_ref, g_ref, b_ref, outp_ref, outs_ref, *, n_prompt_tiles):
    tm = TOK_TM
    i = pl.program_id(0)
    f = _from_token_tiles(f_ref, tm)
    out = _layer_norm(DEEPNORM_ALPHA * x1_ref[...] + f, g_ref[...], b_ref[...])

    @pl.when(i < n_prompt_tiles)
    def _():
        outp_ref[...] = out

    @pl.when(i >= n_prompt_tiles)
    def _():
        outs_ref[...] = out


def _combine(x1, f_tiles, g2, b2, *, n_prompt, n_sample):
    tm = TOK_TM
    assert n_sample == tm
    npt = n_prompt // tm
    nt = npt + 1
    row = lambda i: (i, 0)
    const = lambda shape: pl.BlockSpec(shape, lambda i: (0, 0))
    return pl.pallas_call(
        functools.partial(_combine_kernel, n_prompt_tiles=npt),
        grid=(nt,),
        in_specs=[pl.BlockSpec((tm, D_MODEL), row),
                  pl.BlockSpec((tm * TILE_ROWS, LANES), row),
                  const((1, D_MODEL)), const((1, D_MODEL))],
        out_specs=[pl.BlockSpec((tm, D_MODEL), lambda i: (jnp.minimum(i, npt - 1), 0)),
                   pl.BlockSpec((tm, D_MODEL), lambda i: (0, 0))],
        out_shape=[jax.ShapeDtypeStruct((n_prompt, D_MODEL), F32),
                   jax.ShapeDtypeStruct((n_sample, D_MODEL), F32)],
        compiler_params=_cparams(1),
        name="combine_ln2",
    )(x1, f_tiles, g2, b2)


def _block_diag(w):
    nb, n, _ = w.shape
    eye = jnp.eye(nb, dtype=w.dtype)
    return (eye[:, None, :, None] * w[:, :, None, :]).reshape(nb * n, nb * n)


def kernel(x_prompt, x_sample, cache_k, cache_v, state_h, state_conv, page_table, w_in, lambda_q1, lambda_k1, lambda_q2, lambda_k2, subln_g, conv_w, conv_b, w_rg_a, b_rg_a, w_rg_x, b_rg_x, lru_lambda, w_out, ln1_g, ln1_b, w_group_router, b_group_router, w_expert_router, b_expert_router, w_gate_e, w_up_e, w_down_e, ln2_g, ln2_b):
    bp, sp, _ = x_prompt.shape
    bd, sd, _ = x_sample.shape
    assert w_in.shape[0] == DEPTH == 1
    l = 0
    n_prompt, n_sample = bp * sp, bd * sd
    t_all = n_prompt + n_sample

    xp = x_prompt.reshape(n_prompt, D_MODEL)
    xs = x_sample.reshape(n_sample, D_MODEL)
    w_in_b = w_in[l].astype(BF16)
    lam_params = (lambda_q1[l][None], lambda_k1[l][None], lambda_q2[l][None], lambda_k2[l][None])
    w_gates = jnp.concatenate([_block_diag(w_rg_a[l]), _block_diag(w_rg_x[l])], axis=1).astype(BF16)
    b_gates = jnp.concatenate([b_rg_a[l].reshape(1, -1), b_rg_x[l].reshape(1, -1)], axis=1)
    lru_args = (conv_w[l], conv_b[l][None], w_gates, b_gates, lru_lambda[l][None])

    qt, kh, vt, k_p, v_p, y_p, h_p, xl_tail = _inproj(xp, w_in_b, lru_args, prompt=True,
                                                      tm=INPROJ_TM, seq=sp)
    o_p = _attn_prompt(qt, kh, vt, lam_params, subln_g[l][:, None], batch=bp, seq=sp)

    q_s, k_s, v_s, xl_s, gl_s = _inproj(xs, w_in_b, prompt=False, tm=n_sample)
    n_phys = cache_k.shape[1]
    ck = cache_k.reshape(DEPTH * n_phys * PAGE_SIZE * N_HEADS, V_DIM)
    cv = cache_v.reshape(DEPTH * n_phys * PAGE_SIZE * N_HEADS, V_DIM)
    o_s = _attn_sample(q_s, k_s, v_s, ck, cv, page_table, lam_params, subln_g[l][None],
                       dec_batch=bd, dec_seq=sd)
    conv0 = jnp.concatenate(
        [jnp.zeros((bd, SUBLANES - (CONV_W - 1), LRU_WIDTH), F32), state_conv[l]], axis=1)
    y_s, h_s = _rglru(xl_s, gl_s, conv0, state_h[l][:, None, :], *lru_args,
                      batch=bd, seq=sd, ts=sd)

    w_router = jnp.concatenate(
        [w_group_router[l], w_expert_router[l],
         jnp.zeros((D_MODEL, ROUTE_LANES - N_GROUPS - N_EXPERTS), F32)], axis=1)
    b_router = jnp.concatenate(
        [b_group_router[l], b_expert_router[l],
         jnp.zeros((ROUTE_LANES - N_GROUPS - N_EXPERTS,), F32)])[None]
    x1, x1t, info = _merge(xp, xs, o_p, o_s, y_p, y_s, w_out[l], ln1_g[l][None], ln1_b[l][None],
                           w_router, b_router)
    n_tiles = -(-t_all // MOE_TM) + N_GROUPS
    cum, src0, gsrc, sdst, info_sorted = _expert_plan(info, n_tiles)
    f_tiles = _experts(x1t, cum, src0, gsrc, sdst, info_sorted,
                       w_gate_e[l], w_up_e[l], w_down_e[l],
                       n_tiles=n_tiles, n_rows=t_all + MOE_TM)
    out_p, out_s = _combine(x1, f_tiles, ln2_g[l][None], ln2_b[l][None],
                            n_prompt=n_prompt, n_sample=n_sample)

    conv_p = xl_tail[:, SUBLANES - (CONV_W - 1):]
    conv_s = jnp.concatenate([state_conv[l], xl_s.reshape(bd, sd, LRU_WIDTH)],
                             axis=1)[:, -(CONV_W - 1):]
    return (out_p.reshape(bp, sp, D_MODEL),
            out_s.reshape(bd, sd, D_MODEL),
            k_p.reshape(1, bp, sp, N_HEADS, V_DIM),
            v_p.reshape(1, bp, sp, N_HEADS, V_DIM),
            h_p.reshape(1, bp, LRU_WIDTH),
            conv_p[None],
            k_s.reshape(1, bd, sd, N_HEADS, V_DIM),
            v_s.reshape(1, bd, sd, N_HEADS, V_DIM),
            h_s.reshape(1, bd, LRU_WIDTH),
            conv_s[None])
```

```python
import functools
import math

import jax
import jax.numpy as jnp
from jax import lax
from jax.experimental import pallas as pl
from jax.experimental.pallas import tpu as pltpu

F32 = jnp.float32
BF16 = jnp.bfloat16
I32 = jnp.int32

D_MODEL = 1024
ATTN_WIDTH = 512
LRU_WIDTH = 512
N_HEADS = 4
V_DIM = 128
QK_DIM = 64
CONV_W = 4
LRU_C = 8.0
IN_WIDTH = 3 * ATTN_WIDTH + 2 * LRU_WIDTH
N_GROUPS = 4
EXPERTS_PER_GROUP = 8
N_EXPERTS = 32
D_EXPERT = 256
PAGE_SIZE = 128
LN_EPS = 1e-5
DEPTH = 1
DEEPNORM_ALPHA = (2 * DEPTH) ** 0.25
LAM_INIT = 0.8 - 0.6 * math.exp(-0.3 * 0)
QK_SCALE = QK_DIM ** -0.5 * math.log2(math.e)

SUBLANES = 8
LANES = 128
VMEM_LIMIT_BYTES = 48 * 1024 * 1024

INPROJ_TM = 512
ATTN_TQ = 512
ATTN_TK = 512
ATTN_CB = 256
ATTN_HEADS_PER_STEP = 2
TOK_TM = 512
MOE_TM = 1024
MOE_CHUNK = MOE_TM // EXPERTS_PER_GROUP
PAGES_PER_STEP = 16
ROUTE_LANES = 128
REC_PAIR, REC_W1, REC_W2 = 0, 1, 2
REC_FIELDS = 3
TILE_ROWS = D_MODEL // LANES


def _cparams(n_axes):
    return pltpu.CompilerParams(
        dimension_semantics=("arbitrary",) * n_axes,
        vmem_limit_bytes=VMEM_LIMIT_BYTES)


def _layer_norm(x, g, b):
    mu = jnp.mean(x, axis=-1, keepdims=True)
    xc = x - mu
    var = jnp.mean(xc * xc, axis=-1, keepdims=True)
    return xc * lax.rsqrt(var + LN_EPS) * g + b


def _diff_lambda(lq1, lk1, lq2, lk2):
    s1 = jnp.sum(lq1 * lk1, axis=-1, keepdims=True)
    s2 = jnp.sum(lq2 * lk2, axis=-1, keepdims=True)
    return jnp.exp(s1) - jnp.exp(s2) + LAM_INIT


def _to_token_tiles(ref, x):
    tm = x.shape[0]
    for s in range(TILE_ROWS):
        ref[pl.ds(s, tm, stride=TILE_ROWS), :] = x[:, s * LANES:(s + 1) * LANES]


def _from_token_tiles(ref, tm):
    return jnp.concatenate(
        [ref[pl.ds(s, tm, stride=TILE_ROWS), :] for s in range(TILE_ROWS)], axis=1)


def _inproj_kernel(x_ref, w_ref, *refs, prompt, tiles_per_seq):
    x = x_ref[...].astype(BF16)
    tm = x.shape[0]

    def section(idx, width):
        return jnp.dot(x, w_ref[:, idx:idx + width], preferred_element_type=F32)

    heads = [slice(h * V_DIM, (h + 1) * V_DIM) for h in range(N_HEADS)]

    def to_head_rows(ref, val):
        for h, sl in enumerate(heads):
            ref[pl.ds(h, tm, stride=N_HEADS), :] = val[:, sl]

    if not prompt:
        q_ref, k_ref, v_ref, xl_ref, gl_ref = refs
        q_ref[...] = section(0, ATTN_WIDTH) * QK_SCALE
        to_head_rows(k_ref, section(ATTN_WIDTH, ATTN_WIDTH))
        to_head_rows(v_ref, section(2 * ATTN_WIDTH, ATTN_WIDTH))
        xl_ref[...] = section(3 * ATTN_WIDTH, LRU_WIDTH)
        gl_ref[...] = section(3 * ATTN_WIDTH + LRU_WIDTH, LRU_WIDTH)
        return

    (cw_ref, cb_ref, wgate_ref, bg_ref, lam_ref, qt_ref, kh_ref, vt_ref, k_ref, v_ref,
     y_ref, hlast_ref, xtail_ref, tail_ref, hc_ref) = refs

    @pl.when(pl.program_id(0) % tiles_per_seq == 0)
    def _():
        tail_ref[...] = jnp.zeros(tail_ref.shape, F32)
        hc_ref[...] = jnp.zeros(hc_ref.shape, F32)

    def emit_q():
        q = section(0, ATTN_WIDTH) * QK_SCALE
        chan = lax.broadcasted_iota(I32, (V_DIM, tm), 0)
        for h, sl in enumerate(heads):
            qt = q[:, sl].T
            qt_ref[h, 0] = jnp.where(chan < QK_DIM, qt, 0.0).astype(BF16)
            qt_ref[h, 1] = jnp.where(chan >= QK_DIM, qt, 0.0).astype(BF16)

    def emit_k():
        k = section(ATTN_WIDTH, ATTN_WIDTH)
        for h, sl in enumerate(heads):
            kh_ref[h] = k[:, sl].astype(BF16)
        to_head_rows(k_ref, k)

    def emit_v():
        v = section(2 * ATTN_WIDTH, ATTN_WIDTH)
        for h, sl in enumerate(heads):
            vt_ref[h, 0] = v[:, sl].T.astype(BF16)
        to_head_rows(v_ref, v)

    xl = section(3 * ATTN_WIDTH, LRU_WIDTH)
    gl = section(3 * ATTN_WIDTH + LRU_WIDTH, LRU_WIDTH)
    xtail_ref[0] = xl[tm - SUBLANES:]
    chunks = _rglru_chunks(xl, gl, y_ref, tail_ref, hc_ref, cw_ref, cb_ref, wgate_ref, bg_ref,
                           lam_ref)
    h_ends = [next(chunks)]
    for emit in (emit_q, emit_k, emit_v):
        emit()
        h_ends.append(next(chunks))
    hlast_ref[0] = jnp.concatenate(h_ends, axis=1)


def _inproj(x2d, w_bf16, lru_args=None, *, prompt, tm, seq=None):
    t = x2d.shape[0]
    assert t % tm == 0
    row = lambda i: (i, 0)
    const = lambda shape: pl.BlockSpec(shape, lambda i: (0,) * len(shape))
    spec512 = pl.BlockSpec((tm, ATTN_WIDTH), row)
    head_rows = pl.BlockSpec((tm * N_HEADS, V_DIM), row)
    f32_512 = jax.ShapeDtypeStruct((t, ATTN_WIDTH), F32)
    f32_heads = jax.ShapeDtypeStruct((t * N_HEADS, V_DIM), F32)
    in_specs = [pl.BlockSpec((tm, D_MODEL), row), const((D_MODEL, IN_WIDTH))]
    operands = [x2d, w_bf16]
    scratch = []
    tiles_per_seq = None
    if prompt:
        assert tm == ATTN_TK and seq % tm == 0
        tiles_per_seq = seq // tm
        batch = t // seq
        per_seq = lambda n: pl.BlockSpec((1, n, LRU_WIDTH), lambda i: (i // tiles_per_seq, 0, 0))
        in_specs += [const((CONV_W, LRU_WIDTH)), const((1, LRU_WIDTH)),
                     const((LRU_WIDTH, 2 * LRU_WIDTH)), const((1, 2 * LRU_WIDTH)),
                     const((1, LRU_WIDTH))]
        operands += list(lru_args)
        out_shape = [jax.ShapeDtypeStruct((N_HEADS, 2, V_DIM, t), BF16),
                     jax.ShapeDtypeStruct((N_HEADS, t, V_DIM), BF16),
                     jax.ShapeDtypeStruct((N_HEADS, t // tm, V_DIM, tm), BF16),
                     f32_heads, f32_heads,
                     jax.ShapeDtypeStruct((t, LRU_WIDTH), BF16),
                     jax.ShapeDtypeStruct((batch, 1, LRU_WIDTH), F32),
                     jax.ShapeDtypeStruct((batch, SUBLANES, LRU_WIDTH), F32)]
        out_specs = [pl.BlockSpec((N_HEADS, 2, V_DIM, tm), lambda i: (0, 0, 0, i)),
                     pl.BlockSpec((N_HEADS, tm, V_DIM), lambda i: (0, i, 0)),
                     pl.BlockSpec((N_HEADS, 1, V_DIM, tm), lambda i: (0, i, 0, 0)),
                     head_rows, head_rows, spec512, per_seq(1), per_seq(SUBLANES)]
        scratch = [pltpu.VMEM((SUBLANES, LRU_WIDTH), F32), pltpu.VMEM((1, LRU_WIDTH), F32)]
    else:
        out_shape = [f32_512, f32_heads, f32_heads, f32_512, f32_512]
        out_specs = [spec512, head_rows, head_rows, spec512, spec512]
    return pl.pallas_call(
        functools.partial(_inproj_kernel, prompt=prompt, tiles_per_seq=tiles_per_seq),
        grid=(t // tm,),
        in_specs=in_specs,
        out_specs=out_specs,
        out_shape=out_shape,
        scratch_shapes=scratch,
        compiler_params=_cparams(1),
        name="inproj_rglru_prompt" if prompt else "inproj_sample",
    )(*operands)


def _attn_prompt_kernel(qt_ref, k_ref, vt_ref, lq1_ref, lk1_ref, lq2_ref, lk2_ref, g_ref,
                        o_ref, s_ref, p_ref, a_ref, m_ref, l_ref, acc_ref):
    tq, tk, cb = ATTN_TQ, ATTN_TK, ATTN_CB
    heads = range(ATTN_HEADS_PER_STEP)
    n_cols = 2 * tq
    i = pl.program_id(2)

    m_ref[...] = jnp.full(m_ref.shape, -jnp.inf, F32)
    l_ref[...] = jnp.zeros(l_ref.shape, F32)
    acc_ref[...] = jnp.zeros(acc_ref.shape, F32)
    for hd in heads:
        p_ref[hd, 1] = jnp.zeros(p_ref.shape[2:], BF16)
        a_ref[hd, 1] = jnp.ones(a_ref.shape[2:], F32)

    def scores(j, slot):
        for hd in heads:
            k = k_ref[hd, pl.ds(pl.multiple_of(j * tk, tk), tk), :]
            for mp in range(2):
                s_ref[hd, slot, :, mp * tq:(mp + 1) * tq] = jnp.dot(
                    k, qt_ref[hd, mp], preferred_element_type=F32)

    def softmax(slot, masked):
        for hd in heads:
            for c in range(n_cols // cb):
                off = (c * cb) % tq
                cs = slice(c * cb, (c + 1) * cb)
                s = s_ref[hd, slot, :, cs]
                if masked:
                    key = lax.broadcasted_iota(I32, s.shape, 0)
                    qpos = lax.broadcasted_iota(I32, s.shape, 1) + off
                    s = jnp.where(key <= qpos, s, -jnp.inf)
                m_prev = m_ref[hd, :, cs]
                m_new = jnp.maximum(m_prev, jnp.max(s, axis=0, keepdims=True))
                alpha = jnp.exp2(m_prev - m_new)
                p = jnp.exp2(s - m_new)
                l_ref[hd, :, cs] = alpha * l_ref[hd, :, cs] + jnp.sum(p, axis=0, keepdims=True)
                m_ref[hd, :, cs] = m_new
                p_ref[hd, slot, :, cs] = p.astype(BF16)
                a_ref[hd, slot, :, cs] = alpha

    def accumulate(j, slot):
        for hd in heads:
            acc_ref[hd] = a_ref[hd, slot] * acc_ref[hd] + jnp.dot(
                vt_ref[hd, j], p_ref[hd, slot], preferred_element_type=F32)

    def stage(j, cur):
        scores(j + 1, 1 - cur)
        softmax(cur, masked=False)
        accumulate(jnp.maximum(j - 1, 0), 1 - cur)

    def diagonal(cur):
        accumulate(jnp.maximum(i - 1, 0), 1 - cur)
        softmax(cur, masked=True)
        accumulate(i, cur)

    scores(0, 0)

    def body(t, carry):
        stage(2 * t, 0)
        stage(2 * t + 1, 1)
        return carry

    lax.fori_loop(0, i // 2, body, 0)

    @pl.when(i % 2 == 0)
    def _():
        diagonal(0)

    @pl.when(i % 2 == 1)
    def _():
        stage(i - 1, 0)
        diagonal(1)

    lam = _diff_lambda(lq1_ref[...], lk1_ref[...], lq2_ref[...], lk2_ref[...])
    for hd in heads:
        o_all = acc_ref[hd] / l_ref[hd]
        o = o_all[:, :tq] - lam * o_all[:, tq:]
        ms = jnp.mean(o * o, axis=0, keepdims=True)
        o = o * lax.rsqrt(ms + LN_EPS) * g_ref[...] * (1.0 - LAM_INIT)
        o_ref[:, hd * V_DIM:(hd + 1) * V_DIM] = o.T.astype(o_ref.dtype)


def _attn_prompt(qt, kh, vt, lam_params, subln_g_col, *, batch, seq):
    assert ATTN_TQ == ATTN_TK and seq % ATTN_TQ == 0 and ATTN_TQ % ATTN_CB == 0
    hp = ATTN_HEADS_PER_STEP
    assert N_HEADS % hp == 0
    nq = seq // ATTN_TQ
    t = batch * seq
    vec = lambda n: pl.BlockSpec((1, n), lambda b, h, i: (0, 0))
    return pl.pallas_call(
        _attn_prompt_kernel,
        grid=(batch, N_HEADS // hp, nq),
        in_specs=[pl.BlockSpec((hp, 2, V_DIM, ATTN_TQ), lambda b, h, i: (h, 0, 0, b * nq + i)),
                  pl.BlockSpec((hp, seq, V_DIM), lambda b, h, i: (h, b, 0)),
                  pl.BlockSpec((hp, nq, V_DIM, ATTN_TK), lambda b, h, i: (h, b, 0, 0)),
                  vec(QK_DIM), vec(QK_DIM), vec(QK_DIM), vec(QK_DIM),
                  pl.BlockSpec((V_DIM, 1), lambda b, h, i: (0, 0))],
        out_specs=pl.BlockSpec((ATTN_TQ, hp * V_DIM), lambda b, h, i: (b * nq + i, h)),
        out_shape=jax.ShapeDtypeStruct((t, ATTN_WIDTH), BF16),
        scratch_shapes=[pltpu.VMEM((hp, 2, ATTN_TK, 2 * ATTN_TQ), F32),
                        pltpu.VMEM((hp, 2, ATTN_TK, 2 * ATTN_TQ), BF16),
                        pltpu.VMEM((hp, 2, 1, 2 * ATTN_TQ), F32),
                        pltpu.VMEM((hp, 1, 2 * ATTN_TQ), F32),
                        pltpu.VMEM((hp, 1, 2 * ATTN_TQ), F32),
                        pltpu.VMEM((hp, V_DIM, 2 * ATTN_TQ), F32)],
        compiler_params=_cparams(3),
        name="attn_prompt",
    )(qt, kh, vt, *lam_params, subln_g_col)


def _attn_sample_kernel(pt_ref, q0_ref, q1_ref, kn0_ref, kn1_ref, vn0_ref, vn1_ref,
                        ck_hbm, cv_hbm, lq1_ref, lk1_ref, lq2_ref, lk2_ref, g_ref,
                        o0_ref, o1_ref, kbuf, vbuf, ksem, vsem, bias_ref, m_ref, l_ref, acc_ref,
                        *, n_chunks, n_steps, dec_seq):
    gpp = PAGES_PER_STEP
    page_rows = PAGE_SIZE * N_HEADS
    hrows = 2 * dec_seq
    g = pl.program_id(0)
    c = g % n_chunks
    slot = g % 2
    streams = ((q0_ref, kn0_ref, vn0_ref, o0_ref), (q1_ref, kn1_ref, vn1_ref, o1_ref))

    def page_copies(step, sl):
        copies = []
        for s in range(len(streams)):
            buf = 2 * s + sl
            for p in range(gpp):
                page = pt_ref[(s * n_steps + step) * gpp + p]
                src = pl.ds(pl.multiple_of(page * page_rows, page_rows), page_rows)
                dst = pl.ds(p * page_rows, page_rows)
                copies.append(pltpu.make_async_copy(ck_hbm.at[src], kbuf.at[buf, dst],
                                                    ksem.at[buf]))
                copies.append(pltpu.make_async_copy(cv_hbm.at[src], vbuf.at[buf, dst],
                                                    vsem.at[buf]))
        return copies

    @pl.when(g == 0)
    def _():
        for cp in page_copies(0, 0):
            cp.start()

    @pl.when(g + 1 < n_steps)
    def _():
        for cp in page_copies(g + 1, 1 - slot):
            cp.start()

    def head_bias(shape):
        row_head = lax.broadcasted_iota(I32, shape, 0) // hrows
        col_head = lax.broadcasted_iota(I32, shape, 1) % N_HEADS
        return row_head == col_head

    @pl.when(g == 0)
    def _():
        bias_ref[...] = jnp.where(head_bias(bias_ref.shape), 0.0, -jnp.inf)

    @pl.when(c == 0)
    def _():
        m_ref[...] = jnp.full(m_ref.shape, -jnp.inf, F32)
        l_ref[...] = jnp.zeros(l_ref.shape, F32)
        acc_ref[...] = jnp.zeros(acc_ref.shape, F32)

    lane = lax.broadcasted_iota(I32, (dec_seq, V_DIM), 1)

    def query_block(q_ref):
        q = q_ref[...]
        q_rows = []
        for h in range(N_HEADS):
            qh = q[:, h * V_DIM:(h + 1) * V_DIM]
            q_rows += [jnp.where(lane < QK_DIM, qh, 0.0), jnp.where(lane >= QK_DIM, qh, 0.0)]
        return jnp.concatenate(q_rows, axis=0).astype(BF16)

    def update(s_idx, qblk, k, v, bias):
        s = lax.dot_general(qblk, k, (((1,), (1,)), ((), ())),
                            preferred_element_type=F32) + bias
        m_prev = m_ref[s_idx]
        m_new = jnp.maximum(m_prev, jnp.max(s, axis=-1, keepdims=True))
        alpha = jnp.exp2(m_prev - m_new)
        p = jnp.exp2(s - m_new)
        l_ref[s_idx] = alpha * l_ref[s_idx] + jnp.sum(p, axis=-1, keepdims=True)
        acc_ref[s_idx] = alpha * acc_ref[s_idx] + jnp.dot(p.astype(BF16), v,
                                                          preferred_element_type=F32)
        m_ref[s_idx] = m_new

    for cp in page_copies(g, slot):
        cp.wait()
    qblks = [query_block(refs[0]) for refs in streams]
    for s_idx in range(len(streams)):
        buf = 2 * s_idx + slot
        update(s_idx, qblks[s_idx], kbuf[buf].astype(BF16), vbuf[buf].astype(BF16),
               bias_ref[...])

    @pl.when(c == n_chunks - 1)
    def _():
        new_rows = dec_seq * N_HEADS
        pad = jnp.zeros((LANES - new_rows, V_DIM), F32)
        shape = (N_HEADS * hrows, LANES)
        qi = lax.broadcasted_iota(I32, shape, 0) % dec_seq
        kj = lax.broadcasted_iota(I32, shape, 1) // N_HEADS
        new_bias = jnp.where(head_bias(shape) & (kj <= qi), 0.0, -jnp.inf)
        lam = _diff_lambda(lq1_ref[...], lk1_ref[...], lq2_ref[...], lk2_ref[...])
        for s_idx, (_, kn_ref, vn_ref, o_ref) in enumerate(streams):
            kn = jnp.concatenate([kn_ref[...], pad], axis=0).astype(BF16)
            vn = jnp.concatenate([vn_ref[...], pad], axis=0).astype(BF16)
            update(s_idx, qblks[s_idx], kn, vn, new_bias)
            o_heads = acc_ref[s_idx] / l_ref[s_idx]
            for h in range(N_HEADS):
                o_all = o_heads[h * hrows:(h + 1) * hrows]
                o = o_all[:dec_seq] - lam * o_all[dec_seq:]
                ms = jnp.mean(o * o, axis=-1, keepdims=True)
                o_ref[:, h * V_DIM:(h + 1) * V_DIM] = (
                    o * lax.rsqrt(ms + LN_EPS) * g_ref[...] * (1.0 - LAM_INIT))


def _attn_sample(q, k_new, v_new, cache_k_rows, cache_v_rows, page_table, lam_params, subln_g,
                 *, dec_batch, dec_seq):
    n_pages = page_table.shape[1]
    assert n_pages % PAGES_PER_STEP == 0 and dec_seq % SUBLANES == 0
    n_chunks = n_pages // PAGES_PER_STEP
    n_streams = 2
    assert dec_batch % n_streams == 0
    half = dec_batch // n_streams
    n_steps = half * n_chunks
    rows = 2 * N_HEADS * dec_seq
    chunk_rows = PAGES_PER_STEP * PAGE_SIZE * N_HEADS

    def stream_specs(s):
        tok = pl.BlockSpec((dec_seq, ATTN_WIDTH), lambda g, pt: (s * half + g // n_chunks, 0))
        tok_heads = pl.BlockSpec((dec_seq * N_HEADS, V_DIM),
                                 lambda g, pt: (s * half + g // n_chunks, 0))
        return tok, tok_heads

    (tok0, heads0), (tok1, heads1) = stream_specs(0), stream_specs(1)
    out_spec = pl.BlockSpec((dec_seq, ATTN_WIDTH), lambda g, pt: (g // n_chunks, 0))
    vec = lambda n: pl.BlockSpec((1, n), lambda g, pt: (0, 0))
    any_spec = pl.BlockSpec(memory_space=pl.ANY)
    n_bufs = 2 * n_streams
    grid_spec = pltpu.PrefetchScalarGridSpec(
        num_scalar_prefetch=1,
        grid=(n_steps,),
        in_specs=[tok0, tok1, heads0, heads1, heads0, heads1, any_spec, any_spec,
                  vec(QK_DIM), vec(QK_DIM), vec(QK_DIM), vec(QK_DIM), vec(V_DIM)],
        out_specs=[out_spec, out_spec],
        scratch_shapes=[pltpu.VMEM((n_bufs, chunk_rows, V_DIM), F32),
                        pltpu.VMEM((n_bufs, chunk_rows, V_DIM), F32),
                        pltpu.SemaphoreType.DMA((n_bufs,)),
                        pltpu.SemaphoreType.DMA((n_bufs,)),
                        pltpu.VMEM((rows, chunk_rows), F32),
                        pltpu.VMEM((n_streams, rows, 1), F32),
                        pltpu.VMEM((n_streams, rows, 1), F32),
                        pltpu.VMEM((n_streams, rows, V_DIM), F32)])
    half_out = jax.ShapeDtypeStruct((half * dec_seq, ATTN_WIDTH), F32)
    o0, o1 = pl.pallas_call(
        functools.partial(_attn_sample_kernel, n_chunks=n_chunks, n_steps=n_steps,
                          dec_seq=dec_seq),
        grid_spec=grid_spec,
        out_shape=[half_out, half_out],
        compiler_params=_cparams(1),
        name="attn_sample",
    )(page_table.reshape(-1), q, q, k_new, k_new, v_new, v_new, cache_k_rows, cache_v_rows,
      *lam_params, subln_g)
    return jnp.concatenate([o0, o1], axis=0)


def _softplus(x):
    y = jnp.exp(-jnp.abs(x))
    w = 1.0 + y
    log1p = jnp.where(w == 1.0, y, jnp.log(w) * (y / (w - 1.0)))
    return jnp.maximum(x, 0.0) + log1p


def _rglru_tile(*args):
    return jnp.concatenate(list(_rglru_chunks(*args)), axis=1)


def _rglru_chunks(x, gl, y_ref, tail_ref, hc_ref, cw_ref, cb_ref, wg_ref, bg_ref, lam_ref):
    for c in range(LRU_WIDTH // LANES):
        cs = slice(c * LANES, (c + 1) * LANES)
        gate_w = jnp.concatenate(
            [wg_ref[cs, cs], wg_ref[cs, LRU_WIDTH + c * LANES:LRU_WIDTH + (c + 1) * LANES]], axis=1)
        gate_b = jnp.concatenate(
            [bg_ref[:, cs], bg_ref[:, LRU_WIDTH + c * LANES:LRU_WIDTH + (c + 1) * LANES]], axis=1)
        xs = x[:, cs]
        y, h_end = _rglru_chunk(xs, gl[:, cs], tail_ref[:, cs], hc_ref[:, cs],
                                cw_ref[:, cs], cb_ref[:, cs], gate_w, gate_b, lam_ref[:, cs])
        tail_ref[:, cs] = xs[xs.shape[0] - SUBLANES:]
        hc_ref[:, cs] = h_end
        y_ref[:, cs] = y.astype(y_ref.dtype)
        yield h_end


def _rglru_chunk(x, gl, tail, h_in, cw, cb, gate_w, gate_b, lam):
    ts, n = x.shape
    xfull = jnp.concatenate([tail, x], axis=0)
    xc = cb + cw[CONV_W - 1:CONV_W] * x
    for d in range(1, CONV_W):
        shifted = pltpu.roll(xfull, d, 0)[SUBLANES:]
        xc = xc + cw[CONV_W - 1 - d:CONV_W - d] * shifted

    gates = jnp.dot(xc.astype(BF16), gate_w, preferred_element_type=F32) + gate_b
    r = jax.nn.sigmoid(gates[:, :n])
    ig = jax.nn.sigmoid(gates[:, n:])
    log_a = -LRU_C * r * _softplus(-lam)
    a = jnp.exp(log_a)
    mult = jnp.sqrt(jnp.maximum(-(a * a + 1.0) * jnp.tanh(log_a), 0.0))
    u = mult * ig * xc

    row = lax.broadcasted_iota(I32, a.shape, 0)
    d = 1
    while d < ts:
        if d < SUBLANES:
            keep = row >= d
            a_prev = jnp.where(keep, pltpu.roll(a, d, 0), 1.0)
            u_prev = jnp.where(keep, pltpu.roll(u, d, 0), 0.0)
            u = a * u_prev + u
            a = a * a_prev
        else:
            u = jnp.concatenate([u[:d], a[d:] * u[:ts - d] + u[d:]], axis=0)
            a = jnp.concatenate([a[:d], a[d:] * a[:ts - d]], axis=0)
        d *= 2
    h = a * h_in + u
    return h * jax.nn.gelu(gl, approximate=True), h[ts - 1:ts]


def _rglru_kernel(xl_ref, gl_ref, conv0_ref, h0_ref, cw_ref, cb_ref, wg_ref, bg_ref, lam_ref,
                  y_ref, hlast_ref, tail_ref, hc_ref):
    s = pl.program_id(1)

    @pl.when(s == 0)
    def _():
        tail_ref[...] = conv0_ref[0]
        hc_ref[...] = h0_ref[0]

    hlast_ref[0] = _rglru_tile(xl_ref[...], gl_ref[...], y_ref, tail_ref, hc_ref,
                               cw_ref, cb_ref, wg_ref, bg_ref, lam_ref)


def _rglru(xl, gl, conv0, h0, conv_w, conv_b, w_gates, b_gates, lam, *, batch, seq, ts):
    assert seq % ts == 0 and ts % SUBLANES == 0
    ns = seq // ts
    tile = pl.BlockSpec((ts, LRU_WIDTH), lambda b, s: (b * ns + s, 0))
    const = lambda shape: pl.BlockSpec(shape, lambda b, s: (0,) * len(shape))
    return pl.pallas_call(
        _rglru_kernel,
        grid=(batch, ns),
        in_specs=[tile, tile,
                  pl.BlockSpec((1, SUBLANES, LRU_WIDTH), lambda b, s: (b, 0, 0)),
                  pl.BlockSpec((1, 1, LRU_WIDTH), lambda b, s: (b, 0, 0)),
                  const((CONV_W, LRU_WIDTH)), const((1, LRU_WIDTH)),
                  const((LRU_WIDTH, 2 * LRU_WIDTH)), const((1, 2 * LRU_WIDTH)),
                  const((1, LRU_WIDTH))],
        out_specs=[tile, pl.BlockSpec((1, 1, LRU_WIDTH), lambda b, s: (b, 0, 0))],
        out_shape=[jax.ShapeDtypeStruct((batch * seq, LRU_WIDTH), BF16 if ts > SUBLANES else F32),
                   jax.ShapeDtypeStruct((batch, 1, LRU_WIDTH), F32)],
        scratch_shapes=[pltpu.VMEM((SUBLANES, LRU_WIDTH), F32),
                        pltpu.VMEM((1, LRU_WIDTH), F32)],
        compiler_params=_cparams(2),
        name="rglru_prompt" if ts > SUBLANES else "rglru_sample",
    )(xl, gl, conv0, h0, conv_w, conv_b, w_gates, b_gates, lam)


def _route(logits):
    lane = lax.broadcasted_iota(I32, logits.shape, 1).astype(F32)
    big = float(ROUTE_LANES)
    neg = -jnp.inf
    is_g = lane < N_GROUPS
    gmax = jnp.max(jnp.where(is_g, logits, neg), axis=-1, keepdims=True)
    gsum = jnp.sum(jnp.where(is_g, jnp.exp(logits - gmax), 0.0), axis=-1, keepdims=True)
    gw = 1.0 / gsum
    gid = jnp.min(jnp.where(is_g & (logits == gmax), lane, big), axis=-1, keepdims=True)
    lo = N_GROUPS + EXPERTS_PER_GROUP * gid
    in_grp = (lane >= lo) & (lane < lo + EXPERTS_PER_GROUP)
    v1 = jnp.max(jnp.where(in_grp, logits, neg), axis=-1, keepdims=True)
    i1 = jnp.min(jnp.where(in_grp & (logits == v1), lane, big), axis=-1, keepdims=True)
    rest = in_grp & (lane != i1)
    v2 = jnp.max(jnp.where(rest, logits, neg), axis=-1, keepdims=True)
    i2 = jnp.min(jnp.where(rest & (logits == v2), lane, big), axis=-1, keepdims=True)
    t = jnp.exp(v2 - v1)
    w1 = gw / (1.0 + t)
    w2 = gw * t / (1.0 + t)
    pair = (i1 - N_GROUPS) * N_EXPERTS + (i2 - N_GROUPS)
    return jnp.where(lane == REC_PAIR, pair,
                     jnp.where(lane == REC_W1, w1, jnp.where(lane == REC_W2, w2, 0.0)))


def _merge_kernel(xp_ref, xs_ref, op_ref, os_ref, yp_ref, ys_ref, wo_ref, g_ref, b_ref,
                  wr_ref, br_ref, x1_ref, x1t_ref, info_ref, *, n_prompt_tiles):
    i = pl.program_id(0)

    def body(x_ref, o_ref, y_ref):
        dot = functools.partial(jnp.dot, preferred_element_type=F32)
        m = dot(o_ref[...].astype(BF16), wo_ref[:ATTN_WIDTH, :])
        m = m + dot(y_ref[...].astype(BF16), wo_ref[ATTN_WIDTH:, :])
        x1 = _layer_norm(DEEPNORM_ALPHA * x_ref[...] + m, g_ref[...], b_ref[...])
        x1_ref[...] = x1
        _to_token_tiles(x1t_ref, x1)
        info_ref[...] = _route(dot(x1.astype(BF16), wr_ref[...]) + br_ref[...]).T[:SUBLANES]

    @pl.when(i < n_prompt_tiles)
    def _():
        body(xp_ref, op_ref, yp_ref)

    @pl.when(i >= n_prompt_tiles)
    def _():
        body(xs_ref, os_ref, ys_ref)


def _merge(xp, xs, o_p, o_s, y_p, y_s, wo, g1, b1, wr, br):
    tm = TOK_TM
    tp, ts = xp.shape[0], xs.shape[0]
    assert tp % tm == 0 and ts % tm == 0
    npt, nst = tp // tm, ts // tm
    t_all = tp + ts
    pidx = lambda i: (jnp.minimum(i, npt - 1), 0)
    sidx = lambda i: (jnp.maximum(i - npt, 0), 0)
    const = lambda shape: pl.BlockSpec(shape, lambda i: (0, 0))
    row = lambda i: (i, 0)
    return pl.pallas_call(
        functools.partial(_merge_kernel, n_prompt_tiles=npt),
        grid=(npt + nst,),
        in_specs=[pl.BlockSpec((tm, D_MODEL), pidx), pl.BlockSpec((tm, D_MODEL), sidx),
                  pl.BlockSpec((tm, ATTN_WIDTH), pidx), pl.BlockSpec((tm, ATTN_WIDTH), sidx),
                  pl.BlockSpec((tm, LRU_WIDTH), pidx), pl.BlockSpec((tm, LRU_WIDTH), sidx),
                  const((D_MODEL, D_MODEL)), const((1, D_MODEL)), const((1, D_MODEL)),
                  const((D_MODEL, ROUTE_LANES)), const((1, ROUTE_LANES))],
        out_specs=[pl.BlockSpec((tm, D_MODEL), row),
                   pl.BlockSpec((tm * TILE_ROWS, LANES), row),
                   pl.BlockSpec((SUBLANES, tm), lambda i: (0, i))],
        out_shape=[jax.ShapeDtypeStruct((t_all, D_MODEL), F32),
                   jax.ShapeDtypeStruct((t_all * TILE_ROWS, LANES), F32),
                   jax.ShapeDtypeStruct((SUBLANES, t_all), F32)],
        compiler_params=_cparams(1),
        name="merge_ln1_route",
    )(xp, xs, o_p, o_s, y_p, y_s, wo.astype(BF16), g1, b1, wr.astype(BF16), br)


def _tile_group(i, cum_ref):
    t = jnp.minimum(i, cum_ref[N_GROUPS - 1] - 1)
    g = jnp.int32(0)
    for k in range(N_GROUPS - 1):
        g = g + (cum_ref[k] <= t).astype(I32)
    return g


def _expert_kernel(cum_ref, src0_ref, gsrc_ref, sdst_ref,
                   x_hbm, info_ref, wg_ref, wu_ref, wd_ref, out_hbm,
                   xbuf, ybuf, xb_ref, infot_ref, yacc_ref, gsem, ssem):
    tm, ch, tr = MOE_TM, MOE_CHUNK, TILE_ROWS
    last_e = EXPERTS_PER_GROUP - 1
    i = pl.program_id(0)
    e = pl.program_id(1)
    slot = i % 2
    other = 1 - slot
    n_live = cum_ref[N_GROUPS - 1]

    def start_gather(idx_ref, rows, base, sl):
        for r in rows:
            src = pl.ds(pl.multiple_of(idx_ref[0, 0, r], tr), tr)
            dst = pl.ds(pl.multiple_of((base + r) * tr, tr), tr)
            pltpu.make_async_copy(x_hbm.at[src], xbuf.at[sl, dst],
                                  gsem.at[sl]).start(priority=r % 2)

    def start_scatter(idx_ref, rows, base, sl):
        for r in rows:
            src = pl.ds(pl.multiple_of((base + r) * tr, tr), tr)
            dst = pl.ds(pl.multiple_of(idx_ref[0, 0, r], tr), tr)
            pltpu.make_async_copy(ybuf.at[sl, src], out_hbm.at[dst],
                                  ssem.at[sl]).start(priority=r % 2)

    def wait_tile(buf, sem):
        pltpu.make_async_copy(x_hbm.at[pl.ds(0, tm * tr)], buf, sem).wait()

    def move_rows():
        start_gather(gsrc_ref, range(ch), e * ch, other)
        start_scatter(sdst_ref, range(ch), e * ch, other)

    @pl.when((i == 0) & (e == 0))
    def _():
        ybuf[1] = jnp.zeros(ybuf.shape[1:], F32)
        yacc_ref[...] = jnp.zeros(yacc_ref.shape, F32)
        start_gather(src0_ref, range(tm), 0, 0)

    @pl.when((e == 0) & (i <= n_live))
    def _():
        wait_tile(xbuf.at[slot], gsem.at[slot])

    @pl.when((e == last_e) & (i >= 1) & (i <= n_live))
    def _():
        wait_tile(ybuf.at[slot], ssem.at[slot])

    @pl.when(i == n_live)
    def _():
        move_rows()

    @pl.when((e == 0) & (i < n_live))
    def _():
        xb_ref[...] = _from_token_tiles(xbuf.at[slot], tm).astype(BF16)
        infot_ref[...] = info_ref[0].T

    @pl.when(i < n_live)
    def _():
        move_rows()
        w_in = jnp.concatenate([wg_ref[0], wu_ref[0]], axis=1).astype(BF16)
        gu = jnp.dot(xb_ref[...], w_in, preferred_element_type=F32)
        gate = gu[:, :D_EXPERT]
        up = gu[:, D_EXPERT:]
        expert = (_tile_group(i, cum_ref) * EXPERTS_PER_GROUP + e).astype(F32)
        info = infot_ref[...]
        pair = info[:, REC_PAIR:REC_PAIR + 1]
        first = jnp.floor(pair * (1.0 / N_EXPERTS))
        second = pair - first * N_EXPERTS
        comb = (jnp.where(first == expert, info[:, REC_W1:REC_W1 + 1], 0.0)
                + jnp.where(second == expert, info[:, REC_W2:REC_W2 + 1], 0.0))
        h = (gate * jax.nn.sigmoid(gate) * up * comb).astype(BF16)
        y = jnp.dot(h, wd_ref[0].astype(BF16), preferred_element_type=F32)
        yacc_ref[...] = y + jnp.where(e > 0, yacc_ref[...], 0.0)

    @pl.when((e == last_e) & (i < n_live))
    def _():
        _to_token_tiles(ybuf.at[slot], yacc_ref[...])

    @pl.when((e == last_e) & (i == n_live))
    def _():
        wait_tile(xbuf.at[other], gsem.at[other])
        wait_tile(ybuf.at[other], ssem.at[other])


def _experts(x1t, cum, src0, gsrc, sdst, info_sorted, wg, wu, wd, *, n_tiles, n_rows):
    tm, ch = MOE_TM, MOE_CHUNK
    ne = EXPERTS_PER_GROUP
    last = n_tiles - 1
    smem = lambda n, fn: pl.BlockSpec((1, 1, n), fn, memory_space=pltpu.SMEM)
    weight = lambda shape: pl.BlockSpec(
        shape, lambda i, e, cum: (_tile_group(i, cum) * ne + e, 0, 0))
    any_spec = pl.BlockSpec(memory_space=pl.ANY)
    grid_spec = pltpu.PrefetchScalarGridSpec(
        num_scalar_prefetch=1,
        grid=(n_tiles + 1, ne),
        in_specs=[smem(tm, lambda i, e, cum: (0, 0, 0)),
                  smem(ch, lambda i, e, cum: (jnp.minimum(i + 1, last) * ne + e, 0, 0)),
                  smem(ch, lambda i, e, cum: (i * ne + e, 0, 0)),
                  any_spec,
                  pl.BlockSpec((1, SUBLANES, tm), lambda i, e, cum: (jnp.minimum(i, last), 0, 0)),
                  weight((1, D_MODEL, D_EXPERT)), weight((1, D_MODEL, D_EXPERT)),
                  weight((1, D_EXPERT, D_MODEL))],
        out_specs=any_spec,
        scratch_shapes=[pltpu.VMEM((2, tm * TILE_ROWS, LANES), F32),
                        pltpu.VMEM((2, tm * TILE_ROWS, LANES), F32),
                        pltpu.VMEM((tm, D_MODEL), BF16),
                        pltpu.VMEM((tm, SUBLANES), F32),
                        pltpu.VMEM((tm, D_MODEL), F32),
                        pltpu.SemaphoreType.DMA((2,)),
                        pltpu.SemaphoreType.DMA((2,))])
    return pl.pallas_call(
        _expert_kernel,
        grid_spec=grid_spec,
        out_shape=jax.ShapeDtypeStruct((n_rows * TILE_ROWS, LANES), F32),
        compiler_params=_cparams(2),
        name="expert_mlp",
    )(cum, src0, gsrc, sdst, x1t, info_sorted, wg, wu, wd)


def _expert_plan(info, n_tiles):
    tm, ch = MOE_TM, MOE_CHUNK
    t_all = info.shape[1]
    gid = info[REC_PAIR].astype(I32) // (N_EXPERTS * EXPERTS_PER_GROUP)
    _, order = lax.sort((gid, lax.iota(I32, t_all)), num_keys=1, is_stable=True)
    groups = jnp.arange(N_GROUPS, dtype=I32)
    counts = jnp.sum((gid[:, None] == groups[None, :]).astype(I32), axis=0)
    start = jnp.cumsum(counts) - counts
    tiles_g = (counts + tm - 1) // tm
    cum = jnp.cumsum(tiles_g).astype(I32)
    n_live = cum[-1]
    ti = jnp.arange(n_tiles, dtype=I32)
    g_i = jnp.minimum(jnp.sum((cum[None, :] <= ti[:, None]).astype(I32), axis=1), N_GROUPS - 1)
    j = ti - (cum - tiles_g)[g_i]
    live = ti < n_live
    tile_rows = jnp.where(live, jnp.clip(counts[g_i] - j * tm, 0, tm), 0)
    tile_start = jnp.where(live, start[g_i] + j * tm, 0)
    r = jnp.arange(tm, dtype=I32)[None, :]
    valid = r < tile_rows[:, None]
    tok = order[jnp.minimum(tile_start[:, None] + r, t_all - 1)]
    src = (jnp.where(valid, tok, 0) * TILE_ROWS).astype(I32)
    dump = ((t_all + r) * TILE_ROWS).astype(I32)
    dst = (jnp.where(valid, tok, t_all + r) * TILE_ROWS).astype(I32)
    dst_prev = jnp.concatenate([dump, dst], axis=0)
    rec = [jnp.where(valid, info[c][tok], 0.0) for c in range(REC_FIELDS)]
    rec += [jnp.zeros((n_tiles, tm), F32)] * (SUBLANES - len(rec))
    info_sorted = jnp.stack(rec, axis=1)
    return (cum, src[0].reshape(1, 1, tm), src.reshape(n_tiles * EXPERTS_PER_GROUP, 1, ch),
            dst_prev.reshape((n_tiles + 1) * EXPERTS_PER_GROUP, 1, ch), info_sorted)


def _combine_kernel(x1_ref, f_ref, g_ref, b_ref, outp_ref, outs_ref, *, n_prompt_tiles):
    tm = TOK_TM
    i = pl.program_id(0)
    f = _from_token_tiles(f_ref, tm)
    out = _layer_norm(DEEPNORM_ALPHA * x1_ref[...] + f, g_ref[...], b_ref[...])

    @pl.when(i < n_prompt_tiles)
    def _():
        outp_ref[...] = out

    @pl.when(i >= n_prompt_tiles)
    def _():
        outs_ref[...] = out


def _combine(x1, f_tiles, g2, b2, *, n_prompt, n_sample):
    tm = TOK_TM
    assert n_sample == tm
    npt = n_prompt // tm
    nt = npt + 1
    row = lambda i: (i, 0)
    const = lambda shape: pl.BlockSpec(shape, lambda i: (0, 0))
    return pl.pallas_call(
        functools.partial(_combine_kernel, n_prompt_tiles=npt),
        grid=(nt,),
        in_specs=[pl.BlockSpec((tm, D_MODEL), row),
                  pl.BlockSpec((tm * TILE_ROWS, LANES), row),
                  const((1, D_MODEL)), const((1, D_MODEL))],
        out_specs=[pl.BlockSpec((tm, D_MODEL), lambda i: (jnp.minimum(i, npt - 1), 0)),
                   pl.BlockSpec((tm, D_MODEL), lambda i: (0, 0))],
        out_shape=[jax.ShapeDtypeStruct((n_prompt, D_MODEL), F32),
                   jax.ShapeDtypeStruct((n_sample, D_MODEL), F32)],
        compiler_params=_cparams(1),
        name="combine_ln2",
    )(x1, f_tiles, g2, b2)


def _block_diag(w):
    nb, n, _ = w.shape
    eye = jnp.eye(nb, dtype=w.dtype)
    return (eye[:, None, :, None] * w[:, :, None, :]).reshape(nb * n, nb * n)


def kernel(x_prompt, x_sample, cache_k, cache_v, state_h, state_conv, page_table, w_in, lambda_q1, lambda_k1, lambda_q2, lambda_k2, subln_g, conv_w, conv_b, w_rg_a, b_rg_a, w_rg_x, b_rg_x, lru_lambda, w_out, ln1_g, ln1_b, w_group_router, b_group_router, w_expert_router, b_expert_router, w_gate_e, w_up_e, w_down_e, ln2_g, ln2_b):
    bp, sp, _ = x_prompt.shape
    bd, sd, _ = x_sample.shape
    assert w_in.shape[0] == DEPTH == 1
    l = 0
    n_prompt, n_sample = bp * sp, bd * sd

    xp =x_prompt.reshape(n_prompt, D_MODEL)
    xs = x_sample.reshape(n_sample, D_MODEL)
    w_in_b = w_in[l].astype(BF16)
    lam_params = (lambda_q1[l][None], lambda_k1[l][None], lambda_q2[l][None], lambda_k2[l][None])
    w_gates = jnp.concatenate([_block_diag(w_rg_a[l]), _block_diag(w_rg_x[l])], axis=1).astype(BF16)
    b_gates = jnp.concatenate([b_rg_a[l].reshape(1, -1), b_rg_x[l].reshape(1, -1)], axis=1)
    lru_args = (conv_w[l], conv_b[l][None], w_gates, b_gates, lru_lambda[l][None])

    qt, kh, vt, k_p, v_p, y_p, h_p, xl_tail = _inproj(xp, w_in_b, lru_args, prompt=True,
                                                      tm=INPROJ_TM, seq=sp)
    o_p = _attn_prompt(qt, kh, vt, lam_params, subln_g[l][:, None], batch=bp, seq=sp)

    q_s, k_s, v_s, xl_s, gl_s = _inproj(xs, w_in_b, prompt=False, tm=n_sample)
    n_phys = cache_k.shape[1]
    ck = cache_k.reshape(DEPTH * n_phys * PAGE_SIZE * N_HEADS, V_DIM)
    cv = cache_v.reshape(DEPTH * n_phys * PAGE_SIZE * N_HEADS, V_DIM)
    o_s = _attn_sample(q_s, k_s, v_s, ck, cv, page_table, lam_params, subln_g[l][None],
                       dec_batch=bd, dec_seq=sd)
    conv0 = jnp.concatenate(
        [jnp.zeros((bd, SUBLANES - (CONV_W - 1), LRU_WIDTH), F32), state_conv[l]], axis=1)
    y_s, h_s = _rglru(xl_s, gl_s, conv0, state_h[l][:, None, :], *lru_args,
                      batch=bd, seq=sd, ts=sd)

    w_router = jnp.concatenate(
        [w_group_router[l], w_expert_router[l],
         jnp.zeros((D_MODEL, ROUTE_LANES - N_GROUPS - N_EXPERTS), F32)], axis=1)
    b_router = jnp.concatenate(
        [b_group_router[l], b_expert_router[l],
         jnp.zeros((ROUTE_LANES - N_GROUPS - N_EXPERTS,), F32)])[None]
    pad = (-n_sample) % TOK_TM
    pad_rows = lambda a: jnp.pad(a, ((0, pad), (0, 0)))
    x1, x1t, info = _merge(xp, pad_rows(xs), o_p, pad_rows(o_s), y_p, pad_rows(y_s), w_out[l],
                           ln1_g[l][None], ln1_b[l][None], w_router, b_router)
    t_all = n_prompt + n_sample + pad
    n_tiles = -(-t_all // MOE_TM) + N_GROUPS
    cum, src0, gsrc, sdst, info_sorted = _expert_plan(info, n_tiles)
    f_tiles = _experts(x1t, cum, src0, gsrc, sdst, info_sorted,
                       w_gate_e[l], w_up_e[l], w_down_e[l],
                       n_tiles=n_tiles, n_rows=t_all + MOE_TM)
    out_p, out_s = _combine(x1, f_tiles, ln2_g[l][None], ln2_b[l][None],
                            n_prompt=n_prompt, n_sample=n_sample + pad)
    out_s = out_s[:n_sample]

    conv_p = xl_tail[:, SUBLANES - (CONV_W - 1):]
    conv_s = jnp.concatenate([state_conv[l], xl_s.reshape(bd, sd, LRU_WIDTH)],
                             axis=1)[:, -(CONV_W - 1):]
    return (out_p.reshape(bp, sp, D_MODEL),
            out_s.reshape(bd, sd, D_MODEL),
            k_p.reshape(1, bp, sp, N_HEADS, V_DIM),
            v_p.reshape(1, bp, sp, N_HEADS, V_DIM),
            h_p.reshape(1, bp, LRU_WIDTH),
            conv_p[None],
            k_s.reshape(1, bd, sd, N_HEADS, V_DIM),
            v_s.reshape(1, bd, sd, N_HEADS, V_DIM),
            h_s.reshape(1, bd, LRU_WIDTH),
            conv_s[None])
```

```python
import functools
import math

import jax
import jax.numpy as jnp
from jax import lax
from jax.experimental import pallas as pl
from jax.experimental.pallas import tpu as pltpu

F32 = jnp.float32
BF16 = jnp.bfloat16
I32 = jnp.int32

D_MODEL = 1024
ATTN_WIDTH = 512
LRU_WIDTH = 512
N_HEADS = 4
V_DIM = 128
QK_DIM = 64
CONV_W = 4
LRU_C = 8.0
IN_WIDTH = 3 * ATTN_WIDTH + 2 * LRU_WIDTH
N_GROUPS = 4
EXPERTS_PER_GROUP = 8
N_EXPERTS = 32
D_EXPERT = 256
PAGE_SIZE = 128
LN_EPS = 1e-5
DEPTH = 1
DEEPNORM_ALPHA = (2 * DEPTH) ** 0.25
LAM_INIT = 0.8 - 0.6 * math.exp(-0.3 * 0)
QK_SCALE = QK_DIM ** -0.5 * math.log2(math.e)

SUBLANES = 8
LANES = 128
VMEM_LIMIT_BYTES = 48 * 1024 * 1024

INPROJ_TM = 512
ATTN_TQ = 512
ATTN_TK = 512
ATTN_CB = 256
ATTN_HEADS_PER_STEP = 2
TOK_TM = 512
MOE_TM = 1024
MOE_CHUNK = MOE_TM // EXPERTS_PER_GROUP
PAGES_PER_STEP = 16
ROUTE_LANES = 128
REC_PAIR, REC_W1, REC_W2 = 0, 1, 2
REC_FIELDS = 3
TILE_ROWS = D_MODEL // LANES


def _cparams(n_axes):
    return pltpu.CompilerParams(
        dimension_semantics=("arbitrary",) * n_axes,
        vmem_limit_bytes=VMEM_LIMIT_BYTES)


def _layer_norm(x, g, b):
    mu = jnp.mean(x, axis=-1, keepdims=True)
    xc = x - mu
    var = jnp.mean(xc * xc, axis=-1, keepdims=True)
    return xc * lax.rsqrt(var + LN_EPS) * g + b


def _diff_lambda(lq1, lk1, lq2, lk2):
    s1 = jnp.sum(lq1 * lk1, axis=-1, keepdims=True)
    s2 = jnp.sum(lq2 * lk2, axis=-1, keepdims=True)
    return jnp.exp(s1) - jnp.exp(s2) + LAM_INIT


def _to_token_tiles(ref, x):
    tm = x.shape[0]
    for s in range(TILE_ROWS):
        ref[pl.ds(s, tm, stride=TILE_ROWS), :] = x[:, s * LANES:(s + 1) * LANES]


def _from_token_tiles(ref, tm):
    return jnp.concatenate(
        [ref[pl.ds(s, tm, stride=TILE_ROWS), :] for s in range(TILE_ROWS)], axis=1)


def _inproj_kernel(x_ref, w_ref, *refs, prompt, tiles_per_seq):
    x = x_ref[...].astype(BF16)
    tm = x.shape[0]

    def section(idx, width):
        return jnp.dot(x, w_ref[:, idx:idx + width], preferred_element_type=F32)

    heads = [slice(h * V_DIM, (h + 1) * V_DIM) for h in range(N_HEADS)]

    def to_head_rows(ref, val):
        for h, sl in enumerate(heads):
            ref[pl.ds(h, tm, stride=N_HEADS), :] = val[:, sl]

    if not prompt:
        q_ref, k_ref, v_ref, xl_ref, gl_ref = refs
        q_ref[...] = section(0, ATTN_WIDTH) * QK_SCALE
        to_head_rows(k_ref, section(ATTN_WIDTH, ATTN_WIDTH))
        to_head_rows(v_ref, section(2 * ATTN_WIDTH, ATTN_WIDTH))
        xl_ref[...] = section(3 * ATTN_WIDTH, LRU_WIDTH)
        gl_ref[...] = section(3 * ATTN_WIDTH + LRU_WIDTH, LRU_WIDTH)
        return

    (cw_ref, cb_ref, wgate_ref, bg_ref, lam_ref, qt_ref, kh_ref, vt_ref, k_ref, v_ref,
     y_ref, hlast_ref, xtail_ref, tail_ref, hc_ref) = refs

    @pl.when(pl.program_id(0) % tiles_per_seq == 0)
    def _():
        tail_ref[...] = jnp.zeros(tail_ref.shape, F32)
        hc_ref[...] = jnp.zeros(hc_ref.shape, F32)

    def emit_q():
        q = section(0, ATTN_WIDTH) * QK_SCALE
        chan = lax.broadcasted_iota(I32, (V_DIM, tm), 0)
        for h, sl in enumerate(heads):
            qt = q[:, sl].T
            qt_ref[h, 0] = jnp.where(chan < QK_DIM, qt, 0.0).astype(BF16)
            qt_ref[h, 1] = jnp.where(chan >= QK_DIM, qt, 0.0).astype(BF16)

    def emit_k():
        k = section(ATTN_WIDTH, ATTN_WIDTH)
        for h, sl in enumerate(heads):
            kh_ref[h] = k[:, sl].astype(BF16)
        to_head_rows(k_ref, k)

    def emit_v():
        v = section(2 * ATTN_WIDTH, ATTN_WIDTH)
        for h, sl in enumerate(heads):
            vt_ref[h, 0] = v[:, sl].T.astype(BF16)
        to_head_rows(v_ref, v)

    xl = section(3 * ATTN_WIDTH, LRU_WIDTH)
    gl = section(3 * ATTN_WIDTH + LRU_WIDTH, LRU_WIDTH)
    xtail_ref[0] = xl[tm - SUBLANES:]
    chunks = _rglru_chunks(xl, gl, y_ref, tail_ref, hc_ref, cw_ref, cb_ref, wgate_ref, bg_ref,
                           lam_ref)
    h_ends = [next(chunks)]
    for emit in (emit_q, emit_k, emit_v):
        emit()
        h_ends.append(next(chunks))
    hlast_ref[0] = jnp.concatenate(h_ends, axis=1)


def _inproj(x2d, w_bf16, lru_args=None, *, prompt, tm, seq=None):
    t = x2d.shape[0]
    assert t % tm == 0
    row = lambda i: (i, 0)
    const = lambda shape: pl.BlockSpec(shape, lambda i: (0,) * len(shape))
    spec512 = pl.BlockSpec((tm, ATTN_WIDTH), row)
    head_rows = pl.BlockSpec((tm * N_HEADS, V_DIM), row)
    f32_512 = jax.ShapeDtypeStruct((t, ATTN_WIDTH), F32)
    f32_heads = jax.ShapeDtypeStruct((t * N_HEADS, V_DIM), F32)
    in_specs = [pl.BlockSpec((tm, D_MODEL), row), const((D_MODEL, IN_WIDTH))]
    operands = [x2d, w_bf16]
    scratch = []
    tiles_per_seq = None
    if prompt:
        assert tm == ATTN_TK and seq % tm == 0
        tiles_per_seq = seq // tm
        batch = t // seq
        per_seq = lambda n: pl.BlockSpec((1, n, LRU_WIDTH), lambda i: (i // tiles_per_seq, 0, 0))
        in_specs += [const((CONV_W, LRU_WIDTH)), const((1, LRU_WIDTH)),
                     const((LRU_WIDTH, 2 * LRU_WIDTH)), const((1, 2 * LRU_WIDTH)),
                     const((1, LRU_WIDTH))]
        operands += list(lru_args)
        out_shape = [jax.ShapeDtypeStruct((N_HEADS, 2, V_DIM, t), BF16),
                     jax.ShapeDtypeStruct((N_HEADS, t, V_DIM), BF16),
                     jax.ShapeDtypeStruct((N_HEADS, t // tm, V_DIM, tm), BF16),
                     f32_heads, f32_heads,
                     jax.ShapeDtypeStruct((t, LRU_WIDTH), BF16),
                     jax.ShapeDtypeStruct((batch, 1, LRU_WIDTH), F32),
                     jax.ShapeDtypeStruct((batch, SUBLANES, LRU_WIDTH), F32)]
        out_specs = [pl.BlockSpec((N_HEADS, 2, V_DIM, tm), lambda i: (0, 0, 0, i)),
                     pl.BlockSpec((N_HEADS, tm, V_DIM), lambda i: (0, i, 0)),
                     pl.BlockSpec((N_HEADS, 1, V_DIM, tm), lambda i: (0, i, 0, 0)),
                     head_rows, head_rows, spec512, per_seq(1), per_seq(SUBLANES)]
        scratch = [pltpu.VMEM((SUBLANES, LRU_WIDTH), F32), pltpu.VMEM((1, LRU_WIDTH), F32)]
    else:
        out_shape = [f32_512, f32_heads, f32_heads, f32_512, f32_512]
        out_specs = [spec512, head_rows, head_rows, spec512, spec512]
    return pl.pallas_call(
        functools.partial(_inproj_kernel, prompt=prompt, tiles_per_seq=tiles_per_seq),
        grid=(t // tm,),
        in_specs=in_specs,
        out_specs=out_specs,
        out_shape=out_shape,
        scratch_shapes=scratch,
        compiler_params=_cparams(1),
        name="inproj_rglru_prompt" if prompt else "inproj_sample",
    )(*operands)


def _attn_prompt_kernel(qt_ref, k_ref, vt_ref, lq1_ref, lk1_ref, lq2_ref, lk2_ref, g_ref,
                        o_ref, s_ref, p_ref, a_ref, m_ref, l_ref, acc_ref):
    tq, tk, cb = ATTN_TQ, ATTN_TK, ATTN_CB
    heads = range(ATTN_HEADS_PER_STEP)
    n_cols = 2 * tq
    i = pl.program_id(2)

    m_ref[...] = jnp.full(m_ref.shape, -jnp.inf, F32)
    l_ref[...] = jnp.zeros(l_ref.shape, F32)
    acc_ref[...] = jnp.zeros(acc_ref.shape, F32)
    for hd in heads:
        p_ref[hd, 1] = jnp.zeros(p_ref.shape[2:], BF16)
        a_ref[hd, 1] = jnp.ones(a_ref.shape[2:], F32)

    def scores(j, slot):
        for hd in heads:
            k = k_ref[hd, pl.ds(pl.multiple_of(j * tk, tk), tk), :]
            for mp in range(2):
                s_ref[hd, slot, :, mp * tq:(mp + 1) * tq] = jnp.dot(
                    k, qt_ref[hd, mp], preferred_element_type=F32)

    def softmax(slot, masked):
        for hd in heads:
            for c in range(n_cols // cb):
                off = (c * cb) % tq
                cs = slice(c * cb, (c + 1) * cb)
                s = s_ref[hd, slot, :, cs]
                if masked:
                    key = lax.broadcasted_iota(I32, s.shape, 0)
                    qpos = lax.broadcasted_iota(I32, s.shape, 1) + off
                    s = jnp.where(key <= qpos, s, -jnp.inf)
                m_prev = m_ref[hd, :, cs]
                m_new = jnp.maximum(m_prev, jnp.max(s, axis=0, keepdims=True))
                alpha = jnp.exp2(m_prev - m_new)
                p = jnp.exp2(s - m_new)
                l_ref[hd, :, cs] = alpha * l_ref[hd, :, cs] + jnp.sum(p, axis=0, keepdims=True)
                m_ref[hd, :, cs] = m_new
                p_ref[hd, slot, :, cs] = p.astype(BF16)
                a_ref[hd, slot, :, cs] = alpha

    def accumulate(j, slot):
        for hd in heads:
            acc_ref[hd] = a_ref[hd, slot] * acc_ref[hd] + jnp.dot(
                vt_ref[hd, j], p_ref[hd, slot], preferred_element_type=F32)

    def stage(j, cur):
        scores(j + 1, 1 - cur)
        softmax(cur, masked=False)
        accumulate(jnp.maximum(j - 1, 0), 1 - cur)

    def diagonal(cur):
        accumulate(jnp.maximum(i - 1, 0), 1 - cur)
        softmax(cur, masked=True)
        accumulate(i, cur)

    scores(0, 0)

    def body(t, carry):
        stage(2 * t, 0)
        stage(2 * t + 1, 1)
        return carry

    lax.fori_loop(0, i // 2, body, 0)

    @pl.when(i % 2 == 0)
    def _():
        diagonal(0)

    @pl.when(i % 2 == 1)
    def _():
        stage(i - 1, 0)
        diagonal(1)

    lam = _diff_lambda(lq1_ref[...], lk1_ref[...], lq2_ref[...], lk2_ref[...])
    for hd in heads:
        o_all = acc_ref[hd] / l_ref[hd]
        o = o_all[:, :tq] - lam * o_all[:, tq:]
        ms = jnp.mean(o * o, axis=0, keepdims=True)
        o = o * lax.rsqrt(ms + LN_EPS) * g_ref[...] * (1.0 - LAM_INIT)
        o_ref[:, hd * V_DIM:(hd + 1) * V_DIM] = o.T.astype(o_ref.dtype)


def _attn_prompt(qt, kh, vt, lam_params, subln_g_col, *, batch, seq):
    assert ATTN_TQ == ATTN_TK and seq % ATTN_TQ == 0 and ATTN_TQ % ATTN_CB == 0
    hp = ATTN_HEADS_PER_STEP
    assert N_HEADS % hp == 0
    nq = seq // ATTN_TQ
    t = batch * seq
    vec = lambda n: pl.BlockSpec((1, n), lambda b, h, i: (0, 0))
    return pl.pallas_call(
        _attn_prompt_kernel,
        grid=(batch, N_HEADS // hp, nq),
        in_specs=[pl.BlockSpec((hp, 2, V_DIM, ATTN_TQ), lambda b, h, i: (h, 0, 0, b * nq + i)),
                  pl.BlockSpec((hp, seq, V_DIM), lambda b, h, i: (h, b, 0)),
                  pl.BlockSpec((hp, nq, V_DIM, ATTN_TK), lambda b, h, i: (h, b, 0, 0)),
                  vec(QK_DIM), vec(QK_DIM), vec(QK_DIM), vec(QK_DIM),
                  pl.BlockSpec((V_DIM, 1), lambda b, h, i: (0, 0))],
        out_specs=pl.BlockSpec((ATTN_TQ, hp * V_DIM), lambda b, h, i: (b * nq + i, h)),
        out_shape=jax.ShapeDtypeStruct((t, ATTN_WIDTH), BF16),
        scratch_shapes=[pltpu.VMEM((hp, 2, ATTN_TK, 2 * ATTN_TQ), F32),
                        pltpu.VMEM((hp, 2, ATTN_TK, 2 * ATTN_TQ), BF16),
                        pltpu.VMEM((hp, 2, 1, 2 * ATTN_TQ), F32),
                        pltpu.VMEM((hp, 1, 2 * ATTN_TQ), F32),
                        pltpu.VMEM((hp, 1, 2 * ATTN_TQ), F32),
                        pltpu.VMEM((hp, V_DIM, 2 * ATTN_TQ), F32)],
        compiler_params=_cparams(3),
        name="attn_prompt",
    )(qt, kh, vt, *lam_params, subln_g_col)


def _attn_sample_kernel(pt_ref, q0_ref, q1_ref, kn0_ref, kn1_ref, vn0_ref, vn1_ref,
                        ck_hbm, cv_hbm, lq1_ref, lk1_ref, lq2_ref, lk2_ref, g_ref,
                        o0_ref, o1_ref, kbuf, vbuf, ksem, vsem, bias_ref, m_ref, l_ref, acc_ref,
                        *, n_chunks, n_steps, dec_seq):
    gpp = PAGES_PER_STEP
    page_rows = PAGE_SIZE * N_HEADS
    hrows = 2 * dec_seq
    g = pl.program_id(0)
    c = g % n_chunks
    slot = g % 2
    streams = ((q0_ref, kn0_ref, vn0_ref, o0_ref), (q1_ref, kn1_ref, vn1_ref, o1_ref))

    def page_copies(step, sl):
        copies = []
        for s in range(len(streams)):
            buf = 2 * s + sl
            for p in range(gpp):
                page = pt_ref[(s * n_steps + step) * gpp + p]
                src = pl.ds(pl.multiple_of(page * page_rows, page_rows), page_rows)
                dst = pl.ds(p * page_rows, page_rows)
                copies.append(pltpu.make_async_copy(ck_hbm.at[src], kbuf.at[buf, dst],
                                                    ksem.at[buf]))
                copies.append(pltpu.make_async_copy(cv_hbm.at[src], vbuf.at[buf, dst],
                                                    vsem.at[buf]))
        return copies

    @pl.when(g == 0)
    def _():
        for cp in page_copies(0, 0):
            cp.start()

    @pl.when(g + 1 < n_steps)
    def _():
        for cp in page_copies(g + 1, 1 - slot):
            cp.start()

    def head_bias(shape):
        row_head = lax.broadcasted_iota(I32, shape, 0) // hrows
        col_head = lax.broadcasted_iota(I32, shape, 1) % N_HEADS
        return row_head == col_head

    @pl.when(g == 0)
    def _():
        bias_ref[...] = jnp.where(head_bias(bias_ref.shape), 0.0, -jnp.inf)

    @pl.when(c == 0)
    def _():
        m_ref[...] = jnp.full(m_ref.shape, -jnp.inf, F32)
        l_ref[...] = jnp.zeros(l_ref.shape, F32)
        acc_ref[...] = jnp.zeros(acc_ref.shape, F32)

    lane = lax.broadcasted_iota(I32, (dec_seq, V_DIM), 1)

    def query_block(q_ref):
        q = q_ref[...]
        q_rows = []
        for h in range(N_HEADS):
            qh = q[:, h * V_DIM:(h + 1) * V_DIM]
            q_rows += [jnp.where(lane < QK_DIM, qh, 0.0), jnp.where(lane >= QK_DIM, qh, 0.0)]
        return jnp.concatenate(q_rows, axis=0).astype(BF16)

    def update(s_idx, qblk, k, v, bias):
        s = lax.dot_general(qblk, k, (((1,), (1,)), ((), ())),
                            preferred_element_type=F32) + bias
        m_prev = m_ref[s_idx]
        m_new = jnp.maximum(m_prev, jnp.max(s, axis=-1, keepdims=True))
        alpha = jnp.exp2(m_prev - m_new)
        p = jnp.exp2(s - m_new)
        l_ref[s_idx] = alpha * l_ref[s_idx] + jnp.sum(p, axis=-1, keepdims=True)
        acc_ref[s_idx] = alpha * acc_ref[s_idx] + jnp.dot(p.astype(BF16), v,
                                                          preferred_element_type=F32)
        m_ref[s_idx] = m_new

    for cp in page_copies(g, slot):
        cp.wait()
    qblks = [query_block(refs[0]) for refs in streams]
    for s_idx in range(len(streams)):
        buf = 2 * s_idx + slot
        update(s_idx, qblks[s_idx], kbuf[buf].astype(BF16), vbuf[buf].astype(BF16),
               bias_ref[...])

    @pl.when(c == n_chunks - 1)
    def _():
        new_rows = dec_seq * N_HEADS
        pad = jnp.zeros((LANES - new_rows, V_DIM), F32)
        shape = (N_HEADS * hrows, LANES)
        qi = lax.broadcasted_iota(I32, shape, 0) % dec_seq
        kj = lax.broadcasted_iota(I32, shape, 1) // N_HEADS
        new_bias = jnp.where(head_bias(shape) & (kj <= qi), 0.0, -jnp.inf)
        lam = _diff_lambda(lq1_ref[...], lk1_ref[...], lq2_ref[...], lk2_ref[...])
        for s_idx, (_, kn_ref, vn_ref, o_ref) in enumerate(streams):
            kn = jnp.concatenate([kn_ref[...], pad], axis=0).astype(BF16)
            vn = jnp.concatenate([vn_ref[...], pad], axis=0).astype(BF16)
            update(s_idx, qblks[s_idx], kn, vn, new_bias)
            o_heads = acc_ref[s_idx] / l_ref[s_idx]
            for h in range(N_HEADS):
                o_all = o_heads[h * hrows:(h + 1) * hrows]
                o = o_all[:dec_seq] - lam * o_all[dec_seq:]
                ms = jnp.mean(o * o, axis=-1, keepdims=True)
                o_ref[:, h * V_DIM:(h + 1) * V_DIM] = (
                    o * lax.rsqrt(ms + LN_EPS) * g_ref[...] * (1.0 - LAM_INIT))


def _attn_sample(q, k_new, v_new, cache_k_rows, cache_v_rows, page_table, lam_params, subln_g,
                 *, dec_batch, dec_seq):
    n_pages = page_table.shape[1]
    assert n_pages % PAGES_PER_STEP == 0 and dec_seq % SUBLANES == 0
    n_chunks = n_pages // PAGES_PER_STEP
    n_streams = 2
    assert dec_batch % n_streams == 0
    half = dec_batch // n_streams
    n_steps = half * n_chunks
    rows = 2 * N_HEADS * dec_seq
    chunk_rows = PAGES_PER_STEP * PAGE_SIZE * N_HEADS

    def stream_specs(s):
        tok = pl.BlockSpec((dec_seq, ATTN_WIDTH), lambda g, pt: (s * half + g // n_chunks, 0))
        tok_heads = pl.BlockSpec((dec_seq * N_HEADS, V_DIM),
                                 lambda g, pt: (s * half + g // n_chunks, 0))
        return tok, tok_heads

    (tok0, heads0), (tok1, heads1) = stream_specs(0), stream_specs(1)
    out_spec = pl.BlockSpec((dec_seq, ATTN_WIDTH), lambda g, pt: (g // n_chunks, 0))
    vec = lambda n: pl.BlockSpec((1, n), lambda g, pt: (0, 0))
    any_spec = pl.BlockSpec(memory_space=pl.ANY)
    n_bufs = 2 * n_streams
    grid_spec = pltpu.PrefetchScalarGridSpec(
        num_scalar_prefetch=1,
        grid=(n_steps,),
        in_specs=[tok0, tok1, heads0, heads1, heads0, heads1, any_spec, any_spec,
                  vec(QK_DIM), vec(QK_DIM), vec(QK_DIM), vec(QK_DIM), vec(V_DIM)],
        out_specs=[out_spec, out_spec],
        scratch_shapes=[pltpu.VMEM((n_bufs, chunk_rows, V_DIM), F32),
                        pltpu.VMEM((n_bufs, chunk_rows, V_DIM), F32),
                        pltpu.SemaphoreType.DMA((n_bufs,)),
                        pltpu.SemaphoreType.DMA((n_bufs,)),
                        pltpu.VMEM((rows, chunk_rows), F32),
                        pltpu.VMEM((n_streams, rows, 1), F32),
                        pltpu.VMEM((n_streams, rows, 1), F32),
                        pltpu.VMEM((n_streams, rows, V_DIM), F32)])
    half_out = jax.ShapeDtypeStruct((half * dec_seq, ATTN_WIDTH), F32)
    o0, o1 = pl.pallas_call(
        functools.partial(_attn_sample_kernel, n_chunks=n_chunks, n_steps=n_steps,
                          dec_seq=dec_seq),
        grid_spec=grid_spec,
        out_shape=[half_out, half_out],
        compiler_params=_cparams(1),
        name="attn_sample",
    )(page_table.reshape(-1), q, q, k_new, k_new, v_new, v_new, cache_k_rows, cache_v_rows,
      *lam_params, subln_g)
    return jnp.concatenate([o0, o1], axis=0)


def _softplus(x):
    y = jnp.exp(-jnp.abs(x))
    w = 1.0 + y
    log1p = jnp.where(w == 1.0, y, jnp.log(w) * (y / (w - 1.0)))
    return jnp.maximum(x, 0.0) + log1p


def _rglru_tile(*args):
    return jnp.concatenate(list(_rglru_chunks(*args)), axis=1)


def _rglru_chunks(x, gl, y_ref, tail_ref, hc_ref, cw_ref, cb_ref, wg_ref, bg_ref, lam_ref):
    for c in range(LRU_WIDTH // LANES):
        cs = slice(c * LANES, (c + 1) * LANES)
        gate_w = jnp.concatenate(
            [wg_ref[cs, cs], wg_ref[cs, LRU_WIDTH + c * LANES:LRU_WIDTH + (c + 1) * LANES]], axis=1)
        gate_b = jnp.concatenate(
            [bg_ref[:, cs], bg_ref[:, LRU_WIDTH + c * LANES:LRU_WIDTH + (c + 1) * LANES]], axis=1)
        xs = x[:, cs]
        y, h_end = _rglru_chunk(xs, gl[:, cs], tail_ref[:, cs], hc_ref[:, cs],
                                cw_ref[:, cs], cb_ref[:, cs], gate_w, gate_b, lam_ref[:, cs])
        tail_ref[:, cs] = xs[xs.shape[0] - SUBLANES:]
        hc_ref[:, cs] = h_end
        y_ref[:, cs] = y.astype(y_ref.dtype)
        yield h_end


def _rglru_chunk(x, gl, tail, h_in, cw, cb, gate_w, gate_b, lam):
    ts, n = x.shape
    xfull = jnp.concatenate([tail, x], axis=0)
    xc = cb + cw[CONV_W - 1:CONV_W] * x
    for d in range(1, CONV_W):
        shifted = pltpu.roll(xfull, d, 0)[SUBLANES:]
        xc = xc + cw[CONV_W - 1 - d:CONV_W - d] * shifted

    gates = jnp.dot(xc.astype(BF16), gate_w, preferred_element_type=F32) + gate_b
    r = jax.nn.sigmoid(gates[:, :n])
    ig = jax.nn.sigmoid(gates[:, n:])
    log_a = -LRU_C * r * _softplus(-lam)
    a = jnp.exp(log_a)
    mult = jnp.sqrt(jnp.maximum(-(a * a + 1.0) * jnp.tanh(log_a), 0.0))
    u = mult * ig * xc

    row = lax.broadcasted_iota(I32, a.shape, 0)
    d = 1
    while d < ts:
        if d < SUBLANES:
            keep = row >= d
            a_prev = jnp.where(keep, pltpu.roll(a, d, 0), 1.0)
            u_prev = jnp.where(keep, pltpu.roll(u, d, 0), 0.0)
            u = a * u_prev + u
            a = a * a_prev
        else:
            u = jnp.concatenate([u[:d], a[d:] * u[:ts - d] + u[d:]], axis=0)
            a = jnp.concatenate([a[:d], a[d:] * a[:ts - d]], axis=0)
        d *= 2
    h = a * h_in + u
    return h * jax.nn.gelu(gl, approximate=True), h[ts - 1:ts]


def _rglru_kernel(xl_ref, gl_ref, conv0_ref, h0_ref, cw_ref, cb_ref, wg_ref, bg_ref, lam_ref,
                  y_ref, hlast_ref, tail_ref, hc_ref):
    s = pl.program_id(1)

    @pl.when(s == 0)
    def _():
        tail_ref[...] = conv0_ref[0]
        hc_ref[...] = h0_ref[0]

    hlast_ref[0] = _rglru_tile(xl_ref[...], gl_ref[...], y_ref, tail_ref, hc_ref,
                               cw_ref, cb_ref, wg_ref, bg_ref, lam_ref)


def _rglru(xl, gl, conv0, h0, conv_w, conv_b, w_gates, b_gates, lam, *, batch, seq, ts):
    assert seq % ts == 0 and ts % SUBLANES == 0
    ns = seq // ts
    tile = pl.BlockSpec((ts, LRU_WIDTH), lambda b, s: (b * ns + s, 0))
    const = lambda shape: pl.BlockSpec(shape, lambda b, s: (0,) * len(shape))
    return pl.pallas_call(
        _rglru_kernel,
        grid=(batch, ns),
        in_specs=[tile, tile,
                  pl.BlockSpec((1, SUBLANES, LRU_WIDTH), lambda b, s: (b, 0, 0)),
                  pl.BlockSpec((1, 1, LRU_WIDTH), lambda b, s: (b, 0, 0)),
                  const((CONV_W, LRU_WIDTH)), const((1, LRU_WIDTH)),
                  const((LRU_WIDTH, 2 * LRU_WIDTH)), const((1, 2 * LRU_WIDTH)),
                  const((1, LRU_WIDTH))],
        out_specs=[tile, pl.BlockSpec((1, 1, LRU_WIDTH), lambda b, s: (b, 0, 0))],
        out_shape=[jax.ShapeDtypeStruct((batch * seq, LRU_WIDTH), BF16 if ts > SUBLANES else F32),
                   jax.ShapeDtypeStruct((batch, 1, LRU_WIDTH), F32)],
        scratch_shapes=[pltpu.VMEM((SUBLANES, LRU_WIDTH), F32),
                        pltpu.VMEM((1, LRU_WIDTH), F32)],
        compiler_params=_cparams(2),
        name="rglru_prompt" if ts > SUBLANES else "rglru_sample",
    )(xl, gl, conv0, h0, conv_w, conv_b, w_gates, b_gates, lam)


def _route(logits):
    lane = lax.broadcasted_iota(I32, logits.shape, 1).astype(F32)
    big = float(ROUTE_LANES)
    neg = -jnp.inf
    is_g = lane < N_GROUPS
    gmax = jnp.max(jnp.where(is_g, logits, neg), axis=-1, keepdims=True)
    gsum = jnp.sum(jnp.where(is_g, jnp.exp(logits - gmax), 0.0), axis=-1, keepdims=True)
    gw = 1.0 / gsum
    gid = jnp.min(jnp.where(is_g & (logits == gmax), lane, big), axis=-1, keepdims=True)
    lo = N_GROUPS + EXPERTS_PER_GROUP * gid
    in_grp = (lane >= lo) & (lane < lo + EXPERTS_PER_GROUP)
    v1 = jnp.max(jnp.where(in_grp, logits, neg), axis=-1, keepdims=True)
    i1 = jnp.min(jnp.where(in_grp & (logits == v1), lane, big), axis=-1, keepdims=True)
    rest = in_grp & (lane != i1)
    v2 = jnp.max(jnp.where(rest, logits, neg), axis=-1, keepdims=True)
    i2 = jnp.min(jnp.where(rest & (logits == v2), lane, big), axis=-1, keepdims=True)
    t = jnp.exp(v2 - v1)
    w1 = gw / (1.0 + t)
    w2 = gw * t / (1.0 + t)
    pair = (i1 - N_GROUPS) * N_EXPERTS + (i2 - N_GROUPS)
    return jnp.where(lane == REC_PAIR, pair,
                     jnp.where(lane == REC_W1, w1, jnp.where(lane == REC_W2, w2, 0.0)))


def _merge_kernel(xp_ref, xs_ref, op_ref, os_ref, yp_ref, ys_ref, wo_ref, g_ref, b_ref,
                  wr_ref, br_ref, x1_ref, x1t_ref, info_ref, *, n_prompt_tiles):
    i = pl.program_id(0)

    def body(x_ref, o_ref, y_ref):
        dot = functools.partial(jnp.dot, preferred_element_type=F32)
        m = dot(o_ref[...].astype(BF16), wo_ref[:ATTN_WIDTH, :])
        m = m + dot(y_ref[...].astype(BF16), wo_ref[ATTN_WIDTH:, :])
        x1 = _layer_norm(DEEPNORM_ALPHA * x_ref[...] + m, g_ref[...], b_ref[...])
        x1_ref[...] = x1
        _to_token_tiles(x1t_ref, x1)
        info_ref[...] = _route(dot(x1.astype(BF16), wr_ref[...]) + br_ref[...]).T[:SUBLANES]

    @pl.when(i < n_prompt_tiles)
    def _():
        body(xp_ref, op_ref, yp_ref)

    @pl.when(i >= n_prompt_tiles)
    def _():
        body(xs_ref, os_ref, ys_ref)


def _merge(xp, xs, o_p, o_s, y_p, y_s, wo, g1, b1, wr, br):
    tm = TOK_TM
    tp, ts = xp.shape[0], xs.shape[0]
    assert tp % tm == 0 and ts % tm == 0
    npt, nst = tp // tm, ts // tm
    t_all = tp + ts
    pidx = lambda i: (jnp.minimum(i, npt - 1), 0)
    sidx = lambda i: (jnp.maximum(i - npt, 0), 0)
    const = lambda shape: pl.BlockSpec(shape, lambda i: (0, 0))
    row = lambda i: (i, 0)
    return pl.pallas_call(
        functools.partial(_merge_kernel, n_prompt_tiles=npt),
        grid=(npt + nst,),
        in_specs=[pl.BlockSpec((tm, D_MODEL), pidx), pl.BlockSpec((tm, D_MODEL), sidx),
                  pl.BlockSpec((tm, ATTN_WIDTH), pidx), pl.BlockSpec((tm, ATTN_WIDTH), sidx),
                  pl.BlockSpec((tm, LRU_WIDTH), pidx), pl.BlockSpec((tm, LRU_WIDTH), sidx),
                  const((D_MODEL, D_MODEL)), const((1, D_MODEL)), const((1, D_MODEL)),
                  const((D_MODEL, ROUTE_LANES)), const((1, ROUTE_LANES))],
        out_specs=[pl.BlockSpec((tm, D_MODEL), row),
                   pl.BlockSpec((tm * TILE_ROWS, LANES), row),
                   pl.BlockSpec((SUBLANES, tm), lambda i: (0, i))],
        out_shape=[jax.ShapeDtypeStruct((t_all, D_MODEL), F32),
                   jax.ShapeDtypeStruct((t_all * TILE_ROWS, LANES), F32),
                   jax.ShapeDtypeStruct((SUBLANES, t_all), F32)],
        compiler_params=_cparams(1),
        name="merge_ln1_route",
    )(xp, xs, o_p, o_s, y_p, y_s, wo.astype(BF16), g1, b1, wr.astype(BF16), br)


def _tile_group(i, cum_ref):
    t = jnp.minimum(i, cum_ref[N_GROUPS - 1] - 1)
    g = jnp.int32(0)
    for k in range(N_GROUPS - 1):
        g = g + (cum_ref[k] <= t).astype(I32)
    return g


def _expert_kernel(cum_ref, src0_ref, gsrc_ref, sdst_ref,
                   x_hbm, info_ref, wg_ref, wu_ref, wd_ref, out_hbm,
                   xbuf, ybuf, xb_ref, infot_ref, yacc_ref, gsem, ssem, zsem, *, skip_tokens):
    tm, ch, tr = MOE_TM, MOE_CHUNK, TILE_ROWS
    last_e = EXPERTS_PER_GROUP - 1
    i = pl.program_id(0)
    e = pl.program_id(1)
    slot = i % 2
    other = 1 - slot
    n_live = cum_ref[N_GROUPS - 1]

    def start_gather(idx_ref, rows, base, sl):
        for r in rows:
            src = pl.ds(pl.multiple_of(idx_ref[0, 0, r], tr), tr)
            dst = pl.ds(pl.multiple_of((base + r) * tr, tr), tr)
            pltpu.make_async_copy(x_hbm.at[src], xbuf.at[sl, dst],
                                  gsem.at[sl]).start(priority=r % 2)

    def start_scatter(idx_ref, rows, base, sl):
        for r in rows:
            src = pl.ds(pl.multiple_of((base + r) * tr, tr), tr)
            dst = pl.ds(pl.multiple_of(idx_ref[0, 0, r], tr), tr)
            pltpu.make_async_copy(ybuf.at[sl, src], out_hbm.at[dst],
                                  ssem.at[sl]).start(priority=r % 2)

    def wait_tile(buf, sem):
        pltpu.make_async_copy(x_hbm.at[pl.ds(0, tm * tr)], buf, sem).wait()

    def move_rows():
        start_gather(gsrc_ref, range(ch), e * ch, other)
        start_scatter(sdst_ref, range(ch), e * ch, other)

    @pl.when((i == 0) & (e == 0))
    def _():
        ybuf[1] = jnp.zeros(ybuf.shape[1:], F32)
        yacc_ref[...] = jnp.zeros(yacc_ref.shape, F32)
        start_gather(src0_ref, range(tm), 0, 0)
        first, count = skip_tokens
        if count:
            zero_rows = pltpu.make_async_copy(ybuf.at[1, pl.ds(0, count * tr)],
                                              out_hbm.at[pl.ds(first * tr, count * tr)],
                                              zsem.at[0])
            zero_rows.start()
            zero_rows.wait()

    @pl.when((e == 0) & (i <= n_live))
    def _():
        wait_tile(xbuf.at[slot], gsem.at[slot])

    @pl.when((e == last_e) & (i >= 1) & (i <= n_live))
    def _():
        wait_tile(ybuf.at[slot], ssem.at[slot])

    @pl.when(i == n_live)
    def _():
        move_rows()

    @pl.when((e == 0) & (i < n_live))
    def _():
        xb_ref[...] = _from_token_tiles(xbuf.at[slot], tm).astype(BF16)
        infot_ref[...] = info_ref[0].T

    @pl.when(i < n_live)
    def _():
        move_rows()
        w_in = jnp.concatenate([wg_ref[0], wu_ref[0]], axis=1).astype(BF16)
        gu = jnp.dot(xb_ref[...], w_in, preferred_element_type=F32)
        gate = gu[:, :D_EXPERT]
        up = gu[:, D_EXPERT:]
        expert = (_tile_group(i, cum_ref) * EXPERTS_PER_GROUP + e).astype(F32)
        info = infot_ref[...]
        pair = info[:, REC_PAIR:REC_PAIR + 1]
        first = jnp.floor(pair * (1.0 / N_EXPERTS))
        second = pair - first * N_EXPERTS
        comb = (jnp.where(first == expert, info[:, REC_W1:REC_W1 + 1], 0.0)
                + jnp.where(second == expert, info[:, REC_W2:REC_W2 + 1], 0.0))
        h = (gate * jax.nn.sigmoid(gate) * up * comb).astype(BF16)
        y = jnp.dot(h, wd_ref[0].astype(BF16), preferred_element_type=F32)
        yacc_ref[...] = y + jnp.where(e > 0, yacc_ref[...], 0.0)

    @pl.when((e == last_e) & (i < n_live))
    def _():
        _to_token_tiles(ybuf.at[slot], yacc_ref[...])

    @pl.when((e == last_e) & (i == n_live))
    def _():
        wait_tile(xbuf.at[other], gsem.at[other])
        wait_tile(ybuf.at[other], ssem.at[other])


def _experts(x1t, cum, src0, gsrc, sdst, info_sorted, wg, wu, wd, *, n_tiles, n_rows,
             skip_tokens):
    tm, ch = MOE_TM, MOE_CHUNK
    ne = EXPERTS_PER_GROUP
    last = n_tiles - 1
    smem = lambda n, fn: pl.BlockSpec((1, 1, n), fn, memory_space=pltpu.SMEM)
    weight = lambda shape: pl.BlockSpec(
        shape, lambda i, e, cum: (_tile_group(i, cum) * ne + e, 0, 0))
    any_spec = pl.BlockSpec(memory_space=pl.ANY)
    grid_spec = pltpu.PrefetchScalarGridSpec(
        num_scalar_prefetch=1,
        grid=(n_tiles + 1, ne),
        in_specs=[smem(tm, lambda i, e, cum: (0, 0, 0)),
                  smem(ch, lambda i, e, cum: (jnp.minimum(i + 1, last) * ne + e, 0, 0)),
                  smem(ch, lambda i, e, cum: (i * ne + e, 0, 0)),
                  any_spec,
                  pl.BlockSpec((1, SUBLANES, tm), lambda i, e, cum: (jnp.minimum(i, last), 0, 0)),
                  weight((1, D_MODEL, D_EXPERT)), weight((1, D_MODEL, D_EXPERT)),
                  weight((1, D_EXPERT, D_MODEL))],
        out_specs=any_spec,
        scratch_shapes=[pltpu.VMEM((2, tm * TILE_ROWS, LANES), F32),
                        pltpu.VMEM((2, tm * TILE_ROWS, LANES), F32),
                        pltpu.VMEM((tm, D_MODEL), BF16),
                        pltpu.VMEM((tm, SUBLANES), F32),
                        pltpu.VMEM((tm, D_MODEL), F32),
                        pltpu.SemaphoreType.DMA((2,)),
                        pltpu.SemaphoreType.DMA((2,)),
                        pltpu.SemaphoreType.DMA((1,))])
    assert skip_tokens[1] <= tm
    return pl.pallas_call(
        functools.partial(_expert_kernel, skip_tokens=skip_tokens),
        grid_spec=grid_spec,
        out_shape=jax.ShapeDtypeStruct((n_rows * TILE_ROWS, LANES), F32),
        compiler_params=_cparams(2),
        name="expert_mlp",
    )(cum, src0, gsrc, sdst, x1t, info_sorted, wg, wu, wd)


def _expert_plan(info, n_tiles, n_routed):
    tm, ch = MOE_TM, MOE_CHUNK
    t_all = info.shape[1]
    token = lax.iota(I32, t_all)
    gid = jnp.where(token < n_routed,
                    info[REC_PAIR].astype(I32) // (N_EXPERTS * EXPERTS_PER_GROUP), N_GROUPS)
    _, order = lax.sort((gid, token), num_keys=1, is_stable=True)
    groups = jnp.arange(N_GROUPS, dtype=I32)
    counts = jnp.sum((gid[:, None] == groups[None, :]).astype(I32), axis=0)
    start = jnp.cumsum(counts) - counts
    tiles_g = (counts + tm - 1) // tm
    cum = jnp.cumsum(tiles_g).astype(I32)
    n_live = cum[-1]
    ti = jnp.arange(n_tiles, dtype=I32)
    g_i = jnp.minimum(jnp.sum((cum[None, :] <= ti[:, None]).astype(I32), axis=1), N_GROUPS - 1)
    j = ti - (cum - tiles_g)[g_i]
    live = ti < n_live
    tile_rows = jnp.where(live, jnp.clip(counts[g_i] - j * tm, 0, tm), 0)
    tile_start = jnp.where(live, start[g_i] + j * tm, 0)
    r = jnp.arange(tm, dtype=I32)[None, :]
    valid = r < tile_rows[:, None]
    tok = order[jnp.minimum(tile_start[:, None] + r, t_all - 1)]
    src = (jnp.where(valid, tok, 0) * TILE_ROWS).astype(I32)
    dump = ((t_all + r) * TILE_ROWS).astype(I32)
    dst = (jnp.where(valid, tok, t_all + r) * TILE_ROWS).astype(I32)
    dst_prev = jnp.concatenate([dump, dst], axis=0)
    rec = [jnp.where(valid, info[c][tok], 0.0) for c in range(REC_FIELDS)]
    rec += [jnp.zeros((n_tiles, tm), F32)] * (SUBLANES - len(rec))
    info_sorted = jnp.stack(rec, axis=1)
    return (cum, src[0].reshape(1, 1, tm), src.reshape(n_tiles * EXPERTS_PER_GROUP, 1, ch),
            dst_prev.reshape((n_tiles + 1) * EXPERTS_PER_GROUP, 1, ch), info_sorted)


def _combine_kernel(x1_ref, f_ref, g_ref, b_ref, outp_ref, outs_ref, *, n_prompt_tiles):
    tm = TOK_TM
    i = pl.program_id(0)
    f = _from_token_tiles(f_ref, tm)
    out = _layer_norm(DEEPNORM_ALPHA * x1_ref[...] + f, g_ref[...], b_ref[...])

    @pl.when(i < n_prompt_tiles)
    def _():
        outp_ref[...] = out

    @pl.when(i >= n_prompt_tiles)
    def _():
        outs_ref[...] = out


def _combine(x1, f_tiles, g2, b2, *, n_prompt, n_sample):
    tm = TOK_TM
    assert n_sample == tm
    npt = n_prompt // tm
    nt = npt + 1
    row = lambda i: (i, 0)
    const = lambda shape: pl.BlockSpec(shape, lambda i: (0, 0))
    return pl.pallas_call(
        functools.partial(_combine_kernel, n_prompt_tiles=npt),
        grid=(nt,),
        in_specs=[pl.BlockSpec((tm, D_MODEL), row),
                  pl.BlockSpec((tm * TILE_ROWS, LANES), row),
                  const((1, D_MODEL)), const((1, D_MODEL))],
        out_specs=[pl.BlockSpec((tm, D_MODEL), lambda i: (jnp.minimum(i, npt - 1), 0)),
                   pl.BlockSpec((tm, D_MODEL), lambda i: (0, 0))],
        out_shape=[jax.ShapeDtypeStruct((n_prompt, D_MODEL), F32),
                   jax.ShapeDtypeStruct((n_sample, D_MODEL), F32)],
        compiler_params=_cparams(1),
        name="combine_ln2",
    )(x1, f_tiles, g2, b2)


def _block_diag(w):
    nb, n, _ = w.shape
    eye = jnp.eye(nb, dtype=w.dtype)
    return (eye[:, None, :, None] * w[:, :, None, :]).reshape(nb * n, nb * n)


def kernel(x_prompt, x_sample, cache_k, cache_v, state_h, state_conv, page_table, w_in, lambda_q1, lambda_k1, lambda_q2, lambda_k2, subln_g, conv_w, conv_b, w_rg_a, b_rg_a, w_rg_x, b_rg_x, lru_lambda, w_out, ln1_g, ln1_b, w_group_router, b_group_router, w_expert_router, b_expert_router, w_gate_e, w_up_e, w_down_e, ln2_g, ln2_b):
    bp, sp, _ = x_prompt.shape
    bd, sd, _ = x_sample.shape
    assert w_in.shape[0] == DEPTH == 1
    l = 0
    n_prompt, n_sample = bp * sp, bd * sd

    xp =x_prompt.reshape(n_prompt, D_MODEL)
    xs = x_sample.reshape(n_sample, D_MODEL)
    w_in_b = w_in[l].astype(BF16)
    lam_params = (lambda_q1[l][None], lambda_k1[l][None], lambda_q2[l][None], lambda_k2[l][None])
    w_gates = jnp.concatenate([_block_diag(w_rg_a[l]), _block_diag(w_rg_x[l])], axis=1).astype(BF16)
    b_gates = jnp.concatenate([b_rg_a[l].reshape(1, -1), b_rg_x[l].reshape(1, -1)], axis=1)
    lru_args = (conv_w[l], conv_b[l][None], w_gates, b_gates, lru_lambda[l][None])

    qt, kh, vt, k_p, v_p, y_p, h_p, xl_tail = _inproj(xp, w_in_b, lru_args, prompt=True,
                                                      tm=INPROJ_TM, seq=sp)
    o_p = _attn_prompt(qt, kh, vt, lam_params, subln_g[l][:, None], batch=bp, seq=sp)

    q_s, k_s, v_s, xl_s, gl_s = _inproj(xs, w_in_b, prompt=False, tm=n_sample)
    n_phys = cache_k.shape[1]
    ck = cache_k.reshape(DEPTH * n_phys * PAGE_SIZE * N_HEADS, V_DIM)
    cv = cache_v.reshape(DEPTH * n_phys * PAGE_SIZE * N_HEADS, V_DIM)
    o_s = _attn_sample(q_s, k_s, v_s, ck, cv, page_table, lam_params, subln_g[l][None],
                       dec_batch=bd, dec_seq=sd)
    conv0 = jnp.concatenate(
        [jnp.zeros((bd, SUBLANES - (CONV_W - 1), LRU_WIDTH), F32), state_conv[l]], axis=1)
    y_s, h_s = _rglru(xl_s, gl_s, conv0, state_h[l][:, None, :], *lru_args,
                      batch=bd, seq=sd, ts=sd)

    w_router = jnp.concatenate(
        [w_group_router[l], w_expert_router[l],
         jnp.zeros((D_MODEL, ROUTE_LANES - N_GROUPS - N_EXPERTS), F32)], axis=1)
    b_router = jnp.concatenate(
        [b_group_router[l], b_expert_router[l],
         jnp.zeros((ROUTE_LANES - N_GROUPS - N_EXPERTS,), F32)])[None]
    pad = (-n_sample) % TOK_TM
    pad_rows = lambda a: jnp.pad(a, ((0, pad), (0, 0)))
    x1, x1t, info = _merge(xp, pad_rows(xs), o_p, pad_rows(o_s), y_p, pad_rows(y_s), w_out[l],
                           ln1_g[l][None], ln1_b[l][None], w_router, b_router)
    t_all = n_prompt + n_sample + pad
    n_tiles = -(-t_all // MOE_TM) + N_GROUPS
    cum, src0, gsrc, sdst, info_sorted = _expert_plan(info, n_tiles, n_prompt + n_sample)
    f_tiles = _experts(x1t, cum, src0, gsrc, sdst, info_sorted,
                       w_gate_e[l], w_up_e[l], w_down_e[l],
                       n_tiles=n_tiles, n_rows=t_all + MOE_TM,
                       skip_tokens=(n_prompt + n_sample, pad))
    out_p, out_s = _combine(x1, f_tiles, ln2_g[l][None], ln2_b[l][None],
                            n_prompt=n_prompt, n_sample=n_sample + pad)
    out_s = out_s[:n_sample]

    conv_p = xl_tail[:, SUBLANES - (CONV_W - 1):]
    conv_s = jnp.concatenate([state_conv[l], xl_s.reshape(bd, sd, LRU_WIDTH)],
                             axis=1)[:, -(CONV_W - 1):]
    return (out_p.reshape(bp, sp, D_MODEL),
            out_s.reshape(bd, sd, D_MODEL),
            k_p.reshape(1, bp, sp, N_HEADS, V_DIM),
            v_p.reshape(1, bp, sp, N_HEADS, V_DIM),
            h_p.reshape(1, bp, LRU_WIDTH),
            conv_p[None],
            k_s.reshape(1, bd, sd, N_HEADS, V_DIM),
            v_s.reshape(1, bd, sd, N_HEADS, V_DIM),
            h_s.reshape(1, bd, LRU_WIDTH),
            conv_s[None])
```

```python
import functools
import math

import jax
import jax.numpy as jnp
from jax import lax
from jax.experimental import pallas as pl
from jax.experimental.pallas import tpu as pltpu

F32 = jnp.float32
BF16 = jnp.bfloat16
I32 = jnp.int32

D_MODEL = 1024
ATTN_WIDTH = 512
LRU_WIDTH = 512
N_HEADS = 4
V_DIM = 128
QK_DIM = 64
CONV_W = 4
LRU_C = 8.0
IN_WIDTH = 3 * ATTN_WIDTH + 2 * LRU_WIDTH
N_GROUPS = 4
EXPERTS_PER_GROUP = 8
N_EXPERTS = 32
D_EXPERT = 256
PAGE_SIZE = 128
LN_EPS = 1e-5
DEPTH = 1
DEEPNORM_ALPHA = (2 * DEPTH) ** 0.25
LAM_INIT = 0.8 - 0.6 * math.exp(-0.3 * 0)
QK_SCALE = QK_DIM ** -0.5 * math.log2(math.e)

SUBLANES = 8
LANES = 128
VMEM_LIMIT_BYTES = 56 * 1024 * 1024

INPROJ_TM = 512
ATTN_TQ = 512
ATTN_TK = 512
ATTN_CB = 256
ATTN_HEADS_PER_STEP = 4
TOK_TM = 512
MOE_TM = 1024
MOE_CHUNK = MOE_TM // EXPERTS_PER_GROUP
PAGES_PER_STEP = 16
ROUTE_LANES = 128
REC_PAIR, REC_W1, REC_W2 = 0, 1, 2
REC_FIELDS = 3
TILE_ROWS = D_MODEL // LANES


def _cparams(n_axes):
    return pltpu.CompilerParams(
        dimension_semantics=("arbitrary",) * n_axes,
        vmem_limit_bytes=VMEM_LIMIT_BYTES)


def _layer_norm(x, g, b):
    mu = jnp.mean(x, axis=-1, keepdims=True)
    xc = x - mu
    var = jnp.mean(xc * xc, axis=-1, keepdims=True)
    return xc * lax.rsqrt(var + LN_EPS) * g + b


def _diff_lambda(lq1, lk1, lq2, lk2):
    s1 = jnp.sum(lq1 * lk1, axis=-1, keepdims=True)
    s2 = jnp.sum(lq2 * lk2, axis=-1, keepdims=True)
    return jnp.exp(s1) - jnp.exp(s2) + LAM_INIT


def _to_token_tiles(ref, x):
    tm = x.shape[0]
    for s in range(TILE_ROWS):
        ref[pl.ds(s, tm, stride=TILE_ROWS), :] = x[:, s * LANES:(s + 1) * LANES]


def _from_token_tiles(ref, tm):
    return jnp.concatenate(
        [ref[pl.ds(s, tm, stride=TILE_ROWS), :] for s in range(TILE_ROWS)], axis=1)


def _inproj_kernel(x_ref, w_ref, *refs, prompt, tiles_per_seq):
    x = x_ref[...].astype(BF16)
    tm = x.shape[0]

    def section(idx, width):
        return jnp.dot(x, w_ref[:, idx:idx + width], preferred_element_type=F32)

    heads = [slice(h * V_DIM, (h + 1) * V_DIM) for h in range(N_HEADS)]

    def to_head_rows(ref, val):
        for h, sl in enumerate(heads):
            ref[pl.ds(h, tm, stride=N_HEADS), :] = val[:, sl]

    if not prompt:
        q_ref, k_ref, v_ref, xl_ref, gl_ref = refs
        q_ref[...] = section(0, ATTN_WIDTH) * QK_SCALE
        to_head_rows(k_ref, section(ATTN_WIDTH, ATTN_WIDTH))
        to_head_rows(v_ref, section(2 * ATTN_WIDTH, ATTN_WIDTH))
        xl_ref[...] = section(3 * ATTN_WIDTH, LRU_WIDTH)
        gl_ref[...] = section(3 * ATTN_WIDTH + LRU_WIDTH, LRU_WIDTH)
        return

    (cw_ref, cb_ref, wgate_ref, bg_ref, lam_ref, qt_ref, kh_ref, vt_ref, k_ref, v_ref,
     y_ref, hlast_ref, xtail_ref, tail_ref, hc_ref) = refs

    @pl.when(pl.program_id(0) % tiles_per_seq == 0)
    def _():
        tail_ref[...] = jnp.zeros(tail_ref.shape, F32)
        hc_ref[...] = jnp.zeros(hc_ref.shape, F32)

    def emit_q():
        q = section(0, ATTN_WIDTH) * QK_SCALE
        chan = lax.broadcasted_iota(I32, (V_DIM, tm), 0)
        for h, sl in enumerate(heads):
            qt = q[:, sl].T
            qt_ref[h, 0] = jnp.where(chan < QK_DIM, qt, 0.0).astype(BF16)
            qt_ref[h, 1] = jnp.where(chan >= QK_DIM, qt, 0.0).astype(BF16)

    def emit_k():
        k = section(ATTN_WIDTH, ATTN_WIDTH)
        for h, sl in enumerate(heads):
            kh_ref[h] = k[:, sl].astype(BF16)
        to_head_rows(k_ref, k)

    def emit_v():
        v = section(2 * ATTN_WIDTH, ATTN_WIDTH)
        for h, sl in enumerate(heads):
            vt_ref[h, 0] = v[:, sl].T.astype(BF16)
        to_head_rows(v_ref, v)

    xl = section(3 * ATTN_WIDTH, LRU_WIDTH)
    gl = section(3 * ATTN_WIDTH + LRU_WIDTH, LRU_WIDTH)
    xtail_ref[0] = xl[tm - SUBLANES:]
    chunks = _rglru_chunks(xl, gl, y_ref, tail_ref, hc_ref, cw_ref, cb_ref, wgate_ref, bg_ref,
                           lam_ref)
    h_ends = [next(chunks)]
    for emit in (emit_q, emit_k, emit_v):
        emit()
        h_ends.append(next(chunks))
    hlast_ref[0] = jnp.concatenate(h_ends, axis=1)


def _inproj(x2d, w_bf16, lru_args=None, *, prompt, tm, seq=None):
    t = x2d.shape[0]
    assert t % tm == 0
    row = lambda i: (i, 0)
    const = lambda shape: pl.BlockSpec(shape, lambda i: (0,) * len(shape))
    spec512 = pl.BlockSpec((tm, ATTN_WIDTH), row)
    head_rows = pl.BlockSpec((tm * N_HEADS, V_DIM), row)
    f32_512 = jax.ShapeDtypeStruct((t, ATTN_WIDTH), F32)
    f32_heads = jax.ShapeDtypeStruct((t * N_HEADS, V_DIM), F32)
    in_specs = [pl.BlockSpec((tm, D_MODEL), row), const((D_MODEL, IN_WIDTH))]
    operands = [x2d, w_bf16]
    scratch = []
    tiles_per_seq = None
    if prompt:
        assert tm == ATTN_TK and seq % tm == 0
        tiles_per_seq = seq // tm
        batch = t // seq
        per_seq = lambda n: pl.BlockSpec((1, n, LRU_WIDTH), lambda i: (i // tiles_per_seq, 0, 0))
        in_specs += [const((CONV_W, LRU_WIDTH)), const((1, LRU_WIDTH)),
                     const((LRU_WIDTH, 2 * LRU_WIDTH)), const((1, 2 * LRU_WIDTH)),
                     const((1, LRU_WIDTH))]
        operands += list(lru_args)
        out_shape = [jax.ShapeDtypeStruct((N_HEADS, 2, V_DIM, t), BF16),
                     jax.ShapeDtypeStruct((N_HEADS, t, V_DIM), BF16),
                     jax.ShapeDtypeStruct((N_HEADS, t // tm, V_DIM, tm), BF16),
                     f32_heads, f32_heads,
                     jax.ShapeDtypeStruct((t, LRU_WIDTH), BF16),
                     jax.ShapeDtypeStruct((batch, 1, LRU_WIDTH), F32),
                     jax.ShapeDtypeStruct((batch, SUBLANES, LRU_WIDTH), F32)]
        out_specs = [pl.BlockSpec((N_HEADS, 2, V_DIM, tm), lambda i: (0, 0, 0, i)),
                     pl.BlockSpec((N_HEADS, tm, V_DIM), lambda i: (0, i, 0)),
                     pl.BlockSpec((N_HEADS, 1, V_DIM, tm), lambda i: (0, i, 0, 0)),
                     head_rows, head_rows, spec512, per_seq(1), per_seq(SUBLANES)]
        scratch = [pltpu.VMEM((SUBLANES, LRU_WIDTH), F32), pltpu.VMEM((1, LRU_WIDTH), F32)]
    else:
        out_shape = [f32_512, f32_heads, f32_heads, f32_512, f32_512]
        out_specs = [spec512, head_rows, head_rows, spec512, spec512]
    return pl.pallas_call(
        functools.partial(_inproj_kernel, prompt=prompt, tiles_per_seq=tiles_per_seq),
        grid=(t // tm,),
        in_specs=in_specs,
        out_specs=out_specs,
        out_shape=out_shape,
        scratch_shapes=scratch,
        compiler_params=_cparams(1),
        name="inproj_rglru_prompt" if prompt else "inproj_sample",
    )(*operands)


def _attn_prompt_kernel(qt_ref, k_ref, vt_ref, lq1_ref, lk1_ref, lq2_ref, lk2_ref, g_ref,
                        o_ref, s_ref, p_ref, a_ref, m_ref, l_ref, acc_ref):
    tq, tk, cb = ATTN_TQ, ATTN_TK, ATTN_CB
    heads = range(ATTN_HEADS_PER_STEP)
    n_cols = 2 * tq
    i = pl.program_id(2)

    m_ref[...] = jnp.full(m_ref.shape, -jnp.inf, F32)
    l_ref[...] = jnp.zeros(l_ref.shape, F32)
    acc_ref[...] = jnp.zeros(acc_ref.shape, F32)
    for hd in heads:
        p_ref[hd, 1] = jnp.zeros(p_ref.shape[2:], BF16)
        a_ref[hd, 1] = jnp.ones(a_ref.shape[2:], F32)

    def scores(j, slot):
        for hd in heads:
            k = k_ref[hd, pl.ds(pl.multiple_of(j * tk, tk), tk), :]
            for mp in range(2):
                s_ref[hd, slot, :, mp * tq:(mp + 1) * tq] = jnp.dot(
                    k, qt_ref[hd, mp], preferred_element_type=F32)

    def softmax(slot, masked):
        for hd in heads:
            for c in range(n_cols // cb):
                off = (c * cb) % tq
                cs = slice(c * cb, (c + 1) * cb)
                s = s_ref[hd, slot, :, cs]
                if masked:
                    key = lax.broadcasted_iota(I32, s.shape, 0)
                    qpos = lax.broadcasted_iota(I32, s.shape, 1) + off
                    s = jnp.where(key <= qpos, s, -jnp.inf)
                m_prev = m_ref[hd, :, cs]
                m_new = jnp.maximum(m_prev, jnp.max(s, axis=0, keepdims=True))
                alpha = jnp.exp2(m_prev - m_new)
                p = jnp.exp2(s - m_new)
                l_ref[hd, :, cs] = alpha * l_ref[hd, :, cs] + jnp.sum(p, axis=0, keepdims=True)
                m_ref[hd, :, cs] = m_new
                p_ref[hd, slot, :, cs] = p.astype(BF16)
                a_ref[hd, slot, :, cs] = alpha

    def accumulate(j, slot):
        for hd in heads:
            acc_ref[hd] = a_ref[hd, slot] * acc_ref[hd] + jnp.dot(
                vt_ref[hd, j], p_ref[hd, slot], preferred_element_type=F32)

    def stage(j, cur):
        scores(j + 1, 1 - cur)
        softmax(cur, masked=False)
        accumulate(jnp.maximum(j - 1, 0), 1 - cur)

    def diagonal(cur):
        accumulate(jnp.maximum(i - 1, 0), 1 - cur)
        softmax(cur, masked=True)
        accumulate(i, cur)

    scores(0, 0)

    def body(t, carry):
        stage(2 * t, 0)
        stage(2 * t + 1, 1)
        return carry

    lax.fori_loop(0, i // 2, body, 0)

    @pl.when(i % 2 == 0)
    def _():
        diagonal(0)

    @pl.when(i % 2 == 1)
    def _():
        stage(i - 1, 0)
        diagonal(1)

    lam = _diff_lambda(lq1_ref[...], lk1_ref[...], lq2_ref[...], lk2_ref[...])
    for hd in heads:
        o_all = acc_ref[hd] / l_ref[hd]
        o = o_all[:, :tq] - lam * o_all[:, tq:]
        ms = jnp.mean(o * o, axis=0, keepdims=True)
        o = o * lax.rsqrt(ms + LN_EPS) * g_ref[...] * (1.0 - LAM_INIT)
        o_ref[:, hd * V_DIM:(hd + 1) * V_DIM] = o.T.astype(o_ref.dtype)


def _attn_prompt(qt, kh, vt, lam_params, subln_g_col, *, batch, seq):
    assert ATTN_TQ == ATTN_TK and seq % ATTN_TQ == 0 and ATTN_TQ % ATTN_CB == 0
    hp = ATTN_HEADS_PER_STEP
    assert N_HEADS % hp == 0
    nq = seq // ATTN_TQ
    t = batch * seq
    vec = lambda n: pl.BlockSpec((1, n), lambda b, h, i: (0, 0))
    return pl.pallas_call(
        _attn_prompt_kernel,
        grid=(batch, N_HEADS // hp, nq),
        in_specs=[pl.BlockSpec((hp, 2, V_DIM, ATTN_TQ), lambda b, h, i: (h, 0, 0, b * nq + i)),
                  pl.BlockSpec((hp, seq, V_DIM), lambda b, h, i: (h, b, 0)),
                  pl.BlockSpec((hp, nq, V_DIM, ATTN_TK), lambda b, h, i: (h, b, 0, 0)),
                  vec(QK_DIM), vec(QK_DIM), vec(QK_DIM), vec(QK_DIM),
                  pl.BlockSpec((V_DIM, 1), lambda b, h, i: (0, 0))],
        out_specs=pl.BlockSpec((ATTN_TQ, hp * V_DIM), lambda b, h, i: (b * nq + i, h)),
        out_shape=jax.ShapeDtypeStruct((t, ATTN_WIDTH), BF16),
        scratch_shapes=[pltpu.VMEM((hp, 2, ATTN_TK, 2 * ATTN_TQ), F32),
                        pltpu.VMEM((hp, 2, ATTN_TK, 2 * ATTN_TQ), BF16),
                        pltpu.VMEM((hp, 2, 1, 2 * ATTN_TQ), F32),
                        pltpu.VMEM((hp, 1, 2 * ATTN_TQ), F32),
                        pltpu.VMEM((hp, 1, 2 * ATTN_TQ), F32),
                        pltpu.VMEM((hp, V_DIM, 2 * ATTN_TQ), F32)],
        compiler_params=_cparams(3),
        name="attn_prompt",
    )(qt, kh, vt, *lam_params, subln_g_col)


def _attn_sample_kernel(pt_ref, q0_ref, q1_ref, kn0_ref, kn1_ref, vn0_ref, vn1_ref,
                        ck_hbm, cv_hbm, lq1_ref, lk1_ref, lq2_ref, lk2_ref, g_ref,
                        o0_ref, o1_ref, kbuf, vbuf, ksem, vsem, bias_ref, m_ref, l_ref, acc_ref,
                        *, n_chunks, n_steps, dec_seq):
    gpp = PAGES_PER_STEP
    page_rows = PAGE_SIZE * N_HEADS
    hrows = 2 * dec_seq
    g = pl.program_id(0)
    c = g % n_chunks
    slot = g % 2
    streams = ((q0_ref, kn0_ref, vn0_ref, o0_ref), (q1_ref, kn1_ref, vn1_ref, o1_ref))

    def page_copies(step, sl):
        copies = []
        for s in range(len(streams)):
            buf = 2 * s + sl
            for p in range(gpp):
                page = pt_ref[(s * n_steps + step) * gpp + p]
                src = pl.ds(pl.multiple_of(page * page_rows, page_rows), page_rows)
                dst = pl.ds(p * page_rows, page_rows)
                copies.append(pltpu.make_async_copy(ck_hbm.at[src], kbuf.at[buf, dst],
                                                    ksem.at[buf]))
                copies.append(pltpu.make_async_copy(cv_hbm.at[src], vbuf.at[buf, dst],
                                                    vsem.at[buf]))
        return copies

    @pl.when(g == 0)
    def _():
        for cp in page_copies(0, 0):
            cp.start()

    @pl.when(g + 1 < n_steps)
    def _():
        for cp in page_copies(g + 1, 1 - slot):
            cp.start()

    def head_bias(shape):
        row_head = lax.broadcasted_iota(I32, shape, 0) // hrows
        col_head = lax.broadcasted_iota(I32, shape, 1) % N_HEADS
        return row_head == col_head

    @pl.when(g == 0)
    def _():
        bias_ref[...] = jnp.where(head_bias(bias_ref.shape), 0.0, -jnp.inf)

    @pl.when(c == 0)
    def _():
        m_ref[...] = jnp.full(m_ref.shape, -jnp.inf, F32)
        l_ref[...] = jnp.zeros(l_ref.shape, F32)
        acc_ref[...] = jnp.zeros(acc_ref.shape, F32)

    lane = lax.broadcasted_iota(I32, (dec_seq, V_DIM), 1)

    def query_block(q_ref):
        q = q_ref[...]
        q_rows = []
        for h in range(N_HEADS):
            qh = q[:, h * V_DIM:(h + 1) * V_DIM]
            q_rows += [jnp.where(lane < QK_DIM, qh, 0.0), jnp.where(lane >= QK_DIM, qh, 0.0)]
        return jnp.concatenate(q_rows, axis=0).astype(BF16)

    def update(s_idx, qblk, k, v, bias):
        s = lax.dot_general(qblk, k, (((1,), (1,)), ((), ())),
                            preferred_element_type=F32) + bias
        m_prev = m_ref[s_idx]
        m_new = jnp.maximum(m_prev, jnp.max(s, axis=-1, keepdims=True))
        alpha = jnp.exp2(m_prev - m_new)
        p = jnp.exp2(s - m_new)
        l_ref[s_idx] = alpha * l_ref[s_idx] + jnp.sum(p, axis=-1, keepdims=True)
        acc_ref[s_idx] = alpha * acc_ref[s_idx] + jnp.dot(p.astype(BF16), v,
                                                          preferred_element_type=F32)
        m_ref[s_idx] = m_new

    for cp in page_copies(g, slot):
        cp.wait()
    qblks = [query_block(refs[0]) for refs in streams]
    for s_idx in range(len(streams)):
        buf = 2 * s_idx + slot
        update(s_idx, qblks[s_idx], kbuf[buf].astype(BF16), vbuf[buf].astype(BF16),
               bias_ref[...])

    @pl.when(c == n_chunks - 1)
    def _():
        new_rows = dec_seq * N_HEADS
        pad = jnp.zeros((LANES - new_rows, V_DIM), F32)
        shape = (N_HEADS * hrows, LANES)
        qi = lax.broadcasted_iota(I32, shape, 0) % dec_seq
        kj = lax.broadcasted_iota(I32, shape, 1) // N_HEADS
        new_bias = jnp.where(head_bias(shape) & (kj <= qi), 0.0, -jnp.inf)
        lam = _diff_lambda(lq1_ref[...], lk1_ref[...], lq2_ref[...], lk2_ref[...])
        for s_idx, (_, kn_ref, vn_ref, o_ref) in enumerate(streams):
            kn = jnp.concatenate([kn_ref[...], pad], axis=0).astype(BF16)
            vn = jnp.concatenate([vn_ref[...], pad], axis=0).astype(BF16)
            update(s_idx, qblks[s_idx], kn, vn, new_bias)
            o_heads = acc_ref[s_idx] / l_ref[s_idx]
            for h in range(N_HEADS):
                o_all = o_heads[h * hrows:(h + 1) * hrows]
                o = o_all[:dec_seq] - lam * o_all[dec_seq:]
                ms = jnp.mean(o * o, axis=-1, keepdims=True)
                o_ref[:, h * V_DIM:(h + 1) * V_DIM] = (
                    o * lax.rsqrt(ms + LN_EPS) * g_ref[...] * (1.0 - LAM_INIT))


def _attn_sample(q, k_new, v_new, cache_k_rows, cache_v_rows, page_table, lam_params, subln_g,
                 *, dec_batch, dec_seq):
    n_pages = page_table.shape[1]
    assert n_pages % PAGES_PER_STEP == 0 and dec_seq % SUBLANES == 0
    n_chunks = n_pages // PAGES_PER_STEP
    n_streams = 2
    assert dec_batch % n_streams == 0
    half = dec_batch // n_streams
    n_steps = half * n_chunks
    rows = 2 * N_HEADS * dec_seq
    chunk_rows = PAGES_PER_STEP * PAGE_SIZE * N_HEADS

    def stream_specs(s):
        tok = pl.BlockSpec((dec_seq, ATTN_WIDTH), lambda g, pt: (s * half + g // n_chunks, 0))
        tok_heads = pl.BlockSpec((dec_seq * N_HEADS, V_DIM),
                                 lambda g, pt: (s * half + g // n_chunks, 0))
        return tok, tok_heads

    (tok0, heads0), (tok1, heads1) = stream_specs(0), stream_specs(1)
    out_spec = pl.BlockSpec((dec_seq, ATTN_WIDTH), lambda g, pt: (g // n_chunks, 0))
    vec = lambda n: pl.BlockSpec((1, n), lambda g, pt: (0, 0))
    any_spec = pl.BlockSpec(memory_space=pl.ANY)
    n_bufs = 2 * n_streams
    grid_spec = pltpu.PrefetchScalarGridSpec(
        num_scalar_prefetch=1,
        grid=(n_steps,),
        in_specs=[tok0, tok1, heads0, heads1, heads0, heads1, any_spec, any_spec,
                  vec(QK_DIM), vec(QK_DIM), vec(QK_DIM), vec(QK_DIM), vec(V_DIM)],
        out_specs=[out_spec, out_spec],
        scratch_shapes=[pltpu.VMEM((n_bufs, chunk_rows, V_DIM), F32),
                        pltpu.VMEM((n_bufs, chunk_rows, V_DIM), F32),
                        pltpu.SemaphoreType.DMA((n_bufs,)),
                        pltpu.SemaphoreType.DMA((n_bufs,)),
                        pltpu.VMEM((rows, chunk_rows), F32),
                        pltpu.VMEM((n_streams, rows, 1), F32),
                        pltpu.VMEM((n_streams, rows, 1), F32),
                        pltpu.VMEM((n_streams, rows, V_DIM), F32)])
    half_out = jax.ShapeDtypeStruct((half * dec_seq, ATTN_WIDTH), F32)
    o0, o1 = pl.pallas_call(
        functools.partial(_attn_sample_kernel, n_chunks=n_chunks, n_steps=n_steps,
                          dec_seq=dec_seq),
        grid_spec=grid_spec,
        out_shape=[half_out, half_out],
        compiler_params=_cparams(1),
        name="attn_sample",
    )(page_table.reshape(-1), q, q, k_new, k_new, v_new, v_new, cache_k_rows, cache_v_rows,
      *lam_params, subln_g)
    return jnp.concatenate([o0, o1], axis=0)


def _softplus(x):
    y = jnp.exp(-jnp.abs(x))
    w = 1.0 + y
    log1p = jnp.where(w == 1.0, y, jnp.log(w) * (y / (w - 1.0)))
    return jnp.maximum(x, 0.0) + log1p


def _rglru_tile(*args):
    return jnp.concatenate(list(_rglru_chunks(*args)), axis=1)


def _rglru_chunks(x, gl, y_ref, tail_ref, hc_ref, cw_ref, cb_ref, wg_ref, bg_ref, lam_ref):
    for c in range(LRU_WIDTH // LANES):
        cs = slice(c * LANES, (c + 1) * LANES)
        gate_w = jnp.concatenate(
            [wg_ref[cs, cs], wg_ref[cs, LRU_WIDTH + c * LANES:LRU_WIDTH + (c + 1) * LANES]], axis=1)
        gate_b = jnp.concatenate(
            [bg_ref[:, cs], bg_ref[:, LRU_WIDTH + c * LANES:LRU_WIDTH + (c + 1) * LANES]], axis=1)
        xs = x[:, cs]
        y, h_end = _rglru_chunk(xs, gl[:, cs], tail_ref[:, cs], hc_ref[:, cs],
                                cw_ref[:, cs], cb_ref[:, cs], gate_w, gate_b, lam_ref[:, cs])
        tail_ref[:, cs] = xs[xs.shape[0] - SUBLANES:]
        hc_ref[:, cs] = h_end
        y_ref[:, cs] = y.astype(y_ref.dtype)
        yield h_end


def _rglru_chunk(x, gl, tail, h_in, cw, cb, gate_w, gate_b, lam):
    ts, n = x.shape
    xfull = jnp.concatenate([tail, x], axis=0)
    xc = cb + cw[CONV_W - 1:CONV_W] * x
    for d in range(1, CONV_W):
        shifted = pltpu.roll(xfull, d, 0)[SUBLANES:]
        xc = xc + cw[CONV_W - 1 - d:CONV_W - d] * shifted

    gates = jnp.dot(xc.astype(BF16), gate_w, preferred_element_type=F32) + gate_b
    r = jax.nn.sigmoid(gates[:, :n])
    ig = jax.nn.sigmoid(gates[:, n:])
    log_a = -LRU_C * r * _softplus(-lam)
    a = jnp.exp(log_a)
    mult = jnp.sqrt(jnp.maximum(-(a * a + 1.0) * jnp.tanh(log_a), 0.0))
    u = mult * ig * xc

    row = lax.broadcasted_iota(I32, a.shape, 0)
    d = 1
    while d < ts:
        if d < SUBLANES:
            keep = row >= d
            a_prev = jnp.where(keep, pltpu.roll(a, d, 0), 1.0)
            u_prev = jnp.where(keep, pltpu.roll(u, d, 0), 0.0)
            u = a * u_prev + u
            a = a * a_prev
        else:
            u = jnp.concatenate([u[:d], a[d:] * u[:ts - d] + u[d:]], axis=0)
            a = jnp.concatenate([a[:d], a[d:] * a[:ts - d]], axis=0)
        d *= 2
    h = a * h_in + u
    return h * jax.nn.gelu(gl, approximate=True), h[ts - 1:ts]


def _rglru_kernel(xl_ref, gl_ref, conv0_ref, h0_ref, cw_ref, cb_ref, wg_ref, bg_ref, lam_ref,
                  y_ref, hlast_ref, tail_ref, hc_ref):
    s = pl.program_id(1)

    @pl.when(s == 0)
    def _():
        tail_ref[...] = conv0_ref[0]
        hc_ref[...] = h0_ref[0]

    hlast_ref[0] = _rglru_tile(xl_ref[...], gl_ref[...], y_ref, tail_ref, hc_ref,
                               cw_ref, cb_ref, wg_ref, bg_ref, lam_ref)


def _rglru(xl, gl, conv0, h0, conv_w, conv_b, w_gates, b_gates, lam, *, batch, seq, ts):
    assert seq % ts == 0 and ts % SUBLANES == 0
    ns = seq // ts
    tile = pl.BlockSpec((ts, LRU_WIDTH), lambda b, s: (b * ns + s, 0))
    const = lambda shape: pl.BlockSpec(shape, lambda b, s: (0,) * len(shape))
    return pl.pallas_call(
        _rglru_kernel,
        grid=(batch, ns),
        in_specs=[tile, tile,
                  pl.BlockSpec((1, SUBLANES, LRU_WIDTH), lambda b, s: (b, 0, 0)),
                  pl.BlockSpec((1, 1, LRU_WIDTH), lambda b, s: (b, 0, 0)),
                  const((CONV_W, LRU_WIDTH)), const((1, LRU_WIDTH)),
                  const((LRU_WIDTH, 2 * LRU_WIDTH)), const((1, 2 * LRU_WIDTH)),
                  const((1, LRU_WIDTH))],
        out_specs=[tile, pl.BlockSpec((1, 1, LRU_WIDTH), lambda b, s: (b, 0, 0))],
        out_shape=[jax.ShapeDtypeStruct((batch * seq, LRU_WIDTH), BF16 if ts > SUBLANES else F32),
                   jax.ShapeDtypeStruct((batch, 1, LRU_WIDTH), F32)],
        scratch_shapes=[pltpu.VMEM((SUBLANES, LRU_WIDTH), F32),
                        pltpu.VMEM((1, LRU_WIDTH), F32)],
        compiler_params=_cparams(2),
        name="rglru_prompt" if ts > SUBLANES else "rglru_sample",
    )(xl, gl, conv0, h0, conv_w, conv_b, w_gates, b_gates, lam)


def _route(logits):
    lane = lax.broadcasted_iota(I32, logits.shape, 1).astype(F32)
    big = float(ROUTE_LANES)
    neg = -jnp.inf
    is_g = lane < N_GROUPS
    gmax = jnp.max(jnp.where(is_g, logits, neg), axis=-1, keepdims=True)
    gsum = jnp.sum(jnp.where(is_g, jnp.exp(logits - gmax), 0.0), axis=-1, keepdims=True)
    gw = 1.0 / gsum
    gid = jnp.min(jnp.where(is_g & (logits == gmax), lane, big), axis=-1, keepdims=True)
    lo = N_GROUPS + EXPERTS_PER_GROUP * gid
    in_grp = (lane >= lo) & (lane < lo + EXPERTS_PER_GROUP)
    v1 = jnp.max(jnp.where(in_grp, logits, neg), axis=-1, keepdims=True)
    i1 = jnp.min(jnp.where(in_grp & (logits == v1), lane, big), axis=-1, keepdims=True)
    rest = in_grp & (lane != i1)
    v2 = jnp.max(jnp.where(rest, logits, neg), axis=-1, keepdims=True)
    i2 = jnp.min(jnp.where(rest & (logits == v2), lane, big), axis=-1, keepdims=True)
    t = jnp.exp(v2 - v1)
    w1 = gw / (1.0 + t)
    w2 = gw * t / (1.0 + t)
    pair = (i1 - N_GROUPS) * N_EXPERTS + (i2 - N_GROUPS)
    return jnp.where(lane == REC_PAIR, pair,
                     jnp.where(lane == REC_W1, w1, jnp.where(lane == REC_W2, w2, 0.0)))


def _merge_kernel(xp_ref, xs_ref, op_ref, os_ref, yp_ref, ys_ref, wo_ref, g_ref, b_ref,
                  wr_ref, br_ref, x1_ref, x1t_ref, info_ref, *, n_prompt_tiles):
    i = pl.program_id(0)

    def body(x_ref, o_ref, y_ref):
        dot = functools.partial(jnp.dot, preferred_element_type=F32)
        m = dot(o_ref[...].astype(BF16), wo_ref[:ATTN_WIDTH, :])
        m = m + dot(y_ref[...].astype(BF16), wo_ref[ATTN_WIDTH:, :])
        x1 = _layer_norm(DEEPNORM_ALPHA * x_ref[...] + m, g_ref[...], b_ref[...])
        x1_ref[...] = x1
        _to_token_tiles(x1t_ref, x1)
        info_ref[...] = _route(dot(x1.astype(BF16), wr_ref[...]) + br_ref[...]).T[:SUBLANES]

    @pl.when(i < n_prompt_tiles)
    def _():
        body(xp_ref, op_ref, yp_ref)

    @pl.when(i >= n_prompt_tiles)
    def _():
        body(xs_ref, os_ref, ys_ref)


def _merge(xp, xs, o_p, o_s, y_p, y_s, wo, g1, b1, wr, br):
    tm = TOK_TM
    tp, ts = xp.shape[0], xs.shape[0]
    assert tp % tm == 0 and ts % tm == 0
    npt, nst = tp // tm, ts // tm
    t_all = tp + ts
    pidx = lambda i: (jnp.minimum(i, npt - 1), 0)
    sidx = lambda i: (jnp.maximum(i - npt, 0), 0)
    const = lambda shape: pl.BlockSpec(shape, lambda i: (0, 0))
    row = lambda i: (i, 0)
    return pl.pallas_call(
        functools.partial(_merge_kernel, n_prompt_tiles=npt),
        grid=(npt + nst,),
        in_specs=[pl.BlockSpec((tm, D_MODEL), pidx), pl.BlockSpec((tm, D_MODEL), sidx),
                  pl.BlockSpec((tm, ATTN_WIDTH), pidx), pl.BlockSpec((tm, ATTN_WIDTH), sidx),
                  pl.BlockSpec((tm, LRU_WIDTH), pidx), pl.BlockSpec((tm, LRU_WIDTH), sidx),
                  const((D_MODEL, D_MODEL)), const((1, D_MODEL)), const((1, D_MODEL)),
                  const((D_MODEL, ROUTE_LANES)), const((1, ROUTE_LANES))],
        out_specs=[pl.BlockSpec((tm, D_MODEL), row),
                   pl.BlockSpec((tm * TILE_ROWS, LANES), row),
                   pl.BlockSpec((SUBLANES, tm), lambda i: (0, i))],
        out_shape=[jax.ShapeDtypeStruct((t_all, D_MODEL), F32),
                   jax.ShapeDtypeStruct((t_all * TILE_ROWS, LANES), F32),
                   jax.ShapeDtypeStruct((SUBLANES, t_all), F32)],
        compiler_params=_cparams(1),
        name="merge_ln1_route",
    )(xp, xs, o_p, o_s, y_p, y_s, wo.astype(BF16), g1, b1, wr.astype(BF16), br)


def _tile_group(i, cum_ref):
    t = jnp.minimum(i, cum_ref[N_GROUPS - 1] - 1)
    g = jnp.int32(0)
    for k in range(N_GROUPS - 1):
        g = g + (cum_ref[k] <= t).astype(I32)
    return g


def _expert_kernel(cum_ref, src0_ref, gsrc_ref, sdst_ref,
                   x_hbm, info_ref, wg_ref, wu_ref, wd_ref, out_hbm,
                   xbuf, ybuf, xb_ref, infot_ref, yacc_ref, gsem, ssem, zsem, *, skip_tokens):
    tm, ch, tr = MOE_TM, MOE_CHUNK, TILE_ROWS
    last_e = EXPERTS_PER_GROUP - 1
    i = pl.program_id(0)
    e = pl.program_id(1)
    slot = i % 2
    other = 1 - slot
    n_live = cum_ref[N_GROUPS - 1]

    def start_gather(idx_ref, rows, base, sl):
        for r in rows:
            src = pl.ds(pl.multiple_of(idx_ref[0, 0, r], tr), tr)
            dst = pl.ds(pl.multiple_of((base + r) * tr, tr), tr)
            pltpu.make_async_copy(x_hbm.at[src], xbuf.at[sl, dst],
                                  gsem.at[sl]).start(priority=r % 2)

    def start_scatter(idx_ref, rows, base, sl):
        for r in rows:
            src = pl.ds(pl.multiple_of((base + r) * tr, tr), tr)
            dst = pl.ds(pl.multiple_of(idx_ref[0, 0, r], tr), tr)
            pltpu.make_async_copy(ybuf.at[sl, src], out_hbm.at[dst],
                                  ssem.at[sl]).start(priority=r % 2)

    def wait_tile(buf, sem):
        pltpu.make_async_copy(x_hbm.at[pl.ds(0, tm * tr)], buf, sem).wait()

    def move_rows():
        start_gather(gsrc_ref, range(ch), e * ch, other)
        start_scatter(sdst_ref, range(ch), e * ch, other)

    @pl.when((i == 0) & (e == 0))
    def _():
        ybuf[1] = jnp.zeros(ybuf.shape[1:], F32)
        yacc_ref[...] = jnp.zeros(yacc_ref.shape, F32)
        start_gather(src0_ref, range(tm), 0, 0)
        first, count = skip_tokens
        if count:
            zero_rows = pltpu.make_async_copy(ybuf.at[1, pl.ds(0, count * tr)],
                                              out_hbm.at[pl.ds(first * tr, count * tr)],
                                              zsem.at[0])
            zero_rows.start()
            zero_rows.wait()

    @pl.when((e == 0) & (i <= n_live))
    def _():
        wait_tile(xbuf.at[slot], gsem.at[slot])

    @pl.when((e == last_e) & (i >= 1) & (i <= n_live))
    def _():
        wait_tile(ybuf.at[slot], ssem.at[slot])

    @pl.when(i == n_live)
    def _():
        move_rows()

    @pl.when((e == 0) & (i < n_live))
    def _():
        xb_ref[...] = _from_token_tiles(xbuf.at[slot], tm).astype(BF16)
        infot_ref[...] = info_ref[0].T

    @pl.when(i < n_live)
    def _():
        move_rows()
        w_in = jnp.concatenate([wg_ref[0], wu_ref[0]], axis=1).astype(BF16)
        gu = jnp.dot(xb_ref[...], w_in, preferred_element_type=F32)
        gate = gu[:, :D_EXPERT]
        up = gu[:, D_EXPERT:]
        expert = (_tile_group(i, cum_ref) * EXPERTS_PER_GROUP + e).astype(F32)
        info = infot_ref[...]
        pair = info[:, REC_PAIR:REC_PAIR + 1]
        first = jnp.floor(pair * (1.0 / N_EXPERTS))
        second = pair - first * N_EXPERTS
        comb = (jnp.where(first == expert, info[:, REC_W1:REC_W1 + 1], 0.0)
                + jnp.where(second == expert, info[:, REC_W2:REC_W2 + 1], 0.0))
        h = (gate * jax.nn.sigmoid(gate) * up * comb).astype(BF16)
        y = jnp.dot(h, wd_ref[0].astype(BF16), preferred_element_type=F32)
        yacc_ref[...] = y + jnp.where(e > 0, yacc_ref[...], 0.0)

    @pl.when((e == last_e) & (i < n_live))
    def _():
        _to_token_tiles(ybuf.at[slot], yacc_ref[...])

    @pl.when((e == last_e) & (i == n_live))
    def _():
        wait_tile(xbuf.at[other], gsem.at[other])
        wait_tile(ybuf.at[other], ssem.at[other])


def _experts(x1t, cum, src0, gsrc, sdst, info_sorted, wg, wu, wd, *, n_tiles, n_rows,
             skip_tokens):
    tm, ch = MOE_TM, MOE_CHUNK
    ne = EXPERTS_PER_GROUP
    last = n_tiles - 1
    smem = lambda n, fn: pl.BlockSpec((1, 1, n), fn, memory_space=pltpu.SMEM)
    weight = lambda shape: pl.BlockSpec(
        shape, lambda i, e, cum: (_tile_group(i, cum) * ne + e, 0, 0))
    any_spec = pl.BlockSpec(memory_space=pl.ANY)
    grid_spec = pltpu.PrefetchScalarGridSpec(
        num_scalar_prefetch=1,
        grid=(n_tiles + 1, ne),
        in_specs=[smem(tm, lambda i, e, cum: (0, 0, 0)),
                  smem(ch, lambda i, e, cum: (jnp.minimum(i + 1, last) * ne + e, 0, 0)),
                  smem(ch, lambda i, e, cum: (i * ne + e, 0, 0)),
                  any_spec,
                  pl.BlockSpec((1, SUBLANES, tm), lambda i, e, cum: (jnp.minimum(i, last), 0, 0)),
                  weight((1, D_MODEL, D_EXPERT)), weight((1, D_MODEL, D_EXPERT)),
                  weight((1, D_EXPERT, D_MODEL))],
        out_specs=any_spec,
        scratch_shapes=[pltpu.VMEM((2, tm * TILE_ROWS, LANES), F32),
                        pltpu.VMEM((2, tm * TILE_ROWS, LANES), F32),
                        pltpu.VMEM((tm, D_MODEL), BF16),
                        pltpu.VMEM((tm, SUBLANES), F32),
                        pltpu.VMEM((tm, D_MODEL), F32),
                        pltpu.SemaphoreType.DMA((2,)),
                        pltpu.SemaphoreType.DMA((2,)),
                        pltpu.SemaphoreType.DMA((1,))])
    assert skip_tokens[1] <= tm
    return pl.pallas_call(
        functools.partial(_expert_kernel, skip_tokens=skip_tokens),
        grid_spec=grid_spec,
        out_shape=jax.ShapeDtypeStruct((n_rows * TILE_ROWS, LANES), F32),
        compiler_params=_cparams(2),
        name="expert_mlp",
    )(cum, src0, gsrc, sdst, x1t, info_sorted, wg, wu, wd)


def _expert_plan(info, n_tiles, n_routed):
    tm, ch = MOE_TM, MOE_CHUNK
    t_all = info.shape[1]
    token = lax.iota(I32, t_all)
    gid = jnp.where(token < n_routed,
                    info[REC_PAIR].astype(I32) // (N_EXPERTS * EXPERTS_PER_GROUP), N_GROUPS)
    _, order = lax.sort((gid, token), num_keys=1, is_stable=True)
    groups = jnp.arange(N_GROUPS, dtype=I32)
    counts = jnp.sum((gid[:, None] == groups[None, :]).astype(I32), axis=0)
    start = jnp.cumsum(counts) - counts
    tiles_g = (counts + tm - 1) // tm
    cum = jnp.cumsum(tiles_g).astype(I32)
    n_live = cum[-1]
    ti = jnp.arange(n_tiles, dtype=I32)
    g_i = jnp.minimum(jnp.sum((cum[None, :] <= ti[:, None]).astype(I32), axis=1), N_GROUPS - 1)
    j = ti - (cum - tiles_g)[g_i]
    live = ti < n_live
    tile_rows = jnp.where(live, jnp.clip(counts[g_i] - j * tm, 0, tm), 0)
    tile_start = jnp.where(live, start[g_i] + j * tm, 0)
    r = jnp.arange(tm, dtype=I32)[None, :]
    valid = r < tile_rows[:, None]
    tok = order[jnp.minimum(tile_start[:, None] + r, t_all - 1)]
    src = (jnp.where(valid, tok, 0) * TILE_ROWS).astype(I32)
    dump = ((t_all + r) * TILE_ROWS).astype(I32)
    dst = (jnp.where(valid, tok, t_all + r) * TILE_ROWS).astype(I32)
    dst_prev = jnp.concatenate([dump, dst], axis=0)
    rec = [jnp.where(valid, info[c][tok], 0.0) for c in range(REC_FIELDS)]
    rec += [jnp.zeros((n_tiles, tm), F32)] * (SUBLANES - len(rec))
    info_sorted = jnp.stack(rec, axis=1)
    return (cum, src[0].reshape(1, 1, tm), src.reshape(n_tiles * EXPERTS_PER_GROUP, 1, ch),
            dst_prev.reshape((n_tiles + 1) * EXPERTS_PER_GROUP, 1, ch), info_sorted)


def _combine_kernel(x1_ref, f_ref, g_ref, b_ref, outp_ref, outs_ref, *, n_prompt_tiles):
    tm = TOK_TM
    i = pl.program_id(0)
    f = _from_token_tiles(f_ref, tm)
    out = _layer_norm(DEEPNORM_ALPHA * x1_ref[...] + f, g_ref[...], b_ref[...])

    @pl.when(i < n_prompt_tiles)
    def _():
        outp_ref[...] = out

    @pl.when(i >= n_prompt_tiles)
    def _():
        outs_ref[...] = out


def _combine(x1, f_tiles, g2, b2, *, n_prompt, n_sample):
    tm = TOK_TM
    assert n_sample == tm
    npt = n_prompt // tm
    nt = npt + 1
    row = lambda i: (i, 0)
    const = lambda shape: pl.BlockSpec(shape, lambda i: (0, 0))
    return pl.pallas_call(
        functools.partial(_combine_kernel, n_prompt_tiles=npt),
        grid=(nt,),
        in_specs=[pl.BlockSpec((tm, D_MODEL), row),
                  pl.BlockSpec((tm * TILE_ROWS, LANES), row),
                  const((1, D_MODEL)), const((1, D_MODEL))],
        out_specs=[pl.BlockSpec((tm, D_MODEL), lambda i: (jnp.minimum(i, npt - 1), 0)),
                   pl.BlockSpec((tm, D_MODEL), lambda i: (0, 0))],
        out_shape=[jax.ShapeDtypeStruct((n_prompt, D_MODEL), F32),
                   jax.ShapeDtypeStruct((n_sample, D_MODEL), F32)],
        compiler_params=_cparams(1),
        name="combine_ln2",
    )(x1, f_tiles, g2, b2)


def _block_diag(w):
    nb, n, _ = w.shape
    eye = jnp.eye(nb, dtype=w.dtype)
    return (eye[:, None, :, None] * w[:, :, None, :]).reshape(nb * n, nb * n)


def kernel(x_prompt, x_sample, cache_k, cache_v, state_h, state_conv, page_table, w_in, lambda_q1, lambda_k1, lambda_q2, lambda_k2, subln_g, conv_w, conv_b, w_rg_a, b_rg_a, w_rg_x, b_rg_x, lru_lambda, w_out, ln1_g, ln1_b, w_group_router, b_group_router, w_expert_router, b_expert_router, w_gate_e, w_up_e, w_down_e, ln2_g, ln2_b):
    bp, sp, _ = x_prompt.shape
    bd, sd, _ = x_sample.shape
    assert w_in.shape[0] == DEPTH == 1
    l = 0
    n_prompt, n_sample = bp * sp, bd * sd

    xp =x_prompt.reshape(n_prompt, D_MODEL)
    xs = x_sample.reshape(n_sample, D_MODEL)
    w_in_b = w_in[l].astype(BF16)
    lam_params = (lambda_q1[l][None], lambda_k1[l][None], lambda_q2[l][None], lambda_k2[l][None])
    w_gates = jnp.concatenate([_block_diag(w_rg_a[l]), _block_diag(w_rg_x[l])], axis=1).astype(BF16)
    b_gates = jnp.concatenate([b_rg_a[l].reshape(1, -1), b_rg_x[l].reshape(1, -1)], axis=1)
    lru_args = (conv_w[l], conv_b[l][None], w_gates, b_gates, lru_lambda[l][None])

    qt, kh, vt, k_p, v_p, y_p, h_p, xl_tail = _inproj(xp, w_in_b, lru_args, prompt=True,
                                                      tm=INPROJ_TM, seq=sp)
    o_p = _attn_prompt(qt, kh, vt, lam_params, subln_g[l][:, None], batch=bp, seq=sp)

    q_s, k_s, v_s, xl_s, gl_s = _inproj(xs, w_in_b, prompt=False, tm=n_sample)
    n_phys = cache_k.shape[1]
    ck = cache_k.reshape(DEPTH * n_phys * PAGE_SIZE * N_HEADS, V_DIM)
    cv = cache_v.reshape(DEPTH * n_phys * PAGE_SIZE * N_HEADS, V_DIM)
    o_s = _attn_sample(q_s, k_s, v_s, ck, cv, page_table, lam_params, subln_g[l][None],
                       dec_batch=bd, dec_seq=sd)
    conv0 = jnp.concatenate(
        [jnp.zeros((bd, SUBLANES - (CONV_W - 1), LRU_WIDTH), F32), state_conv[l]], axis=1)
    y_s, h_s = _rglru(xl_s, gl_s, conv0, state_h[l][:, None, :], *lru_args,
                      batch=bd, seq=sd, ts=sd)

    w_router = jnp.concatenate(
        [w_group_router[l], w_expert_router[l],
         jnp.zeros((D_MODEL, ROUTE_LANES - N_GROUPS - N_EXPERTS), F32)], axis=1)
    b_router = jnp.concatenate(
        [b_group_router[l], b_expert_router[l],
         jnp.zeros((ROUTE_LANES - N_GROUPS - N_EXPERTS,), F32)])[None]
    pad = (-n_sample) % TOK_TM
    pad_rows = lambda a: jnp.pad(a, ((0, pad), (0, 0)))
    x1, x1t, info = _merge(xp, pad_rows(xs), o_p, pad_rows(o_s), y_p, pad_rows(y_s), w_out[l],
                           ln1_g[l][None], ln1_b[l][None], w_router, b_router)
    t_all = n_prompt + n_sample + pad
    n_tiles = -(-t_all // MOE_TM) + N_GROUPS
    cum, src0, gsrc, sdst, info_sorted = _expert_plan(info, n_tiles, n_prompt + n_sample)
    f_tiles = _experts(x1t, cum, src0, gsrc, sdst, info_sorted,
                       w_gate_e[l], w_up_e[l], w_down_e[l],
                       n_tiles=n_tiles, n_rows=t_all + MOE_TM,
                       skip_tokens=(n_prompt + n_sample, pad))
    out_p, out_s = _combine(x1, f_tiles, ln2_g[l][None], ln2_b[l][None],
                            n_prompt=n_prompt, n_sample=n_sample + pad)
    out_s = out_s[:n_sample]

    conv_p = xl_tail[:, SUBLANES - (CONV_W - 1):]
    conv_s = jnp.concatenate([state_conv[l], xl_s.reshape(bd, sd, LRU_WIDTH)],
                             axis=1)[:, -(CONV_W - 1):]
    return (out_p.reshape(bp, sp, D_MODEL),
            out_s.reshape(bd, sd, D_MODEL),
            k_p.reshape(1, bp, sp, N_HEADS, V_DIM),
            v_p.reshape(1, bp, sp, N_HEADS, V_DIM),
            h_p.reshape(1, bp, LRU_WIDTH),
            conv_p[None],
            k_s.reshape(1, bd, sd, N_HEADS, V_DIM),
            v_s.reshape(1, bd, sd, N_HEADS, V_DIM),
            h_s.reshape(1, bd, LRU_WIDTH),
            conv_s[None])
```
